```python
import math
import jax
import jax.numpy as jnp
from jax import lax
import numpy as np

D_MODEL = 1024
BATCH = 2
SEQ = 8192
DEPTH = 1

ATTN_WIDTH = D_MODEL // 2
HYENA_WIDTH = D_MODEL - ATTN_WIDTH
DA_HEADS = 4
DA_HEAD_DIM = ATTN_WIDTH // (2 * DA_HEADS)
DA_V_DIM = 2 * DA_HEAD_DIM
ROT_DIM = DA_HEAD_DIM // 4
ROPE_THETA = 500000.0
Q_BLOCK = 128
IN_COLS = 3 * ATTN_WIDTH + 3 * HYENA_WIDTH

SHORT_CONV = 3
FILTER_EMB = 33
FILTER_BANDS = (FILTER_EMB - 1) // 2
FILTER_ORDER = 64
DECAY_TARGET = 1e-2
FAST_DECAY = 0.3
SLOW_DECAY = 1.5

N_EXPERTS = 256
TOP_K = 8
N_GROUPS = 8
TOPK_GROUPS = 4
EXPERT_DIM = 256
ROUTED_SCALE = 2.5
MOE_BLOCK = 64

ALPHA = (2 * DEPTH) ** 0.25
BETA = (8 * DEPTH) ** -0.25
LN_EPS = 1e-5

kernel_name = 'hybrid_diffattn_hyena_moe_encoder'


def _f32(a):
    return a.astype(jnp.float32)


def layer_norm(x, g, b):
    xf = _f32(x)
    mu = jnp.mean(xf, axis=-1, keepdims=True)
    var = jnp.mean(jnp.square(xf - mu), axis=-1, keepdims=True)
    y = (xf - mu) * lax.rsqrt(var + LN_EPS) * _f32(g) + _f32(b)
    return y.astype(x.dtype)


def rms_norm(x, g):
    xf = _f32(x)
    y = xf * lax.rsqrt(jnp.mean(jnp.square(xf), axis=-1, keepdims=True) + LN_EPS) * _f32(g)
    return y.astype(x.dtype)


def rotary_tables(positions):
    inv_freq = ROPE_THETA ** (-jnp.arange(0, ROT_DIM, 2, dtype=jnp.float32) / ROT_DIM)
    ang = _f32(positions)[..., None] * inv_freq
    return jnp.cos(ang), jnp.sin(ang)


def apply_partial_rotary(t, cos, sin):
    half = ROT_DIM // 2
    t1 = t[..., :half]
    t2 = t[..., half:ROT_DIM]
    return jnp.concatenate([t1 * cos - t2 * sin, t2 * cos + t1 * sin, t[..., ROT_DIM:]], axis=-1)


def differential_attention(q, k, v, lam, lam_init, subln_g):
    B, L, H = q.shape[0], q.shape[1], q.shape[2]
    nb = L // Q_BLOCK
    scale = DA_HEAD_DIM ** -0.5
    qb = (q * scale).reshape(B, nb, Q_BLOCK, H, 2, DA_HEAD_DIM).transpose(1, 0, 3, 4, 2, 5)
    kt = k.transpose(0, 2, 3, 1, 4)
    vt = v.transpose(0, 2, 1, 3)
    lam_c = lam.astype(v.dtype)

    def block(q_blk):
        s = _f32(jnp.einsum('bhcqd,bhckd->bhcqk', q_blk, kt))
        p = jax.nn.softmax(s, axis=-1).astype(vt.dtype)
        o = jnp.einsum('bhcqk,bhkv->bhcqv', p, vt)
        return o[:, :, 0] - lam_c * o[:, :, 1]

    o = lax.map(block, qb)
    o = o.transpose(1, 0, 3, 2, 4).reshape(B, L, H, DA_V_DIM)
    o = rms_norm(o, subln_g) * (1.0 - lam_init)
    return o.reshape(B, L, H * DA_V_DIM)


def short_conv(u, w, b):
    C = u.shape[-1]
    pad = SHORT_CONV // 2
    y = lax.conv_general_dilated(u, w[:, None, :].astype(u.dtype), window_strides=(1,),
                                 padding=[(pad, pad)], dimension_numbers=('NWC', 'WIO', 'NWC'),
                                 feature_group_count=C)
    return y + b.astype(u.dtype)


def hyena_long_filter(L, w1, b1, w2, b2, w3, b3, freq, wout):
    C = wout.shape[-1] // 2
    t = jnp.linspace(0.0, 1.0, L, dtype=jnp.float32)[:, None]
    w = 2.0 * math.pi * jnp.arange(L, dtype=jnp.float32)[:, None] / L
    f = jnp.linspace(1e-4, FILTER_BANDS - 1, FILTER_BANDS, dtype=jnp.float32)[None, :]
    z = jnp.concatenate([t, jnp.cos(f * w), -jnp.sin(f * w)], axis=-1)
    freq = _f32(freq)
    h = jnp.sin(freq[0] * (z @ _f32(w1) + _f32(b1)))
    h = jnp.sin(freq[1] * (h @ _f32(w2) + _f32(b2)))
    h = jnp.sin(freq[2] * (h @ _f32(w3) + _f32(b3)))
    h = h @ _f32(wout)
    deltas = jnp.abs(jnp.linspace(math.log(DECAY_TARGET) / SLOW_DECAY,
                                  math.log(DECAY_TARGET) / FAST_DECAY, C, dtype=jnp.float32))
    decay = jnp.exp(-t * deltas)
    h_fwd = h[:, :C] * decay
    h_bwd = h[:, C:] * decay
    taps = jnp.concatenate([h_fwd, jnp.zeros((1, C), jnp.float32), jnp.flip(h_bwd[1:], axis=0)], axis=0)
    taps = taps / jnp.sum(jnp.abs(taps), axis=0, keepdims=True)
    return jnp.fft.rfft(taps, axis=0)


def hyena_mixer(u, conv_w, conv_b, w1, b1, w2, b2, w3, b3, freq, wout, d_skip):
    L = u.shape[1]
    u = short_conv(u, conv_w, conv_b)
    x0, x1, v = jnp.split(u, 3, axis=-1)
    v = v * x1
    filt = hyena_long_filter(L, w1, b1, w2, b2, w3, b3, freq, wout)
    vf = jnp.fft.rfft(_f32(v), n=2 * L, axis=1)
    y = jnp.fft.irfft(vf * filt[None], n=2 * L, axis=1)[:, :L]
    y = y.astype(v.dtype) + v * d_skip.astype(v.dtype)
    return x0 * y


def route(xt, w_router, bias):
    T = xt.shape[0]
    scores = jax.nn.sigmoid(_f32(xt @ w_router))
    biased = scores + _f32(bias)
    grp = biased.reshape(T, N_GROUPS, N_EXPERTS // N_GROUPS)
    grp_score = jnp.sum(lax.top_k(grp, 2)[0], axis=-1)
    _, top_groups = lax.top_k(grp_score, TOPK_GROUPS)
    group_mask = jnp.any(top_groups[..., None] == jnp.arange(N_GROUPS)[None, None, :], axis=1)
    expert_mask = jnp.repeat(group_mask, N_EXPERTS // N_GROUPS, axis=1)
    _, idx = lax.top_k(jnp.where(expert_mask, biased, -jnp.inf), TOP_K)
    gates = jnp.take_along_axis(scores, idx, axis=1)
    gates = gates / jnp.sum(gates, axis=-1, keepdims=True) * ROUTED_SCALE
    return gates, idx


def routed_experts(xt, gates, idx, w_gate, w_up, w_down):
    T, D = xt.shape
    A = T * TOP_K
    nb = -(-A // MOE_BLOCK) + N_EXPERTS
    P = nb * MOE_BLOCK
    flat_e = idx.reshape(A)
    order = jnp.argsort(flat_e)
    sorted_e = flat_e[order]
    counts = jnp.bincount(flat_e, length=N_EXPERTS)
    padded = (counts + MOE_BLOCK - 1) // MOE_BLOCK * MOE_BLOCK
    pad_end = jnp.cumsum(padded)
    pad_start = pad_end - padded
    grp_start = jnp.cumsum(counts) - counts
    dest = pad_start[sorted_e] + jnp.arange(A, dtype=jnp.int32) - grp_start[sorted_e]
    row_tok = jnp.full((P,), T, jnp.int32).at[dest].set((order // TOP_K).astype(jnp.int32))
    row_gate = jnp.zeros((P,), xt.dtype).at[dest].set(gates.reshape(A)[order].astype(xt.dtype))
    block_expert = jnp.minimum(jnp.searchsorted(pad_end, jnp.arange(nb, dtype=jnp.int32) * MOE_BLOCK, side='right'),
                               N_EXPERTS - 1)
    x_pad = jnp.concatenate([xt, jnp.zeros((1, D), xt.dtype)], axis=0)

    def block(args):
        rows, e = args
        xb = x_pad[rows]
        hb = jax.nn.silu(xb @ w_gate[e]) * (xb @ w_up[e])
        return hb @ w_down[e]

    out = lax.map(block, (row_tok.reshape(nb, MOE_BLOCK), block_expert)).reshape(P, D)
    return jax.ops.segment_sum(out * row_gate[:, None], row_tok, num_segments=T + 1)[:T]


def swiglu(x, w_g, w_u, w_d):
    return (jax.nn.silu(x @ w_g) * (x @ w_u)) @ w_d


def setup_inputs(seed: int = 0) -> dict:
    key = jax.random.key(seed)
    ks = jax.random.split(key, 40)
    C = HYENA_WIDTH

    def nrm(k, shape, scale):
        return jax.random.normal(k, shape, jnp.float32) * scale

    x = nrm(ks[0], (BATCH, SEQ, D_MODEL), 1.0)
    offsets = jax.random.randint(ks[1], (BATCH, 1), 0, SEQ, dtype=jnp.int32)
    positions = offsets + jnp.arange(SEQ, dtype=jnp.int32)[None, :]
    return {
        'x': x,
        'positions': positions,
        'emb_ln_g': 1.0 + nrm(ks[2], (D_MODEL,), 0.02),
        'emb_ln_b': nrm(ks[3], (D_MODEL,), 0.02),
        'w_in': nrm(ks[4], (DEPTH, D_MODEL, IN_COLS), D_MODEL ** -0.5),
        'hy_conv_w': nrm(ks[5], (DEPTH, SHORT_CONV, 3 * C), SHORT_CONV ** -0.5),
        'hy_conv_b': nrm(ks[6], (DEPTH, 3 * C), 0.02),
        'hy_f_w1': nrm(ks[7], (DEPTH, FILTER_EMB, FILTER_ORDER), FILTER_EMB ** -0.5),
        'hy_f_b1': nrm(ks[8], (DEPTH, FILTER_ORDER), 0.1),
        'hy_f_w2': nrm(ks[9], (DEPTH, FILTER_ORDER, FILTER_ORDER), FILTER_ORDER ** -0.5),
        'hy_f_b2': nrm(ks[10], (DEPTH, FILTER_ORDER), 0.1),
        'hy_f_w3': nrm(ks[11], (DEPTH, FILTER_ORDER, FILTER_ORDER), FILTER_ORDER ** -0.5),
        'hy_f_b3': nrm(ks[12], (DEPTH, FILTER_ORDER), 0.1),
        'hy_f_freq': 1.0 + nrm(ks[13], (DEPTH, 3, FILTER_ORDER), 0.01),
        'hy_f_wout': nrm(ks[14], (DEPTH, FILTER_ORDER, 2 * C), FILTER_ORDER ** -0.5),
        'hy_d': nrm(ks[15], (DEPTH, C), 0.1),
        'lambda_q1': nrm(ks[16], (DEPTH, DA_HEAD_DIM), 0.1),
        'lambda_k1': nrm(ks[17], (DEPTH, DA_HEAD_DIM), 0.1),
        'lambda_q2': nrm(ks[18], (DEPTH, DA_HEAD_DIM), 0.1),
        'lambda_k2': nrm(ks[19], (DEPTH, DA_HEAD_DIM), 0.1),
        'subln_g': 1.0 + nrm(ks[20], (DEPTH, DA_V_DIM), 0.02),
        'w_o': nrm(ks[21], (DEPTH, D_MODEL, D_MODEL), BETA * D_MODEL ** -0.5),
        'ln1_g': 1.0 + nrm(ks[22], (DEPTH, D_MODEL), 0.02),
        'ln1_b': nrm(ks[23], (DEPTH, D_MODEL), 0.02),
        'w_router': nrm(ks[24], (DEPTH, D_MODEL, N_EXPERTS), D_MODEL ** -0.5),
        'router_bias': nrm(ks[25], (DEPTH, N_EXPERTS), 0.01),
        'w_gate': nrm(ks[26], (DEPTH, N_EXPERTS, D_MODEL, EXPERT_DIM), D_MODEL ** -0.5),
        'w_up': nrm(ks[27], (DEPTH, N_EXPERTS, D_MODEL, EXPERT_DIM), D_MODEL ** -0.5),
        'w_down': nrm(ks[28], (DEPTH, N_EXPERTS, EXPERT_DIM, D_MODEL), BETA * EXPERT_DIM ** -0.5),
        'ws_gate': nrm(ks[29], (DEPTH, D_MODEL, EXPERT_DIM), D_MODEL ** -0.5),
        'ws_up': nrm(ks[30], (DEPTH, D_MODEL, EXPERT_DIM), D_MODEL ** -0.5),
        'ws_down': nrm(ks[31], (DEPTH, EXPERT_DIM, D_MODEL), BETA * EXPERT_DIM ** -0.5),
        'ln2_g': 1.0 + nrm(ks[32], (DEPTH, D_MODEL), 0.02),
        'ln2_b': nrm(ks[33], (DEPTH, D_MODEL), 0.02),
    }


def reference(x, positions, emb_ln_g, emb_ln_b, w_in, hy_conv_w, hy_conv_b, hy_f_w1, hy_f_b1,
              hy_f_w2, hy_f_b2, hy_f_w3, hy_f_b3, hy_f_freq, hy_f_wout, hy_d, lambda_q1, lambda_k1,
              lambda_q2, lambda_k2, subln_g, w_o, ln1_g, ln1_b, w_router, router_bias, w_gate, w_up,
              w_down, ws_gate, ws_up, ws_down, ln2_g, ln2_b):
    B, L, D = x.shape
    cos, sin = rotary_tables(positions)
    cos = cos[:, :, None, None, :].astype(x.dtype)
    sin = sin[:, :, None, None, :].astype(x.dtype)
    h = layer_norm(x, emb_ln_g, emb_ln_b)
    for i in range(DEPTH):
        proj = h @ w_in[i]
        q = proj[..., :ATTN_WIDTH].reshape(B, L, DA_HEADS, 2, DA_HEAD_DIM)
        k = proj[..., ATTN_WIDTH:2 * ATTN_WIDTH].reshape(B, L, DA_HEADS, 2, DA_HEAD_DIM)
        v = proj[..., 2 * ATTN_WIDTH:3 * ATTN_WIDTH].reshape(B, L, DA_HEADS, DA_V_DIM)
        u = proj[..., 3 * ATTN_WIDTH:]
        q = apply_partial_rotary(q, cos, sin)
        k = apply_partial_rotary(k, cos, sin)
        lam_init = 0.8 - 0.6 * math.exp(-0.3 * i)
        lam = (jnp.exp(jnp.sum(_f32(lambda_q1[i]) * _f32(lambda_k1[i])))
               - jnp.exp(jnp.sum(_f32(lambda_q2[i]) * _f32(lambda_k2[i]))) + lam_init)
        attn_out = differential_attention(q, k, v, lam, lam_init, subln_g[i])
        hy_out = hyena_mixer(u, hy_conv_w[i], hy_conv_b[i], hy_f_w1[i], hy_f_b1[i], hy_f_w2[i],
                             hy_f_b2[i], hy_f_w3[i], hy_f_b3[i], hy_f_freq[i], hy_f_wout[i], hy_d[i])
        mixed = jnp.concatenate([attn_out, hy_out], axis=-1) @ w_o[i]
        h = layer_norm(ALPHA * h + mixed, ln1_g[i], ln1_b[i])
        ht = h.reshape(B * L, D)
        gates, idx = route(ht, w_router[i], router_bias[i])
        ffn = swiglu(ht, ws_gate[i], ws_up[i], ws_down[i]) + routed_experts(ht, gates, idx, w_gate[i], w_up[i], w_down[i])
        h = layer_norm(ALPHA * h + ffn.reshape(B, L, D), ln2_g[i], ln2_b[i])
    return h
```

```python
import functools
import math

import numpy as np
import jax
import jax.numpy as jnp
from jax import lax
from jax.experimental import pallas as pl
from jax.experimental.pallas import tpu as pltpu

DA_HEADS = 4
DA_HEAD_DIM = 64
DA_V_DIM = 128
ATTN_WIDTH = 512
HYENA_WIDTH = 512
ROT_DIM = 16
ROPE_THETA = 500000.0
SHORT_CONV = 3
FILTER_EMB = 33
FILTER_BANDS = 16
DECAY_TARGET = 1e-2
FAST_DECAY = 0.3
SLOW_DECAY = 1.5
N_EXPERTS = 256
TOP_K = 8
N_GROUPS = 8
GROUP_SIZE = N_EXPERTS // N_GROUPS
TOPK_GROUPS = 4
EXPERT_DIM = 256
ROUTED_SCALE = 2.5
DEPTH = 1
ALPHA = (2 * DEPTH) ** 0.25
LN_EPS = 1e-5
LAM_INIT = 0.8 - 0.6 * math.exp(-0.3 * 0)

V7X_LANES = 128
V7X_SUBLANES = 8
V7X_VMEM_BYTES = 64 * 1024 * 1024

BF16 = jnp.bfloat16
F32 = jnp.float32


def _cparams(sem, vmem_mb):
    return pltpu.CompilerParams(dimension_semantics=sem, vmem_limit_bytes=vmem_mb * 1024 * 1024)


def _layer_norm_rows(x, g, b):
    mu = jnp.mean(x, axis=-1, keepdims=True)
    xc = x - mu
    var = jnp.mean(xc * xc, axis=-1, keepdims=True)
    return xc * lax.rsqrt(var + LN_EPS) * g + b


def _inproj_kernel(x_ref, g_ref, b_ref, w_ref, ra_ref, rm_ref, rp_ref, q_ref, k_ref, v_ref, u_ref):
    h = _layer_norm_rows(x_ref[...], g_ref[...], b_ref[...]).astype(BF16)
    ra, rm, rp = ra_ref[...], rm_ref[...], rp_ref[...]

    def rot(t):
        return t * ra + pltpu.roll(t, V7X_LANES - ROT_DIM // 2, axis=1) * rm + pltpu.roll(t, ROT_DIM // 2, axis=1) * rp

    aw = ATTN_WIDTH
    qp = jnp.dot(h, w_ref[:, 0:aw], preferred_element_type=F32)
    kp = jnp.dot(h, w_ref[:, aw:2 * aw], preferred_element_type=F32)
    scale = DA_HEAD_DIM ** -0.5
    for c in range(aw // V7X_LANES):
        sl = slice(c * V7X_LANES, (c + 1) * V7X_LANES)
        q_ref[:, sl] = (rot(qp[:, sl]) * scale).astype(BF16)
        k_ref[:, sl] = rot(kp[:, sl]).astype(BF16)
    v_ref[...] = jnp.dot(h, w_ref[:, 2 * aw:3 * aw], preferred_element_type=F32).astype(BF16)
    u_ref[...] = jnp.dot(h, w_ref[:, 3 * aw:], preferred_element_type=F32)


def _inproj(x2, g, b, w_bf, ra, rm, rp, tm=256):
    T, D = x2.shape
    ncol = w_bf.shape[1]
    aw = ATTN_WIDTH
    uw = ncol - 3 * aw
    row = lambda i: (i, 0)
    fixed = lambda i: (0, 0)
    return pl.pallas_call(
        _inproj_kernel,
        grid=(T // tm,),
        in_specs=[
            pl.BlockSpec((tm, D), row),
            pl.BlockSpec((1, D), fixed),
            pl.BlockSpec((1, D), fixed),
            pl.BlockSpec((D, ncol), fixed),
            pl.BlockSpec((tm, V7X_LANES), row),
            pl.BlockSpec((tm, V7X_LANES), row),
            pl.BlockSpec((tm, V7X_LANES), row),
        ],
        out_specs=[
            pl.BlockSpec((tm, aw), row),
            pl.BlockSpec((tm, aw), row),
            pl.BlockSpec((tm, aw), row),
            pl.BlockSpec((tm, uw), row),
        ],
        out_shape=[
            jax.ShapeDtypeStruct((T, aw), BF16),
            jax.ShapeDtypeStruct((T, aw), BF16),
            jax.ShapeDtypeStruct((T, aw), BF16),
            jax.ShapeDtypeStruct((T, uw), F32),
        ],
        compiler_params=_cparams(("parallel",), 48),
        name="inproj",
    )(x2, g, b, w_bf, ra, rm, rp)


def _rotary_tables(positions):
    half = ROT_DIM // 2
    inv_freq = ROPE_THETA ** (-jnp.arange(0, ROT_DIM, 2, dtype=F32) / ROT_DIM)
    ang = positions.astype(F32).reshape(-1)[:, None] * inv_freq
    cos, sin = jnp.cos(ang), jnp.sin(ang)
    T = ang.shape[0]
    ones = jnp.ones((T, DA_HEAD_DIM - ROT_DIM), F32)
    zeros_h = jnp.zeros((T, half), F32)
    zeros_r = jnp.zeros((T, DA_HEAD_DIM - ROT_DIM), F32)
    a64 = jnp.concatenate([cos, cos, ones], axis=1)
    m64 = jnp.concatenate([-sin, zeros_h, zeros_r], axis=1)
    p64 = jnp.concatenate([zeros_h, sin, zeros_r], axis=1)
    rep = V7X_LANES // DA_HEAD_DIM
    return jnp.tile(a64, (1, rep)), jnp.tile(m64, (1, rep)), jnp.tile(p64, (1, rep))


def _attn_kernel(lam_ref, q_ref, k_ref, v_ref, g_ref, o_ref, s_ref, *, kb):
    qb = q_ref.shape[0]
    L = k_ref.shape[0]
    nchunk = L // kb
    q = q_ref[...]
    lane = lax.broadcasted_iota(jnp.int32, q.shape, 1)
    zero = jnp.zeros_like(q)
    outs = []
    for c in range(2):
        qm = jnp.where((lane >= c * DA_HEAD_DIM) & (lane < (c + 1) * DA_HEAD_DIM), q, zero)

        def score_body(j, m):
            kc = k_ref[pl.ds(pl.multiple_of(j * kb, kb), kb), :]
            s = lax.dot_general(qm, kc, (((1,), (1,)), ((), ())), preferred_element_type=F32)
            s_ref[j] = s
            for t in range(kb // V7X_LANES):
                m = jnp.maximum(m, s[:, t * V7X_LANES:(t + 1) * V7X_LANES])
            return m

        m = lax.fori_loop(0, nchunk, score_body, jnp.full((qb, V7X_LANES), -jnp.inf, F32))
        m_row = jnp.max(m, axis=1, keepdims=True)

        def pv_body(j, carry):
            l, acc = carry
            p = jnp.exp(s_ref[j] - m_row)
            for t in range(kb // V7X_LANES):
                l = l + p[:, t * V7X_LANES:(t + 1) * V7X_LANES]
            vc = v_ref[pl.ds(pl.multiple_of(j * kb, kb), kb), :]
            acc = acc + jnp.dot(p.astype(BF16), vc, preferred_element_type=F32)
            return l, acc

        l, acc = lax.fori_loop(0, nchunk, pv_body,
                               (jnp.zeros((qb, V7X_LANES), F32), jnp.zeros((qb, DA_V_DIM), F32)))
        outs.append(acc / jnp.sum(l, axis=1, keepdims=True))
    o = outs[0] - lam_ref[0] * outs[1]
    ms = jnp.mean(o * o, axis=1, keepdims=True)
    o_ref[...] = (o * lax.rsqrt(ms + LN_EPS) * g_ref[...] * (1.0 - LAM_INIT)).astype(o_ref.dtype)


def _attention(q, k, v, lam, subln_g, B, L, qb=256, kb=512):
    T = q.shape[0]
    nq = L // qb
    return pl.pallas_call(
        functools.partial(_attn_kernel, kb=kb),
        grid=(B, DA_HEADS, nq),
        in_specs=[
            pl.BlockSpec(memory_space=pltpu.SMEM),
            pl.BlockSpec((qb, DA_V_DIM), lambda b, h, i: (b * nq + i, h)),
            pl.BlockSpec((L, DA_V_DIM), lambda b, h, i: (b, h)),
            pl.BlockSpec((L, DA_V_DIM), lambda b, h, i: (b, h)),
            pl.BlockSpec((1, DA_V_DIM), lambda b, h, i: (0, 0)),
        ],
        out_specs=pl.BlockSpec((qb, DA_V_DIM), lambda b, h, i: (b * nq + i, h)),
        out_shape=jax.ShapeDtypeStruct((T, ATTN_WIDTH), BF16),
        scratch_shapes=[pltpu.VMEM((L // kb, qb, kb), F32)],
        compiler_params=_cparams(("parallel", "parallel", "parallel"), 48),
        name="attn",
    )(lam, q, k, v, subln_g)


def _hpre_kernel(u_ref, up_ref, un_ref, w_ref, b_ref, x0_ref, vx_ref):
    i = pl.program_id(1)
    n = pl.num_programs(1)
    tb = u_ref.shape[0]
    C = x0_ref.shape[1]
    row = lax.broadcasted_iota(jnp.int32, (tb, V7X_LANES), 0)
    has_prev = (i > 0).astype(F32)
    has_next = (i < n - 1).astype(F32)

    def conv(c0):
        sl = slice(c0, c0 + V7X_LANES)
        u = u_ref[:, sl]
        prev_row = up_ref[V7X_SUBLANES - 1:V7X_SUBLANES, sl] * has_prev
        next_row = un_ref[0:1, sl] * has_next
        u_prev = jnp.where(row == 0, prev_row, pltpu.roll(u, 1, axis=0))
        u_next = jnp.where(row == tb - 1, next_row, pltpu.roll(u, tb - 1, axis=0))
        return u_prev * w_ref[0:1, sl] + u * w_ref[1:2, sl] + u_next * w_ref[2:3, sl] + b_ref[:, sl]

    for c in range(C // V7X_LANES):
        c0 = c * V7X_LANES
        x0_ref[:, c0:c0 + V7X_LANES] = conv(c0)
        vx_ref[:, c0:c0 + V7X_LANES] = conv(2 * C + c0) * conv(C + c0)


def _hpre(u, conv_w, conv_b, B, L, tb=512):
    T, C3 = u.shape
    C = C3 // 3
    nt = L // tb
    sub = V7X_SUBLANES
    cur = lambda b, i: (b * nt + i, 0)
    prev = lambda b, i: (jnp.maximum((b * L + i * tb) // sub - 1, 0), 0)
    nxt = lambda b, i: (jnp.minimum((b * L + (i + 1) * tb) // sub, T // sub - 1), 0)
    fixed = lambda b, i: (0, 0)
    return pl.pallas_call(
        _hpre_kernel,
        grid=(B, nt),
        in_specs=[
            pl.BlockSpec((tb, C3), cur),
            pl.BlockSpec((sub, C3), prev),
            pl.BlockSpec((sub, C3), nxt),
            pl.BlockSpec((SHORT_CONV, C3), fixed),
            pl.BlockSpec((1, C3), fixed),
        ],
        out_specs=[pl.BlockSpec((tb, C), cur), pl.BlockSpec((tb, C), cur)],
        out_shape=[jax.ShapeDtypeStruct((T, C), F32), jax.ShapeDtypeStruct((T, C), F32)],
        compiler_params=_cparams(("parallel", "parallel"), 32),
        name="hpre",
    )(u, u, u, conv_w, conv_b)


def _filt_kernel(z_ref, w1_ref, b1_ref, w2_ref, b2_ref, w3_ref, b3_ref, fr_ref, wo_ref, dl_ref,
                 hfb_ref, asum_ref, *, L):
    i = pl.program_id(0)
    tl = z_ref.shape[0]
    C = dl_ref.shape[1]
    hp = lax.Precision.HIGHEST
    h = jnp.sin(fr_ref[0:1, :] * (jnp.dot(z_ref[...], w1_ref[...], precision=hp, preferred_element_type=F32) + b1_ref[...]))
    h = jnp.sin(fr_ref[1:2, :] * (jnp.dot(h, w2_ref[...], precision=hp, preferred_element_type=F32) + b2_ref[...]))
    h = jnp.sin(fr_ref[2:3, :] * (jnp.dot(h, w3_ref[...], precision=hp, preferred_element_type=F32) + b3_ref[...]))
    o = jnp.dot(h, wo_ref[...], precision=hp, preferred_element_type=F32)
    grow = lax.broadcasted_iota(jnp.int32, (tl, C), 0) + i * tl
    t = grow.astype(F32) * (1.0 / (L - 1))
    decay = jnp.exp(-t * dl_ref[...])
    hf = o[:, :C] * decay
    hb = jnp.where(grow == 0, 0.0, o[:, C:] * decay)
    hfb_ref[:, :C] = hf
    hfb_ref[:, C:] = hb

    @pl.when(i == 0)
    def _():
        asum_ref[...] = jnp.zeros_like(asum_ref)

    asum_ref[...] += jnp.sum(jnp.abs(hf) + jnp.abs(hb), axis=0, keepdims=True)


def _filter_taps(L, w1, b1, w2, b2, w3, b3, freq, wout, tl=512):
    C = wout.shape[1] // 2
    order = w1.shape[1]
    emb = w1.shape[0]
    t = jnp.linspace(0.0, 1.0, L, dtype=F32)[:, None]
    w = 2.0 * math.pi * jnp.arange(L, dtype=F32)[:, None] / L
    f = jnp.linspace(1e-4, FILTER_BANDS - 1, FILTER_BANDS, dtype=F32)[None, :]
    z = jnp.concatenate([t, jnp.cos(f * w), -jnp.sin(f * w)], axis=-1)
    zp = jnp.pad(z, ((0, 0), (0, V7X_LANES - emb)))
    w1p = jnp.pad(w1, ((0, V7X_LANES - emb), (0, 0)))
    deltas = jnp.abs(jnp.linspace(math.log(DECAY_TARGET) / SLOW_DECAY, math.log(DECAY_TARGET) / FAST_DECAY, C, dtype=F32))[None]
    fixed = lambda i: (0, 0)
    return pl.pallas_call(
        functools.partial(_filt_kernel, L=L),
        grid=(L // tl,),
        in_specs=[
            pl.BlockSpec((tl, V7X_LANES), lambda i: (i, 0)),
            pl.BlockSpec((V7X_LANES, order), fixed), pl.BlockSpec((1, order), fixed),
            pl.BlockSpec((order, order), fixed), pl.BlockSpec((1, order), fixed),
            pl.BlockSpec((order, order), fixed), pl.BlockSpec((1, order), fixed),
            pl.BlockSpec((3, order), fixed),
            pl.BlockSpec((order, 2 * C), fixed),
            pl.BlockSpec((1, C), fixed),
        ],
        out_specs=[pl.BlockSpec((tl, 2 * C), lambda i: (i, 0)), pl.BlockSpec((1, C), fixed)],
        out_shape=[jax.ShapeDtypeStruct((L, 2 * C), F32), jax.ShapeDtypeStruct((1, C), F32)],
        compiler_params=_cparams(("arbitrary",), 32),
        name="filt",
    )(zp, w1p, b1[None], w2, b2[None], w3, b3[None], freq, wout, deltas)


DFT_N2 = 128
DFT_N2H = DFT_N2 // V7X_SUBLANES


def _dft_constants(L):
    n1 = 2 * L // DFT_N2
    n1h = n1 // 2
    N = 2 * L
    sub = V7X_SUBLANES
    two_pi = 2.0 * math.pi

    def cs(num, den):
        ang = (num % den).astype(F32) * (two_pi / den)
        return jnp.cos(ang), jnp.sin(ang)

    k1 = jnp.arange(n1, dtype=jnp.int32)
    a = jnp.arange(n1h, dtype=jnp.int32)
    c, s = cs(k1[:, None] * a[None, :], n1)
    eye = jnp.eye(sub, dtype=F32)

    def expand(m):
        r, kk = m.shape
        return (m[:, None, :, None] * eye[None, :, None, :]).reshape(r * sub, kk * sub)

    m1c = jnp.stack([jnp.concatenate([c, s], axis=1), jnp.concatenate([-s, c], axis=1)], axis=1).reshape(2 * n1, 2 * n1h)
    m1r = jnp.stack([c, -s], axis=1).reshape(2 * n1, n1h)
    ct, st = c.T, s.T
    m3 = jnp.stack([jnp.stack([ct, -st], axis=2).reshape(n1h, 2 * n1),
                    jnp.stack([st, ct], axis=2).reshape(n1h, 2 * n1)], axis=0).reshape(2 * n1h, 2 * n1)
    n2 = jnp.arange(DFT_N2, dtype=jnp.int32)
    gc, gs = cs(n2[:, None] * n2[None, :], DFT_N2)
    g_fwd = jnp.concatenate([jnp.concatenate([gc, gs], axis=1), jnp.concatenate([-gs, gc], axis=1)], axis=0)
    g_inv = jnp.concatenate([jnp.concatenate([gc, -gs], axis=1), jnp.concatenate([gs, gc], axis=1)], axis=0)
    tc, ts = cs(k1[:, None] * n2[None, :], N)
    lanes = (n1, DFT_N2H, sub, V7X_LANES)
    twc = jnp.broadcast_to(tc.reshape(n1, DFT_N2H, sub, 1), lanes)
    tws = jnp.broadcast_to(ts.reshape(n1, DFT_N2H, sub, 1), lanes)
    return dict(n1=n1, n1h=n1h, m1c=expand(m1c).astype(BF16), m1r=expand(m1r).astype(BF16),
                m3=expand(m3).astype(BF16), g_fwd=g_fwd.astype(BF16), g_inv=g_inv.astype(BF16), twc=twc, tws=tws)


def _dft1_kernel(x_ref, m_ref, o_ref):
    rows = m_ref.shape[1]
    xs = x_ref[...].reshape(rows, x_ref.shape[-1]).astype(BF16)
    o_ref[...] = jnp.dot(m_ref[...], xs, preferred_element_type=F32).reshape(o_ref.shape)


def _dft1(x5, m, n1, cb):
    P, n1h, _, sub, Cx = x5.shape
    return pl.pallas_call(
        _dft1_kernel,
        grid=(DFT_N2H, Cx // cb),
        in_specs=[
            pl.BlockSpec((P, n1h, None, sub, cb), lambda h, c: (0, 0, h, 0, c)),
            pl.BlockSpec(m.shape, lambda h, c: (0, 0)),
        ],
        out_specs=pl.BlockSpec((None, n1, 2, sub, cb), lambda h, c: (h, 0, 0, 0, c)),
        out_shape=jax.ShapeDtypeStruct((DFT_N2H, n1, 2, sub, Cx), F32),
        compiler_params=_cparams(("parallel", "parallel"), 48),
        name="dft1",
    )(x5, m)


def _dft2_kernel(a_ref, f_ref, twc_ref, tws_ref, gf_ref, gi_ref, o_ref):
    C = a_ref.shape[-1]
    half = DFT_N2
    twc, tws = twc_ref[...], tws_ref[...]

    def lanes(fn):
        return jnp.concatenate([fn(slice(c0, c0 + V7X_LANES)) for c0 in range(0, C, V7X_LANES)], axis=-1)

    def spectrum(ref, col0):
        re = lanes(lambda sl: ref[:, 0, :, col0 + sl.start:col0 + sl.stop] * twc + ref[:, 1, :, col0 + sl.start:col0 + sl.stop] * tws)
        im = lanes(lambda sl: ref[:, 1, :, col0 + sl.start:col0 + sl.stop] * twc - ref[:, 0, :, col0 + sl.start:col0 + sl.stop] * tws)
        t = jnp.concatenate([re.reshape(half, C), im.reshape(half, C)], axis=0).astype(BF16)
        s = jnp.dot(gf_ref[...], t, preferred_element_type=F32)
        return s[:half], s[half:]

    xr, xi = spectrum(a_ref, 0)
    fr, fi = spectrum(f_ref, 0)
    br, bi = spectrum(f_ref, C)
    hr, hi = fr + br, fi - bi
    y = jnp.concatenate([xr * hr - xi * hi, xr * hi + xi * hr], axis=0).astype(BF16)
    b = jnp.dot(gi_ref[...], y, preferred_element_type=F32)
    br2 = b[:half].reshape(DFT_N2H, V7X_SUBLANES, C)
    bi2 = b[half:].reshape(DFT_N2H, V7X_SUBLANES, C)
    for c0 in range(0, C, V7X_LANES):
        sl = slice(c0, c0 + V7X_LANES)
        o_ref[:, 0, :, sl] = br2[:, :, sl] * twc - bi2[:, :, sl] * tws
        o_ref[:, 1, :, sl] = bi2[:, :, sl] * twc + br2[:, :, sl] * tws


def _dft2(a5, f5, k):
    n2h, n1, _, sub, C = a5.shape
    return pl.pallas_call(
        _dft2_kernel,
        grid=(n1,),
        in_specs=[
            pl.BlockSpec((n2h, None, 2, sub, C), lambda i: (0, i, 0, 0, 0)),
            pl.BlockSpec((n2h, None, 2, sub, 2 * C), lambda i: (0, i, 0, 0, 0)),
            pl.BlockSpec((None, n2h, sub, V7X_LANES), lambda i: (i, 0, 0, 0)),
            pl.BlockSpec((None, n2h, sub, V7X_LANES), lambda i: (i, 0, 0, 0)),
            pl.BlockSpec((2 * DFT_N2, 2 * DFT_N2), lambda i: (0, 0)),
            pl.BlockSpec((2 * DFT_N2, 2 * DFT_N2), lambda i: (0, 0)),
        ],
        out_specs=pl.BlockSpec((n2h, None, 2, sub, C), lambda i: (0, i, 0, 0, 0)),
        out_shape=jax.ShapeDtypeStruct(a5.shape, F32),
        compiler_params=_cparams(("parallel",), 32),
        name="dft2",
    )(a5, f5, k["twc"], k["tws"], k["g_fwd"], k["g_inv"])


def _dft3_kernel(b_ref, m_ref, x0_ref, vx_ref, sc_ref, d_ref, o_ref):
    C = b_ref.shape[-1]
    bs = b_ref[...].reshape(m_ref.shape[1], C).astype(BF16)
    y = jnp.dot(m_ref[...], bs, preferred_element_type=F32).reshape(o_ref.shape)
    o_ref[...] = x0_ref[...] * (y * sc_ref[...] + vx_ref[...] * d_ref[...])


def _dft3(b5, m3, x05, vx5, scale, d):
    n2h, n1, _, sub, C = b5.shape
    Bt, n1h = x05.shape[0], x05.shape[1]
    tok = pl.BlockSpec((Bt, n1h, None, sub, C), lambda h: (0, 0, h, 0, 0))
    vec = pl.BlockSpec((1, C), lambda h: (0, 0))
    return pl.pallas_call(
        _dft3_kernel,
        grid=(n2h,),
        in_specs=[
            pl.BlockSpec((None, n1, 2, sub, C), lambda h: (h, 0, 0, 0, 0)),
            pl.BlockSpec(m3.shape, lambda h: (0, 0)),
            tok, tok, vec, vec,
        ],
        out_specs=tok,
        out_shape=jax.ShapeDtypeStruct(x05.shape, F32),
        compiler_params=_cparams(("parallel",), 48),
        name="dft3",
    )(b5, m3, x05, vx5, scale, d)


def _hyena(u, conv_w, conv_b, w1, b1, w2, b2, w3, b3, freq, wout, d_skip, B, L):
    T = u.shape[0]
    C = HYENA_WIDTH
    k = _dft_constants(L)
    n1, n1h = k["n1"], k["n1h"]
    x0, vx = _hpre(u, conv_w, conv_b[None], B, L)
    hfb, asum = _filter_taps(L, w1, b1, w2, b2, w3, b3, freq, wout)
    view = (B, n1h, DFT_N2H, V7X_SUBLANES, C)
    a5 = _dft1(vx.reshape(view), k["m1c"], n1, cb=C)
    f5 = _dft1(hfb.reshape(1, n1h, DFT_N2H, V7X_SUBLANES, 2 * C), k["m1r"], n1, cb=C)
    b5 = _dft2(a5, f5, k)
    scale = 1.0 / (asum * (2 * L))
    hy = _dft3(b5, k["m3"], x0.reshape(view), vx.reshape(view), scale, d_skip[None])
    return hy.reshape(T, C)


def _bf16_bits(x):
    a = pltpu.bitcast(x, jnp.uint32)
    return (a + jnp.uint32(0x7FFF) + ((a >> 16) & jnp.uint32(1))) >> 16


def _pack_pairs(x):
    half = x.shape[1] // 2
    return _bf16_bits(x[:, :half]) | (_bf16_bits(x[:, half:]) << 16)


def _unpack_pairs(w):
    lo = pltpu.bitcast(w << 16, F32)
    hi = pltpu.bitcast(w & jnp.uint32(0xFFFF0000), F32)
    return lo, hi


def _oproj_kernel(x_ref, g0_ref, b0_ref, at_ref, hy_ref, wo_ref, g1_ref, b1_ref, wrh_ref, wrl_ref,
                  h1_ref, h1p_ref, sc_ref):
    aw = at_ref.shape[1]
    h0 = _layer_norm_rows(x_ref[...], g0_ref[...], b0_ref[...])
    mixed = jnp.dot(at_ref[...], wo_ref[0:aw, :], preferred_element_type=F32)
    mixed = mixed + jnp.dot(hy_ref[...].astype(BF16), wo_ref[aw:, :], preferred_element_type=F32)
    h1 = _layer_norm_rows(ALPHA * h0 + mixed, g1_ref[...], b1_ref[...])
    h1_ref[...] = h1
    h1p_ref[...] = _pack_pairs(h1)
    hh = h1.astype(BF16)
    hl = (h1 - hh.astype(F32)).astype(BF16)
    dn = (((1,), (1,)), ((), ()))
    logits = lax.dot_general(wrh_ref[...], hh, dn, preferred_element_type=F32)
    logits = logits + (lax.dot_general(wrh_ref[...], hl, dn, preferred_element_type=F32)
                       + lax.dot_general(wrl_ref[...], hh, dn, preferred_element_type=F32))
    sc_ref[...] = 1.0 / (1.0 + jnp.exp(-logits))


def _oproj(x2, g0, b0, attn, hy, wo_bf, g1, b1, wr_t, tm=256):
    T, D = x2.shape
    E = wr_t.shape[0]
    wrh = wr_t.astype(BF16)
    wrl = (wr_t - wrh.astype(F32)).astype(BF16)
    row = lambda i: (i, 0)
    fixed = lambda i: (0, 0)
    return pl.pallas_call(
        _oproj_kernel,
        grid=(T // tm,),
        in_specs=[
            pl.BlockSpec((tm, D), row), pl.BlockSpec((1, D), fixed), pl.BlockSpec((1, D), fixed),
            pl.BlockSpec((tm, attn.shape[1]), row), pl.BlockSpec((tm, hy.shape[1]), row),
            pl.BlockSpec((D, D), fixed), pl.BlockSpec((1, D), fixed), pl.BlockSpec((1, D), fixed),
            pl.BlockSpec((E, D), fixed), pl.BlockSpec((E, D), fixed),
        ],
        out_specs=[pl.BlockSpec((tm, D), row), pl.BlockSpec((tm, D // 2), row), pl.BlockSpec((E, tm), lambda i: (0, i))],
        out_shape=[
            jax.ShapeDtypeStruct((T, D), F32),
            jax.ShapeDtypeStruct((T, D // 2), jnp.uint32),
            jax.ShapeDtypeStruct((E, T), F32),
        ],
        compiler_params=_cparams(("parallel",), 48),
        name="oproj",
    )(x2, g0, b0, attn, hy, wo_bf, g1, b1, wrh, wrl)


def _route_kernel(sc_ref, bias_ref, tri_ref, idx_ref, gate_ref, rank_ref, cnt_ref, carry_ref):
    step = pl.program_id(0)
    E, tm = sc_ref.shape
    neg = jnp.float32(-jnp.inf)
    scores = sc_ref[...]
    biased = scores + bias_ref[...]
    erow = lax.broadcasted_iota(jnp.int32, (E, tm), 0)
    big = jnp.int32(E)

    def first_argmax(vals, rows):
        m = jnp.max(vals, axis=0, keepdims=True)
        pick = jnp.min(jnp.where(vals == m, rows, big), axis=0, keepdims=True)
        return m, pick

    gsc = []
    for g in range(N_GROUPS):
        blk = biased[g * GROUP_SIZE:(g + 1) * GROUP_SIZE, :]
        rows = erow[g * GROUP_SIZE:(g + 1) * GROUP_SIZE, :]
        m1, p1 = first_argmax(blk, rows)
        m2 = jnp.max(jnp.where(rows == p1, neg, blk), axis=0, keepdims=True)
        gsc.append(m1 + m2)
    gsc = jnp.concatenate(gsc, axis=0)
    grow = lax.broadcasted_iota(jnp.int32, (N_GROUPS, tm), 0)
    gsel = jnp.zeros((N_GROUPS, tm), jnp.bool_)
    work = gsc
    for _ in range(TOPK_GROUPS):
        _, p = first_argmax(work, grow)
        hit = grow == p
        gsel = gsel | hit
        work = jnp.where(hit, neg, work)
    emask = jnp.concatenate(
        [jnp.broadcast_to(gsel[g:g + 1, :], (GROUP_SIZE, tm)) for g in range(N_GROUPS)], axis=0)
    work = jnp.where(emask, biased, neg)

    sel = jnp.zeros((E, tm), jnp.bool_)
    picks, gvals = [], []
    for _ in range(TOP_K):
        _, p = first_argmax(work, erow)
        hit = erow == p
        sel = sel | hit
        picks.append(p)
        gvals.append(jnp.sum(jnp.where(hit, scores, 0.0), axis=0, keepdims=True))
        work = jnp.where(hit, neg, work)
    gv = jnp.concatenate(gvals, axis=0)
    idx_ref[...] = jnp.concatenate(picks, axis=0)
    gate_ref[...] = gv / jnp.sum(gv, axis=0, keepdims=True) * ROUTED_SCALE

    @pl.when(step == 0)
    def _():
        carry_ref[...] = jnp.zeros_like(carry_ref)

    chosen = sel.astype(F32)
    before = jnp.dot(chosen.astype(BF16), tri_ref[...], preferred_element_type=F32) + carry_ref[...]
    rank_ref[...] = jnp.concatenate(
        [jnp.sum(jnp.where(erow == p, before, 0.0), axis=0, keepdims=True) for p in picks], axis=0).astype(jnp.int32)
    carry_ref[...] += jnp.sum(chosen, axis=1, keepdims=True)
    cnt_ref[...] = carry_ref[...].astype(jnp.int32)


def _route(scores_t, bias, tm=128):
    E, T = scores_t.shape
    tri = (jnp.arange(tm)[:, None] < jnp.arange(tm)[None, :]).astype(BF16)
    tok = lambda i: (0, i)
    fixed = lambda i: (0, 0)
    return pl.pallas_call(
        _route_kernel,
        grid=(T // tm,),
        in_specs=[pl.BlockSpec((E, tm), tok), pl.BlockSpec((E, 1), fixed), pl.BlockSpec((tm, tm), fixed)],
        out_specs=[pl.BlockSpec((TOP_K, tm), tok), pl.BlockSpec((TOP_K, tm), tok), pl.BlockSpec((TOP_K, tm), tok),
                   pl.BlockSpec((E, 1), fixed)],
        out_shape=[
            jax.ShapeDtypeStruct((TOP_K, T), jnp.int32),
            jax.ShapeDtypeStruct((TOP_K, T), F32),
            jax.ShapeDtypeStruct((TOP_K, T), jnp.int32),
            jax.ShapeDtypeStruct((E, 1), jnp.int32),
        ],
        scratch_shapes=[pltpu.VMEM((E, 1), F32)],
        compiler_params=_cparams(("arbitrary",), 32),
        name="route",
    )(scores_t, bias, tri)


def _scatter_kernel(dest_ref, h_ref, init_ref, xs_ref, sem):
    del init_ref
    tm = h_ref.shape[0]

    def copy(r, k):
        return pltpu.make_async_copy(h_ref.at[pl.ds(r, 1)], xs_ref.at[pl.ds(dest_ref[k, r], 1)], sem)

    def issue(r, c):
        for k in range(TOP_K):
            copy(r, k).start()
        return c

    def drain(r, c):
        for k in range(TOP_K):
            copy(r, k).wait()
        return c

    lax.fori_loop(0, tm, issue, 0)
    lax.fori_loop(0, tm, drain, 0)


def _scatter_rows(dest, h1p, n_rows, tm=256):
    T, W = h1p.shape
    zeros = jnp.zeros((n_rows, W), h1p.dtype)
    return pl.pallas_call(
        _scatter_kernel,
        grid=(T // tm,),
        in_specs=[
            pl.BlockSpec((TOP_K, tm), lambda i: (0, i), memory_space=pltpu.SMEM),
            pl.BlockSpec((tm, W), lambda i: (i, 0)),
            pl.BlockSpec(memory_space=pl.ANY),
        ],
        out_specs=pl.BlockSpec(memory_space=pl.ANY),
        out_shape=jax.ShapeDtypeStruct((n_rows, W), h1p.dtype),
        scratch_shapes=[pltpu.SemaphoreType.DMA],
        input_output_aliases={2: 0},
        compiler_params=_cparams(("arbitrary",), 32),
        name="scatter",
    )(dest, h1p, zeros)


def _experts_kernel(be_ref, nu_ref, x_ref, wg_ref, wu_ref, wd_ref, y_ref, wgb, wub, wdb):
    b = pl.program_id(0)
    e = be_ref[b]
    changed = jnp.logical_or(b == 0, e != be_ref[jnp.maximum(b - 1, 0)])

    @pl.when(jnp.logical_and(changed, b < nu_ref[0]))
    def _():
        wgb[...] = wg_ref[...].astype(BF16)
        wub[...] = wu_ref[...].astype(BF16)
        wdb[...] = wd_ref[...].astype(BF16)

    @pl.when(b < nu_ref[0])
    def _():
        lo, hi = _unpack_pairs(x_ref[...])
        half = lo.shape[1]
        xl, xh = lo.astype(BF16), hi.astype(BF16)
        g = jnp.dot(xl, wgb[0:half, :], preferred_element_type=F32) + jnp.dot(xh, wgb[half:, :], preferred_element_type=F32)
        u = jnp.dot(xl, wub[0:half, :], preferred_element_type=F32) + jnp.dot(xh, wub[half:, :], preferred_element_type=F32)
        hb = (g / (1.0 + jnp.exp(-g)) * u).astype(BF16)
        y_ref[...] = _pack_pairs(jnp.dot(hb, wdb[...], preferred_element_type=F32))

    @pl.when(b >= nu_ref[0])
    def _():
        y_ref[...] = jnp.zeros_like(y_ref)


def _experts(block_expert, n_used, xs, w_gate, w_up, w_down, rb):
    P, W = xs.shape
    E, D, F = w_gate.shape
    nb = P // rb
    live = lambda b, be, nu: (jnp.minimum(b, nu[0] - 1), 0)
    return pl.pallas_call(
        _experts_kernel,
        grid_spec=pltpu.PrefetchScalarGridSpec(
            num_scalar_prefetch=2,
            grid=(nb,),
            in_specs=[
                pl.BlockSpec((rb, W), live),
                pl.BlockSpec((None, D, F), lambda b, be, nu: (be[b], 0, 0)),
                pl.BlockSpec((None, D, F), lambda b, be, nu: (be[b], 0, 0)),
                pl.BlockSpec((None, F, D), lambda b, be, nu: (be[b], 0, 0)),
            ],
            out_specs=pl.BlockSpec((rb, W), lambda b, be, nu: (b, 0)),
            scratch_shapes=[pltpu.VMEM((D, F), BF16), pltpu.VMEM((D, F), BF16), pltpu.VMEM((F, D), BF16)],
        ),
        out_shape=jax.ShapeDtypeStruct((P, W), jnp.uint32),
        compiler_params=_cparams(("arbitrary",), 48),
        name="experts",
    )(block_expert, n_used, xs, w_gate, w_up, w_down)


def _combine_kernel(dest_ref, h1_ref, gate_ref, ys_ref, sg_ref, su_ref, sd_ref, g2_ref, b2_ref, o_ref, buf, sem):
    tm = h1_ref.shape[0]

    def copy(r, k):
        return pltpu.make_async_copy(ys_ref.at[pl.ds(dest_ref[k, r], 1)], buf.at[k, pl.ds(r, 1)], sem)

    def issue(r, c):
        for k in range(TOP_K):
            copy(r, k).start()
        return c

    def drain(r, c):
        for k in range(TOP_K):
            copy(r, k).wait()
        return c

    lax.fori_loop(0, tm, issue, 0)
    h1 = h1_ref[...]
    hb = h1.astype(BF16)
    g = jnp.dot(hb, sg_ref[...], preferred_element_type=F32)
    u = jnp.dot(hb, su_ref[...], preferred_element_type=F32)
    ffn = jnp.dot((g / (1.0 + jnp.exp(-g)) * u).astype(BF16), sd_ref[...], preferred_element_type=F32)
    lax.fori_loop(0, tm, drain, 0)
    gates = gate_ref[...]
    lo_acc = jnp.zeros((tm, buf.shape[2]), F32)
    hi_acc = jnp.zeros((tm, buf.shape[2]), F32)
    for k in range(TOP_K):
        lo, hi = _unpack_pairs(buf[k])
        gk = gates[:, k:k + 1]
        lo_acc = lo_acc + gk * lo
        hi_acc = hi_acc + gk * hi
    ffn = ffn + jnp.concatenate([lo_acc, hi_acc], axis=1)
    o_ref[...] = _layer_norm_rows(ALPHA * h1 + ffn, g2_ref[...], b2_ref[...])


def _combine(dest, h1, gates_tk, ys, sg, su, sd, g2, b2, tm=128):
    T, D = h1.shape
    W = ys.shape[1]
    F = sg.shape[1]
    row = lambda i: (i, 0)
    fixed = lambda i: (0, 0)
    return pl.pallas_call(
        _combine_kernel,
        grid=(T // tm,),
        in_specs=[
            pl.BlockSpec((TOP_K, tm), lambda i: (0, i), memory_space=pltpu.SMEM),
            pl.BlockSpec((tm, D), row),
            pl.BlockSpec((tm, TOP_K), row),
            pl.BlockSpec(memory_space=pl.ANY),
            pl.BlockSpec((D, F), fixed), pl.BlockSpec((D, F), fixed), pl.BlockSpec((F, D), fixed),
            pl.BlockSpec((1, D), fixed), pl.BlockSpec((1, D), fixed),
        ],
        out_specs=pl.BlockSpec((tm, D), row),
        out_shape=jax.ShapeDtypeStruct((T, D), F32),
        scratch_shapes=[pltpu.VMEM((TOP_K, tm, W), jnp.uint32), pltpu.SemaphoreType.DMA],
        compiler_params=_cparams(("arbitrary",), 48),
        name="combine",
    )(dest, h1, gates_tk, ys, sg, su, sd, g2, b2)


MOE_ROWS = 128


def _moe(h1, h1p, scores_t, router_bias, w_gate, w_up, w_down, ws_gate, ws_up, ws_down, g2, b2):
    T = h1.shape[0]
    E = N_EXPERTS
    rb = MOE_ROWS
    idx, gates, rank, counts = _route(scores_t, router_bias[:, None])
    counts = counts[:, 0]
    padded = (counts + rb - 1) // rb * rb
    pad_end = jnp.cumsum(padded)
    pad_start = pad_end - padded
    dest = pad_start[idx] + rank
    nb = (T * TOP_K) // rb + E
    block_expert = jnp.minimum(jnp.searchsorted(pad_end, jnp.arange(nb, dtype=jnp.int32) * rb, side="right"), E - 1).astype(jnp.int32)
    n_used = (pad_end[-1:] // rb).astype(jnp.int32)
    xs = _scatter_rows(dest, h1p, nb * rb)
    ys = _experts(block_expert, n_used, xs, w_gate, w_up, w_down, rb)
    return _combine(dest, h1, gates.T, ys, ws_gate.astype(BF16), ws_up.astype(BF16), ws_down.astype(BF16), g2, b2)


def kernel(x, positions, emb_ln_g, emb_ln_b, w_in, hy_conv_w, hy_conv_b, hy_f_w1, hy_f_b1, hy_f_w2, hy_f_b2, hy_f_w3, hy_f_b3, hy_f_freq, hy_f_wout, hy_d, lambda_q1, lambda_k1, lambda_q2, lambda_k2, subln_g, w_o, ln1_g, ln1_b, w_router, router_bias, w_gate, w_up, w_down, ws_gate, ws_up, ws_down, ln2_g, ln2_b):
    B, L, D = x.shape
    T = B * L
    assert w_in.shape[0] == DEPTH == 1
    i = 0
    x2 = x.reshape(T, D)
    g0, b0 = emb_ln_g[None], emb_ln_b[None]
    ra, rm, rp = _rotary_tables(positions)
    q, k, v, u = _inproj(x2, g0, b0, w_in[i].astype(BF16), ra, rm, rp)
    lam = (jnp.exp(jnp.sum(lambda_q1[i] * lambda_k1[i])) - jnp.exp(jnp.sum(lambda_q2[i] * lambda_k2[i])) + LAM_INIT)
    attn = _attention(q, k, v, lam.reshape(1).astype(F32), subln_g[i][None], B, L)
    hy = _hyena(u, hy_conv_w[i], hy_conv_b[i], hy_f_w1[i], hy_f_b1[i], hy_f_w2[i], hy_f_b2[i], hy_f_w3[i], hy_f_b3[i],
                hy_f_freq[i], hy_f_wout[i], hy_d[i], B, L)
    h1, h1p, scores_t = _oproj(x2, g0, b0, attn, hy, w_o[i].astype(BF16), ln1_g[i][None], ln1_b[i][None], w_router[i].T)
    out = _moe(h1, h1p, scores_t, router_bias[i], w_gate[i], w_up[i], w_down[i], ws_gate[i], ws_up[i], ws_down[i],
               ln2_g[i][None], ln2_b[i][None])
    return out.reshape(B, L, D)
```

```python
import functools
import math

import numpy as np
import jax
import jax.numpy as jnp
from jax import lax
from jax.experimental import pallas as pl
from jax.experimental.pallas import tpu as pltpu

DA_HEADS = 4
DA_HEAD_DIM = 64
DA_V_DIM = 128
ATTN_WIDTH = 512
HYENA_WIDTH = 512
ROT_DIM = 16
ROPE_THETA = 500000.0
SHORT_CONV = 3
FILTER_EMB = 33
FILTER_BANDS = 16
DECAY_TARGET = 1e-2
FAST_DECAY = 0.3
SLOW_DECAY = 1.5
N_EXPERTS = 256
TOP_K = 8
N_GROUPS = 8
GROUP_SIZE = N_EXPERTS // N_GROUPS
TOPK_GROUPS = 4
EXPERT_DIM = 256
ROUTED_SCALE = 2.5
DEPTH = 1
ALPHA = (2 * DEPTH) ** 0.25
LN_EPS = 1e-5
LAM_INIT = 0.8 - 0.6 * math.exp(-0.3 * 0)

V7X_LANES = 128
V7X_SUBLANES = 8
V7X_VMEM_BYTES = 64 * 1024 * 1024

DFT_N2 = 128
DFT_N2H = DFT_N2 // V7X_SUBLANES

BF16 = jnp.bfloat16
F32 = jnp.float32


def _cparams(sem, vmem_mb):
    return pltpu.CompilerParams(dimension_semantics=sem, vmem_limit_bytes=vmem_mb * 1024 * 1024)


def _layer_norm_rows(x, g, b):
    mu = jnp.mean(x, axis=-1, keepdims=True)
    xc = x - mu
    var = jnp.mean(xc * xc, axis=-1, keepdims=True)
    return xc * lax.rsqrt(var + LN_EPS) * g + b


def _inproj_kernel(x_ref, g_ref, b_ref, w_ref, ra_ref, rm_ref, rp_ref, q_ref, k_ref, v_ref, u_ref):
    h = _layer_norm_rows(x_ref[...], g_ref[...], b_ref[...]).astype(BF16)
    ra, rm, rp = ra_ref[...], rm_ref[...], rp_ref[...]

    def rot(t):
        return t * ra + pltpu.roll(t, V7X_LANES - ROT_DIM // 2, axis=1) * rm + pltpu.roll(t, ROT_DIM // 2, axis=1) * rp

    aw = ATTN_WIDTH
    qp = jnp.dot(h, w_ref[:, 0:aw], preferred_element_type=F32)
    kp = jnp.dot(h, w_ref[:, aw:2 * aw], preferred_element_type=F32)
    scale = DA_HEAD_DIM ** -0.5 * math.log2(math.e)
    for c in range(aw // V7X_LANES):
        sl = slice(c * V7X_LANES, (c + 1) * V7X_LANES)
        q_ref[:, sl] = (rot(qp[:, sl]) * scale).astype(BF16)
        k_ref[:, sl] = rot(kp[:, sl]).astype(BF16)
    vp = jnp.dot(h, w_ref[:, 2 * aw:3 * aw], preferred_element_type=F32).astype(BF16)
    ones = jnp.ones((vp.shape[0], DA_V_DIM), BF16)
    for hd in range(DA_HEADS):
        v_ref[:, 2 * hd * DA_V_DIM:(2 * hd + 1) * DA_V_DIM] = vp[:, hd * DA_V_DIM:(hd + 1) * DA_V_DIM]
        v_ref[:, (2 * hd + 1) * DA_V_DIM:(2 * hd + 2) * DA_V_DIM] = ones
    u_ref[...] = jnp.dot(h, w_ref[:, 3 * aw:], preferred_element_type=F32)


def _inproj(x2, g, b, w_bf, ra, rm, rp, tm=256):
    T, D = x2.shape
    ncol = w_bf.shape[1]
    aw = ATTN_WIDTH
    uw = ncol - 3 * aw
    row = lambda i: (i, 0)
    fixed = lambda i: (0, 0)
    return pl.pallas_call(
        _inproj_kernel,
        grid=(T // tm,),
        in_specs=[
            pl.BlockSpec((tm, D), row),
            pl.BlockSpec((1, D), fixed),
            pl.BlockSpec((1, D), fixed),
            pl.BlockSpec((D, ncol), fixed),
            pl.BlockSpec((tm, V7X_LANES), row),
            pl.BlockSpec((tm, V7X_LANES), row),
            pl.BlockSpec((tm, V7X_LANES), row),
        ],
        out_specs=[
            pl.BlockSpec((tm, aw), row),
            pl.BlockSpec((tm, aw), row),
            pl.BlockSpec((tm, 2 * aw), row),
            pl.BlockSpec((tm, uw), row),
        ],
        out_shape=[
            jax.ShapeDtypeStruct((T, aw), BF16),
            jax.ShapeDtypeStruct((T, aw), BF16),
            jax.ShapeDtypeStruct((T, 2 * aw), BF16),
            jax.ShapeDtypeStruct((T, uw), F32),
        ],
        compiler_params=_cparams(("parallel",), 48),
        name="inproj",
    )(x2, g, b, w_bf, ra, rm, rp)


def _rotary_tables(positions):
    half = ROT_DIM // 2
    inv_freq = ROPE_THETA ** (-jnp.arange(0, ROT_DIM, 2, dtype=F32) / ROT_DIM)
    ang = positions.astype(F32).reshape(-1)[:, None] * inv_freq
    cos, sin = jnp.cos(ang), jnp.sin(ang)
    T = ang.shape[0]
    ones = jnp.ones((T, DA_HEAD_DIM - ROT_DIM), F32)
    zeros_h = jnp.zeros((T, half), F32)
    zeros_r = jnp.zeros((T, DA_HEAD_DIM - ROT_DIM), F32)
    a64 = jnp.concatenate([cos, cos, ones], axis=1)
    m64 = jnp.concatenate([-sin, zeros_h, zeros_r], axis=1)
    p64 = jnp.concatenate([zeros_h, sin, zeros_r], axis=1)
    rep = V7X_LANES // DA_HEAD_DIM
    return jnp.tile(a64, (1, rep)), jnp.tile(m64, (1, rep)), jnp.tile(p64, (1, rep))


def _attn_kernel(lam_ref, q_ref, k_ref, v_ref, g_ref, o_ref, s_ref, m_ref, acc_ref, *, kb, unroll):
    qb = q_ref.shape[0]
    L = k_ref.shape[0]
    nchunk = L // kb
    nl = kb // V7X_LANES
    q = q_ref[...]
    lane = lax.broadcasted_iota(jnp.int32, q.shape, 1)
    zero = jnp.zeros_like(q)
    qm = [jnp.where(lane < DA_HEAD_DIM, q, zero), jnp.where(lane >= DA_HEAD_DIM, q, zero)]
    m_ref[...] = jnp.full(m_ref.shape, -jnp.inf, F32)
    acc_ref[...] = jnp.zeros(acc_ref.shape, F32)

    def score_body(j, carry):
        kc = k_ref[pl.ds(pl.multiple_of(j * kb, kb), kb), :]
        for c in range(2):
            s = lax.dot_general(qm[c], kc, (((1,), (1,)), ((), ())), preferred_element_type=F32)
            s_ref[c, j] = s
            m = s[:, 0:V7X_LANES]
            for t in range(1, nl):
                m = jnp.maximum(m, s[:, t * V7X_LANES:(t + 1) * V7X_LANES])
            m_ref[c] = jnp.maximum(m_ref[c], m)
        return carry

    lax.fori_loop(0, nchunk, score_body, 0, unroll=unroll)
    m_row = [jnp.max(m_ref[c], axis=1, keepdims=True) for c in range(2)]

    def pv_body(j, carry):
        vc = v_ref[pl.ds(pl.multiple_of(j * kb, kb), kb), :]
        for c in range(2):
            p = jnp.exp2(s_ref[c, j] - m_row[c])
            acc_ref[c] += jnp.dot(p.astype(BF16), vc, preferred_element_type=F32)
        return carry

    lax.fori_loop(0, nchunk, pv_body, 0, unroll=unroll)
    outs = [acc_ref[c, :, 0:DA_V_DIM] / acc_ref[c, :, DA_V_DIM:2 * DA_V_DIM] for c in range(2)]
    o = outs[0] - lam_ref[0] * outs[1]
    ms = jnp.mean(o * o, axis=1, keepdims=True)
    o_ref[...] = (o * lax.rsqrt(ms + LN_EPS) * g_ref[...] * (1.0 - LAM_INIT)).astype(o_ref.dtype)


def _attention(q, k, v, lam, subln_g, B, L, qb=256, kb=512, unroll=8):
    T = q.shape[0]
    nq = L // qb
    return pl.pallas_call(
        functools.partial(_attn_kernel, kb=kb, unroll=unroll),
        grid=(B, DA_HEADS, nq),
        in_specs=[
            pl.BlockSpec(memory_space=pltpu.SMEM),
            pl.BlockSpec((qb, DA_V_DIM), lambda b, h, i: (b * nq + i, h)),
            pl.BlockSpec((L, DA_V_DIM), lambda b, h, i: (b, h)),
            pl.BlockSpec((L, 2 * DA_V_DIM), lambda b, h, i: (b, h)),
            pl.BlockSpec((1, DA_V_DIM), lambda b, h, i: (0, 0)),
        ],
        out_specs=pl.BlockSpec((qb, DA_V_DIM), lambda b, h, i: (b * nq + i, h)),
        out_shape=jax.ShapeDtypeStruct((T, ATTN_WIDTH), BF16),
        scratch_shapes=[
            pltpu.VMEM((2, L // kb, qb, kb), F32),
            pltpu.VMEM((2, qb, V7X_LANES), F32),
            pltpu.VMEM((2, qb, 2 * DA_V_DIM), F32),
        ],
        compiler_params=_cparams(("parallel", "parallel", "parallel"), 48),
        name="attn",
    )(lam, q, k, v, subln_g)


def _hpre_kernel(u_ref, up_ref, un_ref, w_ref, b_ref, x0_ref, vx_ref):
    i = pl.program_id(1)
    n = pl.num_programs(1)
    tb = u_ref.shape[0]
    C = x0_ref.shape[-1]
    row = lax.broadcasted_iota(jnp.int32, (tb, V7X_LANES), 0)
    has_prev = (i > 0).astype(F32)
    has_next = (i < n - 1).astype(F32)

    def conv(c0):
        sl = slice(c0, c0 + V7X_LANES)
        u = u_ref[:, sl]
        prev_row = up_ref[V7X_SUBLANES - 1:V7X_SUBLANES, sl] * has_prev
        next_row = un_ref[0:1, sl] * has_next
        u_prev = jnp.where(row == 0, prev_row, pltpu.roll(u, 1, axis=0))
        u_next = jnp.where(row == tb - 1, next_row, pltpu.roll(u, tb - 1, axis=0))
        return u_prev * w_ref[0:1, sl] + u * w_ref[1:2, sl] + u_next * w_ref[2:3, sl] + b_ref[:, sl]

    tile = x0_ref.shape[:-1] + (V7X_LANES,)
    for c in range(C // V7X_LANES):
        c0 = c * V7X_LANES
        x0_ref[:, :, :, c0:c0 + V7X_LANES] = conv(c0).reshape(tile)
        vx_ref[:, :, :, c0:c0 + V7X_LANES] = (conv(2 * C + c0) * conv(C + c0)).reshape(tile)


def _hpre(u, conv_w, conv_b, B, L, tb=512):
    T, C3 = u.shape
    C = C3 // 3
    nt = L // tb
    sub = V7X_SUBLANES
    na = tb // DFT_N2
    split = pl.BlockSpec((None, na, DFT_N2H, sub, C), lambda b, i: (b, i, 0, 0, 0))
    split_shape = jax.ShapeDtypeStruct((B, L // DFT_N2, DFT_N2H, sub, C), F32)
    cur = lambda b, i: (b * nt + i, 0)
    prev = lambda b, i: (jnp.maximum((b * L + i * tb) // sub - 1, 0), 0)
    nxt = lambda b, i: (jnp.minimum((b * L + (i + 1) * tb) // sub, T // sub - 1), 0)
    fixed = lambda b, i: (0, 0)
    return pl.pallas_call(
        _hpre_kernel,
        grid=(B, nt),
        in_specs=[
            pl.BlockSpec((tb, C3), cur),
            pl.BlockSpec((sub, C3), prev),
            pl.BlockSpec((sub, C3), nxt),
            pl.BlockSpec((SHORT_CONV, C3), fixed),
            pl.BlockSpec((1, C3), fixed),
        ],
        out_specs=[split, split],
        out_shape=[split_shape, split_shape],
        compiler_params=_cparams(("parallel", "parallel"), 32),
        name="hpre",
    )(u, u, u, conv_w, conv_b)


def _filt_kernel(z_ref, w1_ref, b1_ref, w2_ref, b2_ref, w3_ref, b3_ref, fr_ref, wo_ref, dl_ref,
                 hfb_ref, asum_ref, *, L):
    i = pl.program_id(0)
    tl = z_ref.shape[0]
    C = dl_ref.shape[1]
    hp = lax.Precision.HIGHEST
    h = jnp.sin(fr_ref[0:1, :] * (jnp.dot(z_ref[...], w1_ref[...], precision=hp, preferred_element_type=F32) + b1_ref[...]))
    h = jnp.sin(fr_ref[1:2, :] * (jnp.dot(h, w2_ref[...], precision=hp, preferred_element_type=F32) + b2_ref[...]))
    h = jnp.sin(fr_ref[2:3, :] * (jnp.dot(h, w3_ref[...], precision=hp, preferred_element_type=F32) + b3_ref[...]))
    o = jnp.dot(h, wo_ref[...], precision=hp, preferred_element_type=F32)
    grow = lax.broadcasted_iota(jnp.int32, (tl, C), 0) + i * tl
    t = grow.astype(F32) * (1.0 / (L - 1))
    decay = jnp.exp(-t * dl_ref[...])
    hf = o[:, :C] * decay
    hb = jnp.where(grow == 0, 0.0, o[:, C:] * decay)
    tile = hfb_ref.shape[:-1] + (C,)
    hfb_ref[:, :, :, :C] = hf.reshape(tile)
    hfb_ref[:, :, :, C:] = hb.reshape(tile)

    @pl.when(i == 0)
    def _():
        asum_ref[...] = jnp.zeros_like(asum_ref)

    asum_ref[...] += jnp.sum(jnp.abs(hf) + jnp.abs(hb), axis=0, keepdims=True)


def _filter_taps(L, w1, b1, w2, b2, w3, b3, freq, wout, tl=512):
    C = wout.shape[1] // 2
    order = w1.shape[1]
    emb = w1.shape[0]
    t = jnp.linspace(0.0, 1.0, L, dtype=F32)[:, None]
    w = 2.0 * math.pi * jnp.arange(L, dtype=F32)[:, None] / L
    f = jnp.linspace(1e-4, FILTER_BANDS - 1, FILTER_BANDS, dtype=F32)[None, :]
    z = jnp.concatenate([t, jnp.cos(f * w), -jnp.sin(f * w)], axis=-1)
    zp = jnp.pad(z, ((0, 0), (0, V7X_LANES - emb)))
    w1p = jnp.pad(w1, ((0, V7X_LANES - emb), (0, 0)))
    deltas = jnp.abs(jnp.linspace(math.log(DECAY_TARGET) / SLOW_DECAY, math.log(DECAY_TARGET) / FAST_DECAY, C, dtype=F32))[None]
    fixed = lambda i: (0, 0)
    return pl.pallas_call(
        functools.partial(_filt_kernel, L=L),
        grid=(L // tl,),
        in_specs=[
            pl.BlockSpec((tl, V7X_LANES), lambda i: (i, 0)),
            pl.BlockSpec((V7X_LANES, order), fixed), pl.BlockSpec((1, order), fixed),
            pl.BlockSpec((order, order), fixed), pl.BlockSpec((1, order), fixed),
            pl.BlockSpec((order, order), fixed), pl.BlockSpec((1, order), fixed),
            pl.BlockSpec((3, order), fixed),
            pl.BlockSpec((order, 2 * C), fixed),
            pl.BlockSpec((1, C), fixed),
        ],
        out_specs=[pl.BlockSpec((None, tl // DFT_N2, DFT_N2H, V7X_SUBLANES, 2 * C), lambda i: (0, i, 0, 0, 0)),
                   pl.BlockSpec((1, C), fixed)],
        out_shape=[jax.ShapeDtypeStruct((1, L // DFT_N2, DFT_N2H, V7X_SUBLANES, 2 * C), F32),
                   jax.ShapeDtypeStruct((1, C), F32)],
        compiler_params=_cparams(("arbitrary",), 32),
        name="filt",
    )(zp, w1p, b1[None], w2, b2[None], w3, b3[None], freq, wout, deltas)


def _dft_constants(L):
    n1 = 2 * L // DFT_N2
    n1h = n1 // 2
    N = 2 * L
    sub = V7X_SUBLANES
    two_pi = 2.0 * math.pi

    def cs(num, den):
        ang = (num % den).astype(F32) * (two_pi / den)
        return jnp.cos(ang), jnp.sin(ang)

    k1 = jnp.arange(n1, dtype=jnp.int32)
    a = jnp.arange(n1h, dtype=jnp.int32)
    c, s = cs(k1[:, None] * a[None, :], n1)
    eye = jnp.eye(sub, dtype=F32)

    def expand(m):
        r, kk = m.shape
        return (m[:, None, :, None] * eye[None, :, None, :]).reshape(r * sub, kk * sub)

    m1c = jnp.stack([jnp.concatenate([c, s], axis=1), jnp.concatenate([-s, c], axis=1)], axis=1).reshape(2 * n1, 2 * n1h)
    m1r = jnp.stack([c, -s], axis=1).reshape(2 * n1, n1h)
    ct, st = c.T, s.T
    m3 = jnp.stack([jnp.stack([ct, -st], axis=2).reshape(n1h, 2 * n1),
                    jnp.stack([st, ct], axis=2).reshape(n1h, 2 * n1)], axis=0).reshape(2 * n1h, 2 * n1)
    n2 = jnp.arange(DFT_N2, dtype=jnp.int32)
    gc, gs = cs(n2[:, None] * n2[None, :], DFT_N2)
    g_fwd = jnp.concatenate([jnp.concatenate([gc, gs], axis=1), jnp.concatenate([-gs, gc], axis=1)], axis=0)
    g_inv = jnp.concatenate([jnp.concatenate([gc, -gs], axis=1), jnp.concatenate([gs, gc], axis=1)], axis=0)
    tc, ts = cs(k1[:, None] * n2[None, :], N)
    lanes = (n1, DFT_N2H, sub, V7X_LANES)
    twc = jnp.broadcast_to(tc.reshape(n1, DFT_N2H, sub, 1), lanes)
    tws = jnp.broadcast_to(ts.reshape(n1, DFT_N2H, sub, 1), lanes)
    return dict(n1=n1, n1h=n1h, m1c=expand(m1c).astype(BF16), m1r=expand(m1r).astype(BF16),
                m3=expand(m3).astype(BF16), g_fwd=g_fwd.astype(BF16), g_inv=g_inv.astype(BF16), twc=twc, tws=tws)


def _dft1_kernel(x_ref, m_ref, o_ref):
    rows = m_ref.shape[1]
    xs = x_ref[...].reshape(rows, x_ref.shape[-1]).astype(BF16)
    o_ref[...] = jnp.dot(m_ref[...], xs, preferred_element_type=F32).reshape(o_ref.shape)


def _dft1(x5, m, n1, cb):
    P, n1h, _, sub, Cx = x5.shape
    return pl.pallas_call(
        _dft1_kernel,
        grid=(DFT_N2H, Cx // cb),
        in_specs=[
            pl.BlockSpec((P, n1h, None, sub, cb), lambda h, c: (0, 0, h, 0, c)),
            pl.BlockSpec(m.shape, lambda h, c: (0, 0)),
        ],
        out_specs=pl.BlockSpec((None, n1, 2, sub, cb), lambda h, c: (h, 0, 0, 0, c)),
        out_shape=jax.ShapeDtypeStruct((DFT_N2H, n1, 2, sub, Cx), F32),
        compiler_params=_cparams(("parallel", "parallel"), 48),
        name="dft1",
    )(x5, m)


def _dft2_kernel(a_ref, f_ref, twc_ref, tws_ref, gf_ref, gi_ref, o_ref):
    C = a_ref.shape[-1]
    half = DFT_N2
    twc, tws = twc_ref[...], tws_ref[...]

    def lanes(fn):
        return jnp.concatenate([fn(slice(c0, c0 + V7X_LANES)) for c0 in range(0, C, V7X_LANES)], axis=-1)

    def spectrum(ref, col0):
        re = lanes(lambda sl: ref[:, 0, :, col0 + sl.start:col0 + sl.stop] * twc + ref[:, 1, :, col0 + sl.start:col0 + sl.stop] * tws)
        im = lanes(lambda sl: ref[:, 1, :, col0 + sl.start:col0 + sl.stop] * twc - ref[:, 0, :, col0 + sl.start:col0 + sl.stop] * tws)
        t = jnp.concatenate([re.reshape(half, C), im.reshape(half, C)], axis=0).astype(BF16)
        s = jnp.dot(gf_ref[...], t, preferred_element_type=F32)
        return s[:half], s[half:]

    xr, xi = spectrum(a_ref, 0)
    fr, fi = spectrum(f_ref, 0)
    br, bi = spectrum(f_ref, C)
    hr, hi = fr + br, fi - bi
    y = jnp.concatenate([xr * hr - xi * hi, xr * hi + xi * hr], axis=0).astype(BF16)
    b = jnp.dot(gi_ref[...], y, preferred_element_type=F32)
    br2 = b[:half].reshape(DFT_N2H, V7X_SUBLANES, C)
    bi2 = b[half:].reshape(DFT_N2H, V7X_SUBLANES, C)
    for c0 in range(0, C, V7X_LANES):
        sl = slice(c0, c0 + V7X_LANES)
        o_ref[:, 0, :, sl] = br2[:, :, sl] * twc - bi2[:, :, sl] * tws
        o_ref[:, 1, :, sl] = bi2[:, :, sl] * twc + br2[:, :, sl] * tws


def _dft2(a5, f5, k):
    n2h, n1, _, sub, C = a5.shape
    return pl.pallas_call(
        _dft2_kernel,
        grid=(n1,),
        in_specs=[
            pl.BlockSpec((n2h, None, 2, sub, C), lambda i: (0, i, 0, 0, 0)),
            pl.BlockSpec((n2h, None, 2, sub, 2 * C), lambda i: (0, i, 0, 0, 0)),
            pl.BlockSpec((None, n2h, sub, V7X_LANES), lambda i: (i, 0, 0, 0)),
            pl.BlockSpec((None, n2h, sub, V7X_LANES), lambda i: (i, 0, 0, 0)),
            pl.BlockSpec((2 * DFT_N2, 2 * DFT_N2), lambda i: (0, 0)),
            pl.BlockSpec((2 * DFT_N2, 2 * DFT_N2), lambda i: (0, 0)),
        ],
        out_specs=pl.BlockSpec((n2h, None, 2, sub, C), lambda i: (0, i, 0, 0, 0)),
        out_shape=jax.ShapeDtypeStruct(a5.shape, F32),
        compiler_params=_cparams(("parallel",), 32),
        name="dft2",
    )(a5, f5, k["twc"], k["tws"], k["g_fwd"], k["g_inv"])


def _dft3_kernel(b_ref, m_ref, x0_ref, vx_ref, sc_ref, d_ref, o_ref):
    C = b_ref.shape[-1]
    bs = b_ref[...].reshape(m_ref.shape[1], C).astype(BF16)
    y = jnp.dot(m_ref[...], bs, preferred_element_type=F32).reshape(o_ref.shape)
    o_ref[...] = x0_ref[...] * (y * sc_ref[...] + vx_ref[...] * d_ref[...])


def _dft3(b5, m3, x05, vx5, scale, d):
    n2h, n1, _, sub, C = b5.shape
    Bt, n1h = x05.shape[0], x05.shape[1]
    tok = pl.BlockSpec((Bt, n1h, None, sub, C), lambda h: (0, 0, h, 0, 0))
    vec = pl.BlockSpec((1, C), lambda h: (0, 0))
    return pl.pallas_call(
        _dft3_kernel,
        grid=(n2h,),
        in_specs=[
            pl.BlockSpec((None, n1, 2, sub, C), lambda h: (h, 0, 0, 0, 0)),
            pl.BlockSpec(m3.shape, lambda h: (0, 0)),
            tok, tok, vec, vec,
        ],
        out_specs=tok,
        out_shape=jax.ShapeDtypeStruct(x05.shape, F32),
        compiler_params=_cparams(("parallel",), 48),
        name="dft3",
    )(b5, m3, x05, vx5, scale, d)


def _hyena(u, conv_w, conv_b, w1, b1, w2, b2, w3, b3, freq, wout, d_skip, B, L):
    T = u.shape[0]
    C = HYENA_WIDTH
    k = _dft_constants(L)
    n1, n1h = k["n1"], k["n1h"]
    x0, vx = _hpre(u, conv_w, conv_b[None], B, L)
    hfb, asum = _filter_taps(L, w1, b1, w2, b2, w3, b3, freq, wout)
    a5 = _dft1(vx, k["m1c"], n1, cb=C)
    f5 = _dft1(hfb, k["m1r"], n1, cb=C)
    b5 = _dft2(a5, f5, k)
    scale = 1.0 / (asum * (2 * L))
    return _dft3(b5, k["m3"], x0, vx, scale, d_skip[None])


def _bf16_bits(x):
    a = pltpu.bitcast(x, jnp.uint32)
    return (a + jnp.uint32(0x7FFF) + ((a >> 16) & jnp.uint32(1))) >> 16


def _pack_pairs(x):
    half = x.shape[1] // 2
    return _bf16_bits(x[:, :half]) | (_bf16_bits(x[:, half:]) << 16)


def _unpack_pairs(w):
    lo = pltpu.bitcast(w << 16, F32)
    hi = pltpu.bitcast(w & jnp.uint32(0xFFFF0000), F32)
    return lo, hi


def _oproj_kernel(x_ref, g0_ref, b0_ref, at_ref, hy_ref, wo_ref, g1_ref, b1_ref, wrh_ref, wrl_ref,
                  h1_ref, h1p_ref, sc_ref):
    aw = at_ref.shape[1]
    h0 = _layer_norm_rows(x_ref[...], g0_ref[...], b0_ref[...])
    mixed = jnp.dot(at_ref[...], wo_ref[0:aw, :], preferred_element_type=F32)
    hy = hy_ref[...].reshape(x_ref.shape[0], hy_ref.shape[-1])
    mixed = mixed + jnp.dot(hy.astype(BF16), wo_ref[aw:, :], preferred_element_type=F32)
    h1 = _layer_norm_rows(ALPHA * h0 + mixed, g1_ref[...], b1_ref[...])
    h1_ref[...] = h1
    h1p_ref[...] = _pack_pairs(h1)
    hh = h1.astype(BF16)
    hl = (h1 - hh.astype(F32)).astype(BF16)
    dn = (((1,), (1,)), ((), ()))
    logits = lax.dot_general(wrh_ref[...], hh, dn, preferred_element_type=F32)
    logits = logits + (lax.dot_general(wrh_ref[...], hl, dn, preferred_element_type=F32)
                       + lax.dot_general(wrl_ref[...], hh, dn, preferred_element_type=F32))
    sc_ref[...] = 1.0 / (1.0 + jnp.exp(-logits))


def _oproj(x2, g0, b0, attn, hy, wo_bf, g1, b1, wr_t, tm=256):
    T, D = x2.shape
    E = wr_t.shape[0]
    wrh = wr_t.astype(BF16)
    wrl = (wr_t - wrh.astype(F32)).astype(BF16)
    row = lambda i: (i, 0)
    fixed = lambda i: (0, 0)
    hy4 = hy.reshape((-1,) + hy.shape[2:])
    return pl.pallas_call(
        _oproj_kernel,
        grid=(T // tm,),
        in_specs=[
            pl.BlockSpec((tm, D), row), pl.BlockSpec((1, D), fixed), pl.BlockSpec((1, D), fixed),
            pl.BlockSpec((tm, attn.shape[1]), row),
            pl.BlockSpec((tm // DFT_N2,) + hy4.shape[1:], lambda i: (i, 0, 0, 0)),
            pl.BlockSpec((D, D), fixed), pl.BlockSpec((1, D), fixed), pl.BlockSpec((1, D), fixed),
            pl.BlockSpec((E, D), fixed), pl.BlockSpec((E, D), fixed),
        ],
        out_specs=[pl.BlockSpec((tm, D), row), pl.BlockSpec((tm, D // 2), row), pl.BlockSpec((E, tm), lambda i: (0, i))],
        out_shape=[
            jax.ShapeDtypeStruct((T, D), F32),
            jax.ShapeDtypeStruct((T, D // 2), jnp.uint32),
            jax.ShapeDtypeStruct((E, T), F32),
        ],
        compiler_params=_cparams(("parallel",), 48),
        name="oproj",
    )(x2, g0, b0, attn, hy4, wo_bf, g1, b1, wrh, wrl)


def _route_kernel(sc_ref, bias_ref, tri_ref, idx_ref, gate_ref, rank_ref, cnt_ref, carry_ref):
    step = pl.program_id(0)
    E, tm = sc_ref.shape
    neg = jnp.float32(-jnp.inf)
    scores = sc_ref[...]
    biased = scores + bias_ref[...]
    erow = lax.broadcasted_iota(jnp.int32, (E, tm), 0)
    big = jnp.int32(E)

    def first_argmax(vals, rows):
        m = jnp.max(vals, axis=0, keepdims=True)
        pick = jnp.min(jnp.where(vals == m, rows, big), axis=0, keepdims=True)
        return m, pick

    gsc = []
    for g in range(N_GROUPS):
        blk = biased[g * GROUP_SIZE:(g + 1) * GROUP_SIZE, :]
        rows = erow[g * GROUP_SIZE:(g + 1) * GROUP_SIZE, :]
        m1, p1 = first_argmax(blk, rows)
        m2 = jnp.max(jnp.where(rows == p1, neg, blk), axis=0, keepdims=True)
        gsc.append(m1 + m2)
    gsc = jnp.concatenate(gsc, axis=0)
    grow = lax.broadcasted_iota(jnp.int32, (N_GROUPS, tm), 0)
    gsel = jnp.zeros((N_GROUPS, tm), jnp.bool_)
    work = gsc
    for _ in range(TOPK_GROUPS):
        _, p = first_argmax(work, grow)
        hit = grow == p
        gsel = gsel | hit
        work = jnp.where(hit, neg, work)
    emask = jnp.concatenate(
        [jnp.broadcast_to(gsel[g:g + 1, :], (GROUP_SIZE, tm)) for g in range(N_GROUPS)], axis=0)
    work = jnp.where(emask, biased, neg)

    sel = jnp.zeros((E, tm), jnp.bool_)
    picks, gvals = [], []
    for _ in range(TOP_K):
        _, p = first_argmax(work, erow)
        hit = erow == p
        sel = sel | hit
        picks.append(p)
        gvals.append(jnp.sum(jnp.where(hit, scores, 0.0), axis=0, keepdims=True))
        work = jnp.where(hit, neg, work)
    gv = jnp.concatenate(gvals, axis=0)
    idx_ref[...] = jnp.concatenate(picks, axis=0)
    gate_ref[...] = gv / jnp.sum(gv, axis=0, keepdims=True) * ROUTED_SCALE

    @pl.when(step == 0)
    def _():
        carry_ref[...] = jnp.zeros_like(carry_ref)

    chosen = sel.astype(F32)
    before = jnp.dot(chosen.astype(BF16), tri_ref[...], preferred_element_type=F32) + carry_ref[...]
    rank_ref[...] = jnp.concatenate(
        [jnp.sum(jnp.where(erow == p, before, 0.0), axis=0, keepdims=True) for p in picks], axis=0).astype(jnp.int32)
    carry_ref[...] += jnp.sum(chosen, axis=1, keepdims=True)
    cnt_ref[...] = carry_ref[...].astype(jnp.int32)


def _route(scores_t, bias, tm=128):
    E, T = scores_t.shape
    tri = (jnp.arange(tm)[:, None] < jnp.arange(tm)[None, :]).astype(BF16)
    tok = lambda i: (0, i)
    fixed = lambda i: (0, 0)
    return pl.pallas_call(
        _route_kernel,
        grid=(T // tm,),
        in_specs=[pl.BlockSpec((E, tm), tok), pl.BlockSpec((E, 1), fixed), pl.BlockSpec((tm, tm), fixed)],
        out_specs=[pl.BlockSpec((TOP_K, tm), tok), pl.BlockSpec((TOP_K, tm), tok), pl.BlockSpec((TOP_K, tm), tok),
                   pl.BlockSpec((E, 1), fixed)],
        out_shape=[
            jax.ShapeDtypeStruct((TOP_K, T), jnp.int32),
            jax.ShapeDtypeStruct((TOP_K, T), F32),
            jax.ShapeDtypeStruct((TOP_K, T), jnp.int32),
            jax.ShapeDtypeStruct((E, 1), jnp.int32),
        ],
        scratch_shapes=[pltpu.VMEM((E, 1), F32)],
        compiler_params=_cparams(("arbitrary",), 32),
        name="route",
    )(scores_t, bias, tri)


def _scatter_kernel(ps_ref, pe_ref, idx_ref, rank_ref, h_ref, xs_ref, zbuf, sem, *, rb):
    tm = h_ref.shape[0]

    @pl.when(pl.program_id(0) == 0)
    def _():
        zbuf[...] = jnp.zeros_like(zbuf)

        def zcopy(e):
            return pltpu.make_async_copy(zbuf, xs_ref.at[pl.ds(pl.multiple_of(pe_ref[e] - rb, rb), rb)], sem)

        def zstart(e, c):
            @pl.when(pe_ref[e] > ps_ref[e])
            def _():
                zcopy(e).start()
            return c

        def zwait(e, c):
            @pl.when(pe_ref[e] > ps_ref[e])
            def _():
                zcopy(e).wait()
            return c

        lax.fori_loop(0, ps_ref.shape[0], zstart, 0)
        lax.fori_loop(0, ps_ref.shape[0], zwait, 0)

    def copy(r, k):
        dst = ps_ref[idx_ref[k, r]] + rank_ref[k, r]
        return pltpu.make_async_copy(h_ref.at[pl.ds(r, 1)], xs_ref.at[pl.ds(dst, 1)], sem)

    def issue(r, c):
        for k in range(TOP_K):
            copy(r, k).start()
        return c

    def drain(r, c):
        for k in range(TOP_K):
            copy(r, k).wait()
        return c

    lax.fori_loop(0, tm, issue, 0)
    lax.fori_loop(0, tm, drain, 0)


def _scatter_rows(pad_start, pad_end, idx, rank, h1p, n_rows, rb, tm=256):
    T, W = h1p.shape
    tok = lambda i, ps, pe: (0, i)
    return pl.pallas_call(
        functools.partial(_scatter_kernel, rb=rb),
        grid_spec=pltpu.PrefetchScalarGridSpec(
            num_scalar_prefetch=2,
            grid=(T // tm,),
            in_specs=[
                pl.BlockSpec((TOP_K, tm), tok, memory_space=pltpu.SMEM),
                pl.BlockSpec((TOP_K, tm), tok, memory_space=pltpu.SMEM),
                pl.BlockSpec((tm, W), lambda i, ps, pe: (i, 0)),
            ],
            out_specs=pl.BlockSpec(memory_space=pl.ANY),
            scratch_shapes=[pltpu.VMEM((rb, W), h1p.dtype), pltpu.SemaphoreType.DMA],
        ),
        out_shape=jax.ShapeDtypeStruct((n_rows, W), h1p.dtype),
        compiler_params=_cparams(("arbitrary",), 32),
        name="scatter",
    )(pad_start, pad_end, idx, rank, h1p)


def _experts_kernel(be_ref, nu_ref, x_ref, wg_ref, wu_ref, wd_ref, y_ref, wgb, wub, wdb):
    b = pl.program_id(0)
    e = be_ref[b]
    changed = jnp.logical_or(b == 0, e != be_ref[jnp.maximum(b - 1, 0)])

    @pl.when(jnp.logical_and(changed, b < nu_ref[0]))
    def _():
        wgb[...] = wg_ref[...].astype(BF16)
        wub[...] = wu_ref[...].astype(BF16)
        wdb[...] = wd_ref[...].astype(BF16)

    @pl.when(b < nu_ref[0])
    def _():
        lo, hi = _unpack_pairs(x_ref[...])
        half = lo.shape[1]
        xl, xh = lo.astype(BF16), hi.astype(BF16)
        g = jnp.dot(xl, wgb[0:half, :], preferred_element_type=F32) + jnp.dot(xh, wgb[half:, :], preferred_element_type=F32)
        u = jnp.dot(xl, wub[0:half, :], preferred_element_type=F32) + jnp.dot(xh, wub[half:, :], preferred_element_type=F32)
        hb = (g / (1.0 + jnp.exp(-g)) * u).astype(BF16)
        y_ref[...] = _pack_pairs(jnp.dot(hb, wdb[...], preferred_element_type=F32))

    @pl.when(b >= nu_ref[0])
    def _():
        y_ref[...] = jnp.zeros_like(y_ref)


def _experts(block_expert, n_used, xs, w_gate, w_up, w_down, rb):
    P, W = xs.shape
    E, D, F = w_gate.shape
    nb = P // rb
    live = lambda b, be, nu: (jnp.minimum(b, nu[0] - 1), 0)
    return pl.pallas_call(
        _experts_kernel,
        grid_spec=pltpu.PrefetchScalarGridSpec(
            num_scalar_prefetch=2,
            grid=(nb,),
            in_specs=[
                pl.BlockSpec((rb, W), live),
                pl.BlockSpec((None, D, F), lambda b, be, nu: (be[b], 0, 0)),
                pl.BlockSpec((None, D, F), lambda b, be, nu: (be[b], 0, 0)),
                pl.BlockSpec((None, F, D), lambda b, be, nu: (be[b], 0, 0)),
            ],
            out_specs=pl.BlockSpec((rb, W), lambda b, be, nu: (b, 0)),
            scratch_shapes=[pltpu.VMEM((D, F), BF16), pltpu.VMEM((D, F), BF16), pltpu.VMEM((F, D), BF16)],
        ),
        out_shape=jax.ShapeDtypeStruct((P, W), jnp.uint32),
        compiler_params=_cparams(("arbitrary",), 48),
        name="experts",
    )(block_expert, n_used, xs, w_gate, w_up, w_down)


def _combine_kernel(ps_ref, idx_ref, rank_ref, idxn_ref, rankn_ref, h1_ref, gate_ref, ys_ref, sg_ref, su_ref, sd_ref,
                    g2_ref, b2_ref, o_ref, buf, sem):
    i = pl.program_id(0)
    n = pl.num_programs(0)
    tm = h1_ref.shape[0]
    slot = i % 2

    def copy(iref, rref, s, r, k):
        src = ps_ref[iref[k, r]] + rref[k, r]
        return pltpu.make_async_copy(ys_ref.at[pl.ds(src, 1)], buf.at[s, k, pl.ds(r, 1)], sem.at[s])

    def issue(iref, rref, s):
        def body(r, c):
            for k in range(TOP_K):
                copy(iref, rref, s, r, k).start()
            return c
        lax.fori_loop(0, tm, body, 0)

    @pl.when(i == 0)
    def _():
        issue(idx_ref, rank_ref, 0)

    @pl.when(i + 1 < n)
    def _():
        issue(idxn_ref, rankn_ref, 1 - slot)

    h1 = h1_ref[...]
    hb = h1.astype(BF16)
    g = jnp.dot(hb, sg_ref[...], preferred_element_type=F32)
    u = jnp.dot(hb, su_ref[...], preferred_element_type=F32)
    ffn = jnp.dot((g / (1.0 + jnp.exp(-g)) * u).astype(BF16), sd_ref[...], preferred_element_type=F32)

    def drain(r, c):
        for k in range(TOP_K):
            copy(idx_ref, rank_ref, slot, r, k).wait()
        return c

    lax.fori_loop(0, tm, drain, 0)
    gates = gate_ref[...]
    lo_acc = jnp.zeros((tm, buf.shape[3]), F32)
    hi_acc = jnp.zeros((tm, buf.shape[3]), F32)
    for k in range(TOP_K):
        lo, hi = _unpack_pairs(buf[slot, k])
        gk = gates[:, k:k + 1]
        lo_acc = lo_acc + gk * lo
        hi_acc = hi_acc + gk * hi
    ffn = ffn + jnp.concatenate([lo_acc, hi_acc], axis=1)
    o_ref[...] = _layer_norm_rows(ALPHA * h1 + ffn, g2_ref[...], b2_ref[...])


def _combine(pad_start, idx, rank, h1, gates_tk, ys, sg, su, sd, g2, b2, tm=128):
    T, D = h1.shape
    W = ys.shape[1]
    F = sg.shape[1]
    nt = T // tm
    row = lambda i, ps: (i, 0)
    fixed = lambda i, ps: (0, 0)
    cur = lambda i, ps: (0, i)
    nxt = lambda i, ps: (0, jnp.minimum(i + 1, nt - 1))
    smem = functools.partial(pl.BlockSpec, (TOP_K, tm), memory_space=pltpu.SMEM)
    return pl.pallas_call(
        _combine_kernel,
        grid_spec=pltpu.PrefetchScalarGridSpec(
            num_scalar_prefetch=1,
            grid=(nt,),
            in_specs=[
                smem(index_map=cur), smem(index_map=cur), smem(index_map=nxt), smem(index_map=nxt),
                pl.BlockSpec((tm, D), row),
                pl.BlockSpec((tm, TOP_K), row),
                pl.BlockSpec(memory_space=pl.ANY),
                pl.BlockSpec((D, F), fixed), pl.BlockSpec((D, F), fixed), pl.BlockSpec((F, D), fixed),
                pl.BlockSpec((1, D), fixed), pl.BlockSpec((1, D), fixed),
            ],
            out_specs=pl.BlockSpec((tm, D), row),
            scratch_shapes=[pltpu.VMEM((2, TOP_K, tm, W), jnp.uint32), pltpu.SemaphoreType.DMA((2,))],
        ),
        out_shape=jax.ShapeDtypeStruct((T, D), F32),
        compiler_params=_cparams(("arbitrary",), 48),
        name="combine",
    )(pad_start, idx, rank, idx, rank, h1, gates_tk, ys, sg, su, sd, g2, b2)


MOE_ROWS = 128


def _moe(h1, h1p, scores_t, router_bias, w_gate, w_up, w_down, ws_gate, ws_up, ws_down, g2, b2):
    T = h1.shape[0]
    E = N_EXPERTS
    rb = MOE_ROWS
    idx, gates, rank, counts = _route(scores_t, router_bias[:, None])
    counts = counts[:, 0]
    padded = (counts + rb - 1) // rb * rb
    pad_end = jnp.cumsum(padded)
    pad_start = pad_end - padded
    nb = (T * TOP_K) // rb + E
    starts = jnp.arange(nb, dtype=jnp.int32) * rb
    block_expert = jnp.minimum(jnp.sum((pad_end[None, :] <= starts[:, None]).astype(jnp.int32), axis=1), E - 1)
    n_used = (pad_end[-1:] // rb).astype(jnp.int32)
    xs = _scatter_rows(pad_start, pad_end, idx, rank, h1p, nb * rb, rb)
    ys = _experts(block_expert, n_used, xs, w_gate, w_up, w_down, rb)
    return _combine(pad_start, idx, rank, h1, gates.T, ys, ws_gate.astype(BF16), ws_up.astype(BF16),
                    ws_down.astype(BF16), g2, b2)


def kernel(x, positions, emb_ln_g, emb_ln_b, w_in, hy_conv_w, hy_conv_b, hy_f_w1, hy_f_b1, hy_f_w2, hy_f_b2, hy_f_w3, hy_f_b3, hy_f_freq, hy_f_wout, hy_d, lambda_q1, lambda_k1, lambda_q2, lambda_k2, subln_g, w_o, ln1_g, ln1_b, w_router, router_bias, w_gate, w_up, w_down, ws_gate, ws_up, ws_down, ln2_g, ln2_b):
    B, L, D = x.shape
    T = B * L
    assert w_in.shape[0] == DEPTH == 1
    i = 0
    x2 = x.reshape(T, D)
    g0, b0 = emb_ln_g[None], emb_ln_b[None]
    ra, rm, rp = _rotary_tables(positions)
    q, k, v, u = _inproj(x2, g0, b0, w_in[i].astype(BF16), ra, rm, rp)
    lam = (jnp.exp(jnp.sum(lambda_q1[i] * lambda_k1[i])) - jnp.exp(jnp.sum(lambda_q2[i] * lambda_k2[i])) + LAM_INIT)
    attn = _attention(q, k, v, lam.reshape(1).astype(F32), subln_g[i][None], B, L)
    hy = _hyena(u, hy_conv_w[i], hy_conv_b[i], hy_f_w1[i], hy_f_b1[i], hy_f_w2[i], hy_f_b2[i], hy_f_w3[i], hy_f_b3[i],
                hy_f_freq[i], hy_f_wout[i], hy_d[i], B, L)
    h1, h1p, scores_t = _oproj(x2, g0, b0, attn, hy, w_o[i].astype(BF16), ln1_g[i][None], ln1_b[i][None], w_router[i].T)
    out = _moe(h1, h1p, scores_t, router_bias[i], w_gate[i], w_up[i], w_down[i], ws_gate[i], ws_up[i], ws_down[i],
               ln2_g[i][None], ln2_b[i][None])
    return out.reshape(B, L, D)
```

```python
import functools
import math

import numpy as np
import jax
import jax.numpy as jnp
from jax import lax
from jax.experimental import pallas as pl
from jax.experimental.pallas import tpu as pltpu

DA_HEADS = 4
DA_HEAD_DIM = 64
DA_V_DIM = 128
ATTN_WIDTH = 512
HYENA_WIDTH = 512
ROT_DIM = 16
ROPE_THETA = 500000.0
SHORT_CONV = 3
FILTER_EMB = 33
FILTER_BANDS = 16
DECAY_TARGET = 1e-2
FAST_DECAY = 0.3
SLOW_DECAY = 1.5
N_EXPERTS = 256
TOP_K = 8
N_GROUPS = 8
GROUP_SIZE = N_EXPERTS // N_GROUPS
TOPK_GROUPS = 4
EXPERT_DIM = 256
ROUTED_SCALE = 2.5
DEPTH = 1
ALPHA = (2 * DEPTH) ** 0.25
LN_EPS = 1e-5
LAM_INIT = 0.8 - 0.6 * math.exp(-0.3 * 0)

V7X_LANES = 128
V7X_SUBLANES = 8
V7X_VMEM_BYTES = 64 * 1024 * 1024

DFT_N2 = 128
DFT_N2H = DFT_N2 // V7X_SUBLANES

BF16 = jnp.bfloat16
F32 = jnp.float32


def _cparams(sem, vmem_mb):
    return pltpu.CompilerParams(dimension_semantics=sem, vmem_limit_bytes=vmem_mb * 1024 * 1024)


def _layer_norm_rows(x, g, b):
    mu = jnp.mean(x, axis=-1, keepdims=True)
    xc = x - mu
    var = jnp.mean(xc * xc, axis=-1, keepdims=True)
    return xc * lax.rsqrt(var + LN_EPS) * g + b


def _inproj_kernel(x_ref, g_ref, b_ref, w_ref, ra_ref, rm_ref, rp_ref, q_ref, k_ref, v_ref, u_ref):
    h = _layer_norm_rows(x_ref[...], g_ref[...], b_ref[...]).astype(BF16)
    ra, rm, rp = ra_ref[...], rm_ref[...], rp_ref[...]

    def rot(t):
        return t * ra + pltpu.roll(t, V7X_LANES - ROT_DIM // 2, axis=1) * rm + pltpu.roll(t, ROT_DIM // 2, axis=1) * rp

    aw = ATTN_WIDTH
    qp = jnp.dot(h, w_ref[:, 0:aw], preferred_element_type=F32)
    kp = jnp.dot(h, w_ref[:, aw:2 * aw], preferred_element_type=F32)
    scale = DA_HEAD_DIM ** -0.5 * math.log2(math.e)
    for c in range(aw // V7X_LANES):
        sl = slice(c * V7X_LANES, (c + 1) * V7X_LANES)
        q_ref[:, sl] = (rot(qp[:, sl]) * scale).astype(BF16)
        k_ref[:, sl] = rot(kp[:, sl]).astype(BF16)
    vp = jnp.dot(h, w_ref[:, 2 * aw:3 * aw], preferred_element_type=F32).astype(BF16)
    ones = jnp.ones((vp.shape[0], DA_V_DIM), BF16)
    for hd in range(DA_HEADS):
        v_ref[:, 2 * hd * DA_V_DIM:(2 * hd + 1) * DA_V_DIM] = vp[:, hd * DA_V_DIM:(hd + 1) * DA_V_DIM]
        v_ref[:, (2 * hd + 1) * DA_V_DIM:(2 * hd + 2) * DA_V_DIM] = ones
    u_ref[...] = jnp.dot(h, w_ref[:, 3 * aw:], preferred_element_type=F32)


def _inproj(x2, g, b, w_bf, ra, rm, rp, tm=256):
    T, D = x2.shape
    ncol = w_bf.shape[1]
    aw = ATTN_WIDTH
    uw = ncol - 3 * aw
    row = lambda i: (i, 0)
    fixed = lambda i: (0, 0)
    return pl.pallas_call(
        _inproj_kernel,
        grid=(T // tm,),
        in_specs=[
            pl.BlockSpec((tm, D), row),
            pl.BlockSpec((1, D), fixed),
            pl.BlockSpec((1, D), fixed),
            pl.BlockSpec((D, ncol), fixed),
            pl.BlockSpec((tm, V7X_LANES), row),
            pl.BlockSpec((tm, V7X_LANES), row),
            pl.BlockSpec((tm, V7X_LANES), row),
        ],
        out_specs=[
            pl.BlockSpec((tm, aw), row),
            pl.BlockSpec((tm, aw), row),
            pl.BlockSpec((tm, 2 * aw), row),
            pl.BlockSpec((tm, uw), row),
        ],
        out_shape=[
            jax.ShapeDtypeStruct((T, aw), BF16),
            jax.ShapeDtypeStruct((T, aw), BF16),
            jax.ShapeDtypeStruct((T, 2 * aw), BF16),
            jax.ShapeDtypeStruct((T, uw), F32),
        ],
        compiler_params=_cparams(("parallel",), 48),
        name="inproj",
    )(x2, g, b, w_bf, ra, rm, rp)


def _rotary_tables(positions):
    half = ROT_DIM // 2
    inv_freq = ROPE_THETA ** (-jnp.arange(0, ROT_DIM, 2, dtype=F32) / ROT_DIM)
    ang = positions.astype(F32).reshape(-1)[:, None] * inv_freq
    cos, sin = jnp.cos(ang), jnp.sin(ang)
    T = ang.shape[0]
    ones = jnp.ones((T, DA_HEAD_DIM - ROT_DIM), F32)
    zeros_h = jnp.zeros((T, half), F32)
    zeros_r = jnp.zeros((T, DA_HEAD_DIM - ROT_DIM), F32)
    a64 = jnp.concatenate([cos, cos, ones], axis=1)
    m64 = jnp.concatenate([-sin, zeros_h, zeros_r], axis=1)
    p64 = jnp.concatenate([zeros_h, sin, zeros_r], axis=1)
    rep = V7X_LANES // DA_HEAD_DIM
    return jnp.tile(a64, (1, rep)), jnp.tile(m64, (1, rep)), jnp.tile(p64, (1, rep))


def _attn_kernel(lam_ref, q_ref, k_ref, v_ref, g_ref, o_ref, s_ref, m_ref, acc_ref, *, kb, unroll):
    qb = q_ref.shape[0]
    L = k_ref.shape[0]
    nchunk = L // kb
    nl = kb // V7X_LANES
    q = q_ref[...]
    lane = lax.broadcasted_iota(jnp.int32, q.shape, 1)
    zero = jnp.zeros_like(q)
    qm = [jnp.where(lane < DA_HEAD_DIM, q, zero), jnp.where(lane >= DA_HEAD_DIM, q, zero)]
    m_ref[...] = jnp.full(m_ref.shape, -jnp.inf, F32)
    acc_ref[...] = jnp.zeros(acc_ref.shape, F32)

    def score_body(j, carry):
        kc = k_ref[pl.ds(pl.multiple_of(j * kb, kb), kb), :]
        for c in range(2):
            s = lax.dot_general(qm[c], kc, (((1,), (1,)), ((), ())), preferred_element_type=F32)
            s_ref[c, j] = s
            m = s[:, 0:V7X_LANES]
            for t in range(1, nl):
                m = jnp.maximum(m, s[:, t * V7X_LANES:(t + 1) * V7X_LANES])
            m_ref[c] = jnp.maximum(m_ref[c], m)
        return carry

    lax.fori_loop(0, nchunk, score_body, 0, unroll=unroll)
    m_row = [jnp.max(m_ref[c], axis=1, keepdims=True) for c in range(2)]

    def pv_body(j, carry):
        vc = v_ref[pl.ds(pl.multiple_of(j * kb, kb), kb), :]
        for c in range(2):
            p = jnp.exp2(s_ref[c, j] - m_row[c])
            acc_ref[c] += jnp.dot(p.astype(BF16), vc, preferred_element_type=F32)
        return carry

    lax.fori_loop(0, nchunk, pv_body, 0, unroll=unroll)
    outs = [acc_ref[c, :, 0:DA_V_DIM] / acc_ref[c, :, DA_V_DIM:2 * DA_V_DIM] for c in range(2)]
    o = outs[0] - lam_ref[0] * outs[1]
    ms = jnp.mean(o * o, axis=1, keepdims=True)
    o_ref[...] = (o * lax.rsqrt(ms + LN_EPS) * g_ref[...] * (1.0 - LAM_INIT)).astype(o_ref.dtype)


def _attention(q, k, v, lam, subln_g, B, L, qb=256, kb=512, unroll=8):
    T = q.shape[0]
    nq = L // qb
    return pl.pallas_call(
        functools.partial(_attn_kernel, kb=kb, unroll=unroll),
        grid=(B, DA_HEADS, nq),
        in_specs=[
            pl.BlockSpec(memory_space=pltpu.SMEM),
            pl.BlockSpec((qb, DA_V_DIM), lambda b, h, i: (b * nq + i, h)),
            pl.BlockSpec((L, DA_V_DIM), lambda b, h, i: (b, h)),
            pl.BlockSpec((L, 2 * DA_V_DIM), lambda b, h, i: (b, h)),
            pl.BlockSpec((1, DA_V_DIM), lambda b, h, i: (0, 0)),
        ],
        out_specs=pl.BlockSpec((qb, DA_V_DIM), lambda b, h, i: (b * nq + i, h)),
        out_shape=jax.ShapeDtypeStruct((T, ATTN_WIDTH), BF16),
        scratch_shapes=[
            pltpu.VMEM((2, L // kb, qb, kb), F32),
            pltpu.VMEM((2, qb, V7X_LANES), F32),
            pltpu.VMEM((2, qb, 2 * DA_V_DIM), F32),
        ],
        compiler_params=_cparams(("parallel", "parallel", "parallel"), 48),
        name="attn",
    )(lam, q, k, v, subln_g)


def _hpre_kernel(u_ref, up_ref, un_ref, w_ref, b_ref, x0_ref, vx_ref):
    i = pl.program_id(1)
    n = pl.num_programs(1)
    tb = u_ref.shape[0]
    C = x0_ref.shape[-1]
    row = lax.broadcasted_iota(jnp.int32, (tb, V7X_LANES), 0)
    has_prev = (i > 0).astype(F32)
    has_next = (i < n - 1).astype(F32)

    def conv(c0):
        sl = slice(c0, c0 + V7X_LANES)
        u = u_ref[:, sl]
        prev_row = up_ref[V7X_SUBLANES - 1:V7X_SUBLANES, sl] * has_prev
        next_row = un_ref[0:1, sl] * has_next
        u_prev = jnp.where(row == 0, prev_row, pltpu.roll(u, 1, axis=0))
        u_next = jnp.where(row == tb - 1, next_row, pltpu.roll(u, tb - 1, axis=0))
        return u_prev * w_ref[0:1, sl] + u * w_ref[1:2, sl] + u_next * w_ref[2:3, sl] + b_ref[:, sl]

    tile = x0_ref.shape[:-1] + (V7X_LANES,)
    for c in range(C // V7X_LANES):
        c0 = c * V7X_LANES
        x0_ref[:, :, :, c0:c0 + V7X_LANES] = conv(c0).reshape(tile)
        vx_ref[:, :, :, c0:c0 + V7X_LANES] = (conv(2 * C + c0) * conv(C + c0)).reshape(tile)


def _hpre(u, conv_w, conv_b, B, L, tb=512):
    T, C3 = u.shape
    C = C3 // 3
    nt = L // tb
    sub = V7X_SUBLANES
    na = tb // DFT_N2
    split = pl.BlockSpec((None, na, DFT_N2H, sub, C), lambda b, i: (b, i, 0, 0, 0))
    split_shape = jax.ShapeDtypeStruct((B, L // DFT_N2, DFT_N2H, sub, C), F32)
    cur = lambda b, i: (b * nt + i, 0)
    prev = lambda b, i: (jnp.maximum((b * L + i * tb) // sub - 1, 0), 0)
    nxt = lambda b, i: (jnp.minimum((b * L + (i + 1) * tb) // sub, T // sub - 1), 0)
    fixed = lambda b, i: (0, 0)
    return pl.pallas_call(
        _hpre_kernel,
        grid=(B, nt),
        in_specs=[
            pl.BlockSpec((tb, C3), cur),
            pl.BlockSpec((sub, C3), prev),
            pl.BlockSpec((sub, C3), nxt),
            pl.BlockSpec((SHORT_CONV, C3), fixed),
            pl.BlockSpec((1, C3), fixed),
        ],
        out_specs=[split, split],
        out_shape=[split_shape, split_shape],
        compiler_params=_cparams(("parallel", "parallel"), 32),
        name="hpre",
    )(u, u, u, conv_w, conv_b)


def _filt_kernel(z_ref, w1_ref, b1_ref, w2_ref, b2_ref, w3_ref, b3_ref, fr_ref, wo_ref, dl_ref,
                 hfb_ref, asum_ref, *, L):
    i = pl.program_id(0)
    tl = z_ref.shape[0]
    C = dl_ref.shape[1]
    hp = lax.Precision.HIGHEST
    h = jnp.sin(fr_ref[0:1, :] * (jnp.dot(z_ref[...], w1_ref[...], precision=hp, preferred_element_type=F32) + b1_ref[...]))
    h = jnp.sin(fr_ref[1:2, :] * (jnp.dot(h, w2_ref[...], precision=hp, preferred_element_type=F32) + b2_ref[...]))
    h = jnp.sin(fr_ref[2:3, :] * (jnp.dot(h, w3_ref[...], precision=hp, preferred_element_type=F32) + b3_ref[...]))
    o = jnp.dot(h, wo_ref[...], precision=hp, preferred_element_type=F32)
    grow = lax.broadcasted_iota(jnp.int32, (tl, C), 0) + i * tl
    t = grow.astype(F32) * (1.0 / (L - 1))
    decay = jnp.exp(-t * dl_ref[...])
    hf = o[:, :C] * decay
    hb = jnp.where(grow == 0, 0.0, o[:, C:] * decay)
    tile = hfb_ref.shape[:-1] + (C,)
    hfb_ref[:, :, :, :C] = hf.reshape(tile)
    hfb_ref[:, :, :, C:] = hb.reshape(tile)

    @pl.when(i == 0)
    def _():
        asum_ref[...] = jnp.zeros_like(asum_ref)

    asum_ref[...] += jnp.sum(jnp.abs(hf) + jnp.abs(hb), axis=0, keepdims=True)


def _filter_taps(L, w1, b1, w2, b2, w3, b3, freq, wout, tl=512):
    C = wout.shape[1] // 2
    order = w1.shape[1]
    emb = w1.shape[0]
    t = jnp.linspace(0.0, 1.0, L, dtype=F32)[:, None]
    w = 2.0 * math.pi * jnp.arange(L, dtype=F32)[:, None] / L
    f = jnp.linspace(1e-4, FILTER_BANDS - 1, FILTER_BANDS, dtype=F32)[None, :]
    z = jnp.concatenate([t, jnp.cos(f * w), -jnp.sin(f * w)], axis=-1)
    zp = jnp.pad(z, ((0, 0), (0, V7X_LANES - emb)))
    w1p = jnp.pad(w1, ((0, V7X_LANES - emb), (0, 0)))
    deltas = jnp.abs(jnp.linspace(math.log(DECAY_TARGET) / SLOW_DECAY, math.log(DECAY_TARGET) / FAST_DECAY, C, dtype=F32))[None]
    fixed = lambda i: (0, 0)
    return pl.pallas_call(
        functools.partial(_filt_kernel, L=L),
        grid=(L // tl,),
        in_specs=[
            pl.BlockSpec((tl, V7X_LANES), lambda i: (i, 0)),
            pl.BlockSpec((V7X_LANES, order), fixed), pl.BlockSpec((1, order), fixed),
            pl.BlockSpec((order, order), fixed), pl.BlockSpec((1, order), fixed),
            pl.BlockSpec((order, order), fixed), pl.BlockSpec((1, order), fixed),
            pl.BlockSpec((3, order), fixed),
            pl.BlockSpec((order, 2 * C), fixed),
            pl.BlockSpec((1, C), fixed),
        ],
        out_specs=[pl.BlockSpec((None, tl // DFT_N2, DFT_N2H, V7X_SUBLANES, 2 * C), lambda i: (0, i, 0, 0, 0)),
                   pl.BlockSpec((1, C), fixed)],
        out_shape=[jax.ShapeDtypeStruct((1, L // DFT_N2, DFT_N2H, V7X_SUBLANES, 2 * C), F32),
                   jax.ShapeDtypeStruct((1, C), F32)],
        compiler_params=_cparams(("arbitrary",), 32),
        name="filt",
    )(zp, w1p, b1[None], w2, b2[None], w3, b3[None], freq, wout, deltas)


def _dft_constants(L):
    n1 = 2 * L // DFT_N2
    n1h = n1 // 2
    N = 2 * L
    sub = V7X_SUBLANES

    def cs(num, den):
        ang = (num % den).astype(np.float64) * (2.0 * np.pi / den)
        return np.cos(ang), np.sin(ang)

    k1 = np.arange(n1, dtype=np.int64)
    a = np.arange(n1h, dtype=np.int64)
    c, s = cs(k1[:, None] * a[None, :], n1)
    eye = np.eye(sub)

    def expand(m):
        r, kk = m.shape
        return (m[:, None, :, None] * eye[None, :, None, :]).reshape(r * sub, kk * sub)

    def const(m):
        return jnp.asarray(m.astype(np.float32).astype(BF16))

    m1c = np.stack([np.concatenate([c, s], axis=1), np.concatenate([-s, c], axis=1)], axis=1).reshape(2 * n1, 2 * n1h)
    m1r = np.stack([c, -s], axis=1).reshape(2 * n1, n1h)
    ct, st = c.T, s.T
    m3 = np.stack([np.stack([ct, -st], axis=2).reshape(n1h, 2 * n1),
                   np.stack([st, ct], axis=2).reshape(n1h, 2 * n1)], axis=0).reshape(2 * n1h, 2 * n1)
    n2 = np.arange(DFT_N2, dtype=np.int64)
    gc, gs = cs(n2[:, None] * n2[None, :], DFT_N2)
    g_fwd = np.concatenate([np.concatenate([gc, gs], axis=1), np.concatenate([-gs, gc], axis=1)], axis=0)
    g_inv = np.concatenate([np.concatenate([gc, -gs], axis=1), np.concatenate([gs, gc], axis=1)], axis=0)
    ph = (jnp.arange(n1, dtype=jnp.int32)[:, None] * jnp.arange(DFT_N2, dtype=jnp.int32)[None, :]) % N
    ang = ph.astype(F32) * (2.0 * math.pi / N)
    lanes = (n1, DFT_N2H, sub, V7X_LANES)
    twc = jnp.broadcast_to(jnp.cos(ang).reshape(n1, DFT_N2H, sub, 1), lanes)
    tws = jnp.broadcast_to(jnp.sin(ang).reshape(n1, DFT_N2H, sub, 1), lanes)
    return dict(n1=n1, n1h=n1h, m1c=const(expand(m1c)), m1r=const(expand(m1r)), m3=const(expand(m3)),
                g_fwd=const(g_fwd), g_inv=const(g_inv), twc=twc, tws=tws)


def _dft1_kernel(x_ref, m_ref, o_ref):
    rows = m_ref.shape[1]
    xs = x_ref[...].reshape(rows, x_ref.shape[-1]).astype(BF16)
    o_ref[...] = jnp.dot(m_ref[...], xs, preferred_element_type=F32).reshape(o_ref.shape)


def _dft1(x5, m, n1, cb):
    P, n1h, _, sub, Cx = x5.shape
    return pl.pallas_call(
        _dft1_kernel,
        grid=(DFT_N2H, Cx // cb),
        in_specs=[
            pl.BlockSpec((P, n1h, None, sub, cb), lambda h, c: (0, 0, h, 0, c)),
            pl.BlockSpec(m.shape, lambda h, c: (0, 0)),
        ],
        out_specs=pl.BlockSpec((None, n1, 2, sub, cb), lambda h, c: (h, 0, 0, 0, c)),
        out_shape=jax.ShapeDtypeStruct((DFT_N2H, n1, 2, sub, Cx), F32),
        compiler_params=_cparams(("parallel", "parallel"), 48),
        name="dft1",
    )(x5, m)


def _dft2_kernel(a_ref, f_ref, twc_ref, tws_ref, gf_ref, gi_ref, o_ref):
    C = a_ref.shape[-1]
    half = DFT_N2
    twc, tws = twc_ref[...], tws_ref[...]

    def lanes(fn):
        return jnp.concatenate([fn(slice(c0, c0 + V7X_LANES)) for c0 in range(0, C, V7X_LANES)], axis=-1)

    def spectrum(ref, col0):
        re = lanes(lambda sl: ref[:, 0, :, col0 + sl.start:col0 + sl.stop] * twc + ref[:, 1, :, col0 + sl.start:col0 + sl.stop] * tws)
        im = lanes(lambda sl: ref[:, 1, :, col0 + sl.start:col0 + sl.stop] * twc - ref[:, 0, :, col0 + sl.start:col0 + sl.stop] * tws)
        t = jnp.concatenate([re.reshape(half, C), im.reshape(half, C)], axis=0).astype(BF16)
        s = jnp.dot(gf_ref[...], t, preferred_element_type=F32)
        return s[:half], s[half:]

    xr, xi = spectrum(a_ref, 0)
    fr, fi = spectrum(f_ref, 0)
    br, bi = spectrum(f_ref, C)
    hr, hi = fr + br, fi - bi
    y = jnp.concatenate([xr * hr - xi * hi, xr * hi + xi * hr], axis=0).astype(BF16)
    b = jnp.dot(gi_ref[...], y, preferred_element_type=F32)
    br2 = b[:half].reshape(DFT_N2H, V7X_SUBLANES, C)
    bi2 = b[half:].reshape(DFT_N2H, V7X_SUBLANES, C)
    for c0 in range(0, C, V7X_LANES):
        sl = slice(c0, c0 + V7X_LANES)
        o_ref[:, 0, :, sl] = br2[:, :, sl] * twc - bi2[:, :, sl] * tws
        o_ref[:, 1, :, sl] = bi2[:, :, sl] * twc + br2[:, :, sl] * tws


def _dft2(a5, f5, k):
    n2h, n1, _, sub, C = a5.shape
    return pl.pallas_call(
        _dft2_kernel,
        grid=(n1,),
        in_specs=[
            pl.BlockSpec((n2h, None, 2, sub, C), lambda i: (0, i, 0, 0, 0)),
            pl.BlockSpec((n2h, None, 2, sub, 2 * C), lambda i: (0, i, 0, 0, 0)),
            pl.BlockSpec((None, n2h, sub, V7X_LANES), lambda i: (i, 0, 0, 0)),
            pl.BlockSpec((None, n2h, sub, V7X_LANES), lambda i: (i, 0, 0, 0)),
            pl.BlockSpec((2 * DFT_N2, 2 * DFT_N2), lambda i: (0, 0)),
            pl.BlockSpec((2 * DFT_N2, 2 * DFT_N2), lambda i: (0, 0)),
        ],
        out_specs=pl.BlockSpec((n2h, None, 2, sub, C), lambda i: (0, i, 0, 0, 0)),
        out_shape=jax.ShapeDtypeStruct(a5.shape, F32),
        compiler_params=_cparams(("parallel",), 32),
        name="dft2",
    )(a5, f5, k["twc"], k["tws"], k["g_fwd"], k["g_inv"])


def _dft3_kernel(b_ref, m_ref, x0_ref, vx_ref, sc_ref, d_ref, o_ref):
    C = b_ref.shape[-1]
    bs = b_ref[...].reshape(m_ref.shape[1], C).astype(BF16)
    y = jnp.dot(m_ref[...], bs, preferred_element_type=F32).reshape(o_ref.shape)
    o_ref[...] = x0_ref[...] * (y * sc_ref[...] + vx_ref[...] * d_ref[...])


def _dft3(b5, m3, x05, vx5, scale, d):
    n2h, n1, _, sub, C = b5.shape
    Bt, n1h = x05.shape[0], x05.shape[1]
    tok = pl.BlockSpec((Bt, n1h, None, sub, C), lambda h: (0, 0, h, 0, 0))
    vec = pl.BlockSpec((1, C), lambda h: (0, 0))
    return pl.pallas_call(
        _dft3_kernel,
        grid=(n2h,),
        in_specs=[
            pl.BlockSpec((None, n1, 2, sub, C), lambda h: (h, 0, 0, 0, 0)),
            pl.BlockSpec(m3.shape, lambda h: (0, 0)),
            tok, tok, vec, vec,
        ],
        out_specs=tok,
        out_shape=jax.ShapeDtypeStruct(x05.shape, F32),
        compiler_params=_cparams(("parallel",), 48),
        name="dft3",
    )(b5, m3, x05, vx5, scale, d)


def _hyena(u, conv_w, conv_b, w1, b1, w2, b2, w3, b3, freq, wout, d_skip, B, L):
    T = u.shape[0]
    C = HYENA_WIDTH
    k = _dft_constants(L)
    n1, n1h = k["n1"], k["n1h"]
    x0, vx = _hpre(u, conv_w, conv_b[None], B, L)
    hfb, asum = _filter_taps(L, w1, b1, w2, b2, w3, b3, freq, wout)
    a5 = _dft1(vx, k["m1c"], n1, cb=C)
    f5 = _dft1(hfb, k["m1r"], n1, cb=C)
    b5 = _dft2(a5, f5, k)
    scale = 1.0 / (asum * (2 * L))
    return _dft3(b5, k["m3"], x0, vx, scale, d_skip[None])


def _bf16_bits(x):
    a = pltpu.bitcast(x, jnp.uint32)
    return (a + jnp.uint32(0x7FFF) + ((a >> 16) & jnp.uint32(1))) >> 16


def _pack_pairs(x):
    half = x.shape[1] // 2
    return _bf16_bits(x[:, :half]) | (_bf16_bits(x[:, half:]) << 16)


def _unpack_pairs(w):
    lo = pltpu.bitcast(w << 16, F32)
    hi = pltpu.bitcast(w & jnp.uint32(0xFFFF0000), F32)
    return lo, hi


def _oproj_kernel(x_ref, g0_ref, b0_ref, at_ref, hy_ref, wo_ref, g1_ref, b1_ref, wrh_ref, wrl_ref,
                  h1_ref, h1p_ref, sc_ref):
    aw = at_ref.shape[1]
    h0 = _layer_norm_rows(x_ref[...], g0_ref[...], b0_ref[...])
    mixed = jnp.dot(at_ref[...], wo_ref[0:aw, :], preferred_element_type=F32)
    hy = hy_ref[...].reshape(x_ref.shape[0], hy_ref.shape[-1])
    mixed = mixed + jnp.dot(hy.astype(BF16), wo_ref[aw:, :], preferred_element_type=F32)
    h1 = _layer_norm_rows(ALPHA * h0 + mixed, g1_ref[...], b1_ref[...])
    h1_ref[...] = h1
    h1p_ref[...] = _pack_pairs(h1)
    hh = h1.astype(BF16)
    hl = (h1 - hh.astype(F32)).astype(BF16)
    dn = (((1,), (1,)), ((), ()))
    logits = lax.dot_general(wrh_ref[...], hh, dn, preferred_element_type=F32)
    logits = logits + (lax.dot_general(wrh_ref[...], hl, dn, preferred_element_type=F32)
                       + lax.dot_general(wrl_ref[...], hh, dn, preferred_element_type=F32))
    sc_ref[...] = 1.0 / (1.0 + jnp.exp(-logits))


def _oproj(x2, g0, b0, attn, hy, wo_bf, g1, b1, wr_t, tm=256):
    T, D = x2.shape
    E = wr_t.shape[0]
    wrh = wr_t.astype(BF16)
    wrl = (wr_t - wrh.astype(F32)).astype(BF16)
    row = lambda i: (i, 0)
    fixed = lambda i: (0, 0)
    hy4 = hy.reshape((-1,) + hy.shape[2:])
    return pl.pallas_call(
        _oproj_kernel,
        grid=(T // tm,),
        in_specs=[
            pl.BlockSpec((tm, D), row), pl.BlockSpec((1, D), fixed), pl.BlockSpec((1, D), fixed),
            pl.BlockSpec((tm, attn.shape[1]), row),
            pl.BlockSpec((tm // DFT_N2,) + hy4.shape[1:], lambda i: (i, 0, 0, 0)),
            pl.BlockSpec((D, D), fixed), pl.BlockSpec((1, D), fixed), pl.BlockSpec((1, D), fixed),
            pl.BlockSpec((E, D), fixed), pl.BlockSpec((E, D), fixed),
        ],
        out_specs=[pl.BlockSpec((tm, D), row), pl.BlockSpec((tm, D // 2), row), pl.BlockSpec((E, tm), lambda i: (0, i))],
        out_shape=[
            jax.ShapeDtypeStruct((T, D), F32),
            jax.ShapeDtypeStruct((T, D // 2), jnp.uint32),
            jax.ShapeDtypeStruct((E, T), F32),
        ],
        compiler_params=_cparams(("parallel",), 48),
        name="oproj",
    )(x2, g0, b0, attn, hy4, wo_bf, g1, b1, wrh, wrl)


def _route_kernel(sc_ref, bias_ref, tri_ref, idx_ref, gate_ref, rank_ref, cnt_ref, carry_ref):
    step = pl.program_id(0)
    E, tm = sc_ref.shape
    neg = jnp.float32(-jnp.inf)
    scores = sc_ref[...]
    biased = scores + bias_ref[...]
    erow = lax.broadcasted_iota(jnp.int32, (E, tm), 0)
    big = jnp.int32(E)

    def first_argmax(vals, rows):
        m = jnp.max(vals, axis=0, keepdims=True)
        pick = jnp.min(jnp.where(vals == m, rows, big), axis=0, keepdims=True)
        return m, pick

    gsc = []
    for g in range(N_GROUPS):
        blk = biased[g * GROUP_SIZE:(g + 1) * GROUP_SIZE, :]
        rows = erow[g * GROUP_SIZE:(g + 1) * GROUP_SIZE, :]
        m1, p1 = first_argmax(blk, rows)
        m2 = jnp.max(jnp.where(rows == p1, neg, blk), axis=0, keepdims=True)
        gsc.append(m1 + m2)
    gsc = jnp.concatenate(gsc, axis=0)
    grow = lax.broadcasted_iota(jnp.int32, (N_GROUPS, tm), 0)
    gsel = jnp.zeros((N_GROUPS, tm), jnp.bool_)
    work = gsc
    for _ in range(TOPK_GROUPS):
        _, p = first_argmax(work, grow)
        hit = grow == p
        gsel = gsel | hit
        work = jnp.where(hit, neg, work)
    emask = jnp.concatenate(
        [jnp.broadcast_to(gsel[g:g + 1, :], (GROUP_SIZE, tm)) for g in range(N_GROUPS)], axis=0)
    work = jnp.where(emask, biased, neg)

    sel = jnp.zeros((E, tm), jnp.bool_)
    picks, gvals = [], []
    for _ in range(TOP_K):
        _, p = first_argmax(work, erow)
        hit = erow == p
        sel = sel | hit
        picks.append(p)
        gvals.append(jnp.sum(jnp.where(hit, scores, 0.0), axis=0, keepdims=True))
        work = jnp.where(hit, neg, work)
    gv = jnp.concatenate(gvals, axis=0)
    idx_ref[...] = jnp.concatenate(picks, axis=0)
    gate_ref[...] = gv / jnp.sum(gv, axis=0, keepdims=True) * ROUTED_SCALE

    @pl.when(step == 0)
    def _():
        carry_ref[...] = jnp.zeros_like(carry_ref)

    chosen = sel.astype(F32)
    before = jnp.dot(chosen.astype(BF16), tri_ref[...], preferred_element_type=F32) + carry_ref[...]
    rank_ref[...] = jnp.concatenate(
        [jnp.sum(jnp.where(erow == p, before, 0.0), axis=0, keepdims=True) for p in picks], axis=0).astype(jnp.int32)
    carry_ref[...] += jnp.sum(chosen, axis=1, keepdims=True)
    cnt_ref[...] = carry_ref[...].astype(jnp.int32)


def _route(scores_t, bias, tm=128):
    E, T = scores_t.shape
    tri = (jnp.arange(tm)[:, None] < jnp.arange(tm)[None, :]).astype(BF16)
    tok = lambda i: (0, i)
    fixed = lambda i: (0, 0)
    return pl.pallas_call(
        _route_kernel,
        grid=(T // tm,),
        in_specs=[pl.BlockSpec((E, tm), tok), pl.BlockSpec((E, 1), fixed), pl.BlockSpec((tm, tm), fixed)],
        out_specs=[pl.BlockSpec((TOP_K, tm), tok), pl.BlockSpec((TOP_K, tm), tok), pl.BlockSpec((TOP_K, tm), tok),
                   pl.BlockSpec((E, 1), fixed)],
        out_shape=[
            jax.ShapeDtypeStruct((TOP_K, T), jnp.int32),
            jax.ShapeDtypeStruct((TOP_K, T), F32),
            jax.ShapeDtypeStruct((TOP_K, T), jnp.int32),
            jax.ShapeDtypeStruct((E, 1), jnp.int32),
        ],
        scratch_shapes=[pltpu.VMEM((E, 1), F32)],
        compiler_params=_cparams(("arbitrary",), 32),
        name="route",
    )(scores_t, bias, tri)


def _dest_kernel(idx_ref, rank_ref, ps_ref, dest_ref):
    E = ps_ref.shape[0]
    tm = idx_ref.shape[1]
    erow = lax.broadcasted_iota(jnp.int32, (E, tm), 0)
    ps = ps_ref[...].astype(F32)
    rows = [jnp.sum(jnp.where(erow == idx_ref[k:k + 1, :], ps, 0.0), axis=0, keepdims=True) for k in range(TOP_K)]
    dest_ref[...] = jnp.concatenate(rows, axis=0).astype(jnp.int32) + rank_ref[...]


def _dest_rows(idx, rank, pad_start, tm=512):
    K, T = idx.shape
    E = pad_start.shape[0]
    tok = lambda i: (0, i)
    return pl.pallas_call(
        _dest_kernel,
        grid=(T // tm,),
        in_specs=[pl.BlockSpec((K, tm), tok), pl.BlockSpec((K, tm), tok), pl.BlockSpec((E, 1), lambda i: (0, 0))],
        out_specs=pl.BlockSpec((K, tm), tok),
        out_shape=jax.ShapeDtypeStruct((K, T), jnp.int32),
        compiler_params=_cparams(("arbitrary",), 32),
        name="dest",
    )(idx, rank, pad_start[:, None])


def _scatter_kernel(ps_ref, pe_ref, dest_ref, h_ref, xs_ref, zbuf, sem, *, rb):
    tm = h_ref.shape[0]

    @pl.when(pl.program_id(0) == 0)
    def _():
        zbuf[...] = jnp.zeros_like(zbuf)

        def zcopy(e):
            return pltpu.make_async_copy(zbuf, xs_ref.at[pl.ds(pl.multiple_of(pe_ref[e] - rb, rb), rb)], sem)

        def zstart(e, c):
            @pl.when(pe_ref[e] > ps_ref[e])
            def _():
                zcopy(e).start()
            return c

        def zwait(e, c):
            @pl.when(pe_ref[e] > ps_ref[e])
            def _():
                zcopy(e).wait()
            return c

        lax.fori_loop(0, ps_ref.shape[0], zstart, 0)
        lax.fori_loop(0, ps_ref.shape[0], zwait, 0)

    sub = V7X_SUBLANES

    def issue(r8, c):
        for j in range(sub):
            for k in range(TOP_K):
                dst = dest_ref[0, r8 * (sub * TOP_K) + (j * TOP_K + k)]
                pltpu.make_async_copy(h_ref.at[pl.ds(r8 * sub + j, 1)], xs_ref.at[pl.ds(dst, 1)], sem).start()
        return c

    lax.fori_loop(0, tm // sub, issue, 0)
    for k in range(TOP_K):
        pltpu.make_async_copy(h_ref, xs_ref.at[pl.ds(0, tm)], sem).wait()


def _dest_tiles(dest, tm):
    K, T = dest.shape
    return dest.T.reshape(T // tm, 1, tm * K)


def _scatter_rows(pad_start, pad_end, dest, h1p, n_rows, rb, tm=256):
    T, W = h1p.shape
    return pl.pallas_call(
        functools.partial(_scatter_kernel, rb=rb),
        grid_spec=pltpu.PrefetchScalarGridSpec(
            num_scalar_prefetch=2,
            grid=(T // tm,),
            in_specs=[
                pl.BlockSpec((None, 1, tm * TOP_K), lambda i, ps, pe: (i, 0, 0), memory_space=pltpu.SMEM),
                pl.BlockSpec((tm, W), lambda i, ps, pe: (i, 0)),
            ],
            out_specs=pl.BlockSpec(memory_space=pl.ANY),
            scratch_shapes=[pltpu.VMEM((rb, W), h1p.dtype), pltpu.SemaphoreType.DMA],
        ),
        out_shape=jax.ShapeDtypeStruct((n_rows, W), h1p.dtype),
        compiler_params=_cparams(("arbitrary",), 32),
        name="scatter",
    )(pad_start, pad_end, _dest_tiles(dest, tm), h1p)


def _experts_kernel(be_ref, nu_ref, x_ref, wg_ref, wu_ref, wd_ref, y_ref, wgub, wdb):
    b = pl.program_id(0)
    e = be_ref[b]
    changed = jnp.logical_or(b == 0, e != be_ref[jnp.maximum(b - 1, 0)])

    F = wg_ref.shape[1]

    @pl.when(jnp.logical_and(changed, b < nu_ref[0]))
    def _():
        wgub[:, 0:F] = wg_ref[...].astype(BF16)
        wgub[:, F:] = wu_ref[...].astype(BF16)
        wdb[...] = wd_ref[...].astype(BF16)

    @pl.when(b < nu_ref[0])
    def _():
        lo, hi = _unpack_pairs(x_ref[...])
        x = jnp.concatenate([lo, hi], axis=1).astype(BF16)
        gu = jnp.dot(x, wgub[...], preferred_element_type=F32)
        g, u = gu[:, 0:F], gu[:, F:]
        hb = (g / (1.0 + jnp.exp(-g)) * u).astype(BF16)
        y_ref[...] = _pack_pairs(jnp.dot(hb, wdb[...], preferred_element_type=F32))

    @pl.when(b >= nu_ref[0])
    def _():
        y_ref[...] = jnp.zeros_like(y_ref)


def _experts(block_expert, n_used, xs, w_gate, w_up, w_down, rb):
    P, W = xs.shape
    E, D, F = w_gate.shape
    nb = P // rb
    live = lambda b, be, nu: (jnp.minimum(b, nu[0] - 1), 0)
    return pl.pallas_call(
        _experts_kernel,
        grid_spec=pltpu.PrefetchScalarGridSpec(
            num_scalar_prefetch=2,
            grid=(nb,),
            in_specs=[
                pl.BlockSpec((rb, W), live),
                pl.BlockSpec((None, D, F), lambda b, be, nu: (be[b], 0, 0)),
                pl.BlockSpec((None, D, F), lambda b, be, nu: (be[b], 0, 0)),
                pl.BlockSpec((None, F, D), lambda b, be, nu: (be[b], 0, 0)),
            ],
            out_specs=pl.BlockSpec((rb, W), lambda b, be, nu: (b, 0)),
            scratch_shapes=[pltpu.VMEM((D, 2 * F), BF16), pltpu.VMEM((F, D), BF16)],
        ),
        out_shape=jax.ShapeDtypeStruct((P, W), jnp.uint32),
        compiler_params=_cparams(("arbitrary",), 48),
        name="experts",
    )(block_expert, n_used, xs, w_gate, w_up, w_down)


def _combine_kernel(dest_ref, destn_ref, h1_ref, gate_ref, ys_ref, sg_ref, su_ref, sd_ref,
                    g2_ref, b2_ref, o_ref, buf, sem):
    i = pl.program_id(0)
    n = pl.num_programs(0)
    tm = h1_ref.shape[0]
    slot = i % 2

    sub = V7X_SUBLANES

    def issue(dref, s):
        def body(r8, c):
            for j in range(sub):
                for k in range(TOP_K):
                    src = dref[0, r8 * (sub * TOP_K) + (j * TOP_K + k)]
                    pltpu.make_async_copy(ys_ref.at[pl.ds(src, 1)], buf.at[s, k, pl.ds(r8 * sub + j, 1)],
                                          sem.at[s]).start()
            return c
        lax.fori_loop(0, tm // sub, body, 0)

    @pl.when(i == 0)
    def _():
        issue(dest_ref, 0)

    @pl.when(i + 1 < n)
    def _():
        issue(destn_ref, 1 - slot)

    h1 = h1_ref[...]
    hb = h1.astype(BF16)
    g = jnp.dot(hb, sg_ref[...], preferred_element_type=F32)
    u = jnp.dot(hb, su_ref[...], preferred_element_type=F32)
    ffn = jnp.dot((g / (1.0 + jnp.exp(-g)) * u).astype(BF16), sd_ref[...], preferred_element_type=F32)

    for k in range(TOP_K):
        pltpu.make_async_copy(ys_ref.at[pl.ds(0, tm)], buf.at[slot, k], sem.at[slot]).wait()
    gates = gate_ref[...]
    lo_acc = jnp.zeros((tm, buf.shape[3]), F32)
    hi_acc = jnp.zeros((tm, buf.shape[3]), F32)
    for k in range(TOP_K):
        lo, hi = _unpack_pairs(buf[slot, k])
        gk = gates[:, k:k + 1]
        lo_acc = lo_acc + gk * lo
        hi_acc = hi_acc + gk * hi
    ffn = ffn + jnp.concatenate([lo_acc, hi_acc], axis=1)
    o_ref[...] = _layer_norm_rows(ALPHA * h1 + ffn, g2_ref[...], b2_ref[...])


def _combine(dest, h1, gates_tk, ys, sg, su, sd, g2, b2, tm=128):
    T, D = h1.shape
    W = ys.shape[1]
    F = sg.shape[1]
    nt = T // tm
    row = lambda i: (i, 0)
    fixed = lambda i: (0, 0)
    smem = functools.partial(pl.BlockSpec, (None, 1, tm * TOP_K), memory_space=pltpu.SMEM)
    dest_t = _dest_tiles(dest, tm)
    return pl.pallas_call(
        _combine_kernel,
        grid=(nt,),
        in_specs=[
            smem(index_map=lambda i: (i, 0, 0)),
            smem(index_map=lambda i: (jnp.minimum(i + 1, nt - 1), 0, 0)),
            pl.BlockSpec((tm, D), row),
            pl.BlockSpec((tm, TOP_K), row),
            pl.BlockSpec(memory_space=pl.ANY),
            pl.BlockSpec((D, F), fixed), pl.BlockSpec((D, F), fixed), pl.BlockSpec((F, D), fixed),
            pl.BlockSpec((1, D), fixed), pl.BlockSpec((1, D), fixed),
        ],
        out_specs=pl.BlockSpec((tm, D), row),
        out_shape=jax.ShapeDtypeStruct((T, D), F32),
        scratch_shapes=[pltpu.VMEM((2, TOP_K, tm, W), jnp.uint32), pltpu.SemaphoreType.DMA((2,))],
        compiler_params=_cparams(("arbitrary",), 48),
        name="combine",
    )(dest_t, dest_t, h1, gates_tk, ys, sg, su, sd, g2, b2)


MOE_ROWS = 256


def _moe(h1, h1p, scores_t, router_bias, w_gate, w_up, w_down, ws_gate, ws_up, ws_down, g2, b2):
    T = h1.shape[0]
    E = N_EXPERTS
    rb = MOE_ROWS
    idx, gates, rank, counts = _route(scores_t, router_bias[:, None])
    counts = counts[:, 0]
    padded = (counts + rb - 1) // rb * rb
    pad_end = jnp.cumsum(padded)
    pad_start = pad_end - padded
    nb = (T * TOP_K) // rb + E
    starts = jnp.arange(nb, dtype=jnp.int32) * rb
    block_expert = jnp.minimum(jnp.sum((pad_end[None, :] <= starts[:, None]).astype(jnp.int32), axis=1), E - 1)
    n_used = (pad_end[-1:] // rb).astype(jnp.int32)
    dest = _dest_rows(idx, rank, pad_start)
    xs = _scatter_rows(pad_start, pad_end, dest, h1p, nb * rb, rb)
    ys = _experts(block_expert, n_used, xs, w_gate, w_up, w_down, rb)
    return _combine(dest, h1, gates.T, ys, ws_gate.astype(BF16), ws_up.astype(BF16), ws_down.astype(BF16), g2, b2)


def kernel(x, positions, emb_ln_g, emb_ln_b, w_in, hy_conv_w, hy_conv_b, hy_f_w1, hy_f_b1, hy_f_w2, hy_f_b2, hy_f_w3, hy_f_b3, hy_f_freq, hy_f_wout, hy_d, lambda_q1, lambda_k1, lambda_q2, lambda_k2, subln_g, w_o, ln1_g, ln1_b, w_router, router_bias, w_gate, w_up, w_down, ws_gate, ws_up, ws_down, ln2_g, ln2_b):
    B, L, D = x.shape
    T = B * L
    assert w_in.shape[0] == DEPTH == 1
    i = 0
    x2 = x.reshape(T, D)
    g0, b0 = emb_ln_g[None], emb_ln_b[None]
    ra, rm, rp = _rotary_tables(positions)
    q, k, v, u = _inproj(x2, g0, b0, w_in[i].astype(BF16), ra, rm, rp)
    lam = (jnp.exp(jnp.sum(lambda_q1[i] * lambda_k1[i])) - jnp.exp(jnp.sum(lambda_q2[i] * lambda_k2[i])) + LAM_INIT)
    attn = _attention(q, k, v, lam.reshape(1).astype(F32), subln_g[i][None], B, L)
    hy = _hyena(u, hy_conv_w[i], hy_conv_b[i], hy_f_w1[i], hy_f_b1[i], hy_f_w2[i], hy_f_b2[i], hy_f_w3[i], hy_f_b3[i],
                hy_f_freq[i], hy_f_wout[i], hy_d[i], B, L)
    h1, h1p, scores_t = _oproj(x2, g0, b0, attn, hy, w_o[i].astype(BF16), ln1_g[i][None], ln1_b[i][None], w_router[i].T)
    out = _moe(h1, h1p, scores_t, router_bias[i], w_gate[i], w_up[i], w_down[i], ws_gate[i], ws_up[i], ws_down[i],
               ln2_g[i][None], ln2_b[i][None])
    return out.reshape(B, L, D)
```

```python
import functools
import math

import numpy as np
import jax
import jax.numpy as jnp
from jax import lax
from jax.experimental import pallas as pl
from jax.experimental.pallas import tpu as pltpu

DA_HEADS = 4
DA_HEAD_DIM = 64
DA_V_DIM = 128
ATTN_WIDTH = 512
HYENA_WIDTH = 512
ROT_DIM = 16
ROPE_THETA = 500000.0
SHORT_CONV = 3
FILTER_EMB = 33
FILTER_BANDS = 16
DECAY_TARGET = 1e-2
FAST_DECAY = 0.3
SLOW_DECAY = 1.5
N_EXPERTS = 256
TOP_K = 8
N_GROUPS = 8
GROUP_SIZE = N_EXPERTS // N_GROUPS
TOPK_GROUPS = 4
EXPERT_DIM = 256
ROUTED_SCALE = 2.5
DEPTH = 1
ALPHA = (2 * DEPTH) ** 0.25
LN_EPS = 1e-5
LAM_INIT = 0.8 - 0.6 * math.exp(-0.3 * 0)

V7X_LANES = 128
V7X_SUBLANES = 8
V7X_VMEM_BYTES = 64 * 1024 * 1024

DFT_N2 = 128
DFT_N2H = DFT_N2 // V7X_SUBLANES

BF16 = jnp.bfloat16
F32 = jnp.float32


def _cparams(sem, vmem_mb):
    return pltpu.CompilerParams(dimension_semantics=sem, vmem_limit_bytes=vmem_mb * 1024 * 1024)


def _layer_norm_rows(x, g, b):
    mu = jnp.mean(x, axis=-1, keepdims=True)
    xc = x - mu
    var = jnp.mean(xc * xc, axis=-1, keepdims=True)
    return xc * lax.rsqrt(var + LN_EPS) * g + b


def _inproj_kernel(x_ref, g_ref, b_ref, w_ref, ra_ref, rm_ref, rp_ref, q_ref, k_ref, v_ref, u_ref):
    h = _layer_norm_rows(x_ref[...], g_ref[...], b_ref[...]).astype(BF16)
    ra, rm, rp = ra_ref[...], rm_ref[...], rp_ref[...]

    def rot(t):
        return t * ra + pltpu.roll(t, V7X_LANES - ROT_DIM // 2, axis=1) * rm + pltpu.roll(t, ROT_DIM // 2, axis=1) * rp

    aw = ATTN_WIDTH
    qp = jnp.dot(h, w_ref[:, 0:aw], preferred_element_type=F32)
    kp = jnp.dot(h, w_ref[:, aw:2 * aw], preferred_element_type=F32)
    scale = DA_HEAD_DIM ** -0.5 * math.log2(math.e)
    for c in range(aw // V7X_LANES):
        sl = slice(c * V7X_LANES, (c + 1) * V7X_LANES)
        q_ref[:, sl] = (rot(qp[:, sl]) * scale).astype(BF16)
        k_ref[:, sl] = rot(kp[:, sl]).astype(BF16)
    vp = jnp.dot(h, w_ref[:, 2 * aw:3 * aw], preferred_element_type=F32).astype(BF16)
    ones = jnp.ones((vp.shape[0], DA_V_DIM), BF16)
    for hd in range(DA_HEADS):
        v_ref[:, 2 * hd * DA_V_DIM:(2 * hd + 1) * DA_V_DIM] = vp[:, hd * DA_V_DIM:(hd + 1) * DA_V_DIM]
        v_ref[:, (2 * hd + 1) * DA_V_DIM:(2 * hd + 2) * DA_V_DIM] = ones
    u_ref[...] = jnp.dot(h, w_ref[:, 3 * aw:], preferred_element_type=F32)


def _inproj(x2, g, b, w_bf, ra, rm, rp, tm=256):
    T, D = x2.shape
    ncol = w_bf.shape[1]
    aw = ATTN_WIDTH
    uw = ncol - 3 * aw
    row = lambda i: (i, 0)
    fixed = lambda i: (0, 0)
    return pl.pallas_call(
        _inproj_kernel,
        grid=(T // tm,),
        in_specs=[
            pl.BlockSpec((tm, D), row),
            pl.BlockSpec((1, D), fixed),
            pl.BlockSpec((1, D), fixed),
            pl.BlockSpec((D, ncol), fixed),
            pl.BlockSpec((tm, V7X_LANES), row),
            pl.BlockSpec((tm, V7X_LANES), row),
            pl.BlockSpec((tm, V7X_LANES), row),
        ],
        out_specs=[
            pl.BlockSpec((tm, aw), row),
            pl.BlockSpec((tm, aw), row),
            pl.BlockSpec((tm, 2 * aw), row),
            pl.BlockSpec((tm, uw), row),
        ],
        out_shape=[
            jax.ShapeDtypeStruct((T, aw), BF16),
            jax.ShapeDtypeStruct((T, aw), BF16),
            jax.ShapeDtypeStruct((T, 2 * aw), BF16),
            jax.ShapeDtypeStruct((T, uw), F32),
        ],
        compiler_params=_cparams(("parallel",), 48),
        name="inproj",
    )(x2, g, b, w_bf, ra, rm, rp)


def _rotary_tables(positions):
    half = ROT_DIM // 2
    inv_freq = ROPE_THETA ** (-jnp.arange(0, ROT_DIM, 2, dtype=F32) / ROT_DIM)
    ang = positions.astype(F32).reshape(-1)[:, None] * inv_freq
    cos, sin = jnp.cos(ang), jnp.sin(ang)
    T = ang.shape[0]
    ones = jnp.ones((T, DA_HEAD_DIM - ROT_DIM), F32)
    zeros_h = jnp.zeros((T, half), F32)
    zeros_r = jnp.zeros((T, DA_HEAD_DIM - ROT_DIM), F32)
    a64 = jnp.concatenate([cos, cos, ones], axis=1)
    m64 = jnp.concatenate([-sin, zeros_h, zeros_r], axis=1)
    p64 = jnp.concatenate([zeros_h, sin, zeros_r], axis=1)
    rep = V7X_LANES // DA_HEAD_DIM
    return jnp.tile(a64, (1, rep)), jnp.tile(m64, (1, rep)), jnp.tile(p64, (1, rep))


def _attn_kernel(lam_ref, q_ref, k_ref, v_ref, g_ref, o_ref, s_ref, m_ref, acc_ref, *, kb, unroll):
    qb = q_ref.shape[0]
    L = k_ref.shape[0]
    nchunk = L // kb
    nl = kb // V7X_LANES
    q = q_ref[...]
    lane = lax.broadcasted_iota(jnp.int32, q.shape, 1)
    zero = jnp.zeros_like(q)
    qm = [jnp.where(lane < DA_HEAD_DIM, q, zero), jnp.where(lane >= DA_HEAD_DIM, q, zero)]
    m_ref[...] = jnp.full(m_ref.shape, -jnp.inf, F32)
    acc_ref[...] = jnp.zeros(acc_ref.shape, F32)

    def score_body(j, carry):
        kc = k_ref[pl.ds(pl.multiple_of(j * kb, kb), kb), :]
        for c in range(2):
            s = lax.dot_general(qm[c], kc, (((1,), (1,)), ((), ())), preferred_element_type=F32)
            s_ref[c, j] = s
            m = s[:, 0:V7X_LANES]
            for t in range(1, nl):
                m = jnp.maximum(m, s[:, t * V7X_LANES:(t + 1) * V7X_LANES])
            m_ref[c] = jnp.maximum(m_ref[c], m)
        return carry

    lax.fori_loop(0, nchunk, score_body, 0, unroll=unroll)
    m_row = [jnp.max(m_ref[c], axis=1, keepdims=True) for c in range(2)]

    def pv_body(j, carry):
        vc = v_ref[pl.ds(pl.multiple_of(j * kb, kb), kb), :]
        for c in range(2):
            p = jnp.exp2(s_ref[c, j] - m_row[c])
            acc_ref[c] += jnp.dot(p.astype(BF16), vc, preferred_element_type=F32)
        return carry

    lax.fori_loop(0, nchunk, pv_body, 0, unroll=unroll)
    outs = [acc_ref[c, :, 0:DA_V_DIM] / acc_ref[c, :, DA_V_DIM:2 * DA_V_DIM] for c in range(2)]
    o = outs[0] - lam_ref[0] * outs[1]
    ms = jnp.mean(o * o, axis=1, keepdims=True)
    o_ref[...] = (o * lax.rsqrt(ms + LN_EPS) * g_ref[...] * (1.0 - LAM_INIT)).astype(o_ref.dtype)


def _attention(q, k, v, lam, subln_g, B, L, qb=512, kb=512, unroll=4):
    T = q.shape[0]
    nq = L // qb
    return pl.pallas_call(
        functools.partial(_attn_kernel, kb=kb, unroll=unroll),
        grid=(B, DA_HEADS, nq),
        in_specs=[
            pl.BlockSpec(memory_space=pltpu.SMEM),
            pl.BlockSpec((qb, DA_V_DIM), lambda b, h, i: (b * nq + i, h)),
            pl.BlockSpec((L, DA_V_DIM), lambda b, h, i: (b, h)),
            pl.BlockSpec((L, 2 * DA_V_DIM), lambda b, h, i: (b, h)),
            pl.BlockSpec((1, DA_V_DIM), lambda b, h, i: (0, 0)),
        ],
        out_specs=pl.BlockSpec((qb, DA_V_DIM), lambda b, h, i: (b * nq + i, h)),
        out_shape=jax.ShapeDtypeStruct((T, ATTN_WIDTH), BF16),
        scratch_shapes=[
            pltpu.VMEM((2, L // kb, qb, kb), F32),
            pltpu.VMEM((2, qb, V7X_LANES), F32),
            pltpu.VMEM((2, qb, 2 * DA_V_DIM), F32),
        ],
        compiler_params=_cparams(("parallel", "parallel", "parallel"), 56),
        name="attn",
    )(lam, q, k, v, subln_g)


def _hpre_kernel(u_ref, up_ref, un_ref, w_ref, b_ref, x0_ref, vx_ref):
    i = pl.program_id(1)
    n = pl.num_programs(1)
    tb = u_ref.shape[0]
    C = x0_ref.shape[-1]
    row = lax.broadcasted_iota(jnp.int32, (tb, V7X_LANES), 0)
    has_prev = (i > 0).astype(F32)
    has_next = (i < n - 1).astype(F32)

    def conv(c0):
        sl = slice(c0, c0 + V7X_LANES)
        u = u_ref[:, sl]
        prev_row = up_ref[V7X_SUBLANES - 1:V7X_SUBLANES, sl] * has_prev
        next_row = un_ref[0:1, sl] * has_next
        u_prev = jnp.where(row == 0, prev_row, pltpu.roll(u, 1, axis=0))
        u_next = jnp.where(row == tb - 1, next_row, pltpu.roll(u, tb - 1, axis=0))
        return u_prev * w_ref[0:1, sl] + u * w_ref[1:2, sl] + u_next * w_ref[2:3, sl] + b_ref[:, sl]

    tile = x0_ref.shape[:-1] + (V7X_LANES,)
    for c in range(C // V7X_LANES):
        c0 = c * V7X_LANES
        x0_ref[:, :, :, c0:c0 + V7X_LANES] = conv(c0).reshape(tile)
        vx_ref[:, :, :, c0:c0 + V7X_LANES] = (conv(2 * C + c0) * conv(C + c0)).reshape(tile)


def _hpre(u, conv_w, conv_b, B, L, tb=512):
    T, C3 = u.shape
    C = C3 // 3
    nt = L // tb
    sub = V7X_SUBLANES
    na = tb // DFT_N2
    split = pl.BlockSpec((None, na, DFT_N2H, sub, C), lambda b, i: (b, i, 0, 0, 0))
    split_shape = jax.ShapeDtypeStruct((B, L // DFT_N2, DFT_N2H, sub, C), F32)
    cur = lambda b, i: (b * nt + i, 0)
    prev = lambda b, i: (jnp.maximum((b * L + i * tb) // sub - 1, 0), 0)
    nxt = lambda b, i: (jnp.minimum((b * L + (i + 1) * tb) // sub, T // sub - 1), 0)
    fixed = lambda b, i: (0, 0)
    return pl.pallas_call(
        _hpre_kernel,
        grid=(B, nt),
        in_specs=[
            pl.BlockSpec((tb, C3), cur),
            pl.BlockSpec((sub, C3), prev),
            pl.BlockSpec((sub, C3), nxt),
            pl.BlockSpec((SHORT_CONV, C3), fixed),
            pl.BlockSpec((1, C3), fixed),
        ],
        out_specs=[split, split],
        out_shape=[split_shape, split_shape],
        compiler_params=_cparams(("parallel", "parallel"), 32),
        name="hpre",
    )(u, u, u, conv_w, conv_b)


def _filt_kernel(z_ref, w1_ref, b1_ref, w2_ref, b2_ref, w3_ref, b3_ref, fr_ref, wo_ref, dl_ref,
                 hfb_ref, asum_ref, *, L):
    i = pl.program_id(0)
    tl = z_ref.shape[0]
    C = dl_ref.shape[1]
    hp = lax.Precision.HIGHEST
    h = jnp.sin(fr_ref[0:1, :] * (jnp.dot(z_ref[...], w1_ref[...], precision=hp, preferred_element_type=F32) + b1_ref[...]))
    h = jnp.sin(fr_ref[1:2, :] * (jnp.dot(h, w2_ref[...], precision=hp, preferred_element_type=F32) + b2_ref[...]))
    h = jnp.sin(fr_ref[2:3, :] * (jnp.dot(h, w3_ref[...], precision=hp, preferred_element_type=F32) + b3_ref[...]))
    o = jnp.dot(h, wo_ref[...], precision=hp, preferred_element_type=F32)
    grow = lax.broadcasted_iota(jnp.int32, (tl, C), 0) + i * tl
    t = grow.astype(F32) * (1.0 / (L - 1))
    decay = jnp.exp(-t * dl_ref[...])
    hf = o[:, :C] * decay
    hb = jnp.where(grow == 0, 0.0, o[:, C:] * decay)
    tile = hfb_ref.shape[:-1] + (C,)
    hfb_ref[:, :, :, :C] = hf.reshape(tile)
    hfb_ref[:, :, :, C:] = hb.reshape(tile)

    @pl.when(i == 0)
    def _():
        asum_ref[...] = jnp.zeros_like(asum_ref)

    asum_ref[...] += jnp.sum(jnp.abs(hf) + jnp.abs(hb), axis=0, keepdims=True)


def _filter_taps(L, w1, b1, w2, b2, w3, b3, freq, wout, tl=512):
    C = wout.shape[1] // 2
    order = w1.shape[1]
    emb = w1.shape[0]
    t = jnp.linspace(0.0, 1.0, L, dtype=F32)[:, None]
    w = 2.0 * math.pi * jnp.arange(L, dtype=F32)[:, None] / L
    f = jnp.linspace(1e-4, FILTER_BANDS - 1, FILTER_BANDS, dtype=F32)[None, :]
    z = jnp.concatenate([t, jnp.cos(f * w), -jnp.sin(f * w)], axis=-1)
    zp = jnp.pad(z, ((0, 0), (0, V7X_LANES - emb)))
    w1p = jnp.pad(w1, ((0, V7X_LANES - emb), (0, 0)))
    deltas = jnp.abs(jnp.linspace(math.log(DECAY_TARGET) / SLOW_DECAY, math.log(DECAY_TARGET) / FAST_DECAY, C, dtype=F32))[None]
    fixed = lambda i: (0, 0)
    return pl.pallas_call(
        functools.partial(_filt_kernel, L=L),
        grid=(L // tl,),
        in_specs=[
            pl.BlockSpec((tl, V7X_LANES), lambda i: (i, 0)),
            pl.BlockSpec((V7X_LANES, order), fixed), pl.BlockSpec((1, order), fixed),
            pl.BlockSpec((order, order), fixed), pl.BlockSpec((1, order), fixed),
            pl.BlockSpec((order, order), fixed), pl.BlockSpec((1, order), fixed),
            pl.BlockSpec((3, order), fixed),
            pl.BlockSpec((order, 2 * C), fixed),
            pl.BlockSpec((1, C), fixed),
        ],
        out_specs=[pl.BlockSpec((None, tl // DFT_N2, DFT_N2H, V7X_SUBLANES, 2 * C), lambda i: (0, i, 0, 0, 0)),
                   pl.BlockSpec((1, C), fixed)],
        out_shape=[jax.ShapeDtypeStruct((1, L // DFT_N2, DFT_N2H, V7X_SUBLANES, 2 * C), F32),
                   jax.ShapeDtypeStruct((1, C), F32)],
        compiler_params=_cparams(("arbitrary",), 32),
        name="filt",
    )(zp, w1p, b1[None], w2, b2[None], w3, b3[None], freq, wout, deltas)


def _dft_constants(L):
    n1 = 2 * L // DFT_N2
    n1h = n1 // 2
    N = 2 * L
    sub = V7X_SUBLANES

    def cs(num, den):
        ang = (num % den).astype(np.float64) * (2.0 * np.pi / den)
        return np.cos(ang), np.sin(ang)

    k1 = np.arange(n1, dtype=np.int64)
    a = np.arange(n1h, dtype=np.int64)
    c, s = cs(k1[:, None] * a[None, :], n1)
    eye = np.eye(sub)

    def expand(m):
        r, kk = m.shape
        return (m[:, None, :, None] * eye[None, :, None, :]).reshape(r * sub, kk * sub)

    def const(m):
        return jnp.asarray(m.astype(np.float32).astype(BF16))

    m1c = np.stack([np.concatenate([c, s], axis=1), np.concatenate([-s, c], axis=1)], axis=1).reshape(2 * n1, 2 * n1h)
    m1r = np.stack([c, -s], axis=1).reshape(2 * n1, n1h)
    ct, st = c.T, s.T
    m3 = np.stack([np.stack([ct, -st], axis=2).reshape(n1h, 2 * n1),
                   np.stack([st, ct], axis=2).reshape(n1h, 2 * n1)], axis=0).reshape(2 * n1h, 2 * n1)
    n2 = np.arange(DFT_N2, dtype=np.int64)
    gc, gs = cs(n2[:, None] * n2[None, :], DFT_N2)
    g_fwd = np.concatenate([np.concatenate([gc, gs], axis=1), np.concatenate([-gs, gc], axis=1)], axis=0)
    g_inv = np.concatenate([np.concatenate([gc, -gs], axis=1), np.concatenate([gs, gc], axis=1)], axis=0)
    ph = (jnp.arange(n1, dtype=jnp.int32)[:, None] * jnp.arange(DFT_N2, dtype=jnp.int32)[None, :]) % N
    ang = ph.astype(F32) * (2.0 * math.pi / N)
    lanes = (n1, DFT_N2H, sub, V7X_LANES)
    twc = jnp.broadcast_to(jnp.cos(ang).reshape(n1, DFT_N2H, sub, 1), lanes)
    tws = jnp.broadcast_to(jnp.sin(ang).reshape(n1, DFT_N2H, sub, 1), lanes)
    return dict(n1=n1, n1h=n1h, m1c=const(expand(m1c)), m1r=const(expand(m1r)), m3=const(expand(m3)),
                g_fwd=const(g_fwd), g_inv=const(g_inv), twc=twc, tws=tws)


def _dft1_kernel(x_ref, m_ref, o_ref):
    rows = m_ref.shape[1]
    xs = x_ref[...].reshape(rows, x_ref.shape[-1]).astype(BF16)
    o_ref[...] = jnp.dot(m_ref[...], xs, preferred_element_type=F32).reshape(o_ref.shape)


def _dft1(x5, m, n1, cb):
    P, n1h, _, sub, Cx = x5.shape
    return pl.pallas_call(
        _dft1_kernel,
        grid=(DFT_N2H, Cx // cb),
        in_specs=[
            pl.BlockSpec((P, n1h, None, sub, cb), lambda h, c: (0, 0, h, 0, c)),
            pl.BlockSpec(m.shape, lambda h, c: (0, 0)),
        ],
        out_specs=pl.BlockSpec((None, n1, 2, sub, cb), lambda h, c: (h, 0, 0, 0, c)),
        out_shape=jax.ShapeDtypeStruct((DFT_N2H, n1, 2, sub, Cx), F32),
        compiler_params=_cparams(("parallel", "parallel"), 48),
        name="dft1",
    )(x5, m)


def _dft2_kernel(a_ref, f_ref, twc_ref, tws_ref, gf_ref, gi_ref, o_ref):
    C = a_ref.shape[-1]
    half = DFT_N2
    twc, tws = twc_ref[...], tws_ref[...]

    def lanes(fn):
        return jnp.concatenate([fn(slice(c0, c0 + V7X_LANES)) for c0 in range(0, C, V7X_LANES)], axis=-1)

    def spectrum(ref, col0):
        re = lanes(lambda sl: ref[:, 0, :, col0 + sl.start:col0 + sl.stop] * twc + ref[:, 1, :, col0 + sl.start:col0 + sl.stop] * tws)
        im = lanes(lambda sl: ref[:, 1, :, col0 + sl.start:col0 + sl.stop] * twc - ref[:, 0, :, col0 + sl.start:col0 + sl.stop] * tws)
        t = jnp.concatenate([re.reshape(half, C), im.reshape(half, C)], axis=0).astype(BF16)
        s = jnp.dot(gf_ref[...], t, preferred_element_type=F32)
        return s[:half], s[half:]

    xr, xi = spectrum(a_ref, 0)
    fr, fi = spectrum(f_ref, 0)
    br, bi = spectrum(f_ref, C)
    hr, hi = fr + br, fi - bi
    y = jnp.concatenate([xr * hr - xi * hi, xr * hi + xi * hr], axis=0).astype(BF16)
    b = jnp.dot(gi_ref[...], y, preferred_element_type=F32)
    br2 = b[:half].reshape(DFT_N2H, V7X_SUBLANES, C)
    bi2 = b[half:].reshape(DFT_N2H, V7X_SUBLANES, C)
    for c0 in range(0, C, V7X_LANES):
        sl = slice(c0, c0 + V7X_LANES)
        o_ref[:, 0, :, sl] = br2[:, :, sl] * twc - bi2[:, :, sl] * tws
        o_ref[:, 1, :, sl] = bi2[:, :, sl] * twc + br2[:, :, sl] * tws


def _dft2(a5, f5, k):
    n2h, n1, _, sub, C = a5.shape
    return pl.pallas_call(
        _dft2_kernel,
        grid=(n1,),
        in_specs=[
            pl.BlockSpec((n2h, None, 2, sub, C), lambda i: (0, i, 0, 0, 0)),
            pl.BlockSpec((n2h, None, 2, sub, 2 * C), lambda i: (0, i, 0, 0, 0)),
            pl.BlockSpec((None, n2h, sub, V7X_LANES), lambda i: (i, 0, 0, 0)),
            pl.BlockSpec((None, n2h, sub, V7X_LANES), lambda i: (i, 0, 0, 0)),
            pl.BlockSpec((2 * DFT_N2, 2 * DFT_N2), lambda i: (0, 0)),
            pl.BlockSpec((2 * DFT_N2, 2 * DFT_N2), lambda i: (0, 0)),
        ],
        out_specs=pl.BlockSpec((n2h, None, 2, sub, C), lambda i: (0, i, 0, 0, 0)),
        out_shape=jax.ShapeDtypeStruct(a5.shape, F32),
        compiler_params=_cparams(("parallel",), 32),
        name="dft2",
    )(a5, f5, k["twc"], k["tws"], k["g_fwd"], k["g_inv"])


def _dft3_kernel(b_ref, m_ref, x0_ref, vx_ref, sc_ref, d_ref, o_ref):
    C = b_ref.shape[-1]
    bs = b_ref[...].reshape(m_ref.shape[1], C).astype(BF16)
    y = jnp.dot(m_ref[...], bs, preferred_element_type=F32).reshape(o_ref.shape)
    o_ref[...] = x0_ref[...] * (y * sc_ref[...] + vx_ref[...] * d_ref[...])


def _dft3(b5, m3, x05, vx5, scale, d):
    n2h, n1, _, sub, C = b5.shape
    Bt, n1h = x05.shape[0], x05.shape[1]
    tok = pl.BlockSpec((Bt, n1h, None, sub, C), lambda h: (0, 0, h, 0, 0))
    vec = pl.BlockSpec((1, C), lambda h: (0, 0))
    return pl.pallas_call(
        _dft3_kernel,
        grid=(n2h,),
        in_specs=[
            pl.BlockSpec((None, n1, 2, sub, C), lambda h: (h, 0, 0, 0, 0)),
            pl.BlockSpec(m3.shape, lambda h: (0, 0)),
            tok, tok, vec, vec,
        ],
        out_specs=tok,
        out_shape=jax.ShapeDtypeStruct(x05.shape, F32),
        compiler_params=_cparams(("parallel",), 48),
        name="dft3",
    )(b5, m3, x05, vx5, scale, d)


def _hyena(u, conv_w, conv_b, w1, b1, w2, b2, w3, b3, freq, wout, d_skip, B, L):
    T = u.shape[0]
    C = HYENA_WIDTH
    k = _dft_constants(L)
    n1, n1h = k["n1"], k["n1h"]
    x0, vx = _hpre(u, conv_w, conv_b[None], B, L)
    hfb, asum = _filter_taps(L, w1, b1, w2, b2, w3, b3, freq, wout)
    a5 = _dft1(vx, k["m1c"], n1, cb=C)
    f5 = _dft1(hfb, k["m1r"], n1, cb=C)
    b5 = _dft2(a5, f5, k)
    scale = 1.0 / (asum * (2 * L))
    return _dft3(b5, k["m3"], x0, vx, scale, d_skip[None])


def _bf16_bits(x):
    a = pltpu.bitcast(x, jnp.uint32)
    return (a + jnp.uint32(0x7FFF) + ((a >> 16) & jnp.uint32(1))) >> 16


def _pack_pairs(x):
    half = x.shape[1] // 2
    return _bf16_bits(x[:, :half]) | (_bf16_bits(x[:, half:]) << 16)


def _unpack_pairs(w):
    lo = pltpu.bitcast(w << 16, F32)
    hi = pltpu.bitcast(w & jnp.uint32(0xFFFF0000), F32)
    return lo, hi


def _oproj_kernel(x_ref, g0_ref, b0_ref, at_ref, hy_ref, wo_ref, g1_ref, b1_ref, wrh_ref, wrl_ref,
                  h1_ref, h1p_ref, sc_ref):
    aw = at_ref.shape[1]
    h0 = _layer_norm_rows(x_ref[...], g0_ref[...], b0_ref[...])
    mixed = jnp.dot(at_ref[...], wo_ref[0:aw, :], preferred_element_type=F32)
    hy = hy_ref[...].reshape(x_ref.shape[0], hy_ref.shape[-1])
    mixed = mixed + jnp.dot(hy.astype(BF16), wo_ref[aw:, :], preferred_element_type=F32)
    h1 = _layer_norm_rows(ALPHA * h0 + mixed, g1_ref[...], b1_ref[...])
    h1_ref[...] = h1
    h1p_ref[...] = _pack_pairs(h1)
    hh = h1.astype(BF16)
    hl = (h1 - hh.astype(F32)).astype(BF16)
    dn = (((1,), (1,)), ((), ()))
    logits = lax.dot_general(wrh_ref[...], hh, dn, preferred_element_type=F32)
    logits = logits + (lax.dot_general(wrh_ref[...], hl, dn, preferred_element_type=F32)
                       + lax.dot_general(wrl_ref[...], hh, dn, preferred_element_type=F32))
    sc_ref[...] = 1.0 / (1.0 + jnp.exp(-logits))


def _oproj(x2, g0, b0, attn, hy, wo_bf, g1, b1, wr_t, tm=256):
    T, D = x2.shape
    E = wr_t.shape[0]
    wrh = wr_t.astype(BF16)
    wrl = (wr_t - wrh.astype(F32)).astype(BF16)
    row = lambda i: (i, 0)
    fixed = lambda i: (0, 0)
    hy4 = hy.reshape((-1,) + hy.shape[2:])
    return pl.pallas_call(
        _oproj_kernel,
        grid=(T // tm,),
        in_specs=[
            pl.BlockSpec((tm, D), row), pl.BlockSpec((1, D), fixed), pl.BlockSpec((1, D), fixed),
            pl.BlockSpec((tm, attn.shape[1]), row),
            pl.BlockSpec((tm // DFT_N2,) + hy4.shape[1:], lambda i: (i, 0, 0, 0)),
            pl.BlockSpec((D, D), fixed), pl.BlockSpec((1, D), fixed), pl.BlockSpec((1, D), fixed),
            pl.BlockSpec((E, D), fixed), pl.BlockSpec((E, D), fixed),
        ],
        out_specs=[pl.BlockSpec((tm, D), row), pl.BlockSpec((tm, D // 2), row), pl.BlockSpec((E, tm), lambda i: (0, i))],
        out_shape=[
            jax.ShapeDtypeStruct((T, D), F32),
            jax.ShapeDtypeStruct((T, D // 2), jnp.uint32),
            jax.ShapeDtypeStruct((E, T), F32),
        ],
        compiler_params=_cparams(("parallel",), 48),
        name="oproj",
    )(x2, g0, b0, attn, hy4, wo_bf, g1, b1, wrh, wrl)


def _route_kernel(sc_ref, bias_ref, tri_ref, idx_ref, gate_ref, rank_ref, cnt_ref, carry_ref):
    step = pl.program_id(0)
    E, tm = sc_ref.shape
    neg = jnp.float32(-jnp.inf)
    scores = sc_ref[...]
    biased = scores + bias_ref[...]
    erow = lax.broadcasted_iota(jnp.int32, (E, tm), 0)
    big = jnp.int32(E)

    def first_argmax(vals, rows):
        m = jnp.max(vals, axis=0, keepdims=True)
        pick = jnp.min(jnp.where(vals == m, rows, big), axis=0, keepdims=True)
        return m, pick

    gsc = []
    for g in range(N_GROUPS):
        blk = biased[g * GROUP_SIZE:(g + 1) * GROUP_SIZE, :]
        rows = erow[g * GROUP_SIZE:(g + 1) * GROUP_SIZE, :]
        m1, p1 = first_argmax(blk, rows)
        m2 = jnp.max(jnp.where(rows == p1, neg, blk), axis=0, keepdims=True)
        gsc.append(m1 + m2)
    gsc = jnp.concatenate(gsc, axis=0)
    grow = lax.broadcasted_iota(jnp.int32, (N_GROUPS, tm), 0)
    gsel = jnp.zeros((N_GROUPS, tm), jnp.bool_)
    work = gsc
    for _ in range(TOPK_GROUPS):
        _, p = first_argmax(work, grow)
        hit = grow == p
        gsel = gsel | hit
        work = jnp.where(hit, neg, work)
    emask = jnp.concatenate(
        [jnp.broadcast_to(gsel[g:g + 1, :], (GROUP_SIZE, tm)) for g in range(N_GROUPS)], axis=0)
    work = jnp.where(emask, biased, neg)

    sel = jnp.zeros((E, tm), jnp.bool_)
    picks, gvals = [], []
    for _ in range(TOP_K):
        _, p = first_argmax(work, erow)
        hit = erow == p
        sel = sel | hit
        picks.append(p)
        gvals.append(jnp.sum(jnp.where(hit, scores, 0.0), axis=0, keepdims=True))
        work = jnp.where(hit, neg, work)
    gv = jnp.concatenate(gvals, axis=0)
    idx_ref[...] = jnp.concatenate(picks, axis=0)
    gate_ref[...] = gv / jnp.sum(gv, axis=0, keepdims=True) * ROUTED_SCALE

    @pl.when(step == 0)
    def _():
        carry_ref[...] = jnp.zeros_like(carry_ref)

    chosen = sel.astype(F32)
    before = jnp.dot(chosen.astype(BF16), tri_ref[...], preferred_element_type=F32) + carry_ref[...]
    rank_ref[...] = jnp.concatenate(
        [jnp.sum(jnp.where(erow == p, before, 0.0), axis=0, keepdims=True) for p in picks], axis=0).astype(jnp.int32)
    carry_ref[...] += jnp.sum(chosen, axis=1, keepdims=True)
    cnt_ref[...] = carry_ref[...].astype(jnp.int32)


def _route(scores_t, bias, tm=128):
    E, T = scores_t.shape
    tri = (jnp.arange(tm)[:, None] < jnp.arange(tm)[None, :]).astype(BF16)
    tok = lambda i: (0, i)
    fixed = lambda i: (0, 0)
    return pl.pallas_call(
        _route_kernel,
        grid=(T // tm,),
        in_specs=[pl.BlockSpec((E, tm), tok), pl.BlockSpec((E, 1), fixed), pl.BlockSpec((tm, tm), fixed)],
        out_specs=[pl.BlockSpec((TOP_K, tm), tok), pl.BlockSpec((TOP_K, tm), tok), pl.BlockSpec((TOP_K, tm), tok),
                   pl.BlockSpec((E, 1), fixed)],
        out_shape=[
            jax.ShapeDtypeStruct((TOP_K, T), jnp.int32),
            jax.ShapeDtypeStruct((TOP_K, T), F32),
            jax.ShapeDtypeStruct((TOP_K, T), jnp.int32),
            jax.ShapeDtypeStruct((E, 1), jnp.int32),
        ],
        scratch_shapes=[pltpu.VMEM((E, 1), F32)],
        compiler_params=_cparams(("arbitrary",), 32),
        name="route",
    )(scores_t, bias, tri)


def _dest_kernel(idx_ref, rank_ref, ps_ref, dest_ref):
    E = ps_ref.shape[0]
    tm = idx_ref.shape[1]
    erow = lax.broadcasted_iota(jnp.int32, (E, tm), 0)
    ps = ps_ref[...].astype(F32)
    rows = [jnp.sum(jnp.where(erow == idx_ref[k:k + 1, :], ps, 0.0), axis=0, keepdims=True) for k in range(TOP_K)]
    dest_ref[...] = jnp.concatenate(rows, axis=0).astype(jnp.int32) + rank_ref[...]


def _dest_rows(idx, rank, pad_start, tm=512):
    K, T = idx.shape
    E = pad_start.shape[0]
    tok = lambda i: (0, i)
    return pl.pallas_call(
        _dest_kernel,
        grid=(T // tm,),
        in_specs=[pl.BlockSpec((K, tm), tok), pl.BlockSpec((K, tm), tok), pl.BlockSpec((E, 1), lambda i: (0, 0))],
        out_specs=pl.BlockSpec((K, tm), tok),
        out_shape=jax.ShapeDtypeStruct((K, T), jnp.int32),
        compiler_params=_cparams(("arbitrary",), 32),
        name="dest",
    )(idx, rank, pad_start[:, None])


def _scatter_kernel(ps_ref, pe_ref, dest_ref, h_ref, xs_ref, zbuf, sem, *, rb):
    tm = h_ref.shape[0]

    @pl.when(pl.program_id(0) == 0)
    def _():
        zbuf[...] = jnp.zeros_like(zbuf)

        def zcopy(e):
            return pltpu.make_async_copy(zbuf, xs_ref.at[pl.ds(pl.multiple_of(pe_ref[e] - rb, rb), rb)], sem)

        def zstart(e, c):
            @pl.when(pe_ref[e] > ps_ref[e])
            def _():
                zcopy(e).start()
            return c

        def zwait(e, c):
            @pl.when(pe_ref[e] > ps_ref[e])
            def _():
                zcopy(e).wait()
            return c

        lax.fori_loop(0, ps_ref.shape[0], zstart, 0)
        lax.fori_loop(0, ps_ref.shape[0], zwait, 0)

    sub = V7X_SUBLANES

    def issue(r8, c):
        for j in range(sub):
            for k in range(TOP_K):
                dst = dest_ref[0, r8 * (sub * TOP_K) + (j * TOP_K + k)]
                pltpu.make_async_copy(h_ref.at[pl.ds(r8 * sub + j, 1)], xs_ref.at[pl.ds(dst, 1)], sem).start(priority=k % 2)
        return c

    lax.fori_loop(0, tm // sub, issue, 0)
    for k in range(TOP_K):
        pltpu.make_async_copy(h_ref, xs_ref.at[pl.ds(0, tm)], sem).wait()


def _dest_tiles(dest, tm):
    K, T = dest.shape
    return dest.T.reshape(T // tm, 1, tm * K)


def _scatter_rows(pad_start, pad_end, dest, h1p, n_rows, rb, tm=256):
    T, W = h1p.shape
    return pl.pallas_call(
        functools.partial(_scatter_kernel, rb=rb),
        grid_spec=pltpu.PrefetchScalarGridSpec(
            num_scalar_prefetch=2,
            grid=(T // tm,),
            in_specs=[
                pl.BlockSpec((None, 1, tm * TOP_K), lambda i, ps, pe: (i, 0, 0), memory_space=pltpu.SMEM),
                pl.BlockSpec((tm, W), lambda i, ps, pe: (i, 0)),
            ],
            out_specs=pl.BlockSpec(memory_space=pl.ANY),
            scratch_shapes=[pltpu.VMEM((rb, W), h1p.dtype), pltpu.SemaphoreType.DMA],
        ),
        out_shape=jax.ShapeDtypeStruct((n_rows, W), h1p.dtype),
        compiler_params=_cparams(("arbitrary",), 32),
        name="scatter",
    )(pad_start, pad_end, _dest_tiles(dest, tm), h1p)


def _experts_kernel(be_ref, nu_ref, x_ref, wg_ref, wu_ref, wd_ref, y_ref, wgub, wdb):
    b = pl.program_id(0)
    e = be_ref[b]
    changed = jnp.logical_or(b == 0, e != be_ref[jnp.maximum(b - 1, 0)])

    F = wg_ref.shape[1]

    @pl.when(jnp.logical_and(changed, b < nu_ref[0]))
    def _():
        wgub[:, 0:F] = wg_ref[...].astype(BF16)
        wgub[:, F:] = wu_ref[...].astype(BF16)
        wdb[...] = wd_ref[...].astype(BF16)

    @pl.when(b < nu_ref[0])
    def _():
        lo, hi = _unpack_pairs(x_ref[...])
        x = jnp.concatenate([lo, hi], axis=1).astype(BF16)
        gu = jnp.dot(x, wgub[...], preferred_element_type=F32)
        g, u = gu[:, 0:F], gu[:, F:]
        hb = (g / (1.0 + jnp.exp(-g)) * u).astype(BF16)
        y_ref[...] = _pack_pairs(jnp.dot(hb, wdb[...], preferred_element_type=F32))

    @pl.when(b >= nu_ref[0])
    def _():
        y_ref[...] = jnp.zeros_like(y_ref)


def _experts(block_expert, n_used, xs, w_gate, w_up, w_down, rb):
    P, W = xs.shape
    E, D, F = w_gate.shape
    nb = P // rb
    live = lambda b, be, nu: (jnp.minimum(b, nu[0] - 1), 0)
    return pl.pallas_call(
        _experts_kernel,
        grid_spec=pltpu.PrefetchScalarGridSpec(
            num_scalar_prefetch=2,
            grid=(nb,),
            in_specs=[
                pl.BlockSpec((rb, W), live),
                pl.BlockSpec((None, D, F), lambda b, be, nu: (be[b], 0, 0)),
                pl.BlockSpec((None, D, F), lambda b, be, nu: (be[b], 0, 0)),
                pl.BlockSpec((None, F, D), lambda b, be, nu: (be[b], 0, 0)),
            ],
            out_specs=pl.BlockSpec((rb, W), lambda b, be, nu: (b, 0)),
            scratch_shapes=[pltpu.VMEM((D, 2 * F), BF16), pltpu.VMEM((F, D), BF16)],
        ),
        out_shape=jax.ShapeDtypeStruct((P, W), jnp.uint32),
        compiler_params=_cparams(("arbitrary",), 48),
        name="experts",
    )(block_expert, n_used, xs, w_gate, w_up, w_down)


def _combine_kernel(dest_ref, destn_ref, h1_ref, gate_ref, ys_ref, sg_ref, su_ref, sd_ref,
                    g2_ref, b2_ref, o_ref, buf, sem):
    i = pl.program_id(0)
    n = pl.num_programs(0)
    tm = h1_ref.shape[0]
    slot = i % 2

    sub = V7X_SUBLANES

    def issue(dref, s):
        def body(r8, c):
            for j in range(sub):
                for k in range(TOP_K):
                    src = dref[0, r8 * (sub * TOP_K) + (j * TOP_K + k)]
                    pltpu.make_async_copy(ys_ref.at[pl.ds(src, 1)], buf.at[s, k, pl.ds(r8 * sub + j, 1)],
                                          sem.at[s]).start(priority=k % 2)
            return c
        lax.fori_loop(0, tm // sub, body, 0)

    @pl.when(i == 0)
    def _():
        issue(dest_ref, 0)

    @pl.when(i + 1 < n)
    def _():
        issue(destn_ref, 1 - slot)

    h1 = h1_ref[...]
    hb = h1.astype(BF16)
    g = jnp.dot(hb, sg_ref[...], preferred_element_type=F32)
    u = jnp.dot(hb, su_ref[...], preferred_element_type=F32)
    ffn = jnp.dot((g / (1.0 + jnp.exp(-g)) * u).astype(BF16), sd_ref[...], preferred_element_type=F32)

    for k in range(TOP_K):
        pltpu.make_async_copy(ys_ref.at[pl.ds(0, tm)], buf.at[slot, k], sem.at[slot]).wait()
    gates = gate_ref[...]
    lo_acc = jnp.zeros((tm, buf.shape[3]), F32)
    hi_acc = jnp.zeros((tm, buf.shape[3]), F32)
    for k in range(TOP_K):
        lo, hi = _unpack_pairs(buf[slot, k])
        gk = gates[:, k:k + 1]
        lo_acc = lo_acc + gk * lo
        hi_acc = hi_acc + gk * hi
    ffn = ffn + jnp.concatenate([lo_acc, hi_acc], axis=1)
    o_ref[...] = _layer_norm_rows(ALPHA * h1 + ffn, g2_ref[...], b2_ref[...])


def _combine(dest, h1, gates_tk, ys, sg, su, sd, g2, b2, tm=128):
    T, D = h1.shape
    W = ys.shape[1]
    F = sg.shape[1]
    nt = T // tm
    row = lambda i: (i, 0)
    fixed = lambda i: (0, 0)
    smem = functools.partial(pl.BlockSpec, (None, 1, tm * TOP_K), memory_space=pltpu.SMEM)
    dest_t = _dest_tiles(dest, tm)
    return pl.pallas_call(
        _combine_kernel,
        grid=(nt,),
        in_specs=[
            smem(index_map=lambda i: (i, 0, 0)),
            smem(index_map=lambda i: (jnp.minimum(i + 1, nt - 1), 0, 0)),
            pl.BlockSpec((tm, D), row),
            pl.BlockSpec((tm, TOP_K), row),
            pl.BlockSpec(memory_space=pl.ANY),
            pl.BlockSpec((D, F), fixed), pl.BlockSpec((D, F), fixed), pl.BlockSpec((F, D), fixed),
            pl.BlockSpec((1, D), fixed), pl.BlockSpec((1, D), fixed),
        ],
        out_specs=pl.BlockSpec((tm, D), row),
        out_shape=jax.ShapeDtypeStruct((T, D), F32),
        scratch_shapes=[pltpu.VMEM((2, TOP_K, tm, W), jnp.uint32), pltpu.SemaphoreType.DMA((2,))],
        compiler_params=_cparams(("arbitrary",), 48),
        name="combine",
    )(dest_t, dest_t, h1, gates_tk, ys, sg, su, sd, g2, b2)


MOE_ROWS = 256


def _moe(h1, h1p, scores_t, router_bias, w_gate, w_up, w_down, ws_gate, ws_up, ws_down, g2, b2):
    T = h1.shape[0]
    E = N_EXPERTS
    rb = MOE_ROWS
    idx, gates, rank, counts = _route(scores_t, router_bias[:, None])
    counts = counts[:, 0]
    padded = (counts + rb - 1) // rb * rb
    pad_end = jnp.cumsum(padded)
    pad_start = pad_end - padded
    nb = (T * TOP_K) // rb + E
    starts = jnp.arange(nb, dtype=jnp.int32) * rb
    block_expert = jnp.minimum(jnp.sum((pad_end[None, :] <= starts[:, None]).astype(jnp.int32), axis=1), E - 1)
    n_used = (pad_end[-1:] // rb).astype(jnp.int32)
    dest = _dest_rows(idx, rank, pad_start)
    xs = _scatter_rows(pad_start, pad_end, dest, h1p, nb * rb, rb)
    ys = _experts(block_expert, n_used, xs, w_gate, w_up, w_down, rb)
    return _combine(dest, h1, gates.T, ys, ws_gate.astype(BF16), ws_up.astype(BF16), ws_down.astype(BF16), g2, b2)


def kernel(x, positions, emb_ln_g, emb_ln_b, w_in, hy_conv_w, hy_conv_b, hy_f_w1, hy_f_b1, hy_f_w2, hy_f_b2, hy_f_w3, hy_f_b3, hy_f_freq, hy_f_wout, hy_d, lambda_q1, lambda_k1, lambda_q2, lambda_k2, subln_g, w_o, ln1_g, ln1_b, w_router, router_bias, w_gate, w_up, w_down, ws_gate, ws_up, ws_down, ln2_g, ln2_b):
    B, L, D = x.shape
    T = B * L
    assert w_in.shape[0] == DEPTH == 1
    i = 0
    x2 = x.reshape(T, D)
    g0, b0 = emb_ln_g[None], emb_ln_b[None]
    ra, rm, rp = _rotary_tables(positions)
    q, k, v, u = _inproj(x2, g0, b0, w_in[i].astype(BF16), ra, rm, rp)
    lam = (jnp.exp(jnp.sum(lambda_q1[i] * lambda_k1[i])) - jnp.exp(jnp.sum(lambda_q2[i] * lambda_k2[i])) + LAM_INIT)
    attn = _attention(q, k, v, lam.reshape(1).astype(F32), subln_g[i][None], B, L)
    hy = _hyena(u, hy_conv_w[i], hy_conv_b[i], hy_f_w1[i], hy_f_b1[i], hy_f_w2[i], hy_f_b2[i], hy_f_w3[i], hy_f_b3[i],
                hy_f_freq[i], hy_f_wout[i], hy_d[i], B, L)
    h1, h1p, scores_t = _oproj(x2, g0, b0, attn, hy, w_o[i].astype(BF16), ln1_g[i][None], ln1_b[i][None], w_router[i].T)
    out = _moe(h1, h1p, scores_t, router_bias[i], w_gate[i], w_up[i], w_down[i], ws_gate[i], ws_up[i], ws_down[i],
               ln2_g[i][None], ln2_b[i][None])
    return out.reshape(B, L, D)
```

```python
import functools
import math

import numpy as np
import jax
import jax.numpy as jnp
from jax import lax
from jax.experimental import pallas as pl
from jax.experimental.pallas import tpu as pltpu

DA_HEADS = 4
DA_HEAD_DIM = 64
DA_V_DIM = 128
ATTN_WIDTH = 512
HYENA_WIDTH = 512
ROT_DIM = 16
ROPE_THETA = 500000.0
SHORT_CONV = 3
FILTER_EMB = 33
FILTER_BANDS = 16
DECAY_TARGET = 1e-2
FAST_DECAY = 0.3
SLOW_DECAY = 1.5
N_EXPERTS = 256
TOP_K = 8
N_GROUPS = 8
GROUP_SIZE = N_EXPERTS // N_GROUPS
TOPK_GROUPS = 4
EXPERT_DIM = 256
ROUTED_SCALE = 2.5
DEPTH = 1
ALPHA = (2 * DEPTH) ** 0.25
LN_EPS = 1e-5
LAM_INIT = 0.8 - 0.6 * math.exp(-0.3 * 0)

V7X_LANES = 128
V7X_SUBLANES = 8
V7X_VMEM_BYTES = 64 * 1024 * 1024

DFT_N2 = 128
DFT_N2H = DFT_N2 // V7X_SUBLANES

BF16 = jnp.bfloat16
F32 = jnp.float32


def _cparams(sem, vmem_mb):
    return pltpu.CompilerParams(dimension_semantics=sem, vmem_limit_bytes=vmem_mb * 1024 * 1024)


def _layer_norm_rows(x, g, b):
    mu = jnp.mean(x, axis=-1, keepdims=True)
    xc = x - mu
    var = jnp.mean(xc * xc, axis=-1, keepdims=True)
    return xc * lax.rsqrt(var + LN_EPS) * g + b


def _inproj_kernel(x_ref, g_ref, b_ref, w_ref, ra_ref, rm_ref, rp_ref, q_ref, k_ref, v_ref, u_ref):
    h = _layer_norm_rows(x_ref[...], g_ref[...], b_ref[...]).astype(BF16)
    ra, rm, rp = ra_ref[...], rm_ref[...], rp_ref[...]

    def rot(t):
        return t * ra + pltpu.roll(t, V7X_LANES - ROT_DIM // 2, axis=1) * rm + pltpu.roll(t, ROT_DIM // 2, axis=1) * rp

    aw = ATTN_WIDTH
    qp = jnp.dot(h, w_ref[:, 0:aw], preferred_element_type=F32)
    kp = jnp.dot(h, w_ref[:, aw:2 * aw], preferred_element_type=F32)
    scale = DA_HEAD_DIM ** -0.5 * math.log2(math.e)
    for c in range(aw // V7X_LANES):
        sl = slice(c * V7X_LANES, (c + 1) * V7X_LANES)
        q_ref[:, sl] = (rot(qp[:, sl]) * scale).astype(BF16)
        k_ref[:, sl] = rot(kp[:, sl]).astype(BF16)
    vp = jnp.dot(h, w_ref[:, 2 * aw:3 * aw], preferred_element_type=F32).astype(BF16)
    ones = jnp.ones((vp.shape[0], DA_V_DIM), BF16)
    for hd in range(DA_HEADS):
        v_ref[:, 2 * hd * DA_V_DIM:(2 * hd + 1) * DA_V_DIM] = vp[:, hd * DA_V_DIM:(hd + 1) * DA_V_DIM]
        v_ref[:, (2 * hd + 1) * DA_V_DIM:(2 * hd + 2) * DA_V_DIM] = ones
    u_ref[...] = jnp.dot(h, w_ref[:, 3 * aw:], preferred_element_type=F32)


def _inproj(x2, g, b, w_bf, ra, rm, rp, tm=256):
    T, D = x2.shape
    ncol = w_bf.shape[1]
    aw = ATTN_WIDTH
    uw = ncol - 3 * aw
    row = lambda i: (i, 0)
    fixed = lambda i: (0, 0)
    return pl.pallas_call(
        _inproj_kernel,
        grid=(T // tm,),
        in_specs=[
            pl.BlockSpec((tm, D), row),
            pl.BlockSpec((1, D), fixed),
            pl.BlockSpec((1, D), fixed),
            pl.BlockSpec((D, ncol), fixed),
            pl.BlockSpec((tm, V7X_LANES), row),
            pl.BlockSpec((tm, V7X_LANES), row),
            pl.BlockSpec((tm, V7X_LANES), row),
        ],
        out_specs=[
            pl.BlockSpec((tm, aw), row),
            pl.BlockSpec((tm, aw), row),
            pl.BlockSpec((tm, 2 * aw), row),
            pl.BlockSpec((tm, uw), row),
        ],
        out_shape=[
            jax.ShapeDtypeStruct((T, aw), BF16),
            jax.ShapeDtypeStruct((T, aw), BF16),
            jax.ShapeDtypeStruct((T, 2 * aw), BF16),
            jax.ShapeDtypeStruct((T, uw), F32),
        ],
        compiler_params=_cparams(("parallel",), 48),
        name="inproj",
    )(x2, g, b, w_bf, ra, rm, rp)


def _rotary_tables(positions):
    half = ROT_DIM // 2
    inv_freq = ROPE_THETA ** (-jnp.arange(0, ROT_DIM, 2, dtype=F32) / ROT_DIM)
    ang = positions.astype(F32).reshape(-1)[:, None] * inv_freq
    cos, sin = jnp.cos(ang), jnp.sin(ang)
    T = ang.shape[0]
    ones = jnp.ones((T, DA_HEAD_DIM - ROT_DIM), F32)
    zeros_h = jnp.zeros((T, half), F32)
    zeros_r = jnp.zeros((T, DA_HEAD_DIM - ROT_DIM), F32)
    a64 = jnp.concatenate([cos, cos, ones], axis=1)
    m64 = jnp.concatenate([-sin, zeros_h, zeros_r], axis=1)
    p64 = jnp.concatenate([zeros_h, sin, zeros_r], axis=1)
    rep = V7X_LANES // DA_HEAD_DIM
    return jnp.tile(a64, (1, rep)), jnp.tile(m64, (1, rep)), jnp.tile(p64, (1, rep))


def _attn_kernel(lam_ref, q_ref, k_ref, v_ref, g_ref, o_ref, s_ref, m_ref, acc_ref, *, kb, unroll):
    qb = q_ref.shape[0]
    L = k_ref.shape[0]
    nchunk = L // kb
    nl = kb // V7X_LANES
    q = q_ref[...]
    lane = lax.broadcasted_iota(jnp.int32, q.shape, 1)
    zero = jnp.zeros_like(q)
    qm = [jnp.where(lane < DA_HEAD_DIM, q, zero), jnp.where(lane >= DA_HEAD_DIM, q, zero)]
    m_ref[...] = jnp.full(m_ref.shape, -jnp.inf, F32)
    acc_ref[...] = jnp.zeros(acc_ref.shape, F32)

    def score_body(j, carry):
        kc = k_ref[pl.ds(pl.multiple_of(j * kb, kb), kb), :]
        for c in range(2):
            s = lax.dot_general(qm[c], kc, (((1,), (1,)), ((), ())), preferred_element_type=F32)
            s_ref[c, j] = s
            m = s[:, 0:V7X_LANES]
            for t in range(1, nl):
                m = jnp.maximum(m, s[:, t * V7X_LANES:(t + 1) * V7X_LANES])
            m_ref[c] = jnp.maximum(m_ref[c], m)
        return carry

    lax.fori_loop(0, nchunk, score_body, 0, unroll=unroll)
    m_row = [jnp.max(m_ref[c], axis=1, keepdims=True) for c in range(2)]

    def pv_body(j, carry):
        vc = v_ref[pl.ds(pl.multiple_of(j * kb, kb), kb), :]
        for c in range(2):
            p = jnp.exp2(s_ref[c, j] - m_row[c])
            acc_ref[c] += jnp.dot(p.astype(BF16), vc, preferred_element_type=F32)
        return carry

    lax.fori_loop(0, nchunk, pv_body, 0, unroll=unroll)
    outs = [acc_ref[c, :, 0:DA_V_DIM] / acc_ref[c, :, DA_V_DIM:2 * DA_V_DIM] for c in range(2)]
    o = outs[0] - lam_ref[0] * outs[1]
    ms = jnp.mean(o * o, axis=1, keepdims=True)
    o_ref[...] = (o * lax.rsqrt(ms + LN_EPS) * g_ref[...] * (1.0 - LAM_INIT)).astype(o_ref.dtype)


def _attention(q, k, v, lam, subln_g, B, L, qb=512, kb=512, unroll=4):
    T = q.shape[0]
    nq = L // qb
    return pl.pallas_call(
        functools.partial(_attn_kernel, kb=kb, unroll=unroll),
        grid=(B, DA_HEADS, nq),
        in_specs=[
            pl.BlockSpec(memory_space=pltpu.SMEM),
            pl.BlockSpec((qb, DA_V_DIM), lambda b, h, i: (b * nq + i, h)),
            pl.BlockSpec((L, DA_V_DIM), lambda b, h, i: (b, h)),
            pl.BlockSpec((L, 2 * DA_V_DIM), lambda b, h, i: (b, h)),
            pl.BlockSpec((1, DA_V_DIM), lambda b, h, i: (0, 0)),
        ],
        out_specs=pl.BlockSpec((qb, DA_V_DIM), lambda b, h, i: (b * nq + i, h)),
        out_shape=jax.ShapeDtypeStruct((T, ATTN_WIDTH), BF16),
        scratch_shapes=[
            pltpu.VMEM((2, L // kb, qb, kb), F32),
            pltpu.VMEM((2, qb, V7X_LANES), F32),
            pltpu.VMEM((2, qb, 2 * DA_V_DIM), F32),
        ],
        compiler_params=_cparams(("parallel", "parallel", "parallel"), 56),
        name="attn",
    )(lam, q, k, v, subln_g)


def _hpre_kernel(u_ref, up_ref, un_ref, w_ref, b_ref, x0_ref, vx_ref):
    i = pl.program_id(1)
    n = pl.num_programs(1)
    tb = u_ref.shape[0]
    C = x0_ref.shape[-1]
    row = lax.broadcasted_iota(jnp.int32, (tb, V7X_LANES), 0)
    has_prev = (i > 0).astype(F32)
    has_next = (i < n - 1).astype(F32)

    def conv(c0):
        sl = slice(c0, c0 + V7X_LANES)
        u = u_ref[:, sl]
        prev_row = up_ref[V7X_SUBLANES - 1:V7X_SUBLANES, sl] * has_prev
        next_row = un_ref[0:1, sl] * has_next
        u_prev = jnp.where(row == 0, prev_row, pltpu.roll(u, 1, axis=0))
        u_next = jnp.where(row == tb - 1, next_row, pltpu.roll(u, tb - 1, axis=0))
        return u_prev * w_ref[0:1, sl] + u * w_ref[1:2, sl] + u_next * w_ref[2:3, sl] + b_ref[:, sl]

    tile = x0_ref.shape[:-1] + (V7X_LANES,)
    for c in range(C // V7X_LANES):
        c0 = c * V7X_LANES
        x0_ref[:, :, :, c0:c0 + V7X_LANES] = conv(c0).reshape(tile)
        vx_ref[:, :, :, c0:c0 + V7X_LANES] = (conv(2 * C + c0) * conv(C + c0)).reshape(tile)


def _hpre(u, conv_w, conv_b, B, L, tb=512):
    T, C3 = u.shape
    C = C3 // 3
    nt = L // tb
    sub = V7X_SUBLANES
    na = tb // DFT_N2
    split = pl.BlockSpec((None, na, DFT_N2H, sub, C), lambda b, i: (b, i, 0, 0, 0))
    split_shape = jax.ShapeDtypeStruct((B, L // DFT_N2, DFT_N2H, sub, C), F32)
    cur = lambda b, i: (b * nt + i, 0)
    prev = lambda b, i: (jnp.maximum((b * L + i * tb) // sub - 1, 0), 0)
    nxt = lambda b, i: (jnp.minimum((b * L + (i + 1) * tb) // sub, T // sub - 1), 0)
    fixed = lambda b, i: (0, 0)
    return pl.pallas_call(
        _hpre_kernel,
        grid=(B, nt),
        in_specs=[
            pl.BlockSpec((tb, C3), cur),
            pl.BlockSpec((sub, C3), prev),
            pl.BlockSpec((sub, C3), nxt),
            pl.BlockSpec((SHORT_CONV, C3), fixed),
            pl.BlockSpec((1, C3), fixed),
        ],
        out_specs=[split, split],
        out_shape=[split_shape, split_shape],
        compiler_params=_cparams(("parallel", "parallel"), 32),
        name="hpre",
    )(u, u, u, conv_w, conv_b)


def _filt_kernel(z_ref, w1_ref, b1_ref, w2_ref, b2_ref, w3_ref, b3_ref, fr_ref, wo_ref, dl_ref,
                 hfb_ref, asum_ref, *, L):
    i = pl.program_id(0)
    tl = z_ref.shape[0]
    C = dl_ref.shape[1]
    hp = lax.Precision.HIGHEST
    h = jnp.sin(fr_ref[0:1, :] * (jnp.dot(z_ref[...], w1_ref[...], precision=hp, preferred_element_type=F32) + b1_ref[...]))
    h = jnp.sin(fr_ref[1:2, :] * (jnp.dot(h, w2_ref[...], precision=hp, preferred_element_type=F32) + b2_ref[...]))
    h = jnp.sin(fr_ref[2:3, :] * (jnp.dot(h, w3_ref[...], precision=hp, preferred_element_type=F32) + b3_ref[...]))
    o = jnp.dot(h, wo_ref[...], precision=hp, preferred_element_type=F32)
    grow = lax.broadcasted_iota(jnp.int32, (tl, C), 0) + i * tl
    t = grow.astype(F32) * (1.0 / (L - 1))
    decay = jnp.exp(-t * dl_ref[...])
    hf = o[:, :C] * decay
    hb = jnp.where(grow == 0, 0.0, o[:, C:] * decay)
    tile = hfb_ref.shape[:-1] + (C,)
    hfb_ref[:, :, :, :C] = hf.reshape(tile)
    hfb_ref[:, :, :, C:] = hb.reshape(tile)

    @pl.when(i == 0)
    def _():
        asum_ref[...] = jnp.zeros_like(asum_ref)

    asum_ref[...] += jnp.sum(jnp.abs(hf) + jnp.abs(hb), axis=0, keepdims=True)


def _filter_taps(L, w1, b1, w2, b2, w3, b3, freq, wout, tl=512):
    C = wout.shape[1] // 2
    order = w1.shape[1]
    emb = w1.shape[0]
    t = jnp.linspace(0.0, 1.0, L, dtype=F32)[:, None]
    w = 2.0 * math.pi * jnp.arange(L, dtype=F32)[:, None] / L
    f = jnp.linspace(1e-4, FILTER_BANDS - 1, FILTER_BANDS, dtype=F32)[None, :]
    z = jnp.concatenate([t, jnp.cos(f * w), -jnp.sin(f * w)], axis=-1)
    zp = jnp.pad(z, ((0, 0), (0, V7X_LANES - emb)))
    w1p = jnp.pad(w1, ((0, V7X_LANES - emb), (0, 0)))
    deltas = jnp.abs(jnp.linspace(math.log(DECAY_TARGET) / SLOW_DECAY, math.log(DECAY_TARGET) / FAST_DECAY, C, dtype=F32))[None]
    fixed = lambda i: (0, 0)
    return pl.pallas_call(
        functools.partial(_filt_kernel, L=L),
        grid=(L // tl,),
        in_specs=[
            pl.BlockSpec((tl, V7X_LANES), lambda i: (i, 0)),
            pl.BlockSpec((V7X_LANES, order), fixed), pl.BlockSpec((1, order), fixed),
            pl.BlockSpec((order, order), fixed), pl.BlockSpec((1, order), fixed),
            pl.BlockSpec((order, order), fixed), pl.BlockSpec((1, order), fixed),
            pl.BlockSpec((3, order), fixed),
            pl.BlockSpec((order, 2 * C), fixed),
            pl.BlockSpec((1, C), fixed),
        ],
        out_specs=[pl.BlockSpec((None, tl // DFT_N2, DFT_N2H, V7X_SUBLANES, 2 * C), lambda i: (0, i, 0, 0, 0)),
                   pl.BlockSpec((1, C), fixed)],
        out_shape=[jax.ShapeDtypeStruct((1, L // DFT_N2, DFT_N2H, V7X_SUBLANES, 2 * C), F32),
                   jax.ShapeDtypeStruct((1, C), F32)],
        compiler_params=_cparams(("arbitrary",), 32),
        name="filt",
    )(zp, w1p, b1[None], w2, b2[None], w3, b3[None], freq, wout, deltas)


def _dft_constants(L):
    n1 = 2 * L // DFT_N2
    n1h = n1 // 2
    N = 2 * L
    sub = V7X_SUBLANES

    def cs(num, den):
        ang = (num % den).astype(np.float64) * (2.0 * np.pi / den)
        return np.cos(ang), np.sin(ang)

    k1 = np.arange(n1, dtype=np.int64)
    a = np.arange(n1h, dtype=np.int64)
    c, s = cs(k1[:, None] * a[None, :], n1)
    eye = np.eye(sub)

    def expand(m):
        r, kk = m.shape
        return (m[:, None, :, None] * eye[None, :, None, :]).reshape(r * sub, kk * sub)

    def const(m):
        return jnp.asarray(m.astype(np.float32).astype(BF16))

    m1c = np.stack([np.concatenate([c, s], axis=1), np.concatenate([-s, c], axis=1)], axis=1).reshape(2 * n1, 2 * n1h)
    m1r = np.stack([c, -s], axis=1).reshape(2 * n1, n1h)
    ct, st = c.T, s.T
    m3 = np.stack([np.stack([ct, -st], axis=2).reshape(n1h, 2 * n1),
                   np.stack([st, ct], axis=2).reshape(n1h, 2 * n1)], axis=0).reshape(2 * n1h, 2 * n1)
    n2 = np.arange(DFT_N2, dtype=np.int64)
    gc, gs = cs(n2[:, None] * n2[None, :], DFT_N2)
    g_fwd = np.concatenate([np.concatenate([gc, gs], axis=1), np.concatenate([-gs, gc], axis=1)], axis=0)
    g_inv = np.concatenate([np.concatenate([gc, -gs], axis=1), np.concatenate([gs, gc], axis=1)], axis=0)
    ph = (jnp.arange(n1, dtype=jnp.int32)[:, None] * jnp.arange(DFT_N2, dtype=jnp.int32)[None, :]) % N
    ang = ph.astype(F32) * (2.0 * math.pi / N)
    lanes = (n1, DFT_N2H, sub, V7X_LANES)
    twc = jnp.broadcast_to(jnp.cos(ang).reshape(n1, DFT_N2H, sub, 1), lanes)
    tws = jnp.broadcast_to(jnp.sin(ang).reshape(n1, DFT_N2H, sub, 1), lanes)
    return dict(n1=n1, n1h=n1h, m1c=const(expand(m1c)), m1r=const(expand(m1r)), m3=const(expand(m3)),
                g_fwd=const(g_fwd), g_inv=const(g_inv), twc=twc, tws=tws)


def _dft1_kernel(x_ref, m_ref, o_ref):
    rows = m_ref.shape[1]
    xs = x_ref[...].reshape(rows, x_ref.shape[-1]).astype(BF16)
    o_ref[...] = jnp.dot(m_ref[...], xs, preferred_element_type=F32).reshape(o_ref.shape)


def _dft1(x5, m, n1, cb):
    P, n1h, _, sub, Cx = x5.shape
    return pl.pallas_call(
        _dft1_kernel,
        grid=(DFT_N2H, Cx // cb),
        in_specs=[
            pl.BlockSpec((P, n1h, None, sub, cb), lambda h, c: (0, 0, h, 0, c)),
            pl.BlockSpec(m.shape, lambda h, c: (0, 0)),
        ],
        out_specs=pl.BlockSpec((None, n1, 2, sub, cb), lambda h, c: (h, 0, 0, 0, c)),
        out_shape=jax.ShapeDtypeStruct((DFT_N2H, n1, 2, sub, Cx), F32),
        compiler_params=_cparams(("parallel", "parallel"), 48),
        name="dft1",
    )(x5, m)


def _dft2_kernel(a_ref, f_ref, twc_ref, tws_ref, gf_ref, gi_ref, o_ref):
    C = a_ref.shape[-1]
    half = DFT_N2
    twc, tws = twc_ref[...], tws_ref[...]

    def lanes(fn):
        return jnp.concatenate([fn(slice(c0, c0 + V7X_LANES)) for c0 in range(0, C, V7X_LANES)], axis=-1)

    def spectrum(ref, col0):
        re = lanes(lambda sl: ref[:, 0, :, col0 + sl.start:col0 + sl.stop] * twc + ref[:, 1, :, col0 + sl.start:col0 + sl.stop] * tws)
        im = lanes(lambda sl: ref[:, 1, :, col0 + sl.start:col0 + sl.stop] * twc - ref[:, 0, :, col0 + sl.start:col0 + sl.stop] * tws)
        t = jnp.concatenate([re.reshape(half, C), im.reshape(half, C)], axis=0).astype(BF16)
        s = jnp.dot(gf_ref[...], t, preferred_element_type=F32)
        return s[:half], s[half:]

    xr, xi = spectrum(a_ref, 0)
    fr, fi = spectrum(f_ref, 0)
    br, bi = spectrum(f_ref, C)
    hr, hi = fr + br, fi - bi
    y = jnp.concatenate([xr * hr - xi * hi, xr * hi + xi * hr], axis=0).astype(BF16)
    b = jnp.dot(gi_ref[...], y, preferred_element_type=F32)
    br2 = b[:half].reshape(DFT_N2H, V7X_SUBLANES, C)
    bi2 = b[half:].reshape(DFT_N2H, V7X_SUBLANES, C)
    for c0 in range(0, C, V7X_LANES):
        sl = slice(c0, c0 + V7X_LANES)
        o_ref[:, 0, :, sl] = br2[:, :, sl] * twc - bi2[:, :, sl] * tws
        o_ref[:, 1, :, sl] = bi2[:, :, sl] * twc + br2[:, :, sl] * tws


def _dft2(a5, f5, k):
    n2h, n1, _, sub, C = a5.shape
    return pl.pallas_call(
        _dft2_kernel,
        grid=(n1,),
        in_specs=[
            pl.BlockSpec((n2h, None, 2, sub, C), lambda i: (0, i, 0, 0, 0)),
            pl.BlockSpec((n2h, None, 2, sub, 2 * C), lambda i: (0, i, 0, 0, 0)),
            pl.BlockSpec((None, n2h, sub, V7X_LANES), lambda i: (i, 0, 0, 0)),
            pl.BlockSpec((None, n2h, sub, V7X_LANES), lambda i: (i, 0, 0, 0)),
            pl.BlockSpec((2 * DFT_N2, 2 * DFT_N2), lambda i: (0, 0)),
            pl.BlockSpec((2 * DFT_N2, 2 * DFT_N2), lambda i: (0, 0)),
        ],
        out_specs=pl.BlockSpec((n2h, None, 2, sub, C), lambda i: (0, i, 0, 0, 0)),
        out_shape=jax.ShapeDtypeStruct(a5.shape, F32),
        compiler_params=_cparams(("parallel",), 32),
        name="dft2",
    )(a5, f5, k["twc"], k["tws"], k["g_fwd"], k["g_inv"])


def _dft3_kernel(b_ref, m_ref, x0_ref, vx_ref, sc_ref, d_ref, o_ref):
    C = b_ref.shape[-1]
    bs = b_ref[...].reshape(m_ref.shape[1], C).astype(BF16)
    y = jnp.dot(m_ref[...], bs, preferred_element_type=F32).reshape(o_ref.shape)
    o_ref[...] = x0_ref[...] * (y * sc_ref[...] + vx_ref[...] * d_ref[...])


def _dft3(b5, m3, x05, vx5, scale, d):
    n2h, n1, _, sub, C = b5.shape
    Bt, n1h = x05.shape[0], x05.shape[1]
    tok = pl.BlockSpec((Bt, n1h, None, sub, C), lambda h: (0, 0, h, 0, 0))
    vec = pl.BlockSpec((1, C), lambda h: (0, 0))
    return pl.pallas_call(
        _dft3_kernel,
        grid=(n2h,),
        in_specs=[
            pl.BlockSpec((None, n1, 2, sub, C), lambda h: (h, 0, 0, 0, 0)),
            pl.BlockSpec(m3.shape, lambda h: (0, 0)),
            tok, tok, vec, vec,
        ],
        out_specs=tok,
        out_shape=jax.ShapeDtypeStruct(x05.shape, F32),
        compiler_params=_cparams(("parallel",), 48),
        name="dft3",
    )(b5, m3, x05, vx5, scale, d)


def _hyena(u, conv_w, conv_b, w1, b1, w2, b2, w3, b3, freq, wout, d_skip, B, L):
    T = u.shape[0]
    C = HYENA_WIDTH
    k = _dft_constants(L)
    n1, n1h = k["n1"], k["n1h"]
    x0, vx = _hpre(u, conv_w, conv_b[None], B, L)
    hfb, asum = _filter_taps(L, w1, b1, w2, b2, w3, b3, freq, wout)
    a5 = _dft1(vx, k["m1c"], n1, cb=C)
    f5 = _dft1(hfb, k["m1r"], n1, cb=C)
    b5 = _dft2(a5, f5, k)
    scale = 1.0 / (asum * (2 * L))
    return _dft3(b5, k["m3"], x0, vx, scale, d_skip[None])


def _bf16_bits(x):
    a = pltpu.bitcast(x, jnp.uint32)
    return (a + jnp.uint32(0x7FFF) + ((a >> 16) & jnp.uint32(1))) >> 16


def _pack_pairs(x):
    half = x.shape[1] // 2
    return _bf16_bits(x[:, :half]) | (_bf16_bits(x[:, half:]) << 16)


def _unpack_pairs(w):
    lo = pltpu.bitcast(w << 16, F32)
    hi = pltpu.bitcast(w & jnp.uint32(0xFFFF0000), F32)
    return lo, hi


def _oproj_kernel(x_ref, g0_ref, b0_ref, at_ref, hy_ref, wo_ref, g1_ref, b1_ref, wrh_ref, wrl_ref,
                  h1_ref, h1p_ref, sc_ref):
    aw = at_ref.shape[1]
    h0 = _layer_norm_rows(x_ref[...], g0_ref[...], b0_ref[...])
    mixed = jnp.dot(at_ref[...], wo_ref[0:aw, :], preferred_element_type=F32)
    hy = hy_ref[...].reshape(x_ref.shape[0], hy_ref.shape[-1])
    mixed = mixed + jnp.dot(hy.astype(BF16), wo_ref[aw:, :], preferred_element_type=F32)
    h1 = _layer_norm_rows(ALPHA * h0 + mixed, g1_ref[...], b1_ref[...])
    h1_ref[...] = h1
    h1p_ref[...] = _pack_pairs(h1)
    hh = h1.astype(BF16)
    hl = (h1 - hh.astype(F32)).astype(BF16)
    dn = (((1,), (1,)), ((), ()))
    logits = lax.dot_general(wrh_ref[...], hh, dn, preferred_element_type=F32)
    logits = logits + (lax.dot_general(wrh_ref[...], hl, dn, preferred_element_type=F32)
                       + lax.dot_general(wrl_ref[...], hh, dn, preferred_element_type=F32))
    sc_ref[...] = 1.0 / (1.0 + jnp.exp(-logits))


def _oproj(x2, g0, b0, attn, hy, wo_bf, g1, b1, wr_t, tm=256):
    T, D = x2.shape
    E = wr_t.shape[0]
    wrh = wr_t.astype(BF16)
    wrl = (wr_t - wrh.astype(F32)).astype(BF16)
    row = lambda i: (i, 0)
    fixed = lambda i: (0, 0)
    hy4 = hy.reshape((-1,) + hy.shape[2:])
    return pl.pallas_call(
        _oproj_kernel,
        grid=(T // tm,),
        in_specs=[
            pl.BlockSpec((tm, D), row), pl.BlockSpec((1, D), fixed), pl.BlockSpec((1, D), fixed),
            pl.BlockSpec((tm, attn.shape[1]), row),
            pl.BlockSpec((tm // DFT_N2,) + hy4.shape[1:], lambda i: (i, 0, 0, 0)),
            pl.BlockSpec((D, D), fixed), pl.BlockSpec((1, D), fixed), pl.BlockSpec((1, D), fixed),
            pl.BlockSpec((E, D), fixed), pl.BlockSpec((E, D), fixed),
        ],
        out_specs=[pl.BlockSpec((tm, D), row), pl.BlockSpec((tm, D // 2), row), pl.BlockSpec((E, tm), lambda i: (0, i))],
        out_shape=[
            jax.ShapeDtypeStruct((T, D), F32),
            jax.ShapeDtypeStruct((T, D // 2), jnp.uint32),
            jax.ShapeDtypeStruct((E, T), F32),
        ],
        compiler_params=_cparams(("parallel",), 48),
        name="oproj",
    )(x2, g0, b0, attn, hy4, wo_bf, g1, b1, wrh, wrl)


def _route_kernel(sc_ref, bias_ref, tri_ref, idx_ref, gate_ref, rank_ref, cnt_ref, carry_ref):
    step = pl.program_id(0)
    E, tm = sc_ref.shape
    neg = jnp.float32(-jnp.inf)
    scores = sc_ref[...]
    biased = scores + bias_ref[...]
    erow = lax.broadcasted_iota(jnp.int32, (E, tm), 0)
    big = jnp.int32(E)

    def first_argmax(vals, rows):
        m = jnp.max(vals, axis=0, keepdims=True)
        pick = jnp.min(jnp.where(vals == m, rows, big), axis=0, keepdims=True)
        return m, pick

    gsc = []
    for g in range(N_GROUPS):
        blk = biased[g * GROUP_SIZE:(g + 1) * GROUP_SIZE, :]
        rows = erow[g * GROUP_SIZE:(g + 1) * GROUP_SIZE, :]
        m1, p1 = first_argmax(blk, rows)
        m2 = jnp.max(jnp.where(rows == p1, neg, blk), axis=0, keepdims=True)
        gsc.append(m1 + m2)
    gsc = jnp.concatenate(gsc, axis=0)
    grow = lax.broadcasted_iota(jnp.int32, (N_GROUPS, tm), 0)
    gsel = jnp.zeros((N_GROUPS, tm), jnp.bool_)
    work = gsc
    for _ in range(TOPK_GROUPS):
        _, p = first_argmax(work, grow)
        hit = grow == p
        gsel = gsel | hit
        work = jnp.where(hit, neg, work)
    emask = jnp.concatenate(
        [jnp.broadcast_to(gsel[g:g + 1, :], (GROUP_SIZE, tm)) for g in range(N_GROUPS)], axis=0)
    work = jnp.where(emask, biased, neg)

    sel = jnp.zeros((E, tm), jnp.bool_)
    picks, gvals = [], []
    for _ in range(TOP_K):
        _, p = first_argmax(work, erow)
        hit = erow == p
        sel = sel | hit
        picks.append(p)
        gvals.append(jnp.sum(jnp.where(hit, scores, 0.0), axis=0, keepdims=True))
        work = jnp.where(hit, neg, work)
    gv = jnp.concatenate(gvals, axis=0)
    idx_ref[...] = jnp.concatenate(picks, axis=0)
    gate_ref[...] = gv / jnp.sum(gv, axis=0, keepdims=True) * ROUTED_SCALE

    @pl.when(step == 0)
    def _():
        carry_ref[...] = jnp.zeros_like(carry_ref)

    chosen = sel.astype(F32)
    before = jnp.dot(chosen.astype(BF16), tri_ref[...], preferred_element_type=F32) + carry_ref[...]
    rank_ref[...] = jnp.concatenate(
        [jnp.sum(jnp.where(erow == p, before, 0.0), axis=0, keepdims=True) for p in picks], axis=0).astype(jnp.int32)
    carry_ref[...] += jnp.sum(chosen, axis=1, keepdims=True)
    cnt_ref[...] = carry_ref[...].astype(jnp.int32)


def _route(scores_t, bias, tm=128):
    E, T = scores_t.shape
    tri = (jnp.arange(tm)[:, None] < jnp.arange(tm)[None, :]).astype(BF16)
    tok = lambda i: (0, i)
    fixed = lambda i: (0, 0)
    return pl.pallas_call(
        _route_kernel,
        grid=(T // tm,),
        in_specs=[pl.BlockSpec((E, tm), tok), pl.BlockSpec((E, 1), fixed), pl.BlockSpec((tm, tm), fixed)],
        out_specs=[pl.BlockSpec((TOP_K, tm), tok), pl.BlockSpec((TOP_K, tm), tok), pl.BlockSpec((TOP_K, tm), tok),
                   pl.BlockSpec((E, 1), fixed)],
        out_shape=[
            jax.ShapeDtypeStruct((TOP_K, T), jnp.int32),
            jax.ShapeDtypeStruct((TOP_K, T), F32),
            jax.ShapeDtypeStruct((TOP_K, T), jnp.int32),
            jax.ShapeDtypeStruct((E, 1), jnp.int32),
        ],
        scratch_shapes=[pltpu.VMEM((E, 1), F32)],
        compiler_params=_cparams(("arbitrary",), 32),
        name="route",
    )(scores_t, bias, tri)


def _dest_kernel(idx_ref, rank_ref, ps_ref, dest_ref):
    E = ps_ref.shape[0]
    tm = idx_ref.shape[1]
    erow = lax.broadcasted_iota(jnp.int32, (E, tm), 0)
    ps = ps_ref[...].astype(F32)
    rows = [jnp.sum(jnp.where(erow == idx_ref[k:k + 1, :], ps, 0.0), axis=0, keepdims=True) for k in range(TOP_K)]
    dest_ref[...] = jnp.concatenate(rows, axis=0).astype(jnp.int32) + rank_ref[...]


def _dest_rows(idx, rank, pad_start, tm=512):
    K, T = idx.shape
    E = pad_start.shape[0]
    tok = lambda i: (0, i)
    return pl.pallas_call(
        _dest_kernel,
        grid=(T // tm,),
        in_specs=[pl.BlockSpec((K, tm), tok), pl.BlockSpec((K, tm), tok), pl.BlockSpec((E, 1), lambda i: (0, 0))],
        out_specs=pl.BlockSpec((K, tm), tok),
        out_shape=jax.ShapeDtypeStruct((K, T), jnp.int32),
        compiler_params=_cparams(("arbitrary",), 32),
        name="dest",
    )(idx, rank, pad_start[:, None])


def _scatter_kernel(ps_ref, pe_ref, dest_ref, h_ref, xs_ref, zbuf, sem, *, rb):
    tm = h_ref.shape[0]

    @pl.when(pl.program_id(0) == 0)
    def _():
        zbuf[...] = jnp.zeros_like(zbuf)

        def zcopy(e):
            return pltpu.make_async_copy(zbuf, xs_ref.at[pl.ds(pl.multiple_of(pe_ref[e] - rb, rb), rb)], sem)

        def zstart(e, c):
            @pl.when(pe_ref[e] > ps_ref[e])
            def _():
                zcopy(e).start()
            return c

        def zwait(e, c):
            @pl.when(pe_ref[e] > ps_ref[e])
            def _():
                zcopy(e).wait()
            return c

        lax.fori_loop(0, ps_ref.shape[0], zstart, 0)
        lax.fori_loop(0, ps_ref.shape[0], zwait, 0)

    sub = V7X_SUBLANES

    def issue(r8, c):
        for j in range(sub):
            for k in range(TOP_K):
                dst = dest_ref[0, r8 * (sub * TOP_K) + (j * TOP_K + k)]
                pltpu.make_async_copy(h_ref.at[pl.ds(r8 * sub + j, 1)], xs_ref.at[pl.ds(dst, 1)], sem).start(priority=k % 2)
        return c

    lax.fori_loop(0, tm // sub, issue, 0)
    for k in range(TOP_K):
        pltpu.make_async_copy(h_ref, xs_ref.at[pl.ds(0, tm)], sem).wait()


def _dest_tiles(dest, tm):
    K, T = dest.shape
    return dest.T.reshape(T // tm, 1, tm * K)


def _scatter_rows(pad_start, pad_end, dest, h1p, n_rows, rb, tm=256):
    T, W = h1p.shape
    return pl.pallas_call(
        functools.partial(_scatter_kernel, rb=rb),
        grid_spec=pltpu.PrefetchScalarGridSpec(
            num_scalar_prefetch=2,
            grid=(T // tm,),
            in_specs=[
                pl.BlockSpec((None, 1, tm * TOP_K), lambda i, ps, pe: (i, 0, 0), memory_space=pltpu.SMEM),
                pl.BlockSpec((tm, W), lambda i, ps, pe: (i, 0)),
            ],
            out_specs=pl.BlockSpec(memory_space=pl.ANY),
            scratch_shapes=[pltpu.VMEM((rb, W), h1p.dtype), pltpu.SemaphoreType.DMA],
        ),
        out_shape=jax.ShapeDtypeStruct((n_rows, W), h1p.dtype),
        compiler_params=_cparams(("arbitrary",), 32),
        name="scatter",
    )(pad_start, pad_end, _dest_tiles(dest, tm), h1p)


def _experts_kernel(first_ref, nblk_ref, wsel_ref, tot_ref, xs_ref, wg_ref, wu_ref, wd_ref, ys_ref,
                    xbuf, ybuf, wgub, wdb, xsem, ysem, *, rb):
    del wsel_ref
    e = pl.program_id(0)
    F = wg_ref.shape[1]
    total = tot_ref[0]
    n = nblk_ref[e]
    g0 = first_ref[e]

    def x_copy(g, slot):
        return pltpu.make_async_copy(xs_ref.at[pl.ds(pl.multiple_of(g * rb, rb), rb)], xbuf.at[slot], xsem.at[slot])

    def y_copy(g, slot):
        return pltpu.make_async_copy(ybuf.at[slot], ys_ref.at[pl.ds(pl.multiple_of(g * rb, rb), rb)], ysem.at[slot])

    @pl.when(e == 0)
    def _():
        x_copy(0, 0).start()

    @pl.when(n > 0)
    def _():
        wgub[:, 0:F] = wg_ref[...].astype(BF16)
        wgub[:, F:] = wu_ref[...].astype(BF16)
        wdb[...] = wd_ref[...].astype(BF16)

        def block(j, c):
            g = g0 + j
            slot = g % 2
            x_copy(g, slot).wait()

            @pl.when(g + 1 < total)
            def _():
                x_copy(g + 1, 1 - slot).start()

            lo, hi = _unpack_pairs(xbuf[slot])
            x = jnp.concatenate([lo, hi], axis=1).astype(BF16)
            gu = jnp.dot(x, wgub[...], preferred_element_type=F32)
            gate, up = gu[:, 0:F], gu[:, F:]
            hb = (gate / (1.0 + jnp.exp(-gate)) * up).astype(BF16)
            y = _pack_pairs(jnp.dot(hb, wdb[...], preferred_element_type=F32))

            @pl.when(g >= 2)
            def _():
                y_copy(g - 2, slot).wait()

            ybuf[slot] = y
            y_copy(g, slot).start()
            return c

        lax.fori_loop(0, n, block, 0)

    @pl.when(e == pl.num_programs(0) - 1)
    def _():
        @pl.when(total >= 2)
        def _():
            y_copy(total - 2, total % 2).wait()

        y_copy(total - 1, (total - 1) % 2).wait()


def _experts(first_blk, nblk, wsel, total_blk, xs, w_gate, w_up, w_down, rb):
    P, W = xs.shape
    E, D, F = w_gate.shape
    wspec = lambda shape: pl.BlockSpec((None,) + shape, lambda e, fb, nbk, ws, tt: (ws[e], 0, 0))
    return pl.pallas_call(
        functools.partial(_experts_kernel, rb=rb),
        grid_spec=pltpu.PrefetchScalarGridSpec(
            num_scalar_prefetch=4,
            grid=(E,),
            in_specs=[pl.BlockSpec(memory_space=pl.ANY), wspec((D, F)), wspec((D, F)), wspec((F, D))],
            out_specs=pl.BlockSpec(memory_space=pl.ANY),
            scratch_shapes=[
                pltpu.VMEM((2, rb, W), jnp.uint32), pltpu.VMEM((2, rb, W), jnp.uint32),
                pltpu.VMEM((D, 2 * F), BF16), pltpu.VMEM((F, D), BF16),
                pltpu.SemaphoreType.DMA((2,)), pltpu.SemaphoreType.DMA((2,)),
            ],
        ),
        out_shape=jax.ShapeDtypeStruct((P, W), jnp.uint32),
        compiler_params=_cparams(("arbitrary",), 48),
        name="experts",
    )(first_blk, nblk, wsel, total_blk, xs, w_gate, w_up, w_down)


def _combine_kernel(dest_ref, destn_ref, h1_ref, gate_ref, ys_ref, sg_ref, su_ref, sd_ref,
                    g2_ref, b2_ref, o_ref, buf, sem):
    i = pl.program_id(0)
    n = pl.num_programs(0)
    tm = h1_ref.shape[0]
    slot = i % 2

    sub = V7X_SUBLANES

    def issue(dref, s):
        def body(r8, c):
            for j in range(sub):
                for k in range(TOP_K):
                    src = dref[0, r8 * (sub * TOP_K) + (j * TOP_K + k)]
                    pltpu.make_async_copy(ys_ref.at[pl.ds(src, 1)], buf.at[s, k, pl.ds(r8 * sub + j, 1)],
                                          sem.at[s]).start(priority=k % 2)
            return c
        lax.fori_loop(0, tm // sub, body, 0)

    @pl.when(i == 0)
    def _():
        issue(dest_ref, 0)

    @pl.when(i + 1 < n)
    def _():
        issue(destn_ref, 1 - slot)

    h1 = h1_ref[...]
    hb = h1.astype(BF16)
    g = jnp.dot(hb, sg_ref[...], preferred_element_type=F32)
    u = jnp.dot(hb, su_ref[...], preferred_element_type=F32)
    ffn = jnp.dot((g / (1.0 + jnp.exp(-g)) * u).astype(BF16), sd_ref[...], preferred_element_type=F32)

    for k in range(TOP_K):
        pltpu.make_async_copy(ys_ref.at[pl.ds(0, tm)], buf.at[slot, k], sem.at[slot]).wait()
    gates = gate_ref[...]
    lo_acc = jnp.zeros((tm, buf.shape[3]), F32)
    hi_acc = jnp.zeros((tm, buf.shape[3]), F32)
    for k in range(TOP_K):
        lo, hi = _unpack_pairs(buf[slot, k])
        gk = gates[:, k:k + 1]
        lo_acc = lo_acc + gk * lo
        hi_acc = hi_acc + gk * hi
    ffn = ffn + jnp.concatenate([lo_acc, hi_acc], axis=1)
    o_ref[...] = _layer_norm_rows(ALPHA * h1 + ffn, g2_ref[...], b2_ref[...])


def _combine(dest, h1, gates_tk, ys, sg, su, sd, g2, b2, tm=128):
    T, D = h1.shape
    W = ys.shape[1]
    F = sg.shape[1]
    nt = T // tm
    row = lambda i: (i, 0)
    fixed = lambda i: (0, 0)
    smem = functools.partial(pl.BlockSpec, (None, 1, tm * TOP_K), memory_space=pltpu.SMEM)
    dest_t = _dest_tiles(dest, tm)
    return pl.pallas_call(
        _combine_kernel,
        grid=(nt,),
        in_specs=[
            smem(index_map=lambda i: (i, 0, 0)),
            smem(index_map=lambda i: (jnp.minimum(i + 1, nt - 1), 0, 0)),
            pl.BlockSpec((tm, D), row),
            pl.BlockSpec((tm, TOP_K), row),
            pl.BlockSpec(memory_space=pl.ANY),
            pl.BlockSpec((D, F), fixed), pl.BlockSpec((D, F), fixed), pl.BlockSpec((F, D), fixed),
            pl.BlockSpec((1, D), fixed), pl.BlockSpec((1, D), fixed),
        ],
        out_specs=pl.BlockSpec((tm, D), row),
        out_shape=jax.ShapeDtypeStruct((T, D), F32),
        scratch_shapes=[pltpu.VMEM((2, TOP_K, tm, W), jnp.uint32), pltpu.SemaphoreType.DMA((2,))],
        compiler_params=_cparams(("arbitrary",), 48),
        name="combine",
    )(dest_t, dest_t, h1, gates_tk, ys, sg, su, sd, g2, b2)


MOE_ROWS = 256


def _moe(h1, h1p, scores_t, router_bias, w_gate, w_up, w_down, ws_gate, ws_up, ws_down, g2, b2):
    T = h1.shape[0]
    E = N_EXPERTS
    rb = MOE_ROWS
    idx, gates, rank, counts = _route(scores_t, router_bias[:, None])
    counts = counts[:, 0]
    padded = (counts + rb - 1) // rb * rb
    pad_end = jnp.cumsum(padded)
    pad_start = pad_end - padded
    nb = (T * TOP_K) // rb + E
    nblk = (padded // rb).astype(jnp.int32)
    first_blk = (pad_start // rb).astype(jnp.int32)
    total_blk = (pad_end[-1:] // rb).astype(jnp.int32)
    wsel = lax.cummax(jnp.where(nblk > 0, jnp.arange(E, dtype=jnp.int32), 0))
    dest = _dest_rows(idx, rank, pad_start)
    xs = _scatter_rows(pad_start, pad_end, dest, h1p, nb * rb, rb)
    ys = _experts(first_blk, nblk, wsel, total_blk, xs, w_gate, w_up, w_down, rb)
    return _combine(dest, h1, gates.T, ys, ws_gate.astype(BF16), ws_up.astype(BF16), ws_down.astype(BF16), g2, b2)


def kernel(x, positions, emb_ln_g, emb_ln_b, w_in, hy_conv_w, hy_conv_b, hy_f_w1, hy_f_b1, hy_f_w2, hy_f_b2, hy_f_w3, hy_f_b3, hy_f_freq, hy_f_wout, hy_d, lambda_q1, lambda_k1, lambda_q2, lambda_k2, subln_g, w_o, ln1_g, ln1_b, w_router, router_bias, w_gate, w_up, w_down, ws_gate, ws_up, ws_down, ln2_g, ln2_b):
    B, L, D = x.shape
    T = B * L
    assert w_in.shape[0] == DEPTH == 1
    i = 0
    x2 = x.reshape(T, D)
    g0, b0 = emb_ln_g[None], emb_ln_b[None]
    ra, rm, rp = _rotary_tables(positions)
    q, k, v, u = _inproj(x2, g0, b0, w_in[i].astype(BF16), ra, rm, rp)
    lam = (jnp.exp(jnp.sum(lambda_q1[i] * lambda_k1[i])) - jnp.exp(jnp.sum(lambda_q2[i] * lambda_k2[i])) + LAM_INIT)
    attn = _attention(q, k, v, lam.reshape(1).astype(F32), subln_g[i][None], B, L)
    hy = _hyena(u, hy_conv_w[i], hy_conv_b[i], hy_f_w1[i], hy_f_b1[i], hy_f_w2[i], hy_f_b2[i], hy_f_w3[i], hy_f_b3[i],
                hy_f_freq[i], hy_f_wout[i], hy_d[i], B, L)
    h1, h1p, scores_t = _oproj(x2, g0, b0, attn, hy, w_o[i].astype(BF16), ln1_g[i][None], ln1_b[i][None], w_router[i].T)
    out = _moe(h1, h1p, scores_t, router_bias[i], w_gate[i], w_up[i], w_down[i], ws_gate[i], ws_up[i], ws_down[i],
               ln2_g[i][None], ln2_b[i][None])
    return out.reshape(B, L, D)
```

```python
import functools
import math

import numpy as np
import jax
import jax.numpy as jnp
from jax import lax
from jax.experimental import pallas as pl
from jax.experimental.pallas import tpu as pltpu

DA_HEADS = 4
DA_HEAD_DIM = 64
DA_V_DIM = 128
ATTN_WIDTH = 512
HYENA_WIDTH = 512
ROT_DIM = 16
ROPE_THETA = 500000.0
SHORT_CONV = 3
FILTER_EMB = 33
FILTER_BANDS = 16
DECAY_TARGET = 1e-2
FAST_DECAY = 0.3
SLOW_DECAY = 1.5
N_EXPERTS = 256
TOP_K = 8
N_GROUPS = 8
GROUP_SIZE = N_EXPERTS // N_GROUPS
TOPK_GROUPS = 4
EXPERT_DIM = 256
ROUTED_SCALE = 2.5
DEPTH = 1
ALPHA = (2 * DEPTH) ** 0.25
LN_EPS = 1e-5
LAM_INIT = 0.8 - 0.6 * math.exp(-0.3 * 0)

V7X_LANES = 128
V7X_SUBLANES = 8
V7X_VMEM_BYTES = 64 * 1024 * 1024

DFT_N2 = 128
DFT_N2H = DFT_N2 // V7X_SUBLANES

BF16 = jnp.bfloat16
F32 = jnp.float32


def _cparams(sem, vmem_mb):
    return pltpu.CompilerParams(dimension_semantics=sem, vmem_limit_bytes=vmem_mb * 1024 * 1024)


def _layer_norm_rows(x, g, b):
    mu = jnp.mean(x, axis=-1, keepdims=True)
    xc = x - mu
    var = jnp.mean(xc * xc, axis=-1, keepdims=True)
    return xc * lax.rsqrt(var + LN_EPS) * g + b


def _inproj_kernel(x_ref, g_ref, b_ref, w_ref, ra_ref, rm_ref, rp_ref, q_ref, k_ref, v_ref, u_ref):
    h = _layer_norm_rows(x_ref[...], g_ref[...], b_ref[...]).astype(BF16)
    ra, rm, rp = ra_ref[...], rm_ref[...], rp_ref[...]

    def rot(t):
        return t * ra + pltpu.roll(t, V7X_LANES - ROT_DIM // 2, axis=1) * rm + pltpu.roll(t, ROT_DIM // 2, axis=1) * rp

    aw = ATTN_WIDTH
    qp = jnp.dot(h, w_ref[:, 0:aw], preferred_element_type=F32)
    kp = jnp.dot(h, w_ref[:, aw:2 * aw], preferred_element_type=F32)
    scale = DA_HEAD_DIM ** -0.5 * math.log2(math.e)
    for c in range(aw // V7X_LANES):
        sl = slice(c * V7X_LANES, (c + 1) * V7X_LANES)
        q_ref[:, sl] = (rot(qp[:, sl]) * scale).astype(BF16)
        k_ref[:, sl] = rot(kp[:, sl]).astype(BF16)
    vp = jnp.dot(h, w_ref[:, 2 * aw:3 * aw], preferred_element_type=F32).astype(BF16)
    ones = jnp.ones((vp.shape[0], DA_V_DIM), BF16)
    for hd in range(DA_HEADS):
        v_ref[:, 2 * hd * DA_V_DIM:(2 * hd + 1) * DA_V_DIM] = vp[:, hd * DA_V_DIM:(hd + 1) * DA_V_DIM]
        v_ref[:, (2 * hd + 1) * DA_V_DIM:(2 * hd + 2) * DA_V_DIM] = ones
    u_ref[...] = jnp.dot(h, w_ref[:, 3 * aw:], preferred_element_type=F32)


def _inproj(x2, g, b, w_bf, ra, rm, rp, tm=256):
    T, D = x2.shape
    ncol = w_bf.shape[1]
    aw = ATTN_WIDTH
    uw = ncol - 3 * aw
    row = lambda i: (i, 0)
    fixed = lambda i: (0, 0)
    return pl.pallas_call(
        _inproj_kernel,
        grid=(T // tm,),
        in_specs=[
            pl.BlockSpec((tm, D), row),
            pl.BlockSpec((1, D), fixed),
            pl.BlockSpec((1, D), fixed),
            pl.BlockSpec((D, ncol), fixed),
            pl.BlockSpec((tm, V7X_LANES), row),
            pl.BlockSpec((tm, V7X_LANES), row),
            pl.BlockSpec((tm, V7X_LANES), row),
        ],
        out_specs=[
            pl.BlockSpec((tm, aw), row),
            pl.BlockSpec((tm, aw), row),
            pl.BlockSpec((tm, 2 * aw), row),
            pl.BlockSpec((tm, uw), row),
        ],
        out_shape=[
            jax.ShapeDtypeStruct((T, aw), BF16),
            jax.ShapeDtypeStruct((T, aw), BF16),
            jax.ShapeDtypeStruct((T, 2 * aw), BF16),
            jax.ShapeDtypeStruct((T, uw), F32),
        ],
        compiler_params=_cparams(("parallel",), 48),
        name="inproj",
    )(x2, g, b, w_bf, ra, rm, rp)


def _rotary_tables(positions):
    half = ROT_DIM // 2
    inv_freq = ROPE_THETA ** (-jnp.arange(0, ROT_DIM, 2, dtype=F32) / ROT_DIM)
    ang = positions.astype(F32).reshape(-1)[:, None] * inv_freq
    cos, sin = jnp.cos(ang), jnp.sin(ang)
    T = ang.shape[0]
    ones = jnp.ones((T, DA_HEAD_DIM - ROT_DIM), F32)
    zeros_h = jnp.zeros((T, half), F32)
    zeros_r = jnp.zeros((T, DA_HEAD_DIM - ROT_DIM), F32)
    a64 = jnp.concatenate([cos, cos, ones], axis=1)
    m64 = jnp.concatenate([-sin, zeros_h, zeros_r], axis=1)
    p64 = jnp.concatenate([zeros_h, sin, zeros_r], axis=1)
    rep = V7X_LANES // DA_HEAD_DIM
    return jnp.tile(a64, (1, rep)), jnp.tile(m64, (1, rep)), jnp.tile(p64, (1, rep))


def _attn_kernel(lam_ref, q_ref, k_ref, v_ref, g_ref, o_ref, s_ref, m_ref, acc_ref, *, kb, unroll):
    qb = q_ref.shape[0]
    L = k_ref.shape[0]
    nchunk = L // kb
    nl = kb // V7X_LANES
    q = q_ref[...]
    lane = lax.broadcasted_iota(jnp.int32, q.shape, 1)
    zero = jnp.zeros_like(q)
    qm = [jnp.where(lane < DA_HEAD_DIM, q, zero), jnp.where(lane >= DA_HEAD_DIM, q, zero)]
    m_ref[...] = jnp.full(m_ref.shape, -jnp.inf, F32)
    acc_ref[...] = jnp.zeros(acc_ref.shape, F32)

    def score_body(j, carry):
        kc = k_ref[pl.ds(pl.multiple_of(j * kb, kb), kb), :]
        for c in range(2):
            s = lax.dot_general(qm[c], kc, (((1,), (1,)), ((), ())), preferred_element_type=F32)
            s_ref[c, j] = s
            m = s[:, 0:V7X_LANES]
            for t in range(1, nl):
                m = jnp.maximum(m, s[:, t * V7X_LANES:(t + 1) * V7X_LANES])
            m_ref[c] = jnp.maximum(m_ref[c], m)
        return carry

    lax.fori_loop(0, nchunk, score_body, 0, unroll=unroll)
    m_row = [jnp.max(m_ref[c], axis=1, keepdims=True) for c in range(2)]

    def pv_body(j, carry):
        vc = v_ref[pl.ds(pl.multiple_of(j * kb, kb), kb), :]
        for c in range(2):
            p = jnp.exp2(s_ref[c, j] - m_row[c])
            acc_ref[c] += jnp.dot(p.astype(BF16), vc, preferred_element_type=F32)
        return carry

    lax.fori_loop(0, nchunk, pv_body, 0, unroll=unroll)
    outs = [acc_ref[c, :, 0:DA_V_DIM] / acc_ref[c, :, DA_V_DIM:2 * DA_V_DIM] for c in range(2)]
    o = outs[0] - lam_ref[0] * outs[1]
    ms = jnp.mean(o * o, axis=1, keepdims=True)
    o_ref[...] = (o * lax.rsqrt(ms + LN_EPS) * g_ref[...] * (1.0 - LAM_INIT)).astype(o_ref.dtype)


def _attention(q, k, v, lam, subln_g, B, L, qb=512, kb=512, unroll=4):
    T = q.shape[0]
    nq = L // qb
    return pl.pallas_call(
        functools.partial(_attn_kernel, kb=kb, unroll=unroll),
        grid=(B, DA_HEADS, nq),
        in_specs=[
            pl.BlockSpec(memory_space=pltpu.SMEM),
            pl.BlockSpec((qb, DA_V_DIM), lambda b, h, i: (b * nq + i, h)),
            pl.BlockSpec((L, DA_V_DIM), lambda b, h, i: (b, h)),
            pl.BlockSpec((L, 2 * DA_V_DIM), lambda b, h, i: (b, h)),
            pl.BlockSpec((1, DA_V_DIM), lambda b, h, i: (0, 0)),
        ],
        out_specs=pl.BlockSpec((qb, DA_V_DIM), lambda b, h, i: (b * nq + i, h)),
        out_shape=jax.ShapeDtypeStruct((T, ATTN_WIDTH), BF16),
        scratch_shapes=[
            pltpu.VMEM((2, L // kb, qb, kb), F32),
            pltpu.VMEM((2, qb, V7X_LANES), F32),
            pltpu.VMEM((2, qb, 2 * DA_V_DIM), F32),
        ],
        compiler_params=_cparams(("parallel", "parallel", "parallel"), 56),
        name="attn",
    )(lam, q, k, v, subln_g)


def _hpre_kernel(u_ref, up_ref, un_ref, w_ref, b_ref, x0_ref, vx_ref):
    i = pl.program_id(1)
    n = pl.num_programs(1)
    tb = u_ref.shape[0]
    C = x0_ref.shape[-1]
    row = lax.broadcasted_iota(jnp.int32, (tb, V7X_LANES), 0)
    has_prev = (i > 0).astype(F32)
    has_next = (i < n - 1).astype(F32)

    def conv(c0):
        sl = slice(c0, c0 + V7X_LANES)
        u = u_ref[:, sl]
        prev_row = up_ref[V7X_SUBLANES - 1:V7X_SUBLANES, sl] * has_prev
        next_row = un_ref[0:1, sl] * has_next
        u_prev = jnp.where(row == 0, prev_row, pltpu.roll(u, 1, axis=0))
        u_next = jnp.where(row == tb - 1, next_row, pltpu.roll(u, tb - 1, axis=0))
        return u_prev * w_ref[0:1, sl] + u * w_ref[1:2, sl] + u_next * w_ref[2:3, sl] + b_ref[:, sl]

    tile = x0_ref.shape[:-1] + (V7X_LANES,)
    for c in range(C // V7X_LANES):
        c0 = c * V7X_LANES
        x0_ref[:, :, :, c0:c0 + V7X_LANES] = conv(c0).reshape(tile)
        vx_ref[:, :, :, c0:c0 + V7X_LANES] = (conv(2 * C + c0) * conv(C + c0)).reshape(tile)


def _hpre(u, conv_w, conv_b, B, L, tb=512):
    T, C3 = u.shape
    C = C3 // 3
    nt = L // tb
    sub = V7X_SUBLANES
    na = tb // DFT_N2
    split = pl.BlockSpec((None, na, DFT_N2H, sub, C), lambda b, i: (b, i, 0, 0, 0))
    split_shape = jax.ShapeDtypeStruct((B, L // DFT_N2, DFT_N2H, sub, C), F32)
    cur = lambda b, i: (b * nt + i, 0)
    prev = lambda b, i: (jnp.maximum((b * L + i * tb) // sub - 1, 0), 0)
    nxt = lambda b, i: (jnp.minimum((b * L + (i + 1) * tb) // sub, T // sub - 1), 0)
    fixed = lambda b, i: (0, 0)
    return pl.pallas_call(
        _hpre_kernel,
        grid=(B, nt),
        in_specs=[
            pl.BlockSpec((tb, C3), cur),
            pl.BlockSpec((sub, C3), prev),
            pl.BlockSpec((sub, C3), nxt),
            pl.BlockSpec((SHORT_CONV, C3), fixed),
            pl.BlockSpec((1, C3), fixed),
        ],
        out_specs=[split, split],
        out_shape=[split_shape, split_shape],
        compiler_params=_cparams(("parallel", "parallel"), 32),
        name="hpre",
    )(u, u, u, conv_w, conv_b)


def _filt_kernel(z_ref, w1_ref, b1_ref, w2_ref, b2_ref, w3_ref, b3_ref, fr_ref, wo_ref, dl_ref,
                 hfb_ref, asum_ref, *, L):
    i = pl.program_id(0)
    tl = z_ref.shape[0]
    C = dl_ref.shape[1]
    hp = lax.Precision.HIGHEST
    h = jnp.sin(fr_ref[0:1, :] * (jnp.dot(z_ref[...], w1_ref[...], precision=hp, preferred_element_type=F32) + b1_ref[...]))
    h = jnp.sin(fr_ref[1:2, :] * (jnp.dot(h, w2_ref[...], precision=hp, preferred_element_type=F32) + b2_ref[...]))
    h = jnp.sin(fr_ref[2:3, :] * (jnp.dot(h, w3_ref[...], precision=hp, preferred_element_type=F32) + b3_ref[...]))
    o = jnp.dot(h, wo_ref[...], precision=hp, preferred_element_type=F32)
    grow = lax.broadcasted_iota(jnp.int32, (tl, C), 0) + i * tl
    t = grow.astype(F32) * (1.0 / (L - 1))
    decay = jnp.exp(-t * dl_ref[...])
    hf = o[:, :C] * decay
    hb = jnp.where(grow == 0, 0.0, o[:, C:] * decay)
    tile = hfb_ref.shape[:-1] + (C,)
    hfb_ref[:, :, :, :C] = hf.reshape(tile)
    hfb_ref[:, :, :, C:] = hb.reshape(tile)

    @pl.when(i == 0)
    def _():
        asum_ref[...] = jnp.zeros_like(asum_ref)

    asum_ref[...] += jnp.sum(jnp.abs(hf) + jnp.abs(hb), axis=0, keepdims=True)


def _filter_taps(L, w1, b1, w2, b2, w3, b3, freq, wout, tl=512):
    C = wout.shape[1] // 2
    order = w1.shape[1]
    emb = w1.shape[0]
    t = jnp.linspace(0.0, 1.0, L, dtype=F32)[:, None]
    w = 2.0 * math.pi * jnp.arange(L, dtype=F32)[:, None] / L
    f = jnp.linspace(1e-4, FILTER_BANDS - 1, FILTER_BANDS, dtype=F32)[None, :]
    z = jnp.concatenate([t, jnp.cos(f * w), -jnp.sin(f * w)], axis=-1)
    zp = jnp.pad(z, ((0, 0), (0, V7X_LANES - emb)))
    w1p = jnp.pad(w1, ((0, V7X_LANES - emb), (0, 0)))
    deltas = jnp.abs(jnp.linspace(math.log(DECAY_TARGET) / SLOW_DECAY, math.log(DECAY_TARGET) / FAST_DECAY, C, dtype=F32))[None]
    fixed = lambda i: (0, 0)
    return pl.pallas_call(
        functools.partial(_filt_kernel, L=L),
        grid=(L // tl,),
        in_specs=[
            pl.BlockSpec((tl, V7X_LANES), lambda i: (i, 0)),
            pl.BlockSpec((V7X_LANES, order), fixed), pl.BlockSpec((1, order), fixed),
            pl.BlockSpec((order, order), fixed), pl.BlockSpec((1, order), fixed),
            pl.BlockSpec((order, order), fixed), pl.BlockSpec((1, order), fixed),
            pl.BlockSpec((3, order), fixed),
            pl.BlockSpec((order, 2 * C), fixed),
            pl.BlockSpec((1, C), fixed),
        ],
        out_specs=[pl.BlockSpec((None, tl // DFT_N2, DFT_N2H, V7X_SUBLANES, 2 * C), lambda i: (0, i, 0, 0, 0)),
                   pl.BlockSpec((1, C), fixed)],
        out_shape=[jax.ShapeDtypeStruct((1, L // DFT_N2, DFT_N2H, V7X_SUBLANES, 2 * C), F32),
                   jax.ShapeDtypeStruct((1, C), F32)],
        compiler_params=_cparams(("arbitrary",), 32),
        name="filt",
    )(zp, w1p, b1[None], w2, b2[None], w3, b3[None], freq, wout, deltas)


def _dft_constants(L):
    n1 = 2 * L // DFT_N2
    n1h = n1 // 2
    N = 2 * L
    sub = V7X_SUBLANES

    def cs(num, den):
        ang = (num % den).astype(np.float64) * (2.0 * np.pi / den)
        return np.cos(ang), np.sin(ang)

    k1 = np.arange(n1, dtype=np.int64)
    a = np.arange(n1h, dtype=np.int64)
    c, s = cs(k1[:, None] * a[None, :], n1)
    eye = np.eye(sub)

    def expand(m):
        r, kk = m.shape
        return (m[:, None, :, None] * eye[None, :, None, :]).reshape(r * sub, kk * sub)

    def const(m):
        return jnp.asarray(m.astype(np.float32).astype(BF16))

    m1c = np.stack([np.concatenate([c, s], axis=1), np.concatenate([-s, c], axis=1)], axis=1).reshape(2 * n1, 2 * n1h)
    m1r = np.stack([c, -s], axis=1).reshape(2 * n1, n1h)
    ct, st = c.T, s.T
    m3 = np.stack([np.stack([ct, -st], axis=2).reshape(n1h, 2 * n1),
                   np.stack([st, ct], axis=2).reshape(n1h, 2 * n1)], axis=0).reshape(2 * n1h, 2 * n1)
    n2 = np.arange(DFT_N2, dtype=np.int64)
    gc, gs = cs(n2[:, None] * n2[None, :], DFT_N2)
    g_fwd = np.concatenate([np.concatenate([gc, gs], axis=1), np.concatenate([-gs, gc], axis=1)], axis=0)
    g_inv = np.concatenate([np.concatenate([gc, -gs], axis=1), np.concatenate([gs, gc], axis=1)], axis=0)
    ph = (jnp.arange(n1, dtype=jnp.int32)[:, None] * jnp.arange(DFT_N2, dtype=jnp.int32)[None, :]) % N
    ang = ph.astype(F32) * (2.0 * math.pi / N)
    lanes = (n1, DFT_N2H, sub, V7X_LANES)
    twc = jnp.broadcast_to(jnp.cos(ang).reshape(n1, DFT_N2H, sub, 1), lanes)
    tws = jnp.broadcast_to(jnp.sin(ang).reshape(n1, DFT_N2H, sub, 1), lanes)
    return dict(n1=n1, n1h=n1h, m1c=const(expand(m1c)), m1r=const(expand(m1r)), m3=const(expand(m3)),
                g_fwd=const(g_fwd), g_inv=const(g_inv), twc=twc, tws=tws)


def _dft1_kernel(x_ref, m_ref, o_ref):
    rows = m_ref.shape[1]
    xs = x_ref[...].reshape(rows, x_ref.shape[-1]).astype(BF16)
    o_ref[...] = jnp.dot(m_ref[...], xs, preferred_element_type=F32).reshape(o_ref.shape)


def _dft1(x5, m, n1, cb):
    P, n1h, _, sub, Cx = x5.shape
    return pl.pallas_call(
        _dft1_kernel,
        grid=(DFT_N2H, Cx // cb),
        in_specs=[
            pl.BlockSpec((P, n1h, None, sub, cb), lambda h, c: (0, 0, h, 0, c)),
            pl.BlockSpec(m.shape, lambda h, c: (0, 0)),
        ],
        out_specs=pl.BlockSpec((None, n1, 2, sub, cb), lambda h, c: (h, 0, 0, 0, c)),
        out_shape=jax.ShapeDtypeStruct((DFT_N2H, n1, 2, sub, Cx), F32),
        compiler_params=_cparams(("parallel", "parallel"), 48),
        name="dft1",
    )(x5, m)


def _dft2_kernel(a_ref, f_ref, twc_ref, tws_ref, gf_ref, gi_ref, o_ref):
    C = a_ref.shape[-1]
    half = DFT_N2
    twc, tws = twc_ref[...], tws_ref[...]

    def lanes(fn):
        return jnp.concatenate([fn(slice(c0, c0 + V7X_LANES)) for c0 in range(0, C, V7X_LANES)], axis=-1)

    def spectrum(ref, col0):
        re = lanes(lambda sl: ref[:, 0, :, col0 + sl.start:col0 + sl.stop] * twc + ref[:, 1, :, col0 + sl.start:col0 + sl.stop] * tws)
        im = lanes(lambda sl: ref[:, 1, :, col0 + sl.start:col0 + sl.stop] * twc - ref[:, 0, :, col0 + sl.start:col0 + sl.stop] * tws)
        t = jnp.concatenate([re.reshape(half, C), im.reshape(half, C)], axis=0).astype(BF16)
        s = jnp.dot(gf_ref[...], t, preferred_element_type=F32)
        return s[:half], s[half:]

    xr, xi = spectrum(a_ref, 0)
    fr, fi = spectrum(f_ref, 0)
    br, bi = spectrum(f_ref, C)
    hr, hi = fr + br, fi - bi
    y = jnp.concatenate([xr * hr - xi * hi, xr * hi + xi * hr], axis=0).astype(BF16)
    b = jnp.dot(gi_ref[...], y, preferred_element_type=F32)
    br2 = b[:half].reshape(DFT_N2H, V7X_SUBLANES, C)
    bi2 = b[half:].reshape(DFT_N2H, V7X_SUBLANES, C)
    for c0 in range(0, C, V7X_LANES):
        sl = slice(c0, c0 + V7X_LANES)
        o_ref[:, 0, :, sl] = br2[:, :, sl] * twc - bi2[:, :, sl] * tws
        o_ref[:, 1, :, sl] = bi2[:, :, sl] * twc + br2[:, :, sl] * tws


def _dft2(a5, f5, k):
    n2h, n1, _, sub, C = a5.shape
    return pl.pallas_call(
        _dft2_kernel,
        grid=(n1,),
        in_specs=[
            pl.BlockSpec((n2h, None, 2, sub, C), lambda i: (0, i, 0, 0, 0)),
            pl.BlockSpec((n2h, None, 2, sub, 2 * C), lambda i: (0, i, 0, 0, 0)),
            pl.BlockSpec((None, n2h, sub, V7X_LANES), lambda i: (i, 0, 0, 0)),
            pl.BlockSpec((None, n2h, sub, V7X_LANES), lambda i: (i, 0, 0, 0)),
            pl.BlockSpec((2 * DFT_N2, 2 * DFT_N2), lambda i: (0, 0)),
            pl.BlockSpec((2 * DFT_N2, 2 * DFT_N2), lambda i: (0, 0)),
        ],
        out_specs=pl.BlockSpec((n2h, None, 2, sub, C), lambda i: (0, i, 0, 0, 0)),
        out_shape=jax.ShapeDtypeStruct(a5.shape, F32),
        compiler_params=_cparams(("parallel",), 32),
        name="dft2",
    )(a5, f5, k["twc"], k["tws"], k["g_fwd"], k["g_inv"])


def _dft3_kernel(b_ref, m_ref, x0_ref, vx_ref, sc_ref, d_ref, o_ref):
    C = b_ref.shape[-1]
    bs = b_ref[...].reshape(m_ref.shape[1], C).astype(BF16)
    y = jnp.dot(m_ref[...], bs, preferred_element_type=F32).reshape(o_ref.shape)
    o_ref[...] = x0_ref[...] * (y * sc_ref[...] + vx_ref[...] * d_ref[...])


def _dft3(b5, m3, x05, vx5, scale, d):
    n2h, n1, _, sub, C = b5.shape
    Bt, n1h = x05.shape[0], x05.shape[1]
    tok = pl.BlockSpec((Bt, n1h, None, sub, C), lambda h: (0, 0, h, 0, 0))
    vec = pl.BlockSpec((1, C), lambda h: (0, 0))
    return pl.pallas_call(
        _dft3_kernel,
        grid=(n2h,),
        in_specs=[
            pl.BlockSpec((None, n1, 2, sub, C), lambda h: (h, 0, 0, 0, 0)),
            pl.BlockSpec(m3.shape, lambda h: (0, 0)),
            tok, tok, vec, vec,
        ],
        out_specs=tok,
        out_shape=jax.ShapeDtypeStruct(x05.shape, F32),
        compiler_params=_cparams(("parallel",), 48),
        name="dft3",
    )(b5, m3, x05, vx5, scale, d)


def _hyena(u, conv_w, conv_b, w1, b1, w2, b2, w3, b3, freq, wout, d_skip, B, L):
    T = u.shape[0]
    C = HYENA_WIDTH
    k = _dft_constants(L)
    n1, n1h = k["n1"], k["n1h"]
    x0, vx = _hpre(u, conv_w, conv_b[None], B, L)
    hfb, asum = _filter_taps(L, w1, b1, w2, b2, w3, b3, freq, wout)
    a5 = _dft1(vx, k["m1c"], n1, cb=C)
    f5 = _dft1(hfb, k["m1r"], n1, cb=C)
    b5 = _dft2(a5, f5, k)
    scale = 1.0 / (asum * (2 * L))
    return _dft3(b5, k["m3"], x0, vx, scale, d_skip[None])


def _bf16_bits(x):
    a = pltpu.bitcast(x, jnp.uint32)
    return (a + jnp.uint32(0x7FFF) + ((a >> 16) & jnp.uint32(1))) >> 16


def _pack_pairs(x):
    half = x.shape[1] // 2
    return _bf16_bits(x[:, :half]) | (_bf16_bits(x[:, half:]) << 16)


def _unpack_pairs(w):
    lo = pltpu.bitcast(w << 16, F32)
    hi = pltpu.bitcast(w & jnp.uint32(0xFFFF0000), F32)
    return lo, hi


def _oproj_kernel(x_ref, g0_ref, b0_ref, at_ref, hy_ref, wo_ref, g1_ref, b1_ref, wrh_ref, wrl_ref,
                  h1_ref, h1p_ref, sc_ref):
    aw = at_ref.shape[1]
    h0 = _layer_norm_rows(x_ref[...], g0_ref[...], b0_ref[...])
    mixed = jnp.dot(at_ref[...], wo_ref[0:aw, :], preferred_element_type=F32)
    hy = hy_ref[...].reshape(x_ref.shape[0], hy_ref.shape[-1])
    mixed = mixed + jnp.dot(hy.astype(BF16), wo_ref[aw:, :], preferred_element_type=F32)
    h1 = _layer_norm_rows(ALPHA * h0 + mixed, g1_ref[...], b1_ref[...])
    h1_ref[...] = h1
    h1p_ref[...] = _pack_pairs(h1)
    hh = h1.astype(BF16)
    hl = (h1 - hh.astype(F32)).astype(BF16)
    dn = (((1,), (1,)), ((), ()))
    logits = lax.dot_general(wrh_ref[...], hh, dn, preferred_element_type=F32)
    logits = logits + (lax.dot_general(wrh_ref[...], hl, dn, preferred_element_type=F32)
                       + lax.dot_general(wrl_ref[...], hh, dn, preferred_element_type=F32))
    sc_ref[...] = 1.0 / (1.0 + jnp.exp(-logits))


def _oproj(x2, g0, b0, attn, hy, wo_bf, g1, b1, wr_t, tm=256):
    T, D = x2.shape
    E = wr_t.shape[0]
    wrh = wr_t.astype(BF16)
    wrl = (wr_t - wrh.astype(F32)).astype(BF16)
    row = lambda i: (i, 0)
    fixed = lambda i: (0, 0)
    hy4 = hy.reshape((-1,) + hy.shape[2:])
    return pl.pallas_call(
        _oproj_kernel,
        grid=(T // tm,),
        in_specs=[
            pl.BlockSpec((tm, D), row), pl.BlockSpec((1, D), fixed), pl.BlockSpec((1, D), fixed),
            pl.BlockSpec((tm, attn.shape[1]), row),
            pl.BlockSpec((tm // DFT_N2,) + hy4.shape[1:], lambda i: (i, 0, 0, 0)),
            pl.BlockSpec((D, D), fixed), pl.BlockSpec((1, D), fixed), pl.BlockSpec((1, D), fixed),
            pl.BlockSpec((E, D), fixed), pl.BlockSpec((E, D), fixed),
        ],
        out_specs=[pl.BlockSpec((tm, D), row), pl.BlockSpec((tm, D // 2), row), pl.BlockSpec((E, tm), lambda i: (0, i))],
        out_shape=[
            jax.ShapeDtypeStruct((T, D), F32),
            jax.ShapeDtypeStruct((T, D // 2), jnp.uint32),
            jax.ShapeDtypeStruct((E, T), F32),
        ],
        compiler_params=_cparams(("parallel",), 48),
        name="oproj",
    )(x2, g0, b0, attn, hy4, wo_bf, g1, b1, wrh, wrl)


def _route_kernel(sc_ref, bias_ref, tri_ref, idx_ref, gate_ref, rank_ref, cnt_ref, carry_ref):
    step = pl.program_id(0)
    E, tm = sc_ref.shape
    neg = jnp.float32(-jnp.inf)
    scores = sc_ref[...]
    biased = scores + bias_ref[...]
    erow = lax.broadcasted_iota(jnp.int32, (E, tm), 0)
    big = jnp.int32(E)

    def first_argmax(vals, rows):
        m = jnp.max(vals, axis=0, keepdims=True)
        pick = jnp.min(jnp.where(vals == m, rows, big), axis=0, keepdims=True)
        return m, pick

    gsc = []
    for g in range(N_GROUPS):
        blk = biased[g * GROUP_SIZE:(g + 1) * GROUP_SIZE, :]
        rows = erow[g * GROUP_SIZE:(g + 1) * GROUP_SIZE, :]
        m1, p1 = first_argmax(blk, rows)
        m2 = jnp.max(jnp.where(rows == p1, neg, blk), axis=0, keepdims=True)
        gsc.append(m1 + m2)
    gsc = jnp.concatenate(gsc, axis=0)
    grow = lax.broadcasted_iota(jnp.int32, (N_GROUPS, tm), 0)
    gsel = jnp.zeros((N_GROUPS, tm), jnp.bool_)
    work = gsc
    for _ in range(TOPK_GROUPS):
        _, p = first_argmax(work, grow)
        hit = grow == p
        gsel = gsel | hit
        work = jnp.where(hit, neg, work)
    emask = jnp.concatenate(
        [jnp.broadcast_to(gsel[g:g + 1, :], (GROUP_SIZE, tm)) for g in range(N_GROUPS)], axis=0)
    work = jnp.where(emask, biased, neg)

    sel = jnp.zeros((E, tm), jnp.bool_)
    picks, gvals = [], []
    for _ in range(TOP_K):
        _, p = first_argmax(work, erow)
        hit = erow == p
        sel = sel | hit
        picks.append(p)
        gvals.append(jnp.sum(jnp.where(hit, scores, 0.0), axis=0, keepdims=True))
        work = jnp.where(hit, neg, work)
    gv = jnp.concatenate(gvals, axis=0)
    idx_ref[...] = jnp.concatenate(picks, axis=0)
    gate_ref[...] = gv / jnp.sum(gv, axis=0, keepdims=True) * ROUTED_SCALE

    @pl.when(step == 0)
    def _():
        carry_ref[...] = jnp.zeros_like(carry_ref)

    chosen = sel.astype(F32)
    before = jnp.dot(chosen.astype(BF16), tri_ref[...], preferred_element_type=F32) + carry_ref[...]
    rank_ref[...] = jnp.concatenate(
        [jnp.sum(jnp.where(erow == p, before, 0.0), axis=0, keepdims=True) for p in picks], axis=0).astype(jnp.int32)
    carry_ref[...] += jnp.sum(chosen, axis=1, keepdims=True)
    cnt_ref[...] = carry_ref[...].astype(jnp.int32)


def _route(scores_t, bias, tm=128):
    E, T = scores_t.shape
    tri = (jnp.arange(tm)[:, None] < jnp.arange(tm)[None, :]).astype(BF16)
    tok = lambda i: (0, i)
    fixed = lambda i: (0, 0)
    return pl.pallas_call(
        _route_kernel,
        grid=(T // tm,),
        in_specs=[pl.BlockSpec((E, tm), tok), pl.BlockSpec((E, 1), fixed), pl.BlockSpec((tm, tm), fixed)],
        out_specs=[pl.BlockSpec((TOP_K, tm), tok), pl.BlockSpec((TOP_K, tm), tok), pl.BlockSpec((TOP_K, tm), tok),
                   pl.BlockSpec((E, 1), fixed)],
        out_shape=[
            jax.ShapeDtypeStruct((TOP_K, T), jnp.int32),
            jax.ShapeDtypeStruct((TOP_K, T), F32),
            jax.ShapeDtypeStruct((TOP_K, T), jnp.int32),
            jax.ShapeDtypeStruct((E, 1), jnp.int32),
        ],
        scratch_shapes=[pltpu.VMEM((E, 1), F32)],
        compiler_params=_cparams(("arbitrary",), 32),
        name="route",
    )(scores_t, bias, tri)


def _dest_kernel(idx_ref, rank_ref, ps_ref, dest_ref):
    E = ps_ref.shape[0]
    tm = idx_ref.shape[1]
    erow = lax.broadcasted_iota(jnp.int32, (E, tm), 0)
    ps = ps_ref[...].astype(F32)
    rows = [jnp.sum(jnp.where(erow == idx_ref[k:k + 1, :], ps, 0.0), axis=0, keepdims=True) for k in range(TOP_K)]
    dest_ref[...] = jnp.concatenate(rows, axis=0).astype(jnp.int32) + rank_ref[...]


def _dest_rows(idx, rank, pad_start, tm=512):
    K, T = idx.shape
    E = pad_start.shape[0]
    tok = lambda i: (0, i)
    return pl.pallas_call(
        _dest_kernel,
        grid=(T // tm,),
        in_specs=[pl.BlockSpec((K, tm), tok), pl.BlockSpec((K, tm), tok), pl.BlockSpec((E, 1), lambda i: (0, 0))],
        out_specs=pl.BlockSpec((K, tm), tok),
        out_shape=jax.ShapeDtypeStruct((K, T), jnp.int32),
        compiler_params=_cparams(("arbitrary",), 32),
        name="dest",
    )(idx, rank, pad_start[:, None])


def _scatter_kernel(ps_ref, pe_ref, dest_ref, h_ref, xs_ref, zbuf, sem, *, rb):
    tm = h_ref.shape[0]

    @pl.when(pl.program_id(0) == 0)
    def _():
        zbuf[...] = jnp.zeros_like(zbuf)

        def zcopy(e):
            return pltpu.make_async_copy(zbuf, xs_ref.at[pl.ds(pl.multiple_of(pe_ref[e] - rb, rb), rb)], sem)

        def zstart(e, c):
            @pl.when(pe_ref[e] > ps_ref[e])
            def _():
                zcopy(e).start()
            return c

        def zwait(e, c):
            @pl.when(pe_ref[e] > ps_ref[e])
            def _():
                zcopy(e).wait()
            return c

        lax.fori_loop(0, ps_ref.shape[0], zstart, 0)
        lax.fori_loop(0, ps_ref.shape[0], zwait, 0)

    sub = V7X_SUBLANES

    def issue(r8, c):
        for j in range(sub):
            for k in range(TOP_K):
                dst = dest_ref[0, r8 * (sub * TOP_K) + (j * TOP_K + k)]
                pltpu.make_async_copy(h_ref.at[pl.ds(r8 * sub + j, 1)], xs_ref.at[pl.ds(dst, 1)], sem).start(priority=k % 2)
        return c

    lax.fori_loop(0, tm // sub, issue, 0)
    for k in range(TOP_K):
        pltpu.make_async_copy(h_ref, xs_ref.at[pl.ds(0, tm)], sem).wait()


def _dest_tiles(dest, tm):
    K, T = dest.shape
    return dest.T.reshape(T // tm, 1, tm * K)


def _scatter_rows(pad_start, pad_end, dest, h1p, n_rows, rb, tm=256):
    T, W = h1p.shape
    return pl.pallas_call(
        functools.partial(_scatter_kernel, rb=rb),
        grid_spec=pltpu.PrefetchScalarGridSpec(
            num_scalar_prefetch=2,
            grid=(T // tm,),
            in_specs=[
                pl.BlockSpec((None, 1, tm * TOP_K), lambda i, ps, pe: (i, 0, 0), memory_space=pltpu.SMEM),
                pl.BlockSpec((tm, W), lambda i, ps, pe: (i, 0)),
            ],
            out_specs=pl.BlockSpec(memory_space=pl.ANY),
            scratch_shapes=[pltpu.VMEM((rb, W), h1p.dtype), pltpu.SemaphoreType.DMA],
        ),
        out_shape=jax.ShapeDtypeStruct((n_rows, W), h1p.dtype),
        compiler_params=_cparams(("arbitrary",), 32),
        name="scatter",
    )(pad_start, pad_end, _dest_tiles(dest, tm), h1p)


EXPERT_X_SLOTS = 4
EXPERT_Y_SLOTS = 3


def _experts_kernel(first_ref, nblk_ref, wsel_ref, tot_ref, xs_ref, wg_ref, wu_ref, wd_ref, ys_ref,
                    xbuf, ybuf, wgub, wdb, xsem, ysem, *, rb):
    del wsel_ref
    e = pl.program_id(0)
    F = wg_ref.shape[1]
    nx, ny = xbuf.shape[0], ybuf.shape[0]
    total = tot_ref[0]
    n = nblk_ref[e]
    g0 = first_ref[e]

    def x_copy(g):
        slot = g % nx
        return pltpu.make_async_copy(xs_ref.at[pl.ds(pl.multiple_of(g * rb, rb), rb)], xbuf.at[slot], xsem.at[slot])

    def y_copy(g):
        slot = g % ny
        return pltpu.make_async_copy(ybuf.at[slot], ys_ref.at[pl.ds(pl.multiple_of(g * rb, rb), rb)], ysem.at[slot])

    @pl.when(e == 0)
    def _():
        for g in range(nx - 1):
            @pl.when(g < total)
            def _():
                x_copy(g).start()

    @pl.when(n > 0)
    def _():
        wgub[:, 0:F] = wg_ref[...].astype(BF16)
        wgub[:, F:] = wu_ref[...].astype(BF16)
        wdb[...] = wd_ref[...].astype(BF16)

        def block(j, c):
            g = g0 + j
            x_copy(g).wait()

            @pl.when(g + (nx - 1) < total)
            def _():
                x_copy(g + (nx - 1)).start()

            lo, hi = _unpack_pairs(xbuf[g % nx])
            x = jnp.concatenate([lo, hi], axis=1).astype(BF16)
            gu = jnp.dot(x, wgub[...], preferred_element_type=F32)
            gate, up = gu[:, 0:F], gu[:, F:]
            hb = (gate / (1.0 + jnp.exp(-gate)) * up).astype(BF16)
            y = _pack_pairs(jnp.dot(hb, wdb[...], preferred_element_type=F32))

            @pl.when(g >= ny)
            def _():
                y_copy(g - ny).wait()

            ybuf[g % ny] = y
            y_copy(g).start()
            return c

        lax.fori_loop(0, n, block, 0)

    @pl.when(e == pl.num_programs(0) - 1)
    def _():
        for back in range(ny, 0, -1):
            @pl.when(total >= back)
            def _():
                y_copy(total - back).wait()


def _experts(first_blk, nblk, wsel, total_blk, xs, w_gate, w_up, w_down, rb):
    P, W = xs.shape
    E, D, F = w_gate.shape
    wspec = lambda shape: pl.BlockSpec((None,) + shape, lambda e, fb, nbk, ws, tt: (ws[e], 0, 0))
    return pl.pallas_call(
        functools.partial(_experts_kernel, rb=rb),
        grid_spec=pltpu.PrefetchScalarGridSpec(
            num_scalar_prefetch=4,
            grid=(E,),
            in_specs=[pl.BlockSpec(memory_space=pl.ANY), wspec((D, F)), wspec((D, F)), wspec((F, D))],
            out_specs=pl.BlockSpec(memory_space=pl.ANY),
            scratch_shapes=[
                pltpu.VMEM((EXPERT_X_SLOTS, rb, W), jnp.uint32), pltpu.VMEM((EXPERT_Y_SLOTS, rb, W), jnp.uint32),
                pltpu.VMEM((D, 2 * F), BF16), pltpu.VMEM((F, D), BF16),
                pltpu.SemaphoreType.DMA((EXPERT_X_SLOTS,)), pltpu.SemaphoreType.DMA((EXPERT_Y_SLOTS,)),
            ],
        ),
        out_shape=jax.ShapeDtypeStruct((P, W), jnp.uint32),
        compiler_params=_cparams(("arbitrary",), 48),
        name="experts",
    )(first_blk, nblk, wsel, total_blk, xs, w_gate, w_up, w_down)


def _combine_kernel(dest_ref, destn_ref, h1_ref, gate_ref, ys_ref, sg_ref, su_ref, sd_ref,
                    g2_ref, b2_ref, o_ref, buf, sem):
    i = pl.program_id(0)
    n = pl.num_programs(0)
    tm = h1_ref.shape[0]
    slot = i % 2

    sub = V7X_SUBLANES

    def issue(dref, s):
        def body(r8, c):
            for j in range(sub):
                for k in range(TOP_K):
                    src = dref[0, r8 * (sub * TOP_K) + (j * TOP_K + k)]
                    pltpu.make_async_copy(ys_ref.at[pl.ds(src, 1)], buf.at[s, k, pl.ds(r8 * sub + j, 1)],
                                          sem.at[s]).start(priority=k % 2)
            return c
        lax.fori_loop(0, tm // sub, body, 0)

    @pl.when(i == 0)
    def _():
        issue(dest_ref, 0)

    @pl.when(i + 1 < n)
    def _():
        issue(destn_ref, 1 - slot)

    h1 = h1_ref[...]
    hb = h1.astype(BF16)
    g = jnp.dot(hb, sg_ref[...], preferred_element_type=F32)
    u = jnp.dot(hb, su_ref[...], preferred_element_type=F32)
    ffn = jnp.dot((g / (1.0 + jnp.exp(-g)) * u).astype(BF16), sd_ref[...], preferred_element_type=F32)

    for k in range(TOP_K):
        pltpu.make_async_copy(ys_ref.at[pl.ds(0, tm)], buf.at[slot, k], sem.at[slot]).wait()
    gates = gate_ref[...]
    lo_acc = jnp.zeros((tm, buf.shape[3]), F32)
    hi_acc = jnp.zeros((tm, buf.shape[3]), F32)
    for k in range(TOP_K):
        lo, hi = _unpack_pairs(buf[slot, k])
        gk = gates[:, k:k + 1]
        lo_acc = lo_acc + gk * lo
        hi_acc = hi_acc + gk * hi
    ffn = ffn + jnp.concatenate([lo_acc, hi_acc], axis=1)
    o_ref[...] = _layer_norm_rows(ALPHA * h1 + ffn, g2_ref[...], b2_ref[...])


def _combine(dest, h1, gates_tk, ys, sg, su, sd, g2, b2, tm=128):
    T, D = h1.shape
    W = ys.shape[1]
    F = sg.shape[1]
    nt = T // tm
    row = lambda i: (i, 0)
    fixed = lambda i: (0, 0)
    smem = functools.partial(pl.BlockSpec, (None, 1, tm * TOP_K), memory_space=pltpu.SMEM)
    dest_t = _dest_tiles(dest, tm)
    return pl.pallas_call(
        _combine_kernel,
        grid=(nt,),
        in_specs=[
            smem(index_map=lambda i: (i, 0, 0)),
            smem(index_map=lambda i: (jnp.minimum(i + 1, nt - 1), 0, 0)),
            pl.BlockSpec((tm, D), row),
            pl.BlockSpec((tm, TOP_K), row),
            pl.BlockSpec(memory_space=pl.ANY),
            pl.BlockSpec((D, F), fixed), pl.BlockSpec((D, F), fixed), pl.BlockSpec((F, D), fixed),
            pl.BlockSpec((1, D), fixed), pl.BlockSpec((1, D), fixed),
        ],
        out_specs=pl.BlockSpec((tm, D), row),
        out_shape=jax.ShapeDtypeStruct((T, D), F32),
        scratch_shapes=[pltpu.VMEM((2, TOP_K, tm, W), jnp.uint32), pltpu.SemaphoreType.DMA((2,))],
        compiler_params=_cparams(("arbitrary",), 48),
        name="combine",
    )(dest_t, dest_t, h1, gates_tk, ys, sg, su, sd, g2, b2)


MOE_ROWS = 256


def _moe(h1, h1p, scores_t, router_bias, w_gate, w_up, w_down, ws_gate, ws_up, ws_down, g2, b2):
    T = h1.shape[0]
    E = N_EXPERTS
    rb = MOE_ROWS
    idx, gates, rank, counts = _route(scores_t, router_bias[:, None])
    counts = counts[:, 0]
    padded = (counts + rb - 1) // rb * rb
    pad_end = jnp.cumsum(padded)
    pad_start = pad_end - padded
    nb = (T * TOP_K) // rb + E
    nblk = (padded // rb).astype(jnp.int32)
    first_blk = (pad_start // rb).astype(jnp.int32)
    total_blk = (pad_end[-1:] // rb).astype(jnp.int32)
    wsel = lax.cummax(jnp.where(nblk > 0, jnp.arange(E, dtype=jnp.int32), 0))
    dest = _dest_rows(idx, rank, pad_start)
    xs = _scatter_rows(pad_start, pad_end, dest, h1p, nb * rb, rb)
    ys = _experts(first_blk, nblk, wsel, total_blk, xs, w_gate, w_up, w_down, rb)
    return _combine(dest, h1, gates.T, ys, ws_gate.astype(BF16), ws_up.astype(BF16), ws_down.astype(BF16), g2, b2)


def kernel(x, positions, emb_ln_g, emb_ln_b, w_in, hy_conv_w, hy_conv_b, hy_f_w1, hy_f_b1, hy_f_w2, hy_f_b2, hy_f_w3, hy_f_b3, hy_f_freq, hy_f_wout, hy_d, lambda_q1, lambda_k1, lambda_q2, lambda_k2, subln_g, w_o, ln1_g, ln1_b, w_router, router_bias, w_gate, w_up, w_down, ws_gate, ws_up, ws_down, ln2_g, ln2_b):
    B, L, D = x.shape
    T = B * L
    assert w_in.shape[0] == DEPTH == 1
    i = 0
    x2 = x.reshape(T, D)
    g0, b0 = emb_ln_g[None], emb_ln_b[None]
    ra, rm, rp = _rotary_tables(positions)
    q, k, v, u = _inproj(x2, g0, b0, w_in[i].astype(BF16), ra, rm, rp)
    lam = (jnp.exp(jnp.sum(lambda_q1[i] * lambda_k1[i])) - jnp.exp(jnp.sum(lambda_q2[i] * lambda_k2[i])) + LAM_INIT)
    attn = _attention(q, k, v, lam.reshape(1).astype(F32), subln_g[i][None], B, L)
    hy = _hyena(u, hy_conv_w[i], hy_conv_b[i], hy_f_w1[i], hy_f_b1[i], hy_f_w2[i], hy_f_b2[i], hy_f_w3[i], hy_f_b3[i],
                hy_f_freq[i], hy_f_wout[i], hy_d[i], B, L)
    h1, h1p, scores_t = _oproj(x2, g0, b0, attn, hy, w_o[i].astype(BF16), ln1_g[i][None], ln1_b[i][None], w_router[i].T)
    out = _moe(h1, h1p, scores_t, router_bias[i], w_gate[i], w_up[i], w_down[i], ws_gate[i], ws_up[i], ws_down[i],
               ln2_g[i][None], ln2_b[i][None])
    return out.reshape(B, L, D)
```

```python
import functools
import math

import numpy as np
import jax
import jax.numpy as jnp
from jax import lax
from jax.experimental import pallas as pl
from jax.experimental.pallas import tpu as pltpu

DA_HEADS = 4
DA_HEAD_DIM = 64
DA_V_DIM = 128
ATTN_WIDTH = 512
HYENA_WIDTH = 512
ROT_DIM = 16
ROPE_THETA = 500000.0
SHORT_CONV = 3
FILTER_EMB = 33
FILTER_BANDS = 16
DECAY_TARGET = 1e-2
FAST_DECAY = 0.3
SLOW_DECAY = 1.5
N_EXPERTS = 256
TOP_K = 8
N_GROUPS = 8
GROUP_SIZE = N_EXPERTS // N_GROUPS
TOPK_GROUPS = 4
EXPERT_DIM = 256
ROUTED_SCALE = 2.5
DEPTH = 1
ALPHA = (2 * DEPTH) ** 0.25
LN_EPS = 1e-5
LAM_INIT = 0.8 - 0.6 * math.exp(-0.3 * 0)

V7X_LANES = 128
V7X_SUBLANES = 8
V7X_VMEM_BYTES = 64 * 1024 * 1024

DFT_N2 = 128
DFT_N2H = DFT_N2 // V7X_SUBLANES

BF16 = jnp.bfloat16
F32 = jnp.float32


def _cparams(sem, vmem_mb):
    return pltpu.CompilerParams(dimension_semantics=sem, vmem_limit_bytes=vmem_mb * 1024 * 1024)


def _layer_norm_rows(x, g, b):
    mu = jnp.mean(x, axis=-1, keepdims=True)
    xc = x - mu
    var = jnp.mean(xc * xc, axis=-1, keepdims=True)
    return xc * lax.rsqrt(var + LN_EPS) * g + b


def _inproj_kernel(x_ref, g_ref, b_ref, w_ref, ra_ref, rm_ref, rp_ref, q_ref, k_ref, v_ref, u_ref):
    h = _layer_norm_rows(x_ref[...], g_ref[...], b_ref[...]).astype(BF16)
    ra, rm, rp = ra_ref[...], rm_ref[...], rp_ref[...]

    def rot(t):
        return t * ra + pltpu.roll(t, V7X_LANES - ROT_DIM // 2, axis=1) * rm + pltpu.roll(t, ROT_DIM // 2, axis=1) * rp

    aw = ATTN_WIDTH
    qp = jnp.dot(h, w_ref[:, 0:aw], preferred_element_type=F32)
    kp = jnp.dot(h, w_ref[:, aw:2 * aw], preferred_element_type=F32)
    scale = DA_HEAD_DIM ** -0.5 * math.log2(math.e)
    for c in range(aw // V7X_LANES):
        sl = slice(c * V7X_LANES, (c + 1) * V7X_LANES)
        q_ref[:, sl] = (rot(qp[:, sl]) * scale).astype(BF16)
        k_ref[:, sl] = rot(kp[:, sl]).astype(BF16)
    vp = jnp.dot(h, w_ref[:, 2 * aw:3 * aw], preferred_element_type=F32).astype(BF16)
    ones = jnp.ones((vp.shape[0], DA_V_DIM), BF16)
    for hd in range(DA_HEADS):
        v_ref[:, 2 * hd * DA_V_DIM:(2 * hd + 1) * DA_V_DIM] = vp[:, hd * DA_V_DIM:(hd + 1) * DA_V_DIM]
        v_ref[:, (2 * hd + 1) * DA_V_DIM:(2 * hd + 2) * DA_V_DIM] = ones
    u_ref[...] = jnp.dot(h, w_ref[:, 3 * aw:], preferred_element_type=F32)


def _inproj(x2, g, b, w_bf, ra, rm, rp, tm=256):
    T, D = x2.shape
    ncol = w_bf.shape[1]
    aw = ATTN_WIDTH
    uw = ncol - 3 * aw
    row = lambda i: (i, 0)
    fixed = lambda i: (0, 0)
    return pl.pallas_call(
        _inproj_kernel,
        grid=(T // tm,),
        in_specs=[
            pl.BlockSpec((tm, D), row),
            pl.BlockSpec((1, D), fixed),
            pl.BlockSpec((1, D), fixed),
            pl.BlockSpec((D, ncol), fixed),
            pl.BlockSpec((tm, V7X_LANES), row),
            pl.BlockSpec((tm, V7X_LANES), row),
            pl.BlockSpec((tm, V7X_LANES), row),
        ],
        out_specs=[
            pl.BlockSpec((tm, aw), row),
            pl.BlockSpec((tm, aw), row),
            pl.BlockSpec((tm, 2 * aw), row),
            pl.BlockSpec((tm, uw), row),
        ],
        out_shape=[
            jax.ShapeDtypeStruct((T, aw), BF16),
            jax.ShapeDtypeStruct((T, aw), BF16),
            jax.ShapeDtypeStruct((T, 2 * aw), BF16),
            jax.ShapeDtypeStruct((T, uw), F32),
        ],
        compiler_params=_cparams(("parallel",), 48),
        name="inproj",
    )(x2, g, b, w_bf, ra, rm, rp)


def _rotary_tables(positions):
    half = ROT_DIM // 2
    inv_freq = ROPE_THETA ** (-jnp.arange(0, ROT_DIM, 2, dtype=F32) / ROT_DIM)
    ang = positions.astype(F32).reshape(-1)[:, None] * inv_freq
    cos, sin = jnp.cos(ang), jnp.sin(ang)
    T = ang.shape[0]
    ones = jnp.ones((T, DA_HEAD_DIM - ROT_DIM), F32)
    zeros_h = jnp.zeros((T, half), F32)
    zeros_r = jnp.zeros((T, DA_HEAD_DIM - ROT_DIM), F32)
    a64 = jnp.concatenate([cos, cos, ones], axis=1)
    m64 = jnp.concatenate([-sin, zeros_h, zeros_r], axis=1)
    p64 = jnp.concatenate([zeros_h, sin, zeros_r], axis=1)
    rep = V7X_LANES // DA_HEAD_DIM
    return jnp.tile(a64, (1, rep)), jnp.tile(m64, (1, rep)), jnp.tile(p64, (1, rep))


def _attn_kernel(lam_ref, q_ref, k_ref, v_ref, g_ref, o_ref, s_ref, m_ref, acc_ref, *, kb, unroll):
    qb = q_ref.shape[0]
    L = k_ref.shape[0]
    nchunk = L // kb
    nl = kb // V7X_LANES
    q = q_ref[...]
    lane = lax.broadcasted_iota(jnp.int32, q.shape, 1)
    zero = jnp.zeros_like(q)
    qm = [jnp.where(lane < DA_HEAD_DIM, q, zero), jnp.where(lane >= DA_HEAD_DIM, q, zero)]
    m_ref[...] = jnp.full(m_ref.shape, -jnp.inf, F32)
    acc_ref[...] = jnp.zeros(acc_ref.shape, F32)

    def score_body(j, carry):
        kc = k_ref[pl.ds(pl.multiple_of(j * kb, kb), kb), :]
        for c in range(2):
            s = lax.dot_general(qm[c], kc, (((1,), (1,)), ((), ())), preferred_element_type=F32)
            s_ref[c, j] = s
            m = s[:, 0:V7X_LANES]
            for t in range(1, nl):
                m = jnp.maximum(m, s[:, t * V7X_LANES:(t + 1) * V7X_LANES])
            m_ref[c] = jnp.maximum(m_ref[c], m)
        return carry

    lax.fori_loop(0, nchunk, score_body, 0, unroll=unroll)
    m_row = [jnp.max(m_ref[c], axis=1, keepdims=True) for c in range(2)]

    def pv_body(j, carry):
        vc = v_ref[pl.ds(pl.multiple_of(j * kb, kb), kb), :]
        for c in range(2):
            p = jnp.exp2(s_ref[c, j] - m_row[c])
            acc_ref[c] += jnp.dot(p.astype(BF16), vc, preferred_element_type=F32)
        return carry

    lax.fori_loop(0, nchunk, pv_body, 0, unroll=unroll)
    outs = [acc_ref[c, :, 0:DA_V_DIM] / acc_ref[c, :, DA_V_DIM:2 * DA_V_DIM] for c in range(2)]
    o = outs[0] - lam_ref[0] * outs[1]
    ms = jnp.mean(o * o, axis=1, keepdims=True)
    o_ref[...] = (o * lax.rsqrt(ms + LN_EPS) * g_ref[...] * (1.0 - LAM_INIT)).astype(o_ref.dtype)


def _attention(q, k, v, lam, subln_g, B, L, qb=512, kb=512, unroll=4):
    T = q.shape[0]
    nq = L // qb
    return pl.pallas_call(
        functools.partial(_attn_kernel, kb=kb, unroll=unroll),
        grid=(B, DA_HEADS, nq),
        in_specs=[
            pl.BlockSpec(memory_space=pltpu.SMEM),
            pl.BlockSpec((qb, DA_V_DIM), lambda b, h, i: (b * nq + i, h)),
            pl.BlockSpec((L, DA_V_DIM), lambda b, h, i: (b, h)),
            pl.BlockSpec((L, 2 * DA_V_DIM), lambda b, h, i: (b, h)),
            pl.BlockSpec((1, DA_V_DIM), lambda b, h, i: (0, 0)),
        ],
        out_specs=pl.BlockSpec((qb, DA_V_DIM), lambda b, h, i: (b * nq + i, h)),
        out_shape=jax.ShapeDtypeStruct((T, ATTN_WIDTH), BF16),
        scratch_shapes=[
            pltpu.VMEM((2, L // kb, qb, kb), F32),
            pltpu.VMEM((2, qb, V7X_LANES), F32),
            pltpu.VMEM((2, qb, 2 * DA_V_DIM), F32),
        ],
        compiler_params=_cparams(("parallel", "parallel", "parallel"), 56),
        name="attn",
    )(lam, q, k, v, subln_g)


def _hpre_kernel(u_ref, up_ref, un_ref, w_ref, b_ref, x0_ref, vx_ref):
    i = pl.program_id(1)
    n = pl.num_programs(1)
    tb = u_ref.shape[0]
    C = x0_ref.shape[-1]
    row = lax.broadcasted_iota(jnp.int32, (tb, V7X_LANES), 0)
    has_prev = (i > 0).astype(F32)
    has_next = (i < n - 1).astype(F32)

    def conv(c0):
        sl = slice(c0, c0 + V7X_LANES)
        u = u_ref[:, sl]
        prev_row = up_ref[V7X_SUBLANES - 1:V7X_SUBLANES, sl] * has_prev
        next_row = un_ref[0:1, sl] * has_next
        u_prev = jnp.where(row == 0, prev_row, pltpu.roll(u, 1, axis=0))
        u_next = jnp.where(row == tb - 1, next_row, pltpu.roll(u, tb - 1, axis=0))
        return u_prev * w_ref[0:1, sl] + u * w_ref[1:2, sl] + u_next * w_ref[2:3, sl] + b_ref[:, sl]

    tile = x0_ref.shape[:-1] + (V7X_LANES,)
    for c in range(C // V7X_LANES):
        c0 = c * V7X_LANES
        x0_ref[:, :, :, c0:c0 + V7X_LANES] = conv(c0).reshape(tile)
        vx_ref[:, :, :, c0:c0 + V7X_LANES] = (conv(2 * C + c0) * conv(C + c0)).reshape(tile)


def _hpre(u, conv_w, conv_b, B, L, tb=512):
    T, C3 = u.shape
    C = C3 // 3
    nt = L // tb
    sub = V7X_SUBLANES
    na = tb // DFT_N2
    split = pl.BlockSpec((None, na, DFT_N2H, sub, C), lambda b, i: (b, i, 0, 0, 0))
    split_shape = jax.ShapeDtypeStruct((B, L // DFT_N2, DFT_N2H, sub, C), F32)
    cur = lambda b, i: (b * nt + i, 0)
    prev = lambda b, i: (jnp.maximum((b * L + i * tb) // sub - 1, 0), 0)
    nxt = lambda b, i: (jnp.minimum((b * L + (i + 1) * tb) // sub, T // sub - 1), 0)
    fixed = lambda b, i: (0, 0)
    return pl.pallas_call(
        _hpre_kernel,
        grid=(B, nt),
        in_specs=[
            pl.BlockSpec((tb, C3), cur),
            pl.BlockSpec((sub, C3), prev),
            pl.BlockSpec((sub, C3), nxt),
            pl.BlockSpec((SHORT_CONV, C3), fixed),
            pl.BlockSpec((1, C3), fixed),
        ],
        out_specs=[split, split],
        out_shape=[split_shape, split_shape],
        compiler_params=_cparams(("parallel", "parallel"), 32),
        name="hpre",
    )(u, u, u, conv_w, conv_b)


def _filt_kernel(z_ref, w1_ref, b1_ref, w2_ref, b2_ref, w3_ref, b3_ref, fr_ref, wo_ref, dl_ref,
                 hfb_ref, asum_ref, *, L):
    i = pl.program_id(0)
    tl = z_ref.shape[0]
    C = dl_ref.shape[1]
    hp = lax.Precision.HIGHEST
    h = jnp.sin(fr_ref[0:1, :] * (jnp.dot(z_ref[...], w1_ref[...], precision=hp, preferred_element_type=F32) + b1_ref[...]))
    h = jnp.sin(fr_ref[1:2, :] * (jnp.dot(h, w2_ref[...], precision=hp, preferred_element_type=F32) + b2_ref[...]))
    h = jnp.sin(fr_ref[2:3, :] * (jnp.dot(h, w3_ref[...], precision=hp, preferred_element_type=F32) + b3_ref[...]))
    o = jnp.dot(h, wo_ref[...], precision=hp, preferred_element_type=F32)
    grow = lax.broadcasted_iota(jnp.int32, (tl, C), 0) + i * tl
    t = grow.astype(F32) * (1.0 / (L - 1))
    decay = jnp.exp(-t * dl_ref[...])
    hf = o[:, :C] * decay
    hb = jnp.where(grow == 0, 0.0, o[:, C:] * decay)
    tile = hfb_ref.shape[:-1] + (C,)
    hfb_ref[:, :, :, :C] = hf.reshape(tile)
    hfb_ref[:, :, :, C:] = hb.reshape(tile)

    @pl.when(i == 0)
    def _():
        asum_ref[...] = jnp.zeros_like(asum_ref)

    asum_ref[...] += jnp.sum(jnp.abs(hf) + jnp.abs(hb), axis=0, keepdims=True)


def _filter_taps(L, w1, b1, w2, b2, w3, b3, freq, wout, tl=512):
    C = wout.shape[1] // 2
    order = w1.shape[1]
    emb = w1.shape[0]
    t = jnp.linspace(0.0, 1.0, L, dtype=F32)[:, None]
    w = 2.0 * math.pi * jnp.arange(L, dtype=F32)[:, None] / L
    f = jnp.linspace(1e-4, FILTER_BANDS - 1, FILTER_BANDS, dtype=F32)[None, :]
    z = jnp.concatenate([t, jnp.cos(f * w), -jnp.sin(f * w)], axis=-1)
    zp = jnp.pad(z, ((0, 0), (0, V7X_LANES - emb)))
    w1p = jnp.pad(w1, ((0, V7X_LANES - emb), (0, 0)))
    deltas = jnp.abs(jnp.linspace(math.log(DECAY_TARGET) / SLOW_DECAY, math.log(DECAY_TARGET) / FAST_DECAY, C, dtype=F32))[None]
    fixed = lambda i: (0, 0)
    return pl.pallas_call(
        functools.partial(_filt_kernel, L=L),
        grid=(L // tl,),
        in_specs=[
            pl.BlockSpec((tl, V7X_LANES), lambda i: (i, 0)),
            pl.BlockSpec((V7X_LANES, order), fixed), pl.BlockSpec((1, order), fixed),
            pl.BlockSpec((order, order), fixed), pl.BlockSpec((1, order), fixed),
            pl.BlockSpec((order, order), fixed), pl.BlockSpec((1, order), fixed),
            pl.BlockSpec((3, order), fixed),
            pl.BlockSpec((order, 2 * C), fixed),
            pl.BlockSpec((1, C), fixed),
        ],
        out_specs=[pl.BlockSpec((None, tl // DFT_N2, DFT_N2H, V7X_SUBLANES, 2 * C), lambda i: (0, i, 0, 0, 0)),
                   pl.BlockSpec((1, C), fixed)],
        out_shape=[jax.ShapeDtypeStruct((1, L // DFT_N2, DFT_N2H, V7X_SUBLANES, 2 * C), F32),
                   jax.ShapeDtypeStruct((1, C), F32)],
        compiler_params=_cparams(("arbitrary",), 32),
        name="filt",
    )(zp, w1p, b1[None], w2, b2[None], w3, b3[None], freq, wout, deltas)


def _dft_constants(L):
    n1 = 2 * L // DFT_N2
    n1h = n1 // 2
    N = 2 * L
    sub = V7X_SUBLANES

    def cs(num, den):
        ang = (num % den).astype(np.float64) * (2.0 * np.pi / den)
        return np.cos(ang), np.sin(ang)

    k1 = np.arange(n1, dtype=np.int64)
    a = np.arange(n1h, dtype=np.int64)
    c, s = cs(k1[:, None] * a[None, :], n1)
    eye = np.eye(sub)

    def expand(m):
        r, kk = m.shape
        return (m[:, None, :, None] * eye[None, :, None, :]).reshape(r * sub, kk * sub)

    def const(m):
        return jnp.asarray(m.astype(np.float32).astype(BF16))

    m1c = np.stack([np.concatenate([c, s], axis=1), np.concatenate([-s, c], axis=1)], axis=1).reshape(2 * n1, 2 * n1h)
    m1r = np.stack([c, -s], axis=1).reshape(2 * n1, n1h)
    ct, st = c.T, s.T
    m3 = np.stack([np.stack([ct, -st], axis=2).reshape(n1h, 2 * n1),
                   np.stack([st, ct], axis=2).reshape(n1h, 2 * n1)], axis=0).reshape(2 * n1h, 2 * n1)
    n2 = np.arange(DFT_N2, dtype=np.int64)
    gc, gs = cs(n2[:, None] * n2[None, :], DFT_N2)
    g_fwd = np.concatenate([np.concatenate([gc, gs], axis=1), np.concatenate([-gs, gc], axis=1)], axis=0)
    g_inv = np.concatenate([np.concatenate([gc, -gs], axis=1), np.concatenate([gs, gc], axis=1)], axis=0)
    ph = (jnp.arange(n1, dtype=jnp.int32)[:, None] * jnp.arange(DFT_N2, dtype=jnp.int32)[None, :]) % N
    ang = ph.astype(F32) * (2.0 * math.pi / N)
    lanes = (n1, DFT_N2H, sub, V7X_LANES)
    twc = jnp.broadcast_to(jnp.cos(ang).reshape(n1, DFT_N2H, sub, 1), lanes)
    tws = jnp.broadcast_to(jnp.sin(ang).reshape(n1, DFT_N2H, sub, 1), lanes)
    return dict(n1=n1, n1h=n1h, m1c=const(expand(m1c)), m1r=const(expand(m1r)), m3=const(expand(m3)),
                g_fwd=const(g_fwd), g_inv=const(g_inv), twc=twc, tws=tws)


def _dft1_kernel(x_ref, m_ref, o_ref):
    rows = m_ref.shape[1]
    xs = x_ref[...].reshape(rows, x_ref.shape[-1]).astype(BF16)
    o_ref[...] = jnp.dot(m_ref[...], xs, preferred_element_type=F32).reshape(o_ref.shape)


def _dft1(x5, m, n1, cb):
    P, n1h, _, sub, Cx = x5.shape
    return pl.pallas_call(
        _dft1_kernel,
        grid=(DFT_N2H, Cx // cb),
        in_specs=[
            pl.BlockSpec((P, n1h, None, sub, cb), lambda h, c: (0, 0, h, 0, c)),
            pl.BlockSpec(m.shape, lambda h, c: (0, 0)),
        ],
        out_specs=pl.BlockSpec((None, n1, 2, sub, cb), lambda h, c: (h, 0, 0, 0, c)),
        out_shape=jax.ShapeDtypeStruct((DFT_N2H, n1, 2, sub, Cx), F32),
        compiler_params=_cparams(("parallel", "parallel"), 48),
        name="dft1",
    )(x5, m)


def _dft2_kernel(a_ref, f_ref, twc_ref, tws_ref, gf_ref, gi_ref, o_ref):
    C = a_ref.shape[-1]
    half = DFT_N2

    def lanes(fn):
        return jnp.concatenate([fn(slice(c0, c0 + V7X_LANES)) for c0 in range(0, C, V7X_LANES)], axis=-1)

    for kk in range(a_ref.shape[1]):
        twc, tws = twc_ref[kk], tws_ref[kk]

        def spectrum(ref, col0):
            def re_part(sl):
                return ref[:, kk, 0, :, col0 + sl.start:col0 + sl.stop] * twc + ref[:, kk, 1, :, col0 + sl.start:col0 + sl.stop] * tws

            def im_part(sl):
                return ref[:, kk, 1, :, col0 + sl.start:col0 + sl.stop] * twc - ref[:, kk, 0, :, col0 + sl.start:col0 + sl.stop] * tws

            t = jnp.concatenate([lanes(re_part).reshape(half, C), lanes(im_part).reshape(half, C)], axis=0).astype(BF16)
            s = jnp.dot(gf_ref[...], t, preferred_element_type=F32)
            return s[:half], s[half:]

        xr, xi = spectrum(a_ref, 0)
        fr, fi = spectrum(f_ref, 0)
        br, bi = spectrum(f_ref, C)
        hr, hi = fr + br, fi - bi
        y = jnp.concatenate([xr * hr - xi * hi, xr * hi + xi * hr], axis=0).astype(BF16)
        b = jnp.dot(gi_ref[...], y, preferred_element_type=F32)
        br2 = b[:half].reshape(DFT_N2H, V7X_SUBLANES, C)
        bi2 = b[half:].reshape(DFT_N2H, V7X_SUBLANES, C)
        for c0 in range(0, C, V7X_LANES):
            sl = slice(c0, c0 + V7X_LANES)
            o_ref[:, kk, 0, :, sl] = br2[:, :, sl] * twc - bi2[:, :, sl] * tws
            o_ref[:, kk, 1, :, sl] = bi2[:, :, sl] * twc + br2[:, :, sl] * tws


def _dft2(a5, f5, k, kb=4):
    n2h, n1, _, sub, C = a5.shape
    kb = min(kb, n1)
    return pl.pallas_call(
        _dft2_kernel,
        grid=(n1 // kb,),
        in_specs=[
            pl.BlockSpec((n2h, kb, 2, sub, C), lambda i: (0, i, 0, 0, 0)),
            pl.BlockSpec((n2h, kb, 2, sub, 2 * C), lambda i: (0, i, 0, 0, 0)),
            pl.BlockSpec((kb, n2h, sub, V7X_LANES), lambda i: (i, 0, 0, 0)),
            pl.BlockSpec((kb, n2h, sub, V7X_LANES), lambda i: (i, 0, 0, 0)),
            pl.BlockSpec((2 * DFT_N2, 2 * DFT_N2), lambda i: (0, 0)),
            pl.BlockSpec((2 * DFT_N2, 2 * DFT_N2), lambda i: (0, 0)),
        ],
        out_specs=pl.BlockSpec((n2h, kb, 2, sub, C), lambda i: (0, i, 0, 0, 0)),
        out_shape=jax.ShapeDtypeStruct(a5.shape, F32),
        compiler_params=_cparams(("parallel",), 48),
        name="dft2",
    )(a5, f5, k["twc"], k["tws"], k["g_fwd"], k["g_inv"])


def _dft3_kernel(b_ref, m_ref, x0_ref, vx_ref, sc_ref, d_ref, o_ref):
    C = b_ref.shape[-1]
    bs = b_ref[...].reshape(m_ref.shape[1], C).astype(BF16)
    y = jnp.dot(m_ref[...], bs, preferred_element_type=F32).reshape(o_ref.shape)
    o_ref[...] = x0_ref[...] * (y * sc_ref[...] + vx_ref[...] * d_ref[...])


def _dft3(b5, m3, x05, vx5, scale, d):
    n2h, n1, _, sub, C = b5.shape
    Bt, n1h = x05.shape[0], x05.shape[1]
    tok = pl.BlockSpec((Bt, n1h, None, sub, C), lambda h: (0, 0, h, 0, 0))
    vec = pl.BlockSpec((1, C), lambda h: (0, 0))
    return pl.pallas_call(
        _dft3_kernel,
        grid=(n2h,),
        in_specs=[
            pl.BlockSpec((None, n1, 2, sub, C), lambda h: (h, 0, 0, 0, 0)),
            pl.BlockSpec(m3.shape, lambda h: (0, 0)),
            tok, tok, vec, vec,
        ],
        out_specs=tok,
        out_shape=jax.ShapeDtypeStruct(x05.shape, F32),
        compiler_params=_cparams(("parallel",), 48),
        name="dft3",
    )(b5, m3, x05, vx5, scale, d)


def _hyena(u, conv_w, conv_b, w1, b1, w2, b2, w3, b3, freq, wout, d_skip, B, L):
    T = u.shape[0]
    C = HYENA_WIDTH
    k = _dft_constants(L)
    n1, n1h = k["n1"], k["n1h"]
    x0, vx = _hpre(u, conv_w, conv_b[None], B, L)
    hfb, asum = _filter_taps(L, w1, b1, w2, b2, w3, b3, freq, wout)
    a5 = _dft1(vx, k["m1c"], n1, cb=C)
    f5 = _dft1(hfb, k["m1r"], n1, cb=C)
    b5 = _dft2(a5, f5, k)
    scale = 1.0 / (asum * (2 * L))
    return _dft3(b5, k["m3"], x0, vx, scale, d_skip[None])


def _bf16_bits(x):
    a = pltpu.bitcast(x, jnp.uint32)
    return (a + jnp.uint32(0x7FFF) + ((a >> 16) & jnp.uint32(1))) >> 16


def _pack_pairs(x):
    half = x.shape[1] // 2
    return _bf16_bits(x[:, :half]) | (_bf16_bits(x[:, half:]) << 16)


def _unpack_pairs(w):
    lo = pltpu.bitcast(w << 16, F32)
    hi = pltpu.bitcast(w & jnp.uint32(0xFFFF0000), F32)
    return lo, hi


def _oproj_kernel(x_ref, g0_ref, b0_ref, at_ref, hy_ref, wo_ref, g1_ref, b1_ref, wrh_ref, wrl_ref,
                  h1_ref, h1p_ref, sc_ref):
    aw = at_ref.shape[1]
    h0 = _layer_norm_rows(x_ref[...], g0_ref[...], b0_ref[...])
    mixed = jnp.dot(at_ref[...], wo_ref[0:aw, :], preferred_element_type=F32)
    hy = hy_ref[...].reshape(x_ref.shape[0], hy_ref.shape[-1])
    mixed = mixed + jnp.dot(hy.astype(BF16), wo_ref[aw:, :], preferred_element_type=F32)
    h1 = _layer_norm_rows(ALPHA * h0 + mixed, g1_ref[...], b1_ref[...])
    h1_ref[...] = h1
    h1p_ref[...] = _pack_pairs(h1)
    hh = h1.astype(BF16)
    hl = (h1 - hh.astype(F32)).astype(BF16)
    dn = (((1,), (1,)), ((), ()))
    logits = lax.dot_general(wrh_ref[...], hh, dn, preferred_element_type=F32)
    logits = logits + (lax.dot_general(wrh_ref[...], hl, dn, preferred_element_type=F32)
                       + lax.dot_general(wrl_ref[...], hh, dn, preferred_element_type=F32))
    sc_ref[...] = 1.0 / (1.0 + jnp.exp(-logits))


def _oproj(x2, g0, b0, attn, hy, wo_bf, g1, b1, wr_t, tm=256):
    T, D = x2.shape
    E = wr_t.shape[0]
    wrh = wr_t.astype(BF16)
    wrl = (wr_t - wrh.astype(F32)).astype(BF16)
    row = lambda i: (i, 0)
    fixed = lambda i: (0, 0)
    hy4 = hy.reshape((-1,) + hy.shape[2:])
    return pl.pallas_call(
        _oproj_kernel,
        grid=(T // tm,),
        in_specs=[
            pl.BlockSpec((tm, D), row), pl.BlockSpec((1, D), fixed), pl.BlockSpec((1, D), fixed),
            pl.BlockSpec((tm, attn.shape[1]), row),
            pl.BlockSpec((tm // DFT_N2,) + hy4.shape[1:], lambda i: (i, 0, 0, 0)),
            pl.BlockSpec((D, D), fixed), pl.BlockSpec((1, D), fixed), pl.BlockSpec((1, D), fixed),
            pl.BlockSpec((E, D), fixed), pl.BlockSpec((E, D), fixed),
        ],
        out_specs=[pl.BlockSpec((tm, D), row), pl.BlockSpec((tm, D // 2), row), pl.BlockSpec((E, tm), lambda i: (0, i))],
        out_shape=[
            jax.ShapeDtypeStruct((T, D), F32),
            jax.ShapeDtypeStruct((T, D // 2), jnp.uint32),
            jax.ShapeDtypeStruct((E, T), F32),
        ],
        compiler_params=_cparams(("parallel",), 48),
        name="oproj",
    )(x2, g0, b0, attn, hy4, wo_bf, g1, b1, wrh, wrl)


def _route_kernel(sc_ref, bias_ref, tri_ref, idx_ref, gate_ref, rank_ref, cnt_ref, carry_ref):
    step = pl.program_id(0)
    E, tm = sc_ref.shape
    neg = jnp.float32(-jnp.inf)
    scores = sc_ref[...]
    biased = scores + bias_ref[...]
    erow = lax.broadcasted_iota(jnp.int32, (E, tm), 0)
    big = jnp.int32(E)

    def first_argmax(vals, rows):
        m = jnp.max(vals, axis=0, keepdims=True)
        pick = jnp.min(jnp.where(vals == m, rows, big), axis=0, keepdims=True)
        return m, pick

    gsc = []
    for g in range(N_GROUPS):
        blk = biased[g * GROUP_SIZE:(g + 1) * GROUP_SIZE, :]
        rows = erow[g * GROUP_SIZE:(g + 1) * GROUP_SIZE, :]
        m1, p1 = first_argmax(blk, rows)
        m2 = jnp.max(jnp.where(rows == p1, neg, blk), axis=0, keepdims=True)
        gsc.append(m1 + m2)
    gsc = jnp.concatenate(gsc, axis=0)
    grow = lax.broadcasted_iota(jnp.int32, (N_GROUPS, tm), 0)
    gsel = jnp.zeros((N_GROUPS, tm), jnp.bool_)
    work = gsc
    for _ in range(TOPK_GROUPS):
        _, p = first_argmax(work, grow)
        hit = grow == p
        gsel = gsel | hit
        work = jnp.where(hit, neg, work)
    emask = jnp.concatenate(
        [jnp.broadcast_to(gsel[g:g + 1, :], (GROUP_SIZE, tm)) for g in range(N_GROUPS)], axis=0)
    work = jnp.where(emask, biased, neg)

    sel = jnp.zeros((E, tm), jnp.bool_)
    picks, gvals = [], []
    for _ in range(TOP_K):
        _, p = first_argmax(work, erow)
        hit = erow == p
        sel = sel | hit
        picks.append(p)
        gvals.append(jnp.sum(jnp.where(hit, scores, 0.0), axis=0, keepdims=True))
        work = jnp.where(hit, neg, work)
    gv = jnp.concatenate(gvals, axis=0)
    idx_ref[...] = jnp.concatenate(picks, axis=0)
    gate_ref[...] = gv / jnp.sum(gv, axis=0, keepdims=True) * ROUTED_SCALE

    @pl.when(step == 0)
    def _():
        carry_ref[...] = jnp.zeros_like(carry_ref)

    chosen = sel.astype(F32)
    before = jnp.dot(chosen.astype(BF16), tri_ref[...], preferred_element_type=F32) + carry_ref[...]
    rank_ref[...] = jnp.concatenate(
        [jnp.sum(jnp.where(erow == p, before, 0.0), axis=0, keepdims=True) for p in picks], axis=0).astype(jnp.int32)
    carry_ref[...] += jnp.sum(chosen, axis=1, keepdims=True)
    cnt_ref[...] = carry_ref[...].astype(jnp.int32)


def _route(scores_t, bias, tm=128):
    E, T = scores_t.shape
    tri = (jnp.arange(tm)[:, None] < jnp.arange(tm)[None, :]).astype(BF16)
    tok = lambda i: (0, i)
    fixed = lambda i: (0, 0)
    return pl.pallas_call(
        _route_kernel,
        grid=(T // tm,),
        in_specs=[pl.BlockSpec((E, tm), tok), pl.BlockSpec((E, 1), fixed), pl.BlockSpec((tm, tm), fixed)],
        out_specs=[pl.BlockSpec((TOP_K, tm), tok), pl.BlockSpec((TOP_K, tm), tok), pl.BlockSpec((TOP_K, tm), tok),
                   pl.BlockSpec((E, 1), fixed)],
        out_shape=[
            jax.ShapeDtypeStruct((TOP_K, T), jnp.int32),
            jax.ShapeDtypeStruct((TOP_K, T), F32),
            jax.ShapeDtypeStruct((TOP_K, T), jnp.int32),
            jax.ShapeDtypeStruct((E, 1), jnp.int32),
        ],
        scratch_shapes=[pltpu.VMEM((E, 1), F32)],
        compiler_params=_cparams(("arbitrary",), 32),
        name="route",
    )(scores_t, bias, tri)


def _dest_kernel(idx_ref, rank_ref, ps_ref, dest_ref):
    E = ps_ref.shape[0]
    tm = idx_ref.shape[1]
    erow = lax.broadcasted_iota(jnp.int32, (E, tm), 0)
    ps = ps_ref[...].astype(F32)
    rows = [jnp.sum(jnp.where(erow == idx_ref[k:k + 1, :], ps, 0.0), axis=0, keepdims=True) for k in range(TOP_K)]
    dest_ref[...] = jnp.concatenate(rows, axis=0).astype(jnp.int32) + rank_ref[...]


def _dest_rows(idx, rank, pad_start, tm=512):
    K, T = idx.shape
    E = pad_start.shape[0]
    tok = lambda i: (0, i)
    return pl.pallas_call(
        _dest_kernel,
        grid=(T // tm,),
        in_specs=[pl.BlockSpec((K, tm), tok), pl.BlockSpec((K, tm), tok), pl.BlockSpec((E, 1), lambda i: (0, 0))],
        out_specs=pl.BlockSpec((K, tm), tok),
        out_shape=jax.ShapeDtypeStruct((K, T), jnp.int32),
        compiler_params=_cparams(("arbitrary",), 32),
        name="dest",
    )(idx, rank, pad_start[:, None])


def _scatter_kernel(ps_ref, pe_ref, dest_ref, h_ref, xs_ref, zbuf, sem, *, rb):
    sub = V7X_SUBLANES
    tm = h_ref.shape[0]

    @pl.when(pl.program_id(0) == 0)
    def _():
        zbuf[...] = jnp.zeros_like(zbuf)

        def zcopy(e):
            return pltpu.make_async_copy(zbuf, xs_ref.at[pl.ds(pl.multiple_of(pe_ref[e] - rb, rb), rb)], sem)

        def zstart(e, c):
            @pl.when(pe_ref[e] > ps_ref[e])
            def _():
                zcopy(e).start()
            return c

        def zwait(e, c):
            @pl.when(pe_ref[e] > ps_ref[e])
            def _():
                zcopy(e).wait()
            return c

        lax.fori_loop(0, ps_ref.shape[0], zstart, 0)
        lax.fori_loop(0, ps_ref.shape[0], zwait, 0)

    def issue(r8, c):
        for j in range(sub):
            for k in range(TOP_K):
                dst = dest_ref[0, r8 * (sub * TOP_K) + (j * TOP_K + k)]
                pltpu.make_async_copy(h_ref.at[pl.ds(r8 * sub + j, 1)], xs_ref.at[pl.ds(dst, 1)], sem).start(priority=k % 2)
        return c

    lax.fori_loop(0, tm // sub, issue, 0)
    for k in range(TOP_K):
        pltpu.make_async_copy(h_ref, xs_ref.at[pl.ds(0, tm)], sem).wait()


def _dest_tiles(dest, tm):
    K, T = dest.shape
    return dest.T.reshape(T // tm, 1, tm * K)


def _scatter_rows(pad_start, pad_end, dest_t, h1p, n_rows, rb, tm=256):
    T, W = h1p.shape
    return pl.pallas_call(
        functools.partial(_scatter_kernel, rb=rb),
        grid_spec=pltpu.PrefetchScalarGridSpec(
            num_scalar_prefetch=2,
            grid=(T // tm,),
            in_specs=[
                pl.BlockSpec((None, 1, tm * TOP_K), lambda i, ps, pe: (i, 0, 0), memory_space=pltpu.SMEM),
                pl.BlockSpec((tm, W), lambda i, ps, pe: (i, 0)),
            ],
            out_specs=pl.BlockSpec(memory_space=pl.ANY),
            scratch_shapes=[pltpu.VMEM((rb, W), h1p.dtype), pltpu.SemaphoreType.DMA],
        ),
        out_shape=jax.ShapeDtypeStruct((n_rows, W), h1p.dtype),
        compiler_params=_cparams(("arbitrary",), 32),
        name="scatter",
    )(pad_start, pad_end, dest_t, h1p)


EXPERT_X_SLOTS = 4
EXPERT_Y_SLOTS = 3


def _experts_kernel(first_ref, nblk_ref, wsel_ref, tot_ref, xs_ref, wg_ref, wu_ref, wd_ref, ys_ref,
                    xbuf, ybuf, wgub, wdb, xsem, ysem, *, rb):
    del wsel_ref
    e = pl.program_id(0)
    F = wg_ref.shape[1]
    nx, ny = xbuf.shape[0], ybuf.shape[0]
    total = tot_ref[0]
    n = nblk_ref[e]
    g0 = first_ref[e]

    def x_copy(g):
        slot = g % nx
        return pltpu.make_async_copy(xs_ref.at[pl.ds(pl.multiple_of(g * rb, rb), rb)], xbuf.at[slot], xsem.at[slot])

    def y_copy(g):
        slot = g % ny
        return pltpu.make_async_copy(ybuf.at[slot], ys_ref.at[pl.ds(pl.multiple_of(g * rb, rb), rb)], ysem.at[slot])

    @pl.when(e == 0)
    def _():
        for g in range(nx - 1):
            @pl.when(g < total)
            def _():
                x_copy(g).start()

    @pl.when(n > 0)
    def _():
        wgub[:, 0:F] = wg_ref[...].astype(BF16)
        wgub[:, F:] = wu_ref[...].astype(BF16)
        wdb[...] = wd_ref[...].astype(BF16)

        def block(j, c):
            g = g0 + j
            x_copy(g).wait()

            @pl.when(g + (nx - 1) < total)
            def _():
                x_copy(g + (nx - 1)).start()

            lo, hi = _unpack_pairs(xbuf[g % nx])
            x = jnp.concatenate([lo, hi], axis=1).astype(BF16)
            gu = jnp.dot(x, wgub[...], preferred_element_type=F32)
            gate, up = gu[:, 0:F], gu[:, F:]
            hb = (gate / (1.0 + jnp.exp(-gate)) * up).astype(BF16)
            y = _pack_pairs(jnp.dot(hb, wdb[...], preferred_element_type=F32))

            @pl.when(g >= ny)
            def _():
                y_copy(g - ny).wait()

            ybuf[g % ny] = y
            y_copy(g).start()
            return c

        lax.fori_loop(0, n, block, 0)

    @pl.when(e == pl.num_programs(0) - 1)
    def _():
        for back in range(ny, 0, -1):
            @pl.when(total >= back)
            def _():
                y_copy(total - back).wait()


def _experts(first_blk, nblk, wsel, total_blk, xs, w_gate, w_up, w_down, rb):
    P, W = xs.shape
    E, D, F = w_gate.shape
    wspec = lambda shape: pl.BlockSpec((None,) + shape, lambda e, fb, nbk, ws, tt: (ws[e], 0, 0))
    return pl.pallas_call(
        functools.partial(_experts_kernel, rb=rb),
        grid_spec=pltpu.PrefetchScalarGridSpec(
            num_scalar_prefetch=4,
            grid=(E,),
            in_specs=[pl.BlockSpec(memory_space=pl.ANY), wspec((D, F)), wspec((D, F)), wspec((F, D))],
            out_specs=pl.BlockSpec(memory_space=pl.ANY),
            scratch_shapes=[
                pltpu.VMEM((EXPERT_X_SLOTS, rb, W), jnp.uint32), pltpu.VMEM((EXPERT_Y_SLOTS, rb, W), jnp.uint32),
                pltpu.VMEM((D, 2 * F), BF16), pltpu.VMEM((F, D), BF16),
                pltpu.SemaphoreType.DMA((EXPERT_X_SLOTS,)), pltpu.SemaphoreType.DMA((EXPERT_Y_SLOTS,)),
            ],
        ),
        out_shape=jax.ShapeDtypeStruct((P, W), jnp.uint32),
        compiler_params=_cparams(("arbitrary",), 48),
        name="experts",
    )(first_blk, nblk, wsel, total_blk, xs, w_gate, w_up, w_down)


def _combine_kernel(dest_ref, destn_ref, h1_ref, gate_ref, ys_ref, ys8_ref, sg_ref, su_ref, sd_ref,
                    g2_ref, b2_ref, o_ref, buf, sem):
    i = pl.program_id(0)
    n = pl.num_programs(0)
    tm = h1_ref.shape[0]
    slot = i % 2

    sub = V7X_SUBLANES

    def issue(dref, s):
        def body(r8, c):
            for j in range(sub):
                for k in range(TOP_K):
                    src = ys_ref.at[pl.ds(dref[0, r8 * (sub * TOP_K) + (j * TOP_K + k)], 1)]
                    pltpu.make_async_copy(src, buf.at[s, r8 * sub + j, pl.ds(k, 1)], sem.at[s]).start(priority=k % 2)
            return c
        lax.fori_loop(0, tm // sub, body, 0)

    @pl.when(i == 0)
    def _():
        issue(dest_ref, 0)

    @pl.when(i + 1 < n)
    def _():
        issue(destn_ref, 1 - slot)

    h1 = h1_ref[...]
    hb = h1.astype(BF16)
    g = jnp.dot(hb, sg_ref[...], preferred_element_type=F32)
    u = jnp.dot(hb, su_ref[...], preferred_element_type=F32)
    ffn = jnp.dot((g / (1.0 + jnp.exp(-g)) * u).astype(BF16), sd_ref[...], preferred_element_type=F32)

    pltpu.make_async_copy(ys8_ref.at[pl.ds(0, tm)], buf.at[slot], sem.at[slot]).wait()
    nk = tm * TOP_K
    lo, hi = _unpack_pairs(buf[slot].reshape(nk, buf.shape[3]))
    y = jnp.concatenate([lo, hi], axis=1).astype(BF16)
    gates = gate_ref[...]
    gh = gates.astype(BF16)
    gl = (gates - gh.astype(F32)).astype(BF16)
    pick = lax.broadcasted_iota(jnp.int32, (TOP_K, nk), 1) % TOP_K == lax.broadcasted_iota(jnp.int32, (TOP_K, nk), 0)
    spread = pick.astype(BF16)
    own = lax.broadcasted_iota(jnp.int32, (tm, nk), 1) // TOP_K == lax.broadcasted_iota(jnp.int32, (tm, nk), 0)
    routed = jnp.zeros((tm, y.shape[1]), F32)
    for part in (gh, gl):
        gm = jnp.where(own, jnp.dot(part, spread, preferred_element_type=F32), 0.0).astype(BF16)
        routed = routed + jnp.dot(gm, y, preferred_element_type=F32)
    o_ref[...] = _layer_norm_rows(ALPHA * h1 + ffn + routed, g2_ref[...], b2_ref[...])


def _combine(dest, h1, gates_tk, ys, sg, su, sd, g2, b2, tm=128):
    T, D = h1.shape
    W = ys.shape[1]
    F = sg.shape[1]
    nt = T // tm
    row = lambda i: (i, 0)
    fixed = lambda i: (0, 0)
    smem = functools.partial(pl.BlockSpec, (None, 1, tm * TOP_K), memory_space=pltpu.SMEM)
    dest_t = _dest_tiles(dest, tm)
    ys8 = ys.reshape(ys.shape[0] // TOP_K, TOP_K, W)
    return pl.pallas_call(
        _combine_kernel,
        grid=(nt,),
        in_specs=[
            smem(index_map=lambda i: (i, 0, 0)),
            smem(index_map=lambda i: (jnp.minimum(i + 1, nt - 1), 0, 0)),
            pl.BlockSpec((tm, D), row),
            pl.BlockSpec((tm, TOP_K), row),
            pl.BlockSpec(memory_space=pl.ANY),
            pl.BlockSpec(memory_space=pl.ANY),
            pl.BlockSpec((D, F), fixed), pl.BlockSpec((D, F), fixed), pl.BlockSpec((F, D), fixed),
            pl.BlockSpec((1, D), fixed), pl.BlockSpec((1, D), fixed),
        ],
        out_specs=pl.BlockSpec((tm, D), row),
        out_shape=jax.ShapeDtypeStruct((T, D), F32),
        scratch_shapes=[pltpu.VMEM((2, tm, TOP_K, W), jnp.uint32), pltpu.SemaphoreType.DMA((2,))],
        compiler_params=_cparams(("arbitrary",), 48),
        name="combine",
    )(dest_t, dest_t, h1, gates_tk, ys, ys8, sg, su, sd, g2, b2)


MOE_ROWS = 256


def _moe(h1, h1p, scores_t, router_bias, w_gate, w_up, w_down, ws_gate, ws_up, ws_down, g2, b2):
    T = h1.shape[0]
    E = N_EXPERTS
    rb = MOE_ROWS
    idx, gates, rank, counts = _route(scores_t, router_bias[:, None])
    counts = counts[:, 0]
    padded = (counts + rb - 1) // rb * rb
    pad_end = jnp.cumsum(padded)
    pad_start = pad_end - padded
    nb = (T * TOP_K) // rb + E
    nblk = (padded // rb).astype(jnp.int32)
    first_blk = (pad_start // rb).astype(jnp.int32)
    total_blk = (pad_end[-1:] // rb).astype(jnp.int32)
    wsel = lax.cummax(jnp.where(nblk > 0, jnp.arange(E, dtype=jnp.int32), 0))
    dest = _dest_rows(idx, rank, pad_start)
    xs = _scatter_rows(pad_start, pad_end, _dest_tiles(dest, 256), h1p, nb * rb, rb, tm=256)
    ys = _experts(first_blk, nblk, wsel, total_blk, xs, w_gate, w_up, w_down, rb)
    return _combine(dest, h1, gates.T, ys, ws_gate.astype(BF16), ws_up.astype(BF16), ws_down.astype(BF16), g2, b2)


def kernel(x, positions, emb_ln_g, emb_ln_b, w_in, hy_conv_w, hy_conv_b, hy_f_w1, hy_f_b1, hy_f_w2, hy_f_b2, hy_f_w3, hy_f_b3, hy_f_freq, hy_f_wout, hy_d, lambda_q1, lambda_k1, lambda_q2, lambda_k2, subln_g, w_o, ln1_g, ln1_b, w_router, router_bias, w_gate, w_up, w_down, ws_gate, ws_up, ws_down, ln2_g, ln2_b):
    B, L, D = x.shape
    T = B * L
    assert w_in.shape[0] == DEPTH == 1
    i = 0
    x2 = x.reshape(T, D)
    g0, b0 = emb_ln_g[None], emb_ln_b[None]
    ra, rm, rp = _rotary_tables(positions)
    q, k, v, u = _inproj(x2, g0, b0, w_in[i].astype(BF16), ra, rm, rp)
    lam = (jnp.exp(jnp.sum(lambda_q1[i] * lambda_k1[i])) - jnp.exp(jnp.sum(lambda_q2[i] * lambda_k2[i])) + LAM_INIT)
    attn = _attention(q, k, v, lam.reshape(1).astype(F32), subln_g[i][None], B, L)
    hy = _hyena(u, hy_conv_w[i], hy_conv_b[i], hy_f_w1[i], hy_f_b1[i], hy_f_w2[i], hy_f_b2[i], hy_f_w3[i], hy_f_b3[i],
                hy_f_freq[i], hy_f_wout[i], hy_d[i], B, L)
    h1, h1p, scores_t = _oproj(x2, g0, b0, attn, hy, w_o[i].astype(BF16), ln1_g[i][None], ln1_b[i][None], w_router[i].T)
    out = _moe(h1, h1p, scores_t, router_bias[i], w_gate[i], w_up[i], w_down[i], ws_gate[i], ws_up[i], ws_down[i],
               ln2_g[i][None], ln2_b[i][None])
    return out.reshape(B, L, D)
```

```python
import functools
import math

import numpy as np
import jax
import jax.numpy as jnp
from jax import lax
from jax.experimental import pallas as pl
from jax.experimental.pallas import tpu as pltpu

DA_HEADS = 4
DA_HEAD_DIM = 64
DA_V_DIM = 128
ATTN_WIDTH = 512
HYENA_WIDTH = 512
ROT_DIM = 16
ROPE_THETA = 500000.0
SHORT_CONV = 3
FILTER_EMB = 33
FILTER_BANDS = 16
DECAY_TARGET = 1e-2
FAST_DECAY = 0.3
SLOW_DECAY = 1.5
N_EXPERTS = 256
TOP_K = 8
N_GROUPS = 8
GROUP_SIZE = N_EXPERTS // N_GROUPS
TOPK_GROUPS = 4
EXPERT_DIM = 256
ROUTED_SCALE = 2.5
DEPTH = 1
ALPHA = (2 * DEPTH) ** 0.25
LN_EPS = 1e-5
LAM_INIT = 0.8 - 0.6 * math.exp(-0.3 * 0)

V7X_LANES = 128
V7X_SUBLANES = 8
V7X_VMEM_BYTES = 64 * 1024 * 1024

DFT_N2 = 128
DFT_N2H = DFT_N2 // V7X_SUBLANES

BF16 = jnp.bfloat16
F32 = jnp.float32


def _cparams(sem, vmem_mb):
    return pltpu.CompilerParams(dimension_semantics=sem, vmem_limit_bytes=vmem_mb * 1024 * 1024)


def _layer_norm_rows(x, g, b):
    mu = jnp.mean(x, axis=-1, keepdims=True)
    xc = x - mu
    var = jnp.mean(xc * xc, axis=-1, keepdims=True)
    return xc * lax.rsqrt(var + LN_EPS) * g + b


def _inproj_kernel(x_ref, g_ref, b_ref, w_ref, ra_ref, rm_ref, rp_ref, q_ref, k_ref, v_ref, u_ref):
    h = _layer_norm_rows(x_ref[...], g_ref[...], b_ref[...]).astype(BF16)
    ra, rm, rp = ra_ref[...], rm_ref[...], rp_ref[...]

    def rot(t):
        return t * ra + pltpu.roll(t, V7X_LANES - ROT_DIM // 2, axis=1) * rm + pltpu.roll(t, ROT_DIM // 2, axis=1) * rp

    aw = ATTN_WIDTH
    qp = jnp.dot(h, w_ref[:, 0:aw], preferred_element_type=F32)
    kp = jnp.dot(h, w_ref[:, aw:2 * aw], preferred_element_type=F32)
    scale = DA_HEAD_DIM ** -0.5 * math.log2(math.e)
    for c in range(aw // V7X_LANES):
        sl = slice(c * V7X_LANES, (c + 1) * V7X_LANES)
        q_ref[:, sl] = (rot(qp[:, sl]) * scale).astype(BF16)
        k_ref[:, sl] = rot(kp[:, sl]).astype(BF16)
    vp = jnp.dot(h, w_ref[:, 2 * aw:3 * aw], preferred_element_type=F32).astype(BF16)
    ones = jnp.ones((vp.shape[0], DA_V_DIM), BF16)
    for hd in range(DA_HEADS):
        v_ref[:, 2 * hd * DA_V_DIM:(2 * hd + 1) * DA_V_DIM] = vp[:, hd * DA_V_DIM:(hd + 1) * DA_V_DIM]
        v_ref[:, (2 * hd + 1) * DA_V_DIM:(2 * hd + 2) * DA_V_DIM] = ones
    u_ref[...] = jnp.dot(h, w_ref[:, 3 * aw:], preferred_element_type=F32)


def _inproj(x2, g, b, w_bf, ra, rm, rp, tm=256):
    T, D = x2.shape
    ncol = w_bf.shape[1]
    aw = ATTN_WIDTH
    uw = ncol - 3 * aw
    row = lambda i: (i, 0)
    fixed = lambda i: (0, 0)
    return pl.pallas_call(
        _inproj_kernel,
        grid=(T // tm,),
        in_specs=[
            pl.BlockSpec((tm, D), row),
            pl.BlockSpec((1, D), fixed),
            pl.BlockSpec((1, D), fixed),
            pl.BlockSpec((D, ncol), fixed),
            pl.BlockSpec((tm, V7X_LANES), row),
            pl.BlockSpec((tm, V7X_LANES), row),
            pl.BlockSpec((tm, V7X_LANES), row),
        ],
        out_specs=[
            pl.BlockSpec((tm, aw), row),
            pl.BlockSpec((tm, aw), row),
            pl.BlockSpec((tm, 2 * aw), row),
            pl.BlockSpec((tm, uw), row),
        ],
        out_shape=[
            jax.ShapeDtypeStruct((T, aw), BF16),
            jax.ShapeDtypeStruct((T, aw), BF16),
            jax.ShapeDtypeStruct((T, 2 * aw), BF16),
            jax.ShapeDtypeStruct((T, uw), F32),
        ],
        compiler_params=_cparams(("parallel",), 48),
        name="inproj",
    )(x2, g, b, w_bf, ra, rm, rp)


def _rotary_tables(positions):
    half = ROT_DIM // 2
    inv_freq = ROPE_THETA ** (-jnp.arange(0, ROT_DIM, 2, dtype=F32) / ROT_DIM)
    ang = positions.astype(F32).reshape(-1)[:, None] * inv_freq
    cos, sin = jnp.cos(ang), jnp.sin(ang)
    T = ang.shape[0]
    ones = jnp.ones((T, DA_HEAD_DIM - ROT_DIM), F32)
    zeros_h = jnp.zeros((T, half), F32)
    zeros_r = jnp.zeros((T, DA_HEAD_DIM - ROT_DIM), F32)
    a64 = jnp.concatenate([cos, cos, ones], axis=1)
    m64 = jnp.concatenate([-sin, zeros_h, zeros_r], axis=1)
    p64 = jnp.concatenate([zeros_h, sin, zeros_r], axis=1)
    rep = V7X_LANES // DA_HEAD_DIM
    return jnp.tile(a64, (1, rep)), jnp.tile(m64, (1, rep)), jnp.tile(p64, (1, rep))


def _attn_kernel(lam_ref, q_ref, k_ref, v_ref, g_ref, o_ref, s_ref, m_ref, acc_ref, *, kb, unroll):
    qb = q_ref.shape[0]
    L = k_ref.shape[0]
    nchunk = L // kb
    nl = kb // V7X_LANES
    q = q_ref[...]
    lane = lax.broadcasted_iota(jnp.int32, q.shape, 1)
    zero = jnp.zeros_like(q)
    qm = [jnp.where(lane < DA_HEAD_DIM, q, zero), jnp.where(lane >= DA_HEAD_DIM, q, zero)]
    m_ref[...] = jnp.full(m_ref.shape, -jnp.inf, F32)
    acc_ref[...] = jnp.zeros(acc_ref.shape, F32)

    def score_body(j, carry):
        kc = k_ref[pl.ds(pl.multiple_of(j * kb, kb), kb), :]
        for c in range(2):
            s = lax.dot_general(qm[c], kc, (((1,), (1,)), ((), ())), preferred_element_type=F32)
            s_ref[c, j] = s
            m = s[:, 0:V7X_LANES]
            for t in range(1, nl):
                m = jnp.maximum(m, s[:, t * V7X_LANES:(t + 1) * V7X_LANES])
            m_ref[c] = jnp.maximum(m_ref[c], m)
        return carry

    lax.fori_loop(0, nchunk, score_body, 0, unroll=unroll)
    m_row = [jnp.max(m_ref[c], axis=1, keepdims=True) for c in range(2)]

    def pv_body(j, carry):
        vc = v_ref[pl.ds(pl.multiple_of(j * kb, kb), kb), :]
        for c in range(2):
            p = jnp.exp2(s_ref[c, j] - m_row[c])
            acc_ref[c] += jnp.dot(p.astype(BF16), vc, preferred_element_type=F32)
        return carry

    lax.fori_loop(0, nchunk, pv_body, 0, unroll=unroll)
    outs = [acc_ref[c, :, 0:DA_V_DIM] / acc_ref[c, :, DA_V_DIM:2 * DA_V_DIM] for c in range(2)]
    o = outs[0] - lam_ref[0] * outs[1]
    ms = jnp.mean(o * o, axis=1, keepdims=True)
    o_ref[...] = (o * lax.rsqrt(ms + LN_EPS) * g_ref[...] * (1.0 - LAM_INIT)).astype(o_ref.dtype)


def _attention(q, k, v, lam, subln_g, B, L, qb=512, kb=512, unroll=4):
    T = q.shape[0]
    nq = L // qb
    return pl.pallas_call(
        functools.partial(_attn_kernel, kb=kb, unroll=unroll),
        grid=(B, DA_HEADS, nq),
        in_specs=[
            pl.BlockSpec(memory_space=pltpu.SMEM),
            pl.BlockSpec((qb, DA_V_DIM), lambda b, h, i: (b * nq + i, h)),
            pl.BlockSpec((L, DA_V_DIM), lambda b, h, i: (b, h)),
            pl.BlockSpec((L, 2 * DA_V_DIM), lambda b, h, i: (b, h)),
            pl.BlockSpec((1, DA_V_DIM), lambda b, h, i: (0, 0)),
        ],
        out_specs=pl.BlockSpec((qb, DA_V_DIM), lambda b, h, i: (b * nq + i, h)),
        out_shape=jax.ShapeDtypeStruct((T, ATTN_WIDTH), BF16),
        scratch_shapes=[
            pltpu.VMEM((2, L // kb, qb, kb), F32),
            pltpu.VMEM((2, qb, V7X_LANES), F32),
            pltpu.VMEM((2, qb, 2 * DA_V_DIM), F32),
        ],
        compiler_params=_cparams(("parallel", "parallel", "parallel"), 56),
        name="attn",
    )(lam, q, k, v, subln_g)


def _hpre_kernel(u_ref, up_ref, un_ref, w_ref, b_ref, x0_ref, vx_ref):
    i = pl.program_id(1)
    n = pl.num_programs(1)
    tb = u_ref.shape[0]
    C = x0_ref.shape[-1]
    row = lax.broadcasted_iota(jnp.int32, (tb, V7X_LANES), 0)
    has_prev = (i > 0).astype(F32)
    has_next = (i < n - 1).astype(F32)

    def conv(c0):
        sl = slice(c0, c0 + V7X_LANES)
        u = u_ref[:, sl]
        prev_row = up_ref[V7X_SUBLANES - 1:V7X_SUBLANES, sl] * has_prev
        next_row = un_ref[0:1, sl] * has_next
        u_prev = jnp.where(row == 0, prev_row, pltpu.roll(u, 1, axis=0))
        u_next = jnp.where(row == tb - 1, next_row, pltpu.roll(u, tb - 1, axis=0))
        return u_prev * w_ref[0:1, sl] + u * w_ref[1:2, sl] + u_next * w_ref[2:3, sl] + b_ref[:, sl]

    tile = x0_ref.shape[:-1] + (V7X_LANES,)
    for c in range(C // V7X_LANES):
        c0 = c * V7X_LANES
        x0_ref[:, :, :, c0:c0 + V7X_LANES] = conv(c0).reshape(tile)
        vx_ref[:, :, :, c0:c0 + V7X_LANES] = (conv(2 * C + c0) * conv(C + c0)).reshape(tile)


def _hpre(u, conv_w, conv_b, B, L, tb=512):
    T, C3 = u.shape
    C = C3 // 3
    nt = L // tb
    sub = V7X_SUBLANES
    na = tb // DFT_N2
    split = pl.BlockSpec((None, na, DFT_N2H, sub, C), lambda b, i: (b, i, 0, 0, 0))
    split_shape = jax.ShapeDtypeStruct((B, L // DFT_N2, DFT_N2H, sub, C), F32)
    cur = lambda b, i: (b * nt + i, 0)
    prev = lambda b, i: (jnp.maximum((b * L + i * tb) // sub - 1, 0), 0)
    nxt = lambda b, i: (jnp.minimum((b * L + (i + 1) * tb) // sub, T // sub - 1), 0)
    fixed = lambda b, i: (0, 0)
    return pl.pallas_call(
        _hpre_kernel,
        grid=(B, nt),
        in_specs=[
            pl.BlockSpec((tb, C3), cur),
            pl.BlockSpec((sub, C3), prev),
            pl.BlockSpec((sub, C3), nxt),
            pl.BlockSpec((SHORT_CONV, C3), fixed),
            pl.BlockSpec((1, C3), fixed),
        ],
        out_specs=[split, split],
        out_shape=[split_shape, split_shape],
        compiler_params=_cparams(("parallel", "parallel"), 32),
        name="hpre",
    )(u, u, u, conv_w, conv_b)


def _filt_kernel(z_ref, w1_ref, b1_ref, w2_ref, b2_ref, w3_ref, b3_ref, fr_ref, wo_ref, dl_ref,
                 hfb_ref, asum_ref, *, L):
    i = pl.program_id(0)
    tl = z_ref.shape[0]
    C = dl_ref.shape[1]
    hp = lax.Precision.HIGHEST
    h = jnp.sin(fr_ref[0:1, :] * (jnp.dot(z_ref[...], w1_ref[...], precision=hp, preferred_element_type=F32) + b1_ref[...]))
    h = jnp.sin(fr_ref[1:2, :] * (jnp.dot(h, w2_ref[...], precision=hp, preferred_element_type=F32) + b2_ref[...]))
    h = jnp.sin(fr_ref[2:3, :] * (jnp.dot(h, w3_ref[...], precision=hp, preferred_element_type=F32) + b3_ref[...]))
    o = jnp.dot(h, wo_ref[...], precision=hp, preferred_element_type=F32)
    grow = lax.broadcasted_iota(jnp.int32, (tl, C), 0) + i * tl
    t = grow.astype(F32) * (1.0 / (L - 1))
    decay = jnp.exp(-t * dl_ref[...])
    hf = o[:, :C] * decay
    hb = jnp.where(grow == 0, 0.0, o[:, C:] * decay)
    tile = hfb_ref.shape[:-1] + (C,)
    hfb_ref[:, :, :, :C] = hf.reshape(tile)
    hfb_ref[:, :, :, C:] = hb.reshape(tile)

    @pl.when(i == 0)
    def _():
        asum_ref[...] = jnp.zeros_like(asum_ref)

    asum_ref[...] += jnp.sum(jnp.abs(hf) + jnp.abs(hb), axis=0, keepdims=True)


def _filter_taps(L, w1, b1, w2, b2, w3, b3, freq, wout, tl=512):
    C = wout.shape[1] // 2
    order = w1.shape[1]
    emb = w1.shape[0]
    t = jnp.linspace(0.0, 1.0, L, dtype=F32)[:, None]
    w = 2.0 * math.pi * jnp.arange(L, dtype=F32)[:, None] / L
    f = jnp.linspace(1e-4, FILTER_BANDS - 1, FILTER_BANDS, dtype=F32)[None, :]
    z = jnp.concatenate([t, jnp.cos(f * w), -jnp.sin(f * w)], axis=-1)
    zp = jnp.pad(z, ((0, 0), (0, V7X_LANES - emb)))
    w1p = jnp.pad(w1, ((0, V7X_LANES - emb), (0, 0)))
    deltas = jnp.abs(jnp.linspace(math.log(DECAY_TARGET) / SLOW_DECAY, math.log(DECAY_TARGET) / FAST_DECAY, C, dtype=F32))[None]
    fixed = lambda i: (0, 0)
    return pl.pallas_call(
        functools.partial(_filt_kernel, L=L),
        grid=(L // tl,),
        in_specs=[
            pl.BlockSpec((tl, V7X_LANES), lambda i: (i, 0)),
            pl.BlockSpec((V7X_LANES, order), fixed), pl.BlockSpec((1, order), fixed),
            pl.BlockSpec((order, order), fixed), pl.BlockSpec((1, order), fixed),
            pl.BlockSpec((order, order), fixed), pl.BlockSpec((1, order), fixed),
            pl.BlockSpec((3, order), fixed),
            pl.BlockSpec((order, 2 * C), fixed),
            pl.BlockSpec((1, C), fixed),
        ],
        out_specs=[pl.BlockSpec((None, tl // DFT_N2, DFT_N2H, V7X_SUBLANES, 2 * C), lambda i: (0, i, 0, 0, 0)),
                   pl.BlockSpec((1, C), fixed)],
        out_shape=[jax.ShapeDtypeStruct((1, L // DFT_N2, DFT_N2H, V7X_SUBLANES, 2 * C), F32),
                   jax.ShapeDtypeStruct((1, C), F32)],
        compiler_params=_cparams(("arbitrary",), 32),
        name="filt",
    )(zp, w1p, b1[None], w2, b2[None], w3, b3[None], freq, wout, deltas)


def _dft_constants(L):
    n1 = 2 * L // DFT_N2
    n1h = n1 // 2
    N = 2 * L
    sub = V7X_SUBLANES

    def cs(num, den):
        ang = (num % den).astype(np.float64) * (2.0 * np.pi / den)
        return np.cos(ang), np.sin(ang)

    k1 = np.arange(n1, dtype=np.int64)
    a = np.arange(n1h, dtype=np.int64)
    c, s = cs(k1[:, None] * a[None, :], n1)
    eye = np.eye(sub)

    def expand(m):
        r, kk = m.shape
        return (m[:, None, :, None] * eye[None, :, None, :]).reshape(r * sub, kk * sub)

    def const(m):
        return jnp.asarray(m.astype(np.float32).astype(BF16))

    m1c = np.stack([np.concatenate([c, s], axis=1), np.concatenate([-s, c], axis=1)], axis=1).reshape(2 * n1, 2 * n1h)
    m1r = np.stack([c, -s], axis=1).reshape(2 * n1, n1h)
    ct, st = c.T, s.T
    m3 = np.stack([np.stack([ct, -st], axis=2).reshape(n1h, 2 * n1),
                   np.stack([st, ct], axis=2).reshape(n1h, 2 * n1)], axis=0).reshape(2 * n1h, 2 * n1)
    n2 = np.arange(DFT_N2, dtype=np.int64)
    gc, gs = cs(n2[:, None] * n2[None, :], DFT_N2)
    g_fwd = np.concatenate([np.concatenate([gc, gs], axis=1), np.concatenate([-gs, gc], axis=1)], axis=0)
    g_inv = np.concatenate([np.concatenate([gc, -gs], axis=1), np.concatenate([gs, gc], axis=1)], axis=0)
    ph = (jnp.arange(n1, dtype=jnp.int32)[:, None] * jnp.arange(DFT_N2, dtype=jnp.int32)[None, :]) % N
    ang = ph.astype(F32) * (2.0 * math.pi / N)
    lanes = (n1, DFT_N2H, sub, V7X_LANES)
    twc = jnp.broadcast_to(jnp.cos(ang).reshape(n1, DFT_N2H, sub, 1), lanes)
    tws = jnp.broadcast_to(jnp.sin(ang).reshape(n1, DFT_N2H, sub, 1), lanes)
    return dict(n1=n1, n1h=n1h, m1c=const(expand(m1c)), m1r=const(expand(m1r)), m3=const(expand(m3)),
                g_fwd=const(g_fwd), g_inv=const(g_inv), twc=twc, tws=tws)


def _dft1_kernel(x_ref, m_ref, o_ref):
    rows = m_ref.shape[1]
    xs = x_ref[...].reshape(rows, x_ref.shape[-1]).astype(BF16)
    o_ref[...] = jnp.dot(m_ref[...], xs, preferred_element_type=F32).reshape(o_ref.shape)


def _dft1(x5, m, n1, cb):
    P, n1h, _, sub, Cx = x5.shape
    return pl.pallas_call(
        _dft1_kernel,
        grid=(DFT_N2H, Cx // cb),
        in_specs=[
            pl.BlockSpec((P, n1h, None, sub, cb), lambda h, c: (0, 0, h, 0, c)),
            pl.BlockSpec(m.shape, lambda h, c: (0, 0)),
        ],
        out_specs=pl.BlockSpec((None, n1, 2, sub, cb), lambda h, c: (h, 0, 0, 0, c)),
        out_shape=jax.ShapeDtypeStruct((DFT_N2H, n1, 2, sub, Cx), F32),
        compiler_params=_cparams(("parallel", "parallel"), 48),
        name="dft1",
    )(x5, m)


def _dft2_kernel(a_ref, f_ref, twc_ref, tws_ref, gf_ref, gi_ref, o_ref):
    C = a_ref.shape[-1]
    half = DFT_N2

    def lanes(fn):
        return jnp.concatenate([fn(slice(c0, c0 + V7X_LANES)) for c0 in range(0, C, V7X_LANES)], axis=-1)

    for kk in range(a_ref.shape[1]):
        twc, tws = twc_ref[kk], tws_ref[kk]

        def spectrum(ref, col0):
            def re_part(sl):
                return ref[:, kk, 0, :, col0 + sl.start:col0 + sl.stop] * twc + ref[:, kk, 1, :, col0 + sl.start:col0 + sl.stop] * tws

            def im_part(sl):
                return ref[:, kk, 1, :, col0 + sl.start:col0 + sl.stop] * twc - ref[:, kk, 0, :, col0 + sl.start:col0 + sl.stop] * tws

            t = jnp.concatenate([lanes(re_part).reshape(half, C), lanes(im_part).reshape(half, C)], axis=0).astype(BF16)
            s = jnp.dot(gf_ref[...], t, preferred_element_type=F32)
            return s[:half], s[half:]

        xr, xi = spectrum(a_ref, 0)
        fr, fi = spectrum(f_ref, 0)
        br, bi = spectrum(f_ref, C)
        hr, hi = fr + br, fi - bi
        y = jnp.concatenate([xr * hr - xi * hi, xr * hi + xi * hr], axis=0).astype(BF16)
        b = jnp.dot(gi_ref[...], y, preferred_element_type=F32)
        br2 = b[:half].reshape(DFT_N2H, V7X_SUBLANES, C)
        bi2 = b[half:].reshape(DFT_N2H, V7X_SUBLANES, C)
        for c0 in range(0, C, V7X_LANES):
            sl = slice(c0, c0 + V7X_LANES)
            o_ref[:, kk, 0, :, sl] = br2[:, :, sl] * twc - bi2[:, :, sl] * tws
            o_ref[:, kk, 1, :, sl] = bi2[:, :, sl] * twc + br2[:, :, sl] * tws


def _dft2(a5, f5, k, kb=4):
    n2h, n1, _, sub, C = a5.shape
    kb = min(kb, n1)
    return pl.pallas_call(
        _dft2_kernel,
        grid=(n1 // kb,),
        in_specs=[
            pl.BlockSpec((n2h, kb, 2, sub, C), lambda i: (0, i, 0, 0, 0)),
            pl.BlockSpec((n2h, kb, 2, sub, 2 * C), lambda i: (0, i, 0, 0, 0)),
            pl.BlockSpec((kb, n2h, sub, V7X_LANES), lambda i: (i, 0, 0, 0)),
            pl.BlockSpec((kb, n2h, sub, V7X_LANES), lambda i: (i, 0, 0, 0)),
            pl.BlockSpec((2 * DFT_N2, 2 * DFT_N2), lambda i: (0, 0)),
            pl.BlockSpec((2 * DFT_N2, 2 * DFT_N2), lambda i: (0, 0)),
        ],
        out_specs=pl.BlockSpec((n2h, kb, 2, sub, C), lambda i: (0, i, 0, 0, 0)),
        out_shape=jax.ShapeDtypeStruct(a5.shape, F32),
        compiler_params=_cparams(("parallel",), 48),
        name="dft2",
    )(a5, f5, k["twc"], k["tws"], k["g_fwd"], k["g_inv"])


def _dft3_kernel(b_ref, m_ref, x0_ref, vx_ref, sc_ref, d_ref, o_ref):
    C = b_ref.shape[-1]
    bs = b_ref[...].reshape(m_ref.shape[1], C).astype(BF16)
    y = jnp.dot(m_ref[...], bs, preferred_element_type=F32).reshape(o_ref.shape)
    o_ref[...] = x0_ref[...] * (y * sc_ref[...] + vx_ref[...] * d_ref[...])


def _dft3(b5, m3, x05, vx5, scale, d):
    n2h, n1, _, sub, C = b5.shape
    Bt, n1h = x05.shape[0], x05.shape[1]
    tok = pl.BlockSpec((Bt, n1h, None, sub, C), lambda h: (0, 0, h, 0, 0))
    vec = pl.BlockSpec((1, C), lambda h: (0, 0))
    return pl.pallas_call(
        _dft3_kernel,
        grid=(n2h,),
        in_specs=[
            pl.BlockSpec((None, n1, 2, sub, C), lambda h: (h, 0, 0, 0, 0)),
            pl.BlockSpec(m3.shape, lambda h: (0, 0)),
            tok, tok, vec, vec,
        ],
        out_specs=tok,
        out_shape=jax.ShapeDtypeStruct(x05.shape, F32),
        compiler_params=_cparams(("parallel",), 48),
        name="dft3",
    )(b5, m3, x05, vx5, scale, d)


def _hyena(u, conv_w, conv_b, w1, b1, w2, b2, w3, b3, freq, wout, d_skip, B, L):
    T = u.shape[0]
    C = HYENA_WIDTH
    k = _dft_constants(L)
    n1, n1h = k["n1"], k["n1h"]
    x0, vx = _hpre(u, conv_w, conv_b[None], B, L)
    hfb, asum = _filter_taps(L, w1, b1, w2, b2, w3, b3, freq, wout)
    a5 = _dft1(vx, k["m1c"], n1, cb=C)
    f5 = _dft1(hfb, k["m1r"], n1, cb=C)
    b5 = _dft2(a5, f5, k)
    scale = 1.0 / (asum * (2 * L))
    return _dft3(b5, k["m3"], x0, vx, scale, d_skip[None])


PACKED = jnp.uint32


def _pack_pairs(x):
    half = x.shape[1] // 2
    bits = pltpu.bitcast(x.astype(BF16).astype(F32), PACKED)
    return (bits[:, :half] >> 16) | bits[:, half:]


def _unpack_pairs(w):
    lo = pltpu.bitcast(w << 16, F32)
    hi = pltpu.bitcast(w & jnp.uint32(0xFFFF0000), F32)
    return lo, hi


def _oproj_kernel(x_ref, g0_ref, b0_ref, at_ref, hy_ref, wo_ref, g1_ref, b1_ref, wrh_ref, wrl_ref,
                  h1_ref, h1p_ref, sc_ref):
    aw = at_ref.shape[1]
    h0 = _layer_norm_rows(x_ref[...], g0_ref[...], b0_ref[...])
    mixed = jnp.dot(at_ref[...], wo_ref[0:aw, :], preferred_element_type=F32)
    hy = hy_ref[...].reshape(x_ref.shape[0], hy_ref.shape[-1])
    mixed = mixed + jnp.dot(hy.astype(BF16), wo_ref[aw:, :], preferred_element_type=F32)
    h1 = _layer_norm_rows(ALPHA * h0 + mixed, g1_ref[...], b1_ref[...])
    h1_ref[...] = h1
    h1p_ref[...] = _pack_pairs(h1)
    hh = h1.astype(BF16)
    hl = (h1 - hh.astype(F32)).astype(BF16)
    dn = (((1,), (1,)), ((), ()))
    logits = lax.dot_general(wrh_ref[...], hh, dn, preferred_element_type=F32)
    logits = logits + (lax.dot_general(wrh_ref[...], hl, dn, preferred_element_type=F32)
                       + lax.dot_general(wrl_ref[...], hh, dn, preferred_element_type=F32))
    sc_ref[...] = 1.0 / (1.0 + jnp.exp(-logits))


def _oproj(x2, g0, b0, attn, hy, wo_bf, g1, b1, wr_t, tm=256):
    T, D = x2.shape
    E = wr_t.shape[0]
    wrh = wr_t.astype(BF16)
    wrl = (wr_t - wrh.astype(F32)).astype(BF16)
    row = lambda i: (i, 0)
    fixed = lambda i: (0, 0)
    hy4 = hy.reshape((-1,) + hy.shape[2:])
    return pl.pallas_call(
        _oproj_kernel,
        grid=(T // tm,),
        in_specs=[
            pl.BlockSpec((tm, D), row), pl.BlockSpec((1, D), fixed), pl.BlockSpec((1, D), fixed),
            pl.BlockSpec((tm, attn.shape[1]), row),
            pl.BlockSpec((tm // DFT_N2,) + hy4.shape[1:], lambda i: (i, 0, 0, 0)),
            pl.BlockSpec((D, D), fixed), pl.BlockSpec((1, D), fixed), pl.BlockSpec((1, D), fixed),
            pl.BlockSpec((E, D), fixed), pl.BlockSpec((E, D), fixed),
        ],
        out_specs=[pl.BlockSpec((tm, D), row), pl.BlockSpec((tm, D // 2), row), pl.BlockSpec((E, tm), lambda i: (0, i))],
        out_shape=[
            jax.ShapeDtypeStruct((T, D), F32),
            jax.ShapeDtypeStruct((T, D // 2), PACKED),
            jax.ShapeDtypeStruct((E, T), F32),
        ],
        compiler_params=_cparams(("parallel",), 48),
        name="oproj",
    )(x2, g0, b0, attn, hy4, wo_bf, g1, b1, wrh, wrl)


def _route_kernel(sc_ref, bias_ref, tri_ref, idx_ref, gate_ref, rank_ref, cnt_ref, carry_ref):
    step = pl.program_id(0)
    E, tm = sc_ref.shape
    neg = jnp.float32(-jnp.inf)
    scores = sc_ref[...]
    biased = scores + bias_ref[...]
    erow = lax.broadcasted_iota(jnp.int32, (E, tm), 0)
    big = jnp.int32(E)

    def first_argmax(vals, rows):
        m = jnp.max(vals, axis=0, keepdims=True)
        pick = jnp.min(jnp.where(vals == m, rows, big), axis=0, keepdims=True)
        return m, pick

    gsc = []
    for g in range(N_GROUPS):
        blk = biased[g * GROUP_SIZE:(g + 1) * GROUP_SIZE, :]
        rows = erow[g * GROUP_SIZE:(g + 1) * GROUP_SIZE, :]
        m1, p1 = first_argmax(blk, rows)
        m2 = jnp.max(jnp.where(rows == p1, neg, blk), axis=0, keepdims=True)
        gsc.append(m1 + m2)
    gsc = jnp.concatenate(gsc, axis=0)
    grow = lax.broadcasted_iota(jnp.int32, (N_GROUPS, tm), 0)
    gsel = jnp.zeros((N_GROUPS, tm), jnp.bool_)
    work = gsc
    for _ in range(TOPK_GROUPS):
        _, p = first_argmax(work, grow)
        hit = grow == p
        gsel = gsel | hit
        work = jnp.where(hit, neg, work)
    emask = jnp.concatenate(
        [jnp.broadcast_to(gsel[g:g + 1, :], (GROUP_SIZE, tm)) for g in range(N_GROUPS)], axis=0)
    work = jnp.where(emask, biased, neg)

    sel = jnp.zeros((E, tm), jnp.bool_)
    picks, gvals = [], []
    for _ in range(TOP_K):
        _, p = first_argmax(work, erow)
        hit = erow == p
        sel = sel | hit
        picks.append(p)
        gvals.append(jnp.sum(jnp.where(hit, scores, 0.0), axis=0, keepdims=True))
        work = jnp.where(hit, neg, work)
    gv = jnp.concatenate(gvals, axis=0)
    idx_ref[...] = jnp.concatenate(picks, axis=0)
    gate_ref[...] = gv / jnp.sum(gv, axis=0, keepdims=True) * ROUTED_SCALE

    @pl.when(step == 0)
    def _():
        carry_ref[...] = jnp.zeros_like(carry_ref)

    chosen = sel.astype(F32)
    before = jnp.dot(chosen.astype(BF16), tri_ref[...], preferred_element_type=F32) + carry_ref[...]
    rank_ref[...] = jnp.concatenate(
        [jnp.sum(jnp.where(erow == p, before, 0.0), axis=0, keepdims=True) for p in picks], axis=0).astype(jnp.int32)
    carry_ref[...] += jnp.sum(chosen, axis=1, keepdims=True)
    cnt_ref[...] = carry_ref[...].astype(jnp.int32)


def _route(scores_t, bias, tm=128):
    E, T = scores_t.shape
    tri = (jnp.arange(tm)[:, None] < jnp.arange(tm)[None, :]).astype(BF16)
    tok = lambda i: (0, i)
    fixed = lambda i: (0, 0)
    return pl.pallas_call(
        _route_kernel,
        grid=(T // tm,),
        in_specs=[pl.BlockSpec((E, tm), tok), pl.BlockSpec((E, 1), fixed), pl.BlockSpec((tm, tm), fixed)],
        out_specs=[pl.BlockSpec((TOP_K, tm), tok), pl.BlockSpec((TOP_K, tm), tok), pl.BlockSpec((TOP_K, tm), tok),
                   pl.BlockSpec((E, 1), fixed)],
        out_shape=[
            jax.ShapeDtypeStruct((TOP_K, T), jnp.int32),
            jax.ShapeDtypeStruct((TOP_K, T), F32),
            jax.ShapeDtypeStruct((TOP_K, T), jnp.int32),
            jax.ShapeDtypeStruct((E, 1), jnp.int32),
        ],
        scratch_shapes=[pltpu.VMEM((E, 1), F32)],
        compiler_params=_cparams(("arbitrary",), 32),
        name="route",
    )(scores_t, bias, tri)


def _dest_kernel(idx_ref, rank_ref, ps_ref, dest_ref):
    E = ps_ref.shape[0]
    tm = idx_ref.shape[1]
    erow = lax.broadcasted_iota(jnp.int32, (E, tm), 0)
    ps = ps_ref[...].astype(F32)
    rows = [jnp.sum(jnp.where(erow == idx_ref[k:k + 1, :], ps, 0.0), axis=0, keepdims=True) for k in range(TOP_K)]
    dest_ref[...] = jnp.concatenate(rows, axis=0).astype(jnp.int32) + rank_ref[...]


def _dest_rows(idx, rank, pad_start, tm=512):
    K, T = idx.shape
    E = pad_start.shape[0]
    tok = lambda i: (0, i)
    return pl.pallas_call(
        _dest_kernel,
        grid=(T // tm,),
        in_specs=[pl.BlockSpec((K, tm), tok), pl.BlockSpec((K, tm), tok), pl.BlockSpec((E, 1), lambda i: (0, 0))],
        out_specs=pl.BlockSpec((K, tm), tok),
        out_shape=jax.ShapeDtypeStruct((K, T), jnp.int32),
        compiler_params=_cparams(("arbitrary",), 32),
        name="dest",
    )(idx, rank, pad_start[:, None])


def _scatter_kernel(ps_ref, pe_ref, dest_ref, h_ref, xs_ref, zbuf, sem, *, rb):
    sub = V7X_SUBLANES
    tm = h_ref.shape[0]

    @pl.when(pl.program_id(0) == 0)
    def _():
        zbuf[...] = jnp.zeros_like(zbuf)

        def zcopy(e):
            return pltpu.make_async_copy(zbuf, xs_ref.at[pl.ds(pl.multiple_of(pe_ref[e] - rb, rb), rb)], sem)

        def zstart(e, c):
            @pl.when(pe_ref[e] > ps_ref[e])
            def _():
                zcopy(e).start()
            return c

        def zwait(e, c):
            @pl.when(pe_ref[e] > ps_ref[e])
            def _():
                zcopy(e).wait()
            return c

        lax.fori_loop(0, ps_ref.shape[0], zstart, 0)
        lax.fori_loop(0, ps_ref.shape[0], zwait, 0)

    def issue(r8, c):
        for j in range(sub):
            for k in range(TOP_K):
                dst = dest_ref[0, r8 * (sub * TOP_K) + (j * TOP_K + k)]
                pltpu.make_async_copy(h_ref.at[pl.ds(r8 * sub + j, 1)], xs_ref.at[pl.ds(dst, 1)], sem).start(priority=k % 2)
        return c

    lax.fori_loop(0, tm // sub, issue, 0)
    for k in range(TOP_K):
        pltpu.make_async_copy(h_ref, xs_ref.at[pl.ds(0, tm)], sem).wait()


def _dest_tiles(dest, tm):
    K, T = dest.shape
    return dest.T.reshape(T // tm, 1, tm * K)


def _scatter_rows(pad_start, pad_end, dest_t, h1p, n_rows, rb, tm=256):
    T, W = h1p.shape
    return pl.pallas_call(
        functools.partial(_scatter_kernel, rb=rb),
        grid_spec=pltpu.PrefetchScalarGridSpec(
            num_scalar_prefetch=2,
            grid=(T // tm,),
            in_specs=[
                pl.BlockSpec((None, 1, tm * TOP_K), lambda i, ps, pe: (i, 0, 0), memory_space=pltpu.SMEM),
                pl.BlockSpec((tm, W), lambda i, ps, pe: (i, 0)),
            ],
            out_specs=pl.BlockSpec(memory_space=pl.ANY),
            scratch_shapes=[pltpu.VMEM((rb, W), h1p.dtype), pltpu.SemaphoreType.DMA],
        ),
        out_shape=jax.ShapeDtypeStruct((n_rows, W), h1p.dtype),
        compiler_params=_cparams(("arbitrary",), 32),
        name="scatter",
    )(pad_start, pad_end, dest_t, h1p)


EXPERT_X_SLOTS = 4
EXPERT_Y_SLOTS = 3


def _experts_kernel(first_ref, nblk_ref, wsel_ref, tot_ref, xs_ref, wg_ref, wu_ref, wd_ref, ys_ref,
                    xbuf, ybuf, wgub, wdb, xsem, ysem, *, rb):
    del wsel_ref
    e = pl.program_id(0)
    F = wg_ref.shape[1]
    nx, ny = xbuf.shape[0], ybuf.shape[0]
    total = tot_ref[0]
    n = nblk_ref[e]
    g0 = first_ref[e]

    def x_copy(g):
        slot = g % nx
        return pltpu.make_async_copy(xs_ref.at[pl.ds(pl.multiple_of(g * rb, rb), rb)], xbuf.at[slot], xsem.at[slot])

    def y_copy(g):
        slot = g % ny
        return pltpu.make_async_copy(ybuf.at[slot], ys_ref.at[pl.ds(pl.multiple_of(g * rb, rb), rb)], ysem.at[slot])

    @pl.when(e == 0)
    def _():
        for g in range(nx):
            @pl.when(g < total)
            def _():
                x_copy(g).start()

    def process(g, count):
        for d in range(count):
            x_copy(g + d).wait()
        ys = []
        for d in range(count):
            lo, hi = _unpack_pairs(xbuf[(g + d) % nx])
            x = jnp.concatenate([lo, hi], axis=1).astype(BF16)
            gu = jnp.dot(x, wgub[...], preferred_element_type=F32)
            gate, up = gu[:, 0:F], gu[:, F:]
            hb = (gate / (1.0 + jnp.exp(-gate)) * up).astype(BF16)
            ys.append(_pack_pairs(jnp.dot(hb, wdb[...], preferred_element_type=F32)))
        for d in range(count):
            @pl.when(g + d >= ny)
            def _():
                y_copy(g + d - ny).wait()

            ybuf[(g + d) % ny] = ys[d]
            y_copy(g + d).start()
        for d in range(count):
            @pl.when(g + d + nx < total)
            def _():
                x_copy(g + d + nx).start()

    @pl.when(n > 0)
    def _():
        wgub[:, 0:F] = wg_ref[...].astype(BF16)
        wgub[:, F:] = wu_ref[...].astype(BF16)
        wdb[...] = wd_ref[...].astype(BF16)

        def pair(j, c):
            process(g0 + 2 * j, 2)
            return c

        lax.fori_loop(0, n // 2, pair, 0)

        @pl.when(n % 2 == 1)
        def _():
            process(g0 + n - 1, 1)

    @pl.when(e == pl.num_programs(0) - 1)
    def _():
        for back in range(ny, 0, -1):
            @pl.when(total >= back)
            def _():
                y_copy(total - back).wait()


def _experts(first_blk, nblk, wsel, total_blk, xs, w_gate, w_up, w_down, rb):
    P, W = xs.shape
    E, D, F = w_gate.shape
    wspec = lambda shape: pl.BlockSpec((None,) + shape, lambda e, fb, nbk, ws, tt: (ws[e], 0, 0))
    return pl.pallas_call(
        functools.partial(_experts_kernel, rb=rb),
        grid_spec=pltpu.PrefetchScalarGridSpec(
            num_scalar_prefetch=4,
            grid=(E,),
            in_specs=[pl.BlockSpec(memory_space=pl.ANY), wspec((D, F)), wspec((D, F)), wspec((F, D))],
            out_specs=pl.BlockSpec(memory_space=pl.ANY),
            scratch_shapes=[
                pltpu.VMEM((EXPERT_X_SLOTS, rb, W), PACKED), pltpu.VMEM((EXPERT_Y_SLOTS, rb, W), PACKED),
                pltpu.VMEM((D, 2 * F), BF16), pltpu.VMEM((F, D), BF16),
                pltpu.SemaphoreType.DMA((EXPERT_X_SLOTS,)), pltpu.SemaphoreType.DMA((EXPERT_Y_SLOTS,)),
            ],
        ),
        out_shape=jax.ShapeDtypeStruct((P, W), PACKED),
        compiler_params=_cparams(("arbitrary",), 48),
        name="experts",
    )(first_blk, nblk, wsel, total_blk, xs, w_gate, w_up, w_down)


def _combine_kernel(dest_ref, destn_ref, h1_ref, gate_ref, ys_ref, ys8_ref, sg_ref, su_ref, sd_ref,
                    g2_ref, b2_ref, o_ref, buf, sem):
    i = pl.program_id(0)
    n = pl.num_programs(0)
    tm = h1_ref.shape[0]
    slot = i % 2

    sub = V7X_SUBLANES

    def issue(dref, s):
        def body(r8, c):
            for j in range(sub):
                for k in range(TOP_K):
                    src = ys_ref.at[pl.ds(dref[0, r8 * (sub * TOP_K) + (j * TOP_K + k)], 1)]
                    pltpu.make_async_copy(src, buf.at[s, r8 * sub + j, pl.ds(k, 1)], sem.at[s]).start(priority=k % 2)
            return c
        lax.fori_loop(0, tm // sub, body, 0)

    @pl.when(i == 0)
    def _():
        issue(dest_ref, 0)

    @pl.when(i + 1 < n)
    def _():
        issue(destn_ref, 1 - slot)

    h1 = h1_ref[...]
    hb = h1.astype(BF16)
    g = jnp.dot(hb, sg_ref[...], preferred_element_type=F32)
    u = jnp.dot(hb, su_ref[...], preferred_element_type=F32)
    ffn = jnp.dot((g / (1.0 + jnp.exp(-g)) * u).astype(BF16), sd_ref[...], preferred_element_type=F32)

    pltpu.make_async_copy(ys8_ref.at[pl.ds(0, tm)], buf.at[slot], sem.at[slot]).wait()
    nk = tm * TOP_K
    lo, hi = _unpack_pairs(buf[slot].reshape(nk, buf.shape[3]))
    y = jnp.concatenate([lo, hi], axis=1).astype(BF16)
    gates = gate_ref[...]
    gh = gates.astype(BF16)
    gl = (gates - gh.astype(F32)).astype(BF16)
    pick = lax.broadcasted_iota(jnp.int32, (TOP_K, nk), 1) % TOP_K == lax.broadcasted_iota(jnp.int32, (TOP_K, nk), 0)
    spread = pick.astype(BF16)
    own = lax.broadcasted_iota(jnp.int32, (tm, nk), 1) // TOP_K == lax.broadcasted_iota(jnp.int32, (tm, nk), 0)
    routed = jnp.zeros((tm, y.shape[1]), F32)
    for part in (gh, gl):
        gm = jnp.where(own, jnp.dot(part, spread, preferred_element_type=F32), 0.0).astype(BF16)
        routed = routed + jnp.dot(gm, y, preferred_element_type=F32)
    o_ref[...] = _layer_norm_rows(ALPHA * h1 + ffn + routed, g2_ref[...], b2_ref[...])


def _combine(dest, h1, gates_tk, ys, sg, su, sd, g2, b2, tm=128):
    T, D = h1.shape
    W = ys.shape[1]
    F = sg.shape[1]
    nt = T // tm
    row = lambda i: (i, 0)
    fixed = lambda i: (0, 0)
    smem = functools.partial(pl.BlockSpec, (None, 1, tm * TOP_K), memory_space=pltpu.SMEM)
    dest_t = _dest_tiles(dest, tm)
    ys8 = ys.reshape(ys.shape[0] // TOP_K, TOP_K, W)
    return pl.pallas_call(
        _combine_kernel,
        grid=(nt,),
        in_specs=[
            smem(index_map=lambda i: (i, 0, 0)),
            smem(index_map=lambda i: (jnp.minimum(i + 1, nt - 1), 0, 0)),
            pl.BlockSpec((tm, D), row),
            pl.BlockSpec((tm, TOP_K), row),
            pl.BlockSpec(memory_space=pl.ANY),
            pl.BlockSpec(memory_space=pl.ANY),
            pl.BlockSpec((D, F), fixed), pl.BlockSpec((D, F), fixed), pl.BlockSpec((F, D), fixed),
            pl.BlockSpec((1, D), fixed), pl.BlockSpec((1, D), fixed),
        ],
        out_specs=pl.BlockSpec((tm, D), row),
        out_shape=jax.ShapeDtypeStruct((T, D), F32),
        scratch_shapes=[pltpu.VMEM((2, tm, TOP_K, W), PACKED), pltpu.SemaphoreType.DMA((2,))],
        compiler_params=_cparams(("arbitrary",), 48),
        name="combine",
    )(dest_t, dest_t, h1, gates_tk, ys, ys8, sg, su, sd, g2, b2)


MOE_ROWS = 256


def _moe(h1, h1p, scores_t, router_bias, w_gate, w_up, w_down, ws_gate, ws_up, ws_down, g2, b2):
    T = h1.shape[0]
    E = N_EXPERTS
    rb = MOE_ROWS
    idx, gates, rank, counts = _route(scores_t, router_bias[:, None])
    counts = counts[:, 0]
    padded = (counts + rb - 1) // rb * rb
    pad_end = jnp.cumsum(padded)
    pad_start = pad_end - padded
    nb = (T * TOP_K) // rb + E
    nblk = (padded // rb).astype(jnp.int32)
    first_blk = (pad_start // rb).astype(jnp.int32)
    total_blk = (pad_end[-1:] // rb).astype(jnp.int32)
    wsel = lax.cummax(jnp.where(nblk > 0, jnp.arange(E, dtype=jnp.int32), 0))
    dest = _dest_rows(idx, rank, pad_start)
    xs = _scatter_rows(pad_start, pad_end, _dest_tiles(dest, 256), h1p, nb * rb, rb, tm=256)
    ys = _experts(first_blk, nblk, wsel, total_blk, xs, w_gate, w_up, w_down, rb)
    return _combine(dest, h1, gates.T, ys, ws_gate.astype(BF16), ws_up.astype(BF16), ws_down.astype(BF16), g2, b2)


def kernel(x, positions, emb_ln_g, emb_ln_b, w_in, hy_conv_w, hy_conv_b, hy_f_w1, hy_f_b1, hy_f_w2, hy_f_b2, hy_f_w3, hy_f_b3, hy_f_freq, hy_f_wout, hy_d, lambda_q1, lambda_k1, lambda_q2, lambda_k2, subln_g, w_o, ln1_g, ln1_b, w_router, router_bias, w_gate, w_up, w_down, ws_gate, ws_up, ws_down, ln2_g, ln2_b):
    B, L, D = x.shape
    T = B * L
    assert w_in.shape[0] == DEPTH == 1
    i = 0
    x2 = x.reshape(T, D)
    g0, b0 = emb_ln_g[None], emb_ln_b[None]
    ra, rm, rp = _rotary_tables(positions)
    q, k, v, u = _inproj(x2, g0, b0, w_in[i].astype(BF16), ra, rm, rp)
    lam = (jnp.exp(jnp.sum(lambda_q1[i] * lambda_k1[i])) - jnp.exp(jnp.sum(lambda_q2[i] * lambda_k2[i])) + LAM_INIT)
    attn = _attention(q, k, v, lam.reshape(1).astype(F32), subln_g[i][None], B, L)
    hy = _hyena(u, hy_conv_w[i], hy_conv_b[i], hy_f_w1[i], hy_f_b1[i], hy_f_w2[i], hy_f_b2[i], hy_f_w3[i], hy_f_b3[i],
                hy_f_freq[i], hy_f_wout[i], hy_d[i], B, L)
    h1, h1p, scores_t = _oproj(x2, g0, b0, attn, hy, w_o[i].astype(BF16), ln1_g[i][None], ln1_b[i][None], w_router[i].T)
    out = _moe(h1, h1p, scores_t, router_bias[i], w_gate[i], w_up[i], w_down[i], ws_gate[i], ws_up[i], ws_down[i],
               ln2_g[i][None], ln2_b[i][None])
    return out.reshape(B, L, D)
```

```python
import functools
import math

import numpy as np
import jax
import jax.numpy as jnp
from jax import lax
from jax.experimental import pallas as pl
from jax.experimental.pallas import tpu as pltpu

DA_HEADS = 4
DA_HEAD_DIM = 64
DA_V_DIM = 128
ATTN_WIDTH = 512
HYENA_WIDTH = 512
ROT_DIM = 16
ROPE_THETA = 500000.0
SHORT_CONV = 3
FILTER_EMB = 33
FILTER_BANDS = 16
DECAY_TARGET = 1e-2
FAST_DECAY = 0.3
SLOW_DECAY = 1.5
N_EXPERTS = 256
TOP_K = 8
N_GROUPS = 8
GROUP_SIZE = N_EXPERTS // N_GROUPS
TOPK_GROUPS = 4
EXPERT_DIM = 256
ROUTED_SCALE = 2.5
DEPTH = 1
ALPHA = (2 * DEPTH) ** 0.25
LN_EPS = 1e-5
LAM_INIT = 0.8 - 0.6 * math.exp(-0.3 * 0)

V7X_LANES = 128
V7X_SUBLANES = 8
V7X_VMEM_BYTES = 64 * 1024 * 1024

DFT_N2 = 128
DFT_N2H = DFT_N2 // V7X_SUBLANES

BF16 = jnp.bfloat16
F32 = jnp.float32


def _cparams(sem, vmem_mb):
    return pltpu.CompilerParams(dimension_semantics=sem, vmem_limit_bytes=vmem_mb * 1024 * 1024)


def _layer_norm_rows(x, g, b):
    mu = jnp.mean(x, axis=-1, keepdims=True)
    xc = x - mu
    var = jnp.mean(xc * xc, axis=-1, keepdims=True)
    return xc * lax.rsqrt(var + LN_EPS) * g + b


def _inproj_kernel(x_ref, g_ref, b_ref, w_ref, ra_ref, rm_ref, rp_ref, q_ref, k_ref, v_ref, u_ref):
    h = _layer_norm_rows(x_ref[...], g_ref[...], b_ref[...]).astype(BF16)
    ra, rm, rp = ra_ref[...], rm_ref[...], rp_ref[...]

    def rot(t):
        return t * ra + pltpu.roll(t, V7X_LANES - ROT_DIM // 2, axis=1) * rm + pltpu.roll(t, ROT_DIM // 2, axis=1) * rp

    aw = ATTN_WIDTH
    qp = jnp.dot(h, w_ref[:, 0:aw], preferred_element_type=F32)
    kp = jnp.dot(h, w_ref[:, aw:2 * aw], preferred_element_type=F32)
    scale = DA_HEAD_DIM ** -0.5 * math.log2(math.e)
    for c in range(aw // V7X_LANES):
        sl = slice(c * V7X_LANES, (c + 1) * V7X_LANES)
        q_ref[:, sl] = (rot(qp[:, sl]) * scale).astype(BF16)
        k_ref[:, sl] = rot(kp[:, sl]).astype(BF16)
    vp = jnp.dot(h, w_ref[:, 2 * aw:3 * aw], preferred_element_type=F32).astype(BF16)
    ones = jnp.ones((vp.shape[0], DA_V_DIM), BF16)
    for hd in range(DA_HEADS):
        v_ref[:, 2 * hd * DA_V_DIM:(2 * hd + 1) * DA_V_DIM] = vp[:, hd * DA_V_DIM:(hd + 1) * DA_V_DIM]
        v_ref[:, (2 * hd + 1) * DA_V_DIM:(2 * hd + 2) * DA_V_DIM] = ones
    u_ref[...] = jnp.dot(h, w_ref[:, 3 * aw:], preferred_element_type=F32)


def _inproj(x2, g, b, w_bf, ra, rm, rp, tm=256):
    T, D = x2.shape
    ncol = w_bf.shape[1]
    aw = ATTN_WIDTH
    uw = ncol - 3 * aw
    row = lambda i: (i, 0)
    fixed = lambda i: (0, 0)
    return pl.pallas_call(
        _inproj_kernel,
        grid=(T // tm,),
        in_specs=[
            pl.BlockSpec((tm, D), row),
            pl.BlockSpec((1, D), fixed),
            pl.BlockSpec((1, D), fixed),
            pl.BlockSpec((D, ncol), fixed),
            pl.BlockSpec((tm, V7X_LANES), row),
            pl.BlockSpec((tm, V7X_LANES), row),
            pl.BlockSpec((tm, V7X_LANES), row),
        ],
        out_specs=[
            pl.BlockSpec((tm, aw), row),
            pl.BlockSpec((tm, aw), row),
            pl.BlockSpec((tm, 2 * aw), row),
            pl.BlockSpec((tm, uw), row),
        ],
        out_shape=[
            jax.ShapeDtypeStruct((T, aw), BF16),
            jax.ShapeDtypeStruct((T, aw), BF16),
            jax.ShapeDtypeStruct((T, 2 * aw), BF16),
            jax.ShapeDtypeStruct((T, uw), F32),
        ],
        compiler_params=_cparams(("parallel",), 48),
        name="inproj",
    )(x2, g, b, w_bf, ra, rm, rp)


def _rotary_tables(positions):
    half = ROT_DIM // 2
    inv_freq = ROPE_THETA ** (-jnp.arange(0, ROT_DIM, 2, dtype=F32) / ROT_DIM)
    ang = positions.astype(F32).reshape(-1)[:, None] * inv_freq
    cos, sin = jnp.cos(ang), jnp.sin(ang)
    T = ang.shape[0]
    ones = jnp.ones((T, DA_HEAD_DIM - ROT_DIM), F32)
    zeros_h = jnp.zeros((T, half), F32)
    zeros_r = jnp.zeros((T, DA_HEAD_DIM - ROT_DIM), F32)
    a64 = jnp.concatenate([cos, cos, ones], axis=1)
    m64 = jnp.concatenate([-sin, zeros_h, zeros_r], axis=1)
    p64 = jnp.concatenate([zeros_h, sin, zeros_r], axis=1)
    rep = V7X_LANES // DA_HEAD_DIM
    return jnp.tile(a64, (1, rep)), jnp.tile(m64, (1, rep)), jnp.tile(p64, (1, rep))


def _attn_kernel(lam_ref, q_ref, k_ref, v_ref, g_ref, o_ref, s_ref, m_ref, acc_ref, *, kb, unroll):
    qb = q_ref.shape[0]
    L = k_ref.shape[0]
    nchunk = L // kb
    nl = kb // V7X_LANES
    q = q_ref[...]
    lane = lax.broadcasted_iota(jnp.int32, q.shape, 1)
    zero = jnp.zeros_like(q)
    qm = [jnp.where(lane < DA_HEAD_DIM, q, zero), jnp.where(lane >= DA_HEAD_DIM, q, zero)]
    m_ref[...] = jnp.full(m_ref.shape, -jnp.inf, F32)
    acc_ref[...] = jnp.zeros(acc_ref.shape, F32)

    def score_body(j, carry):
        kc = k_ref[pl.ds(pl.multiple_of(j * kb, kb), kb), :]
        for c in range(2):
            s = lax.dot_general(qm[c], kc, (((1,), (1,)), ((), ())), preferred_element_type=F32)
            s_ref[c, j] = s
            m = s[:, 0:V7X_LANES]
            for t in range(1, nl):
                m = jnp.maximum(m, s[:, t * V7X_LANES:(t + 1) * V7X_LANES])
            m_ref[c] = jnp.maximum(m_ref[c], m)
        return carry

    lax.fori_loop(0, nchunk, score_body, 0, unroll=unroll)
    m_row = [jnp.max(m_ref[c], axis=1, keepdims=True) for c in range(2)]

    def pv_body(j, carry):
        vc = v_ref[pl.ds(pl.multiple_of(j * kb, kb), kb), :]
        for c in range(2):
            p = jnp.exp2(s_ref[c, j] - m_row[c])
            acc_ref[c] += jnp.dot(p.astype(BF16), vc, preferred_element_type=F32)
        return carry

    lax.fori_loop(0, nchunk, pv_body, 0, unroll=unroll)
    outs = [acc_ref[c, :, 0:DA_V_DIM] / acc_ref[c, :, DA_V_DIM:2 * DA_V_DIM] for c in range(2)]
    o = outs[0] - lam_ref[0] * outs[1]
    ms = jnp.mean(o * o, axis=1, keepdims=True)
    o_ref[...] = (o * lax.rsqrt(ms + LN_EPS) * g_ref[...] * (1.0 - LAM_INIT)).astype(o_ref.dtype)


def _attention(q, k, v, lam, subln_g, B, L, qb=512, kb=1024, unroll=8):
    T = q.shape[0]
    nq = L // qb
    return pl.pallas_call(
        functools.partial(_attn_kernel, kb=kb, unroll=unroll),
        grid=(B, DA_HEADS, nq),
        in_specs=[
            pl.BlockSpec(memory_space=pltpu.SMEM),
            pl.BlockSpec((qb, DA_V_DIM), lambda b, h, i: (b * nq + i, h)),
            pl.BlockSpec((L, DA_V_DIM), lambda b, h, i: (b, h)),
            pl.BlockSpec((L, 2 * DA_V_DIM), lambda b, h, i: (b, h)),
            pl.BlockSpec((1, DA_V_DIM), lambda b, h, i: (0, 0)),
        ],
        out_specs=pl.BlockSpec((qb, DA_V_DIM), lambda b, h, i: (b * nq + i, h)),
        out_shape=jax.ShapeDtypeStruct((T, ATTN_WIDTH), BF16),
        scratch_shapes=[
            pltpu.VMEM((2, L // kb, qb, kb), F32),
            pltpu.VMEM((2, qb, V7X_LANES), F32),
            pltpu.VMEM((2, qb, 2 * DA_V_DIM), F32),
        ],
        compiler_params=_cparams(("parallel", "parallel", "parallel"), 56),
        name="attn",
    )(lam, q, k, v, subln_g)


def _hpre_kernel(u_ref, up_ref, un_ref, w_ref, b_ref, x0_ref, vx_ref):
    i = pl.program_id(1)
    n = pl.num_programs(1)
    tb = u_ref.shape[0]
    C = x0_ref.shape[-1]
    row = lax.broadcasted_iota(jnp.int32, (tb, V7X_LANES), 0)
    has_prev = (i > 0).astype(F32)
    has_next = (i < n - 1).astype(F32)

    def conv(c0):
        sl = slice(c0, c0 + V7X_LANES)
        u = u_ref[:, sl]
        prev_row = up_ref[V7X_SUBLANES - 1:V7X_SUBLANES, sl] * has_prev
        next_row = un_ref[0:1, sl] * has_next
        u_prev = jnp.where(row == 0, prev_row, pltpu.roll(u, 1, axis=0))
        u_next = jnp.where(row == tb - 1, next_row, pltpu.roll(u, tb - 1, axis=0))
        return u_prev * w_ref[0:1, sl] + u * w_ref[1:2, sl] + u_next * w_ref[2:3, sl] + b_ref[:, sl]

    tile = x0_ref.shape[:-1] + (V7X_LANES,)
    for c in range(C // V7X_LANES):
        c0 = c * V7X_LANES
        x0_ref[:, :, :, c0:c0 + V7X_LANES] = conv(c0).reshape(tile)
        vx_ref[:, :, :, c0:c0 + V7X_LANES] = (conv(2 * C + c0) * conv(C + c0)).reshape(tile)


def _hpre(u, conv_w, conv_b, B, L, tb=512):
    T, C3 = u.shape
    C = C3 // 3
    nt = L // tb
    sub = V7X_SUBLANES
    na = tb // DFT_N2
    split = pl.BlockSpec((None, na, DFT_N2H, sub, C), lambda b, i: (b, i, 0, 0, 0))
    split_shape = jax.ShapeDtypeStruct((B, L // DFT_N2, DFT_N2H, sub, C), F32)
    cur = lambda b, i: (b * nt + i, 0)
    prev = lambda b, i: (jnp.maximum((b * L + i * tb) // sub - 1, 0), 0)
    nxt = lambda b, i: (jnp.minimum((b * L + (i + 1) * tb) // sub, T // sub - 1), 0)
    fixed = lambda b, i: (0, 0)
    return pl.pallas_call(
        _hpre_kernel,
        grid=(B, nt),
        in_specs=[
            pl.BlockSpec((tb, C3), cur),
            pl.BlockSpec((sub, C3), prev),
            pl.BlockSpec((sub, C3), nxt),
            pl.BlockSpec((SHORT_CONV, C3), fixed),
            pl.BlockSpec((1, C3), fixed),
        ],
        out_specs=[split, split],
        out_shape=[split_shape, split_shape],
        compiler_params=_cparams(("parallel", "parallel"), 32),
        name="hpre",
    )(u, u, u, conv_w, conv_b)


def _filt_kernel(z_ref, w1_ref, b1_ref, w2_ref, b2_ref, w3_ref, b3_ref, fr_ref, wo_ref, dl_ref,
                 hfb_ref, asum_ref, *, L):
    i = pl.program_id(0)
    tl = z_ref.shape[0]
    C = dl_ref.shape[1]
    hp = lax.Precision.HIGHEST
    h = jnp.sin(fr_ref[0:1, :] * (jnp.dot(z_ref[...], w1_ref[...], precision=hp, preferred_element_type=F32) + b1_ref[...]))
    h = jnp.sin(fr_ref[1:2, :] * (jnp.dot(h, w2_ref[...], precision=hp, preferred_element_type=F32) + b2_ref[...]))
    h = jnp.sin(fr_ref[2:3, :] * (jnp.dot(h, w3_ref[...], precision=hp, preferred_element_type=F32) + b3_ref[...]))
    o = jnp.dot(h, wo_ref[...], precision=hp, preferred_element_type=F32)
    grow = lax.broadcasted_iota(jnp.int32, (tl, C), 0) + i * tl
    t = grow.astype(F32) * (1.0 / (L - 1))
    decay = jnp.exp(-t * dl_ref[...])
    hf = o[:, :C] * decay
    hb = jnp.where(grow == 0, 0.0, o[:, C:] * decay)
    tile = hfb_ref.shape[:-1] + (C,)
    hfb_ref[:, :, :, :C] = hf.reshape(tile)
    hfb_ref[:, :, :, C:] = hb.reshape(tile)

    @pl.when(i == 0)
    def _():
        asum_ref[...] = jnp.zeros_like(asum_ref)

    asum_ref[...] += jnp.sum(jnp.abs(hf) + jnp.abs(hb), axis=0, keepdims=True)


def _filter_taps(L, w1, b1, w2, b2, w3, b3, freq, wout, tl=512):
    C = wout.shape[1] // 2
    order = w1.shape[1]
    emb = w1.shape[0]
    t = jnp.linspace(0.0, 1.0, L, dtype=F32)[:, None]
    w = 2.0 * math.pi * jnp.arange(L, dtype=F32)[:, None] / L
    f = jnp.linspace(1e-4, FILTER_BANDS - 1, FILTER_BANDS, dtype=F32)[None, :]
    z = jnp.concatenate([t, jnp.cos(f * w), -jnp.sin(f * w)], axis=-1)
    zp = jnp.pad(z, ((0, 0), (0, V7X_LANES - emb)))
    w1p = jnp.pad(w1, ((0, V7X_LANES - emb), (0, 0)))
    deltas = jnp.abs(jnp.linspace(math.log(DECAY_TARGET) / SLOW_DECAY, math.log(DECAY_TARGET) / FAST_DECAY, C, dtype=F32))[None]
    fixed = lambda i: (0, 0)
    return pl.pallas_call(
        functools.partial(_filt_kernel, L=L),
        grid=(L // tl,),
        in_specs=[
            pl.BlockSpec((tl, V7X_LANES), lambda i: (i, 0)),
            pl.BlockSpec((V7X_LANES, order), fixed), pl.BlockSpec((1, order), fixed),
            pl.BlockSpec((order, order), fixed), pl.BlockSpec((1, order), fixed),
            pl.BlockSpec((order, order), fixed), pl.BlockSpec((1, order), fixed),
            pl.BlockSpec((3, order), fixed),
            pl.BlockSpec((order, 2 * C), fixed),
            pl.BlockSpec((1, C), fixed),
        ],
        out_specs=[pl.BlockSpec((None, tl // DFT_N2, DFT_N2H, V7X_SUBLANES, 2 * C), lambda i: (0, i, 0, 0, 0)),
                   pl.BlockSpec((1, C), fixed)],
        out_shape=[jax.ShapeDtypeStruct((1, L // DFT_N2, DFT_N2H, V7X_SUBLANES, 2 * C), F32),
                   jax.ShapeDtypeStruct((1, C), F32)],
        compiler_params=_cparams(("arbitrary",), 32),
        name="filt",
    )(zp, w1p, b1[None], w2, b2[None], w3, b3[None], freq, wout, deltas)


def _dft_constants(L):
    n1 = 2 * L // DFT_N2
    n1h = n1 // 2
    N = 2 * L
    sub = V7X_SUBLANES

    def cs(num, den):
        ang = (num % den).astype(np.float64) * (2.0 * np.pi / den)
        return np.cos(ang), np.sin(ang)

    k1 = np.arange(n1, dtype=np.int64)
    a = np.arange(n1h, dtype=np.int64)
    c, s = cs(k1[:, None] * a[None, :], n1)
    eye = np.eye(sub)

    def expand(m):
        r, kk = m.shape
        return (m[:, None, :, None] * eye[None, :, None, :]).reshape(r * sub, kk * sub)

    def const(m):
        return jnp.asarray(m.astype(np.float32).astype(BF16))

    m1c = np.stack([np.concatenate([c, s], axis=1), np.concatenate([-s, c], axis=1)], axis=1).reshape(2 * n1, 2 * n1h)
    m1r = np.stack([c, -s], axis=1).reshape(2 * n1, n1h)
    ct, st = c.T, s.T
    m3 = np.stack([np.stack([ct, -st], axis=2).reshape(n1h, 2 * n1),
                   np.stack([st, ct], axis=2).reshape(n1h, 2 * n1)], axis=0).reshape(2 * n1h, 2 * n1)
    n2 = np.arange(DFT_N2, dtype=np.int64)
    gc, gs = cs(n2[:, None] * n2[None, :], DFT_N2)
    g_fwd = np.concatenate([np.concatenate([gc, gs], axis=1), np.concatenate([-gs, gc], axis=1)], axis=0)
    g_inv = np.concatenate([np.concatenate([gc, -gs], axis=1), np.concatenate([gs, gc], axis=1)], axis=0)
    ph = (jnp.arange(n1, dtype=jnp.int32)[:, None] * jnp.arange(DFT_N2, dtype=jnp.int32)[None, :]) % N
    ang = ph.astype(F32) * (2.0 * math.pi / N)
    lanes = (n1, DFT_N2H, sub, V7X_LANES)
    twc = jnp.broadcast_to(jnp.cos(ang).reshape(n1, DFT_N2H, sub, 1), lanes)
    tws = jnp.broadcast_to(jnp.sin(ang).reshape(n1, DFT_N2H, sub, 1), lanes)
    return dict(n1=n1, n1h=n1h, m1c=const(expand(m1c)), m1r=const(expand(m1r)), m3=const(expand(m3)),
                g_fwd=const(g_fwd), g_inv=const(g_inv), twc=twc, tws=tws)


def _dft1_kernel(x_ref, m_ref, o_ref):
    rows = m_ref.shape[1]
    xs = x_ref[...].reshape(rows, x_ref.shape[-1]).astype(BF16)
    o_ref[...] = jnp.dot(m_ref[...], xs, preferred_element_type=F32).reshape(o_ref.shape)


def _dft1(x5, m, n1, cb):
    P, n1h, _, sub, Cx = x5.shape
    return pl.pallas_call(
        _dft1_kernel,
        grid=(DFT_N2H, Cx // cb),
        in_specs=[
            pl.BlockSpec((P, n1h, None, sub, cb), lambda h, c: (0, 0, h, 0, c)),
            pl.BlockSpec(m.shape, lambda h, c: (0, 0)),
        ],
        out_specs=pl.BlockSpec((None, n1, 2, sub, cb), lambda h, c: (h, 0, 0, 0, c)),
        out_shape=jax.ShapeDtypeStruct((DFT_N2H, n1, 2, sub, Cx), F32),
        compiler_params=_cparams(("parallel", "parallel"), 48),
        name="dft1",
    )(x5, m)


def _dft2_kernel(a_ref, f_ref, twc_ref, tws_ref, gf_ref, gi_ref, o_ref):
    C = a_ref.shape[-1]
    half = DFT_N2

    def lanes(fn):
        return jnp.concatenate([fn(slice(c0, c0 + V7X_LANES)) for c0 in range(0, C, V7X_LANES)], axis=-1)

    for kk in range(a_ref.shape[1]):
        twc, tws = twc_ref[kk], tws_ref[kk]

        def spectrum(ref, col0):
            def re_part(sl):
                return ref[:, kk, 0, :, col0 + sl.start:col0 + sl.stop] * twc + ref[:, kk, 1, :, col0 + sl.start:col0 + sl.stop] * tws

            def im_part(sl):
                return ref[:, kk, 1, :, col0 + sl.start:col0 + sl.stop] * twc - ref[:, kk, 0, :, col0 + sl.start:col0 + sl.stop] * tws

            t = jnp.concatenate([lanes(re_part).reshape(half, C), lanes(im_part).reshape(half, C)], axis=0).astype(BF16)
            s = jnp.dot(gf_ref[...], t, preferred_element_type=F32)
            return s[:half], s[half:]

        xr, xi = spectrum(a_ref, 0)
        fr, fi = spectrum(f_ref, 0)
        br, bi = spectrum(f_ref, C)
        hr, hi = fr + br, fi - bi
        y = jnp.concatenate([xr * hr - xi * hi, xr * hi + xi * hr], axis=0).astype(BF16)
        b = jnp.dot(gi_ref[...], y, preferred_element_type=F32)
        br2 = b[:half].reshape(DFT_N2H, V7X_SUBLANES, C)
        bi2 = b[half:].reshape(DFT_N2H, V7X_SUBLANES, C)
        for c0 in range(0, C, V7X_LANES):
            sl = slice(c0, c0 + V7X_LANES)
            o_ref[:, kk, 0, :, sl] = br2[:, :, sl] * twc - bi2[:, :, sl] * tws
            o_ref[:, kk, 1, :, sl] = bi2[:, :, sl] * twc + br2[:, :, sl] * tws


def _dft2(a5, f5, k, kb=4):
    n2h, n1, _, sub, C = a5.shape
    kb = min(kb, n1)
    return pl.pallas_call(
        _dft2_kernel,
        grid=(n1 // kb,),
        in_specs=[
            pl.BlockSpec((n2h, kb, 2, sub, C), lambda i: (0, i, 0, 0, 0)),
            pl.BlockSpec((n2h, kb, 2, sub, 2 * C), lambda i: (0, i, 0, 0, 0)),
            pl.BlockSpec((kb, n2h, sub, V7X_LANES), lambda i: (i, 0, 0, 0)),
            pl.BlockSpec((kb, n2h, sub, V7X_LANES), lambda i: (i, 0, 0, 0)),
            pl.BlockSpec((2 * DFT_N2, 2 * DFT_N2), lambda i: (0, 0)),
            pl.BlockSpec((2 * DFT_N2, 2 * DFT_N2), lambda i: (0, 0)),
        ],
        out_specs=pl.BlockSpec((n2h, kb, 2, sub, C), lambda i: (0, i, 0, 0, 0)),
        out_shape=jax.ShapeDtypeStruct(a5.shape, F32),
        compiler_params=_cparams(("parallel",), 48),
        name="dft2",
    )(a5, f5, k["twc"], k["tws"], k["g_fwd"], k["g_inv"])


def _dft3_kernel(b_ref, m_ref, x0_ref, vx_ref, sc_ref, d_ref, o_ref):
    C = b_ref.shape[-1]
    bs = b_ref[...].reshape(m_ref.shape[1], C).astype(BF16)
    y = jnp.dot(m_ref[...], bs, preferred_element_type=F32).reshape(o_ref.shape)
    o_ref[...] = x0_ref[...] * (y * sc_ref[...] + vx_ref[...] * d_ref[...])


def _dft3(b5, m3, x05, vx5, scale, d):
    n2h, n1, _, sub, C = b5.shape
    Bt, n1h = x05.shape[0], x05.shape[1]
    tok = pl.BlockSpec((Bt, n1h, None, sub, C), lambda h: (0, 0, h, 0, 0))
    vec = pl.BlockSpec((1, C), lambda h: (0, 0))
    return pl.pallas_call(
        _dft3_kernel,
        grid=(n2h,),
        in_specs=[
            pl.BlockSpec((None, n1, 2, sub, C), lambda h: (h, 0, 0, 0, 0)),
            pl.BlockSpec(m3.shape, lambda h: (0, 0)),
            tok, tok, vec, vec,
        ],
        out_specs=tok,
        out_shape=jax.ShapeDtypeStruct(x05.shape, F32),
        compiler_params=_cparams(("parallel",), 48),
        name="dft3",
    )(b5, m3, x05, vx5, scale, d)


def _hyena(u, conv_w, conv_b, w1, b1, w2, b2, w3, b3, freq, wout, d_skip, B, L):
    T = u.shape[0]
    C = HYENA_WIDTH
    k = _dft_constants(L)
    n1, n1h = k["n1"], k["n1h"]
    x0, vx = _hpre(u, conv_w, conv_b[None], B, L)
    hfb, asum = _filter_taps(L, w1, b1, w2, b2, w3, b3, freq, wout)
    a5 = _dft1(vx, k["m1c"], n1, cb=C)
    f5 = _dft1(hfb, k["m1r"], n1, cb=C)
    b5 = _dft2(a5, f5, k)
    scale = 1.0 / (asum * (2 * L))
    return _dft3(b5, k["m3"], x0, vx, scale, d_skip[None])


PACKED = jnp.uint32


def _pack_pairs(x):
    half = x.shape[1] // 2
    bits = pltpu.bitcast(x.astype(BF16).astype(F32), PACKED)
    return (bits[:, :half] >> 16) | bits[:, half:]


def _unpack_pairs(w):
    lo = pltpu.bitcast(w << 16, F32)
    hi = pltpu.bitcast(w & jnp.uint32(0xFFFF0000), F32)
    return lo, hi


def _oproj_kernel(x_ref, g0_ref, b0_ref, at_ref, hy_ref, wo_ref, g1_ref, b1_ref, wrh_ref, wrl_ref,
                  h1_ref, h1p_ref, sc_ref):
    aw = at_ref.shape[1]
    h0 = _layer_norm_rows(x_ref[...], g0_ref[...], b0_ref[...])
    mixed = jnp.dot(at_ref[...], wo_ref[0:aw, :], preferred_element_type=F32)
    hy = hy_ref[...].reshape(x_ref.shape[0], hy_ref.shape[-1])
    mixed = mixed + jnp.dot(hy.astype(BF16), wo_ref[aw:, :], preferred_element_type=F32)
    h1 = _layer_norm_rows(ALPHA * h0 + mixed, g1_ref[...], b1_ref[...])
    h1_ref[...] = h1
    h1p_ref[...] = _pack_pairs(h1)
    hh = h1.astype(BF16)
    hl = (h1 - hh.astype(F32)).astype(BF16)
    dn = (((1,), (1,)), ((), ()))
    logits = lax.dot_general(wrh_ref[...], hh, dn, preferred_element_type=F32)
    logits = logits + (lax.dot_general(wrh_ref[...], hl, dn, preferred_element_type=F32)
                       + lax.dot_general(wrl_ref[...], hh, dn, preferred_element_type=F32))
    sc_ref[...] = 1.0 / (1.0 + jnp.exp(-logits))


def _oproj(x2, g0, b0, attn, hy, wo_bf, g1, b1, wr_t, tm=256):
    T, D = x2.shape
    E = wr_t.shape[0]
    wrh = wr_t.astype(BF16)
    wrl = (wr_t - wrh.astype(F32)).astype(BF16)
    row = lambda i: (i, 0)
    fixed = lambda i: (0, 0)
    hy4 = hy.reshape((-1,) + hy.shape[2:])
    return pl.pallas_call(
        _oproj_kernel,
        grid=(T // tm,),
        in_specs=[
            pl.BlockSpec((tm, D), row), pl.BlockSpec((1, D), fixed), pl.BlockSpec((1, D), fixed),
            pl.BlockSpec((tm, attn.shape[1]), row),
            pl.BlockSpec((tm // DFT_N2,) + hy4.shape[1:], lambda i: (i, 0, 0, 0)),
            pl.BlockSpec((D, D), fixed), pl.BlockSpec((1, D), fixed), pl.BlockSpec((1, D), fixed),
            pl.BlockSpec((E, D), fixed), pl.BlockSpec((E, D), fixed),
        ],
        out_specs=[pl.BlockSpec((tm, D), row), pl.BlockSpec((tm, D // 2), row), pl.BlockSpec((E, tm), lambda i: (0, i))],
        out_shape=[
            jax.ShapeDtypeStruct((T, D), F32),
            jax.ShapeDtypeStruct((T, D // 2), PACKED),
            jax.ShapeDtypeStruct((E, T), F32),
        ],
        compiler_params=_cparams(("parallel",), 48),
        name="oproj",
    )(x2, g0, b0, attn, hy4, wo_bf, g1, b1, wrh, wrl)


def _route_kernel(sc_ref, bias_ref, tri_ref, idx_ref, gate_ref, rank_ref, cnt_ref, carry_ref):
    step = pl.program_id(0)
    E, tm = sc_ref.shape
    neg = jnp.float32(-jnp.inf)
    scores = sc_ref[...]
    biased = scores + bias_ref[...]
    erow = lax.broadcasted_iota(jnp.int32, (E, tm), 0)
    big = jnp.int32(E)

    def first_argmax(vals, rows):
        m = jnp.max(vals, axis=0, keepdims=True)
        pick = jnp.min(jnp.where(vals == m, rows, big), axis=0, keepdims=True)
        return m, pick

    gsc = []
    for g in range(N_GROUPS):
        blk = biased[g * GROUP_SIZE:(g + 1) * GROUP_SIZE, :]
        rows = erow[g * GROUP_SIZE:(g + 1) * GROUP_SIZE, :]
        m1, p1 = first_argmax(blk, rows)
        m2 = jnp.max(jnp.where(rows == p1, neg, blk), axis=0, keepdims=True)
        gsc.append(m1 + m2)
    gsc = jnp.concatenate(gsc, axis=0)
    grow = lax.broadcasted_iota(jnp.int32, (N_GROUPS, tm), 0)
    gsel = jnp.zeros((N_GROUPS, tm), jnp.bool_)
    work = gsc
    for _ in range(TOPK_GROUPS):
        _, p = first_argmax(work, grow)
        hit = grow == p
        gsel = gsel | hit
        work = jnp.where(hit, neg, work)
    emask = jnp.concatenate(
        [jnp.broadcast_to(gsel[g:g + 1, :], (GROUP_SIZE, tm)) for g in range(N_GROUPS)], axis=0)
    work = jnp.where(emask, biased, neg)

    sel = jnp.zeros((E, tm), jnp.bool_)
    picks, gvals = [], []
    for _ in range(TOP_K):
        _, p = first_argmax(work, erow)
        hit = erow == p
        sel = sel | hit
        picks.append(p)
        gvals.append(jnp.sum(jnp.where(hit, scores, 0.0), axis=0, keepdims=True))
        work = jnp.where(hit, neg, work)
    gv = jnp.concatenate(gvals, axis=0)
    idx_ref[...] = jnp.concatenate(picks, axis=0)
    gate_ref[...] = gv / jnp.sum(gv, axis=0, keepdims=True) * ROUTED_SCALE

    @pl.when(step == 0)
    def _():
        carry_ref[...] = jnp.zeros_like(carry_ref)

    chosen = sel.astype(F32)
    before = jnp.dot(chosen.astype(BF16), tri_ref[...], preferred_element_type=F32) + carry_ref[...]
    rank_ref[...] = jnp.concatenate(
        [jnp.sum(jnp.where(erow == p, before, 0.0), axis=0, keepdims=True) for p in picks], axis=0).astype(jnp.int32)
    carry_ref[...] += jnp.sum(chosen, axis=1, keepdims=True)
    cnt_ref[...] = carry_ref[...].astype(jnp.int32)


def _route(scores_t, bias, tm=128):
    E, T = scores_t.shape
    tri = (jnp.arange(tm)[:, None] < jnp.arange(tm)[None, :]).astype(BF16)
    tok = lambda i: (0, i)
    fixed = lambda i: (0, 0)
    return pl.pallas_call(
        _route_kernel,
        grid=(T // tm,),
        in_specs=[pl.BlockSpec((E, tm), tok), pl.BlockSpec((E, 1), fixed), pl.BlockSpec((tm, tm), fixed)],
        out_specs=[pl.BlockSpec((TOP_K, tm), tok), pl.BlockSpec((TOP_K, tm), tok), pl.BlockSpec((TOP_K, tm), tok),
                   pl.BlockSpec((E, 1), fixed)],
        out_shape=[
            jax.ShapeDtypeStruct((TOP_K, T), jnp.int32),
            jax.ShapeDtypeStruct((TOP_K, T), F32),
            jax.ShapeDtypeStruct((TOP_K, T), jnp.int32),
            jax.ShapeDtypeStruct((E, 1), jnp.int32),
        ],
        scratch_shapes=[pltpu.VMEM((E, 1), F32)],
        compiler_params=_cparams(("arbitrary",), 32),
        name="route",
    )(scores_t, bias, tri)


def _dest_kernel(idx_ref, rank_ref, ps_ref, dest_ref):
    E = ps_ref.shape[0]
    tm = idx_ref.shape[1]
    erow = lax.broadcasted_iota(jnp.int32, (E, tm), 0)
    ps = ps_ref[...].astype(F32)
    rows = [jnp.sum(jnp.where(erow == idx_ref[k:k + 1, :], ps, 0.0), axis=0, keepdims=True) for k in range(TOP_K)]
    dest_ref[...] = jnp.concatenate(rows, axis=0).astype(jnp.int32) + rank_ref[...]


def _dest_rows(idx, rank, pad_start, tm=512):
    K, T = idx.shape
    E = pad_start.shape[0]
    tok = lambda i: (0, i)
    return pl.pallas_call(
        _dest_kernel,
        grid=(T // tm,),
        in_specs=[pl.BlockSpec((K, tm), tok), pl.BlockSpec((K, tm), tok), pl.BlockSpec((E, 1), lambda i: (0, 0))],
        out_specs=pl.BlockSpec((K, tm), tok),
        out_shape=jax.ShapeDtypeStruct((K, T), jnp.int32),
        compiler_params=_cparams(("arbitrary",), 32),
        name="dest",
    )(idx, rank, pad_start[:, None])


def _scatter_kernel(ps_ref, pe_ref, dest_ref, h_ref, xs_ref, zbuf, sem, *, rb):
    sub = V7X_SUBLANES
    tm = h_ref.shape[0]

    @pl.when(pl.program_id(0) == 0)
    def _():
        zbuf[...] = jnp.zeros_like(zbuf)

        def zcopy(e):
            return pltpu.make_async_copy(zbuf, xs_ref.at[pl.ds(pl.multiple_of(pe_ref[e] - rb, rb), rb)], sem)

        def zstart(e, c):
            @pl.when(pe_ref[e] > ps_ref[e])
            def _():
                zcopy(e).start()
            return c

        def zwait(e, c):
            @pl.when(pe_ref[e] > ps_ref[e])
            def _():
                zcopy(e).wait()
            return c

        lax.fori_loop(0, ps_ref.shape[0], zstart, 0)
        lax.fori_loop(0, ps_ref.shape[0], zwait, 0)

    def issue(r8, c):
        for j in range(sub):
            for k in range(TOP_K):
                dst = dest_ref[0, r8 * (sub * TOP_K) + (j * TOP_K + k)]
                pltpu.make_async_copy(h_ref.at[pl.ds(r8 * sub + j, 1)], xs_ref.at[pl.ds(dst, 1)], sem).start(priority=k % 2)
        return c

    lax.fori_loop(0, tm // sub, issue, 0)
    for k in range(TOP_K):
        pltpu.make_async_copy(h_ref, xs_ref.at[pl.ds(0, tm)], sem).wait()


def _dest_tiles(dest, tm):
    K, T = dest.shape
    return dest.T.reshape(T // tm, 1, tm * K)


def _scatter_rows(pad_start, pad_end, dest_t, h1p, n_rows, rb, tm=256):
    T, W = h1p.shape
    return pl.pallas_call(
        functools.partial(_scatter_kernel, rb=rb),
        grid_spec=pltpu.PrefetchScalarGridSpec(
            num_scalar_prefetch=2,
            grid=(T // tm,),
            in_specs=[
                pl.BlockSpec((None, 1, tm * TOP_K), lambda i, ps, pe: (i, 0, 0), memory_space=pltpu.SMEM),
                pl.BlockSpec((tm, W), lambda i, ps, pe: (i, 0)),
            ],
            out_specs=pl.BlockSpec(memory_space=pl.ANY),
            scratch_shapes=[pltpu.VMEM((rb, W), h1p.dtype), pltpu.SemaphoreType.DMA],
        ),
        out_shape=jax.ShapeDtypeStruct((n_rows, W), h1p.dtype),
        compiler_params=_cparams(("arbitrary",), 32),
        name="scatter",
    )(pad_start, pad_end, dest_t, h1p)


EXPERT_X_SLOTS = 4
EXPERT_Y_SLOTS = 3


def _experts_kernel(first_ref, nblk_ref, wsel_ref, tot_ref, xs_ref, wg_ref, wu_ref, wd_ref, ys_ref,
                    xbuf, ybuf, wgub, wdb, xsem, ysem, *, rb):
    del wsel_ref
    e = pl.program_id(0)
    F = wg_ref.shape[1]
    nx, ny = xbuf.shape[0], ybuf.shape[0]
    total = tot_ref[0]
    n = nblk_ref[e]
    g0 = first_ref[e]

    def x_copy(g):
        slot = g % nx
        return pltpu.make_async_copy(xs_ref.at[pl.ds(pl.multiple_of(g * rb, rb), rb)], xbuf.at[slot], xsem.at[slot])

    def y_copy(g):
        slot = g % ny
        return pltpu.make_async_copy(ybuf.at[slot], ys_ref.at[pl.ds(pl.multiple_of(g * rb, rb), rb)], ysem.at[slot])

    @pl.when(e == 0)
    def _():
        for g in range(nx):
            @pl.when(g < total)
            def _():
                x_copy(g).start()

    def process(g, count):
        for d in range(count):
            x_copy(g + d).wait()
        ys = []
        for d in range(count):
            lo, hi = _unpack_pairs(xbuf[(g + d) % nx])
            x = jnp.concatenate([lo, hi], axis=1).astype(BF16)
            gu = jnp.dot(x, wgub[...], preferred_element_type=F32)
            gate, up = gu[:, 0:F], gu[:, F:]
            hb = (gate / (1.0 + jnp.exp(-gate)) * up).astype(BF16)
            ys.append(_pack_pairs(jnp.dot(hb, wdb[...], preferred_element_type=F32)))
        for d in range(count):
            @pl.when(g + d >= ny)
            def _():
                y_copy(g + d - ny).wait()

            ybuf[(g + d) % ny] = ys[d]
            y_copy(g + d).start()
        for d in range(count):
            @pl.when(g + d + nx < total)
            def _():
                x_copy(g + d + nx).start()

    @pl.when(n > 0)
    def _():
        wgub[:, 0:F] = wg_ref[...].astype(BF16)
        wgub[:, F:] = wu_ref[...].astype(BF16)
        wdb[...] = wd_ref[...].astype(BF16)

        def pair(j, c):
            process(g0 + 2 * j, 2)
            return c

        lax.fori_loop(0, n // 2, pair, 0)

        @pl.when(n % 2 == 1)
        def _():
            process(g0 + n - 1, 1)

    @pl.when(e == pl.num_programs(0) - 1)
    def _():
        for back in range(ny, 0, -1):
            @pl.when(total >= back)
            def _():
                y_copy(total - back).wait()


def _experts(first_blk, nblk, wsel, total_blk, xs, w_gate, w_up, w_down, rb):
    P, W = xs.shape
    E, D, F = w_gate.shape
    wspec = lambda shape: pl.BlockSpec((None,) + shape, lambda e, fb, nbk, ws, tt: (ws[e], 0, 0))
    return pl.pallas_call(
        functools.partial(_experts_kernel, rb=rb),
        grid_spec=pltpu.PrefetchScalarGridSpec(
            num_scalar_prefetch=4,
            grid=(E,),
            in_specs=[pl.BlockSpec(memory_space=pl.ANY), wspec((D, F)), wspec((D, F)), wspec((F, D))],
            out_specs=pl.BlockSpec(memory_space=pl.ANY),
            scratch_shapes=[
                pltpu.VMEM((EXPERT_X_SLOTS, rb, W), PACKED), pltpu.VMEM((EXPERT_Y_SLOTS, rb, W), PACKED),
                pltpu.VMEM((D, 2 * F), BF16), pltpu.VMEM((F, D), BF16),
                pltpu.SemaphoreType.DMA((EXPERT_X_SLOTS,)), pltpu.SemaphoreType.DMA((EXPERT_Y_SLOTS,)),
            ],
        ),
        out_shape=jax.ShapeDtypeStruct((P, W), PACKED),
        compiler_params=_cparams(("arbitrary",), 48),
        name="experts",
    )(first_blk, nblk, wsel, total_blk, xs, w_gate, w_up, w_down)


def _combine_kernel(dest_ref, destn_ref, h1_ref, gate_ref, ys_ref, ys8_ref, sg_ref, su_ref, sd_ref,
                    g2_ref, b2_ref, o_ref, buf, sem):
    i = pl.program_id(0)
    n = pl.num_programs(0)
    tm = h1_ref.shape[0]
    slot = i % 2

    sub = V7X_SUBLANES

    def issue(dref, s):
        def body(r8, c):
            for j in range(sub):
                for k in range(TOP_K):
                    src = ys_ref.at[pl.ds(dref[0, r8 * (sub * TOP_K) + (j * TOP_K + k)], 1)]
                    pltpu.make_async_copy(src, buf.at[s, r8 * sub + j, pl.ds(k, 1)], sem.at[s]).start(priority=k % 2)
            return c
        lax.fori_loop(0, tm // sub, body, 0)

    @pl.when(i == 0)
    def _():
        issue(dest_ref, 0)

    @pl.when(i + 1 < n)
    def _():
        issue(destn_ref, 1 - slot)

    h1 = h1_ref[...]
    hb = h1.astype(BF16)
    g = jnp.dot(hb, sg_ref[...], preferred_element_type=F32)
    u = jnp.dot(hb, su_ref[...], preferred_element_type=F32)
    ffn = jnp.dot((g / (1.0 + jnp.exp(-g)) * u).astype(BF16), sd_ref[...], preferred_element_type=F32)

    pltpu.make_async_copy(ys8_ref.at[pl.ds(0, tm)], buf.at[slot], sem.at[slot]).wait()
    nk = tm * TOP_K
    lo, hi = _unpack_pairs(buf[slot].reshape(nk, buf.shape[3]))
    y = jnp.concatenate([lo, hi], axis=1).astype(BF16)
    gates = gate_ref[...]
    gh = gates.astype(BF16)
    gl = (gates - gh.astype(F32)).astype(BF16)
    pick = lax.broadcasted_iota(jnp.int32, (TOP_K, nk), 1) % TOP_K == lax.broadcasted_iota(jnp.int32, (TOP_K, nk), 0)
    spread = pick.astype(BF16)
    own = lax.broadcasted_iota(jnp.int32, (tm, nk), 1) // TOP_K == lax.broadcasted_iota(jnp.int32, (tm, nk), 0)
    routed = jnp.zeros((tm, y.shape[1]), F32)
    for part in (gh, gl):
        gm = jnp.where(own, jnp.dot(part, spread, preferred_element_type=F32), 0.0).astype(BF16)
        routed = routed + jnp.dot(gm, y, preferred_element_type=F32)
    o_ref[...] = _layer_norm_rows(ALPHA * h1 + ffn + routed, g2_ref[...], b2_ref[...])


def _combine(dest, h1, gates_tk, ys, sg, su, sd, g2, b2, tm=128):
    T, D = h1.shape
    W = ys.shape[1]
    F = sg.shape[1]
    nt = T // tm
    row = lambda i: (i, 0)
    fixed = lambda i: (0, 0)
    smem = functools.partial(pl.BlockSpec, (None, 1, tm * TOP_K), memory_space=pltpu.SMEM)
    dest_t = _dest_tiles(dest, tm)
    ys8 = ys.reshape(ys.shape[0] // TOP_K, TOP_K, W)
    return pl.pallas_call(
        _combine_kernel,
        grid=(nt,),
        in_specs=[
            smem(index_map=lambda i: (i, 0, 0)),
            smem(index_map=lambda i: (jnp.minimum(i + 1, nt - 1), 0, 0)),
            pl.BlockSpec((tm, D), row),
            pl.BlockSpec((tm, TOP_K), row),
            pl.BlockSpec(memory_space=pl.ANY),
            pl.BlockSpec(memory_space=pl.ANY),
            pl.BlockSpec((D, F), fixed), pl.BlockSpec((D, F), fixed), pl.BlockSpec((F, D), fixed),
            pl.BlockSpec((1, D), fixed), pl.BlockSpec((1, D), fixed),
        ],
        out_specs=pl.BlockSpec((tm, D), row),
        out_shape=jax.ShapeDtypeStruct((T, D), F32),
        scratch_shapes=[pltpu.VMEM((2, tm, TOP_K, W), PACKED), pltpu.SemaphoreType.DMA((2,))],
        compiler_params=_cparams(("arbitrary",), 48),
        name="combine",
    )(dest_t, dest_t, h1, gates_tk, ys, ys8, sg, su, sd, g2, b2)


MOE_ROWS = 256


def _moe(h1, h1p, scores_t, router_bias, w_gate, w_up, w_down, ws_gate, ws_up, ws_down, g2, b2):
    T = h1.shape[0]
    E = N_EXPERTS
    rb = MOE_ROWS
    idx, gates, rank, counts = _route(scores_t, router_bias[:, None])
    counts = counts[:, 0]
    padded = (counts + rb - 1) // rb * rb
    pad_end = jnp.cumsum(padded)
    pad_start = pad_end - padded
    nb = (T * TOP_K) // rb + E
    nblk = (padded // rb).astype(jnp.int32)
    first_blk = (pad_start // rb).astype(jnp.int32)
    total_blk = (pad_end[-1:] // rb).astype(jnp.int32)
    wsel = lax.cummax(jnp.where(nblk > 0, jnp.arange(E, dtype=jnp.int32), 0))
    dest = _dest_rows(idx, rank, pad_start)
    xs = _scatter_rows(pad_start, pad_end, _dest_tiles(dest, 256), h1p, nb * rb, rb, tm=256)
    ys = _experts(first_blk, nblk, wsel, total_blk, xs, w_gate, w_up, w_down, rb)
    return _combine(dest, h1, gates.T, ys, ws_gate.astype(BF16), ws_up.astype(BF16), ws_down.astype(BF16), g2, b2)


def kernel(x, positions, emb_ln_g, emb_ln_b, w_in, hy_conv_w, hy_conv_b, hy_f_w1, hy_f_b1, hy_f_w2, hy_f_b2, hy_f_w3, hy_f_b3, hy_f_freq, hy_f_wout, hy_d, lambda_q1, lambda_k1, lambda_q2, lambda_k2, subln_g, w_o, ln1_g, ln1_b, w_router, router_bias, w_gate, w_up, w_down, ws_gate, ws_up, ws_down, ln2_g, ln2_b):
    B, L, D = x.shape
    T = B * L
    assert w_in.shape[0] == DEPTH == 1
    i = 0
    x2 = x.reshape(T, D)
    g0, b0 = emb_ln_g[None], emb_ln_b[None]
    ra, rm, rp = _rotary_tables(positions)
    q, k, v, u = _inproj(x2, g0, b0, w_in[i].astype(BF16), ra, rm, rp)
    lam = (jnp.exp(jnp.sum(lambda_q1[i] * lambda_k1[i])) - jnp.exp(jnp.sum(lambda_q2[i] * lambda_k2[i])) + LAM_INIT)
    attn = _attention(q, k, v, lam.reshape(1).astype(F32), subln_g[i][None], B, L)
    hy = _hyena(u, hy_conv_w[i], hy_conv_b[i], hy_f_w1[i], hy_f_b1[i], hy_f_w2[i], hy_f_b2[i], hy_f_w3[i], hy_f_b3[i],
                hy_f_freq[i], hy_f_wout[i], hy_d[i], B, L)
    h1, h1p, scores_t = _oproj(x2, g0, b0, attn, hy, w_o[i].astype(BF16), ln1_g[i][None], ln1_b[i][None], w_router[i].T)
    out = _moe(h1, h1p, scores_t, router_bias[i], w_gate[i], w_up[i], w_down[i], ws_gate[i], ws_up[i], ws_down[i],
               ln2_g[i][None], ln2_b[i][None])
    return out.reshape(B, L, D)
```

```python
import functools
import math

import numpy as np
import jax
import jax.numpy as jnp
from jax import lax
from jax.experimental import pallas as pl
from jax.experimental.pallas import tpu as pltpu

DA_HEADS = 4
DA_HEAD_DIM = 64
DA_V_DIM = 128
ATTN_WIDTH = 512
HYENA_WIDTH = 512
ROT_DIM = 16
ROPE_THETA = 500000.0
SHORT_CONV = 3
FILTER_EMB = 33
FILTER_BANDS = 16
DECAY_TARGET = 1e-2
FAST_DECAY = 0.3
SLOW_DECAY = 1.5
N_EXPERTS = 256
TOP_K = 8
N_GROUPS = 8
GROUP_SIZE = N_EXPERTS // N_GROUPS
TOPK_GROUPS = 4
EXPERT_DIM = 256
ROUTED_SCALE = 2.5
DEPTH = 1
ALPHA = (2 * DEPTH) ** 0.25
LN_EPS = 1e-5
LAM_INIT = 0.8 - 0.6 * math.exp(-0.3 * 0)

V7X_LANES = 128
V7X_SUBLANES = 8
V7X_VMEM_BYTES = 64 * 1024 * 1024

DFT_N2 = 128
DFT_N2H = DFT_N2 // V7X_SUBLANES

BF16 = jnp.bfloat16
F32 = jnp.float32


def _cparams(sem, vmem_mb):
    return pltpu.CompilerParams(dimension_semantics=sem, vmem_limit_bytes=vmem_mb * 1024 * 1024)


def _layer_norm_rows(x, g, b):
    mu = jnp.mean(x, axis=-1, keepdims=True)
    xc = x - mu
    var = jnp.mean(xc * xc, axis=-1, keepdims=True)
    return xc * lax.rsqrt(var + LN_EPS) * g + b


def _inproj_kernel(x_ref, g_ref, b_ref, w_ref, ra_ref, rm_ref, rp_ref, q_ref, k_ref, v_ref, u_ref):
    h = _layer_norm_rows(x_ref[...], g_ref[...], b_ref[...]).astype(BF16)
    ra, rm, rp = ra_ref[...], rm_ref[...], rp_ref[...]

    def rot(t):
        return t * ra + pltpu.roll(t, V7X_LANES - ROT_DIM // 2, axis=1) * rm + pltpu.roll(t, ROT_DIM // 2, axis=1) * rp

    aw = ATTN_WIDTH
    qp = jnp.dot(h, w_ref[:, 0:aw], preferred_element_type=F32)
    kp = jnp.dot(h, w_ref[:, aw:2 * aw], preferred_element_type=F32)
    scale = DA_HEAD_DIM ** -0.5 * math.log2(math.e)
    for c in range(aw // V7X_LANES):
        sl = slice(c * V7X_LANES, (c + 1) * V7X_LANES)
        q_ref[:, sl] = (rot(qp[:, sl]) * scale).astype(BF16)
        k_ref[:, sl] = rot(kp[:, sl]).astype(BF16)
    vp = jnp.dot(h, w_ref[:, 2 * aw:3 * aw], preferred_element_type=F32).astype(BF16)
    ones = jnp.ones((vp.shape[0], DA_V_DIM), BF16)
    for hd in range(DA_HEADS):
        v_ref[:, 2 * hd * DA_V_DIM:(2 * hd + 1) * DA_V_DIM] = vp[:, hd * DA_V_DIM:(hd + 1) * DA_V_DIM]
        v_ref[:, (2 * hd + 1) * DA_V_DIM:(2 * hd + 2) * DA_V_DIM] = ones
    u_ref[...] = jnp.dot(h, w_ref[:, 3 * aw:], preferred_element_type=F32)


def _inproj(x2, g, b, w_bf, ra, rm, rp, tm=256):
    T, D = x2.shape
    ncol = w_bf.shape[1]
    aw = ATTN_WIDTH
    uw = ncol - 3 * aw
    row = lambda i: (i, 0)
    fixed = lambda i: (0, 0)
    return pl.pallas_call(
        _inproj_kernel,
        grid=(T // tm,),
        in_specs=[
            pl.BlockSpec((tm, D), row),
            pl.BlockSpec((1, D), fixed),
            pl.BlockSpec((1, D), fixed),
            pl.BlockSpec((D, ncol), fixed),
            pl.BlockSpec((tm, V7X_LANES), row),
            pl.BlockSpec((tm, V7X_LANES), row),
            pl.BlockSpec((tm, V7X_LANES), row),
        ],
        out_specs=[
            pl.BlockSpec((tm, aw), row),
            pl.BlockSpec((tm, aw), row),
            pl.BlockSpec((tm, 2 * aw), row),
            pl.BlockSpec((tm, uw), row),
        ],
        out_shape=[
            jax.ShapeDtypeStruct((T, aw), BF16),
            jax.ShapeDtypeStruct((T, aw), BF16),
            jax.ShapeDtypeStruct((T, 2 * aw), BF16),
            jax.ShapeDtypeStruct((T, uw), F32),
        ],
        compiler_params=_cparams(("parallel",), 48),
        name="inproj",
    )(x2, g, b, w_bf, ra, rm, rp)


def _rotary_tables(positions):
    half = ROT_DIM // 2
    inv_freq = ROPE_THETA ** (-jnp.arange(0, ROT_DIM, 2, dtype=F32) / ROT_DIM)
    ang = positions.astype(F32).reshape(-1)[:, None] * inv_freq
    cos, sin = jnp.cos(ang), jnp.sin(ang)
    T = ang.shape[0]
    ones = jnp.ones((T, DA_HEAD_DIM - ROT_DIM), F32)
    zeros_h = jnp.zeros((T, half), F32)
    zeros_r = jnp.zeros((T, DA_HEAD_DIM - ROT_DIM), F32)
    a64 = jnp.concatenate([cos, cos, ones], axis=1)
    m64 = jnp.concatenate([-sin, zeros_h, zeros_r], axis=1)
    p64 = jnp.concatenate([zeros_h, sin, zeros_r], axis=1)
    rep = V7X_LANES // DA_HEAD_DIM
    return jnp.tile(a64, (1, rep)), jnp.tile(m64, (1, rep)), jnp.tile(p64, (1, rep))


def _attn_kernel(lam_ref, q_ref, k_ref, v_ref, g_ref, o_ref, s_ref, m_ref, acc_ref, *, kb, unroll):
    qb = q_ref.shape[0]
    L = k_ref.shape[0]
    nchunk = L // kb
    nl = kb // V7X_LANES
    q = q_ref[...]
    lane = lax.broadcasted_iota(jnp.int32, q.shape, 1)
    zero = jnp.zeros_like(q)
    qm = [jnp.where(lane < DA_HEAD_DIM, q, zero), jnp.where(lane >= DA_HEAD_DIM, q, zero)]
    m_ref[...] = jnp.full(m_ref.shape, -jnp.inf, F32)
    acc_ref[...] = jnp.zeros(acc_ref.shape, F32)

    def score_body(j, carry):
        kc = k_ref[pl.ds(pl.multiple_of(j * kb, kb), kb), :]
        for c in range(2):
            s = lax.dot_general(qm[c], kc, (((1,), (1,)), ((), ())), preferred_element_type=F32)
            s_ref[c, j] = s
            m = s[:, 0:V7X_LANES]
            for t in range(1, nl):
                m = jnp.maximum(m, s[:, t * V7X_LANES:(t + 1) * V7X_LANES])
            m_ref[c] = jnp.maximum(m_ref[c], m)
        return carry

    lax.fori_loop(0, nchunk, score_body, 0, unroll=unroll)
    m_row = [jnp.max(m_ref[c], axis=1, keepdims=True) for c in range(2)]

    def pv_body(j, carry):
        vc = v_ref[pl.ds(pl.multiple_of(j * kb, kb), kb), :]
        for c in range(2):
            p = jnp.exp2(s_ref[c, j] - m_row[c])
            acc_ref[c] += jnp.dot(p.astype(BF16), vc, preferred_element_type=F32)
        return carry

    lax.fori_loop(0, nchunk, pv_body, 0, unroll=unroll)
    outs = [acc_ref[c, :, 0:DA_V_DIM] / acc_ref[c, :, DA_V_DIM:2 * DA_V_DIM] for c in range(2)]
    o = outs[0] - lam_ref[0] * outs[1]
    ms = jnp.mean(o * o, axis=1, keepdims=True)
    o_ref[...] = (o * lax.rsqrt(ms + LN_EPS) * g_ref[...] * (1.0 - LAM_INIT)).astype(o_ref.dtype)


def _attention(q, k, v, lam, subln_g, B, L, qb=512, kb=1024, unroll=8):
    T = q.shape[0]
    nq = L // qb
    return pl.pallas_call(
        functools.partial(_attn_kernel, kb=kb, unroll=unroll),
        grid=(B, DA_HEADS, nq),
        in_specs=[
            pl.BlockSpec(memory_space=pltpu.SMEM),
            pl.BlockSpec((qb, DA_V_DIM), lambda b, h, i: (b * nq + i, h)),
            pl.BlockSpec((L, DA_V_DIM), lambda b, h, i: (b, h)),
            pl.BlockSpec((L, 2 * DA_V_DIM), lambda b, h, i: (b, h)),
            pl.BlockSpec((1, DA_V_DIM), lambda b, h, i: (0, 0)),
        ],
        out_specs=pl.BlockSpec((qb, DA_V_DIM), lambda b, h, i: (b * nq + i, h)),
        out_shape=jax.ShapeDtypeStruct((T, ATTN_WIDTH), BF16),
        scratch_shapes=[
            pltpu.VMEM((2, L // kb, qb, kb), F32),
            pltpu.VMEM((2, qb, V7X_LANES), F32),
            pltpu.VMEM((2, qb, 2 * DA_V_DIM), F32),
        ],
        compiler_params=_cparams(("parallel", "parallel", "parallel"), 56),
        name="attn",
    )(lam, q, k, v, subln_g)


def _hpre_kernel(u_ref, up_ref, un_ref, w_ref, b_ref, x0_ref, vx_ref):
    i = pl.program_id(1)
    n = pl.num_programs(1)
    tb = u_ref.shape[0]
    C = x0_ref.shape[-1]
    row = lax.broadcasted_iota(jnp.int32, (tb, V7X_LANES), 0)
    has_prev = (i > 0).astype(F32)
    has_next = (i < n - 1).astype(F32)

    def conv(c0):
        sl = slice(c0, c0 + V7X_LANES)
        u = u_ref[:, sl]
        prev_row = up_ref[V7X_SUBLANES - 1:V7X_SUBLANES, sl] * has_prev
        next_row = un_ref[0:1, sl] * has_next
        u_prev = jnp.where(row == 0, prev_row, pltpu.roll(u, 1, axis=0))
        u_next = jnp.where(row == tb - 1, next_row, pltpu.roll(u, tb - 1, axis=0))
        return u_prev * w_ref[0:1, sl] + u * w_ref[1:2, sl] + u_next * w_ref[2:3, sl] + b_ref[:, sl]

    tile = x0_ref.shape[:-1] + (V7X_LANES,)
    for c in range(C // V7X_LANES):
        c0 = c * V7X_LANES
        x0_ref[:, :, :, c0:c0 + V7X_LANES] = conv(c0).reshape(tile)
        vx_ref[:, :, :, c0:c0 + V7X_LANES] = (conv(2 * C + c0) * conv(C + c0)).reshape(tile)


def _hpre(u, conv_w, conv_b, B, L, tb=512):
    T, C3 = u.shape
    C = C3 // 3
    nt = L // tb
    sub = V7X_SUBLANES
    na = tb // DFT_N2
    split = pl.BlockSpec((None, na, DFT_N2H, sub, C), lambda b, i: (b, i, 0, 0, 0))
    split_shape = jax.ShapeDtypeStruct((B, L // DFT_N2, DFT_N2H, sub, C), F32)
    cur = lambda b, i: (b * nt + i, 0)
    prev = lambda b, i: (jnp.maximum((b * L + i * tb) // sub - 1, 0), 0)
    nxt = lambda b, i: (jnp.minimum((b * L + (i + 1) * tb) // sub, T // sub - 1), 0)
    fixed = lambda b, i: (0, 0)
    return pl.pallas_call(
        _hpre_kernel,
        grid=(B, nt),
        in_specs=[
            pl.BlockSpec((tb, C3), cur),
            pl.BlockSpec((sub, C3), prev),
            pl.BlockSpec((sub, C3), nxt),
            pl.BlockSpec((SHORT_CONV, C3), fixed),
            pl.BlockSpec((1, C3), fixed),
        ],
        out_specs=[split, split],
        out_shape=[split_shape, split_shape],
        compiler_params=_cparams(("parallel", "parallel"), 32),
        name="hpre",
    )(u, u, u, conv_w, conv_b)


def _filt_kernel(z_ref, w1_ref, b1_ref, w2_ref, b2_ref, w3_ref, b3_ref, fr_ref, wo_ref, dl_ref,
                 hfb_ref, asum_ref, *, L):
    i = pl.program_id(0)
    tl = z_ref.shape[0]
    C = dl_ref.shape[1]
    hp = lax.Precision.HIGHEST
    h = jnp.sin(fr_ref[0:1, :] * (jnp.dot(z_ref[...], w1_ref[...], precision=hp, preferred_element_type=F32) + b1_ref[...]))
    h = jnp.sin(fr_ref[1:2, :] * (jnp.dot(h, w2_ref[...], precision=hp, preferred_element_type=F32) + b2_ref[...]))
    h = jnp.sin(fr_ref[2:3, :] * (jnp.dot(h, w3_ref[...], precision=hp, preferred_element_type=F32) + b3_ref[...]))
    o = jnp.dot(h, wo_ref[...], precision=hp, preferred_element_type=F32)
    grow = lax.broadcasted_iota(jnp.int32, (tl, C), 0) + i * tl
    t = grow.astype(F32) * (1.0 / (L - 1))
    decay = jnp.exp(-t * dl_ref[...])
    hf = o[:, :C] * decay
    hb = jnp.where(grow == 0, 0.0, o[:, C:] * decay)
    tile = hfb_ref.shape[:-1] + (C,)
    hfb_ref[:, :, :, :C] = hf.reshape(tile)
    hfb_ref[:, :, :, C:] = hb.reshape(tile)

    @pl.when(i == 0)
    def _():
        asum_ref[...] = jnp.zeros_like(asum_ref)

    asum_ref[...] += jnp.sum(jnp.abs(hf) + jnp.abs(hb), axis=0, keepdims=True)


def _filter_taps(L, w1, b1, w2, b2, w3, b3, freq, wout, tl=512):
    C = wout.shape[1] // 2
    order = w1.shape[1]
    emb = w1.shape[0]
    t = jnp.linspace(0.0, 1.0, L, dtype=F32)[:, None]
    w = 2.0 * math.pi * jnp.arange(L, dtype=F32)[:, None] / L
    f = jnp.linspace(1e-4, FILTER_BANDS - 1, FILTER_BANDS, dtype=F32)[None, :]
    z = jnp.concatenate([t, jnp.cos(f * w), -jnp.sin(f * w)], axis=-1)
    zp = jnp.pad(z, ((0, 0), (0, V7X_LANES - emb)))
    w1p = jnp.pad(w1, ((0, V7X_LANES - emb), (0, 0)))
    deltas = jnp.abs(jnp.linspace(math.log(DECAY_TARGET) / SLOW_DECAY, math.log(DECAY_TARGET) / FAST_DECAY, C, dtype=F32))[None]
    fixed = lambda i: (0, 0)
    return pl.pallas_call(
        functools.partial(_filt_kernel, L=L),
        grid=(L // tl,),
        in_specs=[
            pl.BlockSpec((tl, V7X_LANES), lambda i: (i, 0)),
            pl.BlockSpec((V7X_LANES, order), fixed), pl.BlockSpec((1, order), fixed),
            pl.BlockSpec((order, order), fixed), pl.BlockSpec((1, order), fixed),
            pl.BlockSpec((order, order), fixed), pl.BlockSpec((1, order), fixed),
            pl.BlockSpec((3, order), fixed),
            pl.BlockSpec((order, 2 * C), fixed),
            pl.BlockSpec((1, C), fixed),
        ],
        out_specs=[pl.BlockSpec((None, tl // DFT_N2, DFT_N2H, V7X_SUBLANES, 2 * C), lambda i: (0, i, 0, 0, 0)),
                   pl.BlockSpec((1, C), fixed)],
        out_shape=[jax.ShapeDtypeStruct((1, L // DFT_N2, DFT_N2H, V7X_SUBLANES, 2 * C), F32),
                   jax.ShapeDtypeStruct((1, C), F32)],
        compiler_params=_cparams(("arbitrary",), 32),
        name="filt",
    )(zp, w1p, b1[None], w2, b2[None], w3, b3[None], freq, wout, deltas)


def _dft_constants(L):
    n1 = 2 * L // DFT_N2
    n1h = n1 // 2
    N = 2 * L
    sub = V7X_SUBLANES

    def cs(num, den):
        ang = (num % den).astype(np.float64) * (2.0 * np.pi / den)
        return np.cos(ang), np.sin(ang)

    k1 = np.arange(n1, dtype=np.int64)
    a = np.arange(n1h, dtype=np.int64)
    c, s = cs(k1[:, None] * a[None, :], n1)
    eye = np.eye(sub)

    def expand(m):
        r, kk = m.shape
        return (m[:, None, :, None] * eye[None, :, None, :]).reshape(r * sub, kk * sub)

    def const(m):
        return jnp.asarray(m.astype(np.float32).astype(BF16))

    m1c = np.stack([np.concatenate([c, s], axis=1), np.concatenate([-s, c], axis=1)], axis=1).reshape(2 * n1, 2 * n1h)
    m1r = np.stack([c, -s], axis=1).reshape(2 * n1, n1h)
    ct, st = c.T, s.T
    m3 = np.stack([np.stack([ct, -st], axis=2).reshape(n1h, 2 * n1),
                   np.stack([st, ct], axis=2).reshape(n1h, 2 * n1)], axis=0).reshape(2 * n1h, 2 * n1)
    n2 = np.arange(DFT_N2, dtype=np.int64)
    gc, gs = cs(n2[:, None] * n2[None, :], DFT_N2)
    g_fwd = np.concatenate([np.concatenate([gc, gs], axis=1), np.concatenate([-gs, gc], axis=1)], axis=0)
    g_inv = np.concatenate([np.concatenate([gc, -gs], axis=1), np.concatenate([gs, gc], axis=1)], axis=0)
    ph = (jnp.arange(n1, dtype=jnp.int32)[:, None] * jnp.arange(DFT_N2, dtype=jnp.int32)[None, :]) % N
    ang = ph.astype(F32) * (2.0 * math.pi / N)
    lanes = (n1, DFT_N2H, sub, V7X_LANES)
    twc = jnp.broadcast_to(jnp.cos(ang).reshape(n1, DFT_N2H, sub, 1), lanes)
    tws = jnp.broadcast_to(jnp.sin(ang).reshape(n1, DFT_N2H, sub, 1), lanes)
    return dict(n1=n1, n1h=n1h, m1c=const(expand(m1c)), m1r=const(expand(m1r)), m3=const(expand(m3)),
                g_fwd=const(g_fwd), g_inv=const(g_inv), twc=twc, tws=tws)


def _dft1_kernel(x_ref, m_ref, o_ref):
    rows = m_ref.shape[1]
    xs = x_ref[...].reshape(rows, x_ref.shape[-1]).astype(BF16)
    o_ref[...] = jnp.dot(m_ref[...], xs, preferred_element_type=F32).reshape(o_ref.shape)


def _dft1(x5, m, n1, cb):
    P, n1h, _, sub, Cx = x5.shape
    return pl.pallas_call(
        _dft1_kernel,
        grid=(DFT_N2H, Cx // cb),
        in_specs=[
            pl.BlockSpec((P, n1h, None, sub, cb), lambda h, c: (0, 0, h, 0, c)),
            pl.BlockSpec(m.shape, lambda h, c: (0, 0)),
        ],
        out_specs=pl.BlockSpec((None, n1, 2, sub, cb), lambda h, c: (h, 0, 0, 0, c)),
        out_shape=jax.ShapeDtypeStruct((DFT_N2H, n1, 2, sub, Cx), F32),
        compiler_params=_cparams(("parallel", "parallel"), 48),
        name="dft1",
    )(x5, m)


def _dft2_kernel(a_ref, f_ref, twc_ref, tws_ref, gf_ref, gi_ref, o_ref):
    C = a_ref.shape[-1]
    half = DFT_N2

    def lanes(fn):
        return jnp.concatenate([fn(slice(c0, c0 + V7X_LANES)) for c0 in range(0, C, V7X_LANES)], axis=-1)

    for kk in range(a_ref.shape[1]):
        twc, tws = twc_ref[kk], tws_ref[kk]

        def spectrum(ref, col0):
            def re_part(sl):
                return ref[:, kk, 0, :, col0 + sl.start:col0 + sl.stop] * twc + ref[:, kk, 1, :, col0 + sl.start:col0 + sl.stop] * tws

            def im_part(sl):
                return ref[:, kk, 1, :, col0 + sl.start:col0 + sl.stop] * twc - ref[:, kk, 0, :, col0 + sl.start:col0 + sl.stop] * tws

            t = jnp.concatenate([lanes(re_part).reshape(half, C), lanes(im_part).reshape(half, C)], axis=0).astype(BF16)
            s = jnp.dot(gf_ref[...], t, preferred_element_type=F32)
            return s[:half], s[half:]

        xr, xi = spectrum(a_ref, 0)
        fr, fi = spectrum(f_ref, 0)
        br, bi = spectrum(f_ref, C)
        hr, hi = fr + br, fi - bi
        y = jnp.concatenate([xr * hr - xi * hi, xr * hi + xi * hr], axis=0).astype(BF16)
        b = jnp.dot(gi_ref[...], y, preferred_element_type=F32)
        br2 = b[:half].reshape(DFT_N2H, V7X_SUBLANES, C)
        bi2 = b[half:].reshape(DFT_N2H, V7X_SUBLANES, C)
        for c0 in range(0, C, V7X_LANES):
            sl = slice(c0, c0 + V7X_LANES)
            o_ref[:, kk, 0, :, sl] = br2[:, :, sl] * twc - bi2[:, :, sl] * tws
            o_ref[:, kk, 1, :, sl] = bi2[:, :, sl] * twc + br2[:, :, sl] * tws


def _dft2(a5, f5, k, kb=4):
    n2h, n1, _, sub, C = a5.shape
    kb = min(kb, n1)
    return pl.pallas_call(
        _dft2_kernel,
        grid=(n1 // kb,),
        in_specs=[
            pl.BlockSpec((n2h, kb, 2, sub, C), lambda i: (0, i, 0, 0, 0)),
            pl.BlockSpec((n2h, kb, 2, sub, 2 * C), lambda i: (0, i, 0, 0, 0)),
            pl.BlockSpec((kb, n2h, sub, V7X_LANES), lambda i: (i, 0, 0, 0)),
            pl.BlockSpec((kb, n2h, sub, V7X_LANES), lambda i: (i, 0, 0, 0)),
            pl.BlockSpec((2 * DFT_N2, 2 * DFT_N2), lambda i: (0, 0)),
            pl.BlockSpec((2 * DFT_N2, 2 * DFT_N2), lambda i: (0, 0)),
        ],
        out_specs=pl.BlockSpec((n2h, kb, 2, sub, C), lambda i: (0, i, 0, 0, 0)),
        out_shape=jax.ShapeDtypeStruct(a5.shape, F32),
        compiler_params=_cparams(("parallel",), 48),
        name="dft2",
    )(a5, f5, k["twc"], k["tws"], k["g_fwd"], k["g_inv"])


def _dft3_kernel(b_ref, m_ref, x0_ref, vx_ref, sc_ref, d_ref, o_ref):
    C = b_ref.shape[-1]
    bs = b_ref[...].reshape(m_ref.shape[1], C).astype(BF16)
    y = jnp.dot(m_ref[...], bs, preferred_element_type=F32).reshape(o_ref.shape)
    o_ref[...] = x0_ref[...] * (y * sc_ref[...] + vx_ref[...] * d_ref[...])


def _dft3(b5, m3, x05, vx5, scale, d):
    n2h, n1, _, sub, C = b5.shape
    Bt, n1h = x05.shape[0], x05.shape[1]
    tok = pl.BlockSpec((Bt, n1h, None, sub, C), lambda h: (0, 0, h, 0, 0))
    vec = pl.BlockSpec((1, C), lambda h: (0, 0))
    return pl.pallas_call(
        _dft3_kernel,
        grid=(n2h,),
        in_specs=[
            pl.BlockSpec((None, n1, 2, sub, C), lambda h: (h, 0, 0, 0, 0)),
            pl.BlockSpec(m3.shape, lambda h: (0, 0)),
            tok, tok, vec, vec,
        ],
        out_specs=tok,
        out_shape=jax.ShapeDtypeStruct(x05.shape, F32),
        compiler_params=_cparams(("parallel",), 48),
        name="dft3",
    )(b5, m3, x05, vx5, scale, d)


def _hyena(u, conv_w, conv_b, w1, b1, w2, b2, w3, b3, freq, wout, d_skip, B, L):
    T = u.shape[0]
    C = HYENA_WIDTH
    k = _dft_constants(L)
    n1, n1h = k["n1"], k["n1h"]
    x0, vx = _hpre(u, conv_w, conv_b[None], B, L)
    hfb, asum = _filter_taps(L, w1, b1, w2, b2, w3, b3, freq, wout)
    a5 = _dft1(vx, k["m1c"], n1, cb=C)
    f5 = _dft1(hfb, k["m1r"], n1, cb=C)
    b5 = _dft2(a5, f5, k)
    scale = 1.0 / (asum * (2 * L))
    return _dft3(b5, k["m3"], x0, vx, scale, d_skip[None])


PACKED = jnp.uint32


def _pack_pairs(x):
    half = x.shape[1] // 2
    bits = pltpu.bitcast(x.astype(BF16).astype(F32), PACKED)
    return (bits[:, :half] >> 16) | bits[:, half:]


def _unpack_pairs(w):
    lo = pltpu.bitcast(w << 16, F32)
    hi = pltpu.bitcast(w & jnp.uint32(0xFFFF0000), F32)
    return lo, hi


def _oproj_kernel(x_ref, g0_ref, b0_ref, at_ref, hy_ref, wo_ref, g1_ref, b1_ref, wrh_ref, wrl_ref,
                  h1_ref, h1p_ref, sc_ref):
    aw = at_ref.shape[1]
    h0 = _layer_norm_rows(x_ref[...], g0_ref[...], b0_ref[...])
    mixed = jnp.dot(at_ref[...], wo_ref[0:aw, :], preferred_element_type=F32)
    hy = hy_ref[...].reshape(x_ref.shape[0], hy_ref.shape[-1])
    mixed = mixed + jnp.dot(hy.astype(BF16), wo_ref[aw:, :], preferred_element_type=F32)
    h1 = _layer_norm_rows(ALPHA * h0 + mixed, g1_ref[...], b1_ref[...])
    h1_ref[...] = h1
    h1p_ref[...] = _pack_pairs(h1)
    hh = h1.astype(BF16)
    hl = (h1 - hh.astype(F32)).astype(BF16)
    dn = (((1,), (1,)), ((), ()))
    logits = lax.dot_general(wrh_ref[...], hh, dn, preferred_element_type=F32)
    logits = logits + (lax.dot_general(wrh_ref[...], hl, dn, preferred_element_type=F32)
                       + lax.dot_general(wrl_ref[...], hh, dn, preferred_element_type=F32))
    sc_ref[...] = 1.0 / (1.0 + jnp.exp(-logits))


def _oproj(x2, g0, b0, attn, hy, wo_bf, g1, b1, wr_t, tm=256):
    T, D = x2.shape
    E = wr_t.shape[0]
    wrh = wr_t.astype(BF16)
    wrl = (wr_t - wrh.astype(F32)).astype(BF16)
    row = lambda i: (i, 0)
    fixed = lambda i: (0, 0)
    hy4 = hy.reshape((-1,) + hy.shape[2:])
    return pl.pallas_call(
        _oproj_kernel,
        grid=(T // tm,),
        in_specs=[
            pl.BlockSpec((tm, D), row), pl.BlockSpec((1, D), fixed), pl.BlockSpec((1, D), fixed),
            pl.BlockSpec((tm, attn.shape[1]), row),
            pl.BlockSpec((tm // DFT_N2,) + hy4.shape[1:], lambda i: (i, 0, 0, 0)),
            pl.BlockSpec((D, D), fixed), pl.BlockSpec((1, D), fixed), pl.BlockSpec((1, D), fixed),
            pl.BlockSpec((E, D), fixed), pl.BlockSpec((E, D), fixed),
        ],
        out_specs=[pl.BlockSpec((tm, D), row), pl.BlockSpec((tm, D // 2), row), pl.BlockSpec((E, tm), lambda i: (0, i))],
        out_shape=[
            jax.ShapeDtypeStruct((T, D), F32),
            jax.ShapeDtypeStruct((T, D // 2), PACKED),
            jax.ShapeDtypeStruct((E, T), F32),
        ],
        compiler_params=_cparams(("parallel",), 48),
        name="oproj",
    )(x2, g0, b0, attn, hy4, wo_bf, g1, b1, wrh, wrl)


def _route_kernel(sc_ref, bias_ref, tri_ref, idx_ref, gate_ref, rank_ref, cnt_ref, carry_ref):
    step = pl.program_id(0)
    E, tm = sc_ref.shape
    neg = jnp.float32(-jnp.inf)
    scores = sc_ref[...]
    biased = scores + bias_ref[...]
    erow = lax.broadcasted_iota(jnp.int32, (E, tm), 0)
    big = jnp.int32(E)

    def first_argmax(vals, rows):
        m = jnp.max(vals, axis=0, keepdims=True)
        pick = jnp.min(jnp.where(vals == m, rows, big), axis=0, keepdims=True)
        return m, pick

    gsc = []
    for g in range(N_GROUPS):
        blk = biased[g * GROUP_SIZE:(g + 1) * GROUP_SIZE, :]
        rows = erow[g * GROUP_SIZE:(g + 1) * GROUP_SIZE, :]
        m1, p1 = first_argmax(blk, rows)
        m2 = jnp.max(jnp.where(rows == p1, neg, blk), axis=0, keepdims=True)
        gsc.append(m1 + m2)
    gsc = jnp.concatenate(gsc, axis=0)
    grow = lax.broadcasted_iota(jnp.int32, (N_GROUPS, tm), 0)
    gsel = jnp.zeros((N_GROUPS, tm), jnp.bool_)
    work = gsc
    for _ in range(TOPK_GROUPS):
        _, p = first_argmax(work, grow)
        hit = grow == p
        gsel = gsel | hit
        work = jnp.where(hit, neg, work)
    emask = jnp.concatenate(
        [jnp.broadcast_to(gsel[g:g + 1, :], (GROUP_SIZE, tm)) for g in range(N_GROUPS)], axis=0)
    work = jnp.where(emask, biased, neg)

    sel = jnp.zeros((E, tm), jnp.bool_)
    picks, gvals = [], []
    for _ in range(TOP_K):
        _, p = first_argmax(work, erow)
        hit = erow == p
        sel = sel | hit
        picks.append(p)
        gvals.append(jnp.sum(jnp.where(hit, scores, 0.0), axis=0, keepdims=True))
        work = jnp.where(hit, neg, work)
    gv = jnp.concatenate(gvals, axis=0)
    idx_ref[...] = jnp.concatenate(picks, axis=0)
    gate_ref[...] = gv / jnp.sum(gv, axis=0, keepdims=True) * ROUTED_SCALE

    @pl.when(step == 0)
    def _():
        carry_ref[...] = jnp.zeros_like(carry_ref)

    chosen = sel.astype(F32)
    before = jnp.dot(chosen.astype(BF16), tri_ref[...], preferred_element_type=F32) + carry_ref[...]
    rank_ref[...] = jnp.concatenate(
        [jnp.sum(jnp.where(erow == p, before, 0.0), axis=0, keepdims=True) for p in picks], axis=0).astype(jnp.int32)
    carry_ref[...] += jnp.sum(chosen, axis=1, keepdims=True)
    cnt_ref[...] = carry_ref[...].astype(jnp.int32)


def _route(scores_t, bias, tm=128):
    E, T = scores_t.shape
    tri = (jnp.arange(tm)[:, None] < jnp.arange(tm)[None, :]).astype(BF16)
    tok = lambda i: (0, i)
    fixed = lambda i: (0, 0)
    return pl.pallas_call(
        _route_kernel,
        grid=(T // tm,),
        in_specs=[pl.BlockSpec((E, tm), tok), pl.BlockSpec((E, 1), fixed), pl.BlockSpec((tm, tm), fixed)],
        out_specs=[pl.BlockSpec((TOP_K, tm), tok), pl.BlockSpec((TOP_K, tm), tok), pl.BlockSpec((TOP_K, tm), tok),
                   pl.BlockSpec((E, 1), fixed)],
        out_shape=[
            jax.ShapeDtypeStruct((TOP_K, T), jnp.int32),
            jax.ShapeDtypeStruct((TOP_K, T), F32),
            jax.ShapeDtypeStruct((TOP_K, T), jnp.int32),
            jax.ShapeDtypeStruct((E, 1), jnp.int32),
        ],
        scratch_shapes=[pltpu.VMEM((E, 1), F32)],
        compiler_params=_cparams(("arbitrary",), 32),
        name="route",
    )(scores_t, bias, tri)


def _dest_kernel(idx_ref, rank_ref, ps_ref, dest_ref):
    E = ps_ref.shape[0]
    tm = idx_ref.shape[1]
    erow = lax.broadcasted_iota(jnp.int32, (E, tm), 0)
    ps = ps_ref[...].astype(F32)
    rows = [jnp.sum(jnp.where(erow == idx_ref[k:k + 1, :], ps, 0.0), axis=0, keepdims=True) for k in range(TOP_K)]
    dest_ref[...] = jnp.concatenate(rows, axis=0).astype(jnp.int32) + rank_ref[...]


def _dest_rows(idx, rank, pad_start, tm=512):
    K, T = idx.shape
    E = pad_start.shape[0]
    tok = lambda i: (0, i)
    return pl.pallas_call(
        _dest_kernel,
        grid=(T // tm,),
        in_specs=[pl.BlockSpec((K, tm), tok), pl.BlockSpec((K, tm), tok), pl.BlockSpec((E, 1), lambda i: (0, 0))],
        out_specs=pl.BlockSpec((K, tm), tok),
        out_shape=jax.ShapeDtypeStruct((K, T), jnp.int32),
        compiler_params=_cparams(("arbitrary",), 32),
        name="dest",
    )(idx, rank, pad_start[:, None])


def _scatter_kernel(ps_ref, pe_ref, dest_ref, h_ref, xs_ref, zbuf, sem, *, rb):
    sub = V7X_SUBLANES
    tm = h_ref.shape[0]

    @pl.when(pl.program_id(0) == 0)
    def _():
        zbuf[...] = jnp.zeros_like(zbuf)

        def zcopy(e):
            return pltpu.make_async_copy(zbuf, xs_ref.at[pl.ds(pl.multiple_of(pe_ref[e] - rb, rb), rb)], sem)

        def zstart(e, c):
            @pl.when(pe_ref[e] > ps_ref[e])
            def _():
                zcopy(e).start()
            return c

        def zwait(e, c):
            @pl.when(pe_ref[e] > ps_ref[e])
            def _():
                zcopy(e).wait()
            return c

        lax.fori_loop(0, ps_ref.shape[0], zstart, 0)
        lax.fori_loop(0, ps_ref.shape[0], zwait, 0)

    def issue(r8, c):
        for j in range(sub):
            for k in range(TOP_K):
                dst = dest_ref[0, r8 * (sub * TOP_K) + (j * TOP_K + k)]
                pltpu.make_async_copy(h_ref.at[pl.ds(r8 * sub + j, 1)], xs_ref.at[pl.ds(dst, 1)], sem).start(priority=k % 2)
        return c

    lax.fori_loop(0, tm // sub, issue, 0)
    for k in range(TOP_K):
        pltpu.make_async_copy(h_ref, xs_ref.at[pl.ds(0, tm)], sem).wait()


def _dest_tiles(dest, tm):
    K, T = dest.shape
    return dest.T.reshape(T // tm, 1, tm * K)


def _scatter_rows(pad_start, pad_end, dest_t, h1p, n_rows, rb, tm=256):
    T, W = h1p.shape
    return pl.pallas_call(
        functools.partial(_scatter_kernel, rb=rb),
        grid_spec=pltpu.PrefetchScalarGridSpec(
            num_scalar_prefetch=2,
            grid=(T // tm,),
            in_specs=[
                pl.BlockSpec((None, 1, tm * TOP_K), lambda i, ps, pe: (i, 0, 0), memory_space=pltpu.SMEM),
                pl.BlockSpec((tm, W), lambda i, ps, pe: (i, 0)),
            ],
            out_specs=pl.BlockSpec(memory_space=pl.ANY),
            scratch_shapes=[pltpu.VMEM((rb, W), h1p.dtype), pltpu.SemaphoreType.DMA],
        ),
        out_shape=jax.ShapeDtypeStruct((n_rows, W), h1p.dtype),
        compiler_params=_cparams(("arbitrary",), 32),
        name="scatter",
    )(pad_start, pad_end, dest_t, h1p)


EXPERT_X_SLOTS = 4
EXPERT_Y_SLOTS = 3


def _experts_kernel(first_ref, nblk_ref, wsel_ref, tot_ref, xs_ref, wg_ref, wu_ref, wd_ref, ys_ref,
                    xbuf, ybuf, wgub, wdb, xsem, ysem, *, rb):
    del wsel_ref
    e = pl.program_id(0)
    F = wg_ref.shape[1]
    nx, ny = xbuf.shape[0], ybuf.shape[0]
    total = tot_ref[0]
    n = nblk_ref[e]
    g0 = first_ref[e]

    def x_copy(g):
        slot = g % nx
        return pltpu.make_async_copy(xs_ref.at[pl.ds(pl.multiple_of(g * rb, rb), rb)], xbuf.at[slot], xsem.at[slot])

    def y_copy(g):
        slot = g % ny
        return pltpu.make_async_copy(ybuf.at[slot], ys_ref.at[pl.ds(pl.multiple_of(g * rb, rb), rb)], ysem.at[slot])

    @pl.when(e == 0)
    def _():
        for g in range(nx):
            @pl.when(g < total)
            def _():
                x_copy(g).start()

    def process(g, count):
        for d in range(count):
            x_copy(g + d).wait()
        ys = []
        for d in range(count):
            lo, hi = _unpack_pairs(xbuf[(g + d) % nx])
            x = jnp.concatenate([lo, hi], axis=1).astype(BF16)
            gu = jnp.dot(x, wgub[...], preferred_element_type=F32)
            gate, up = gu[:, 0:F], gu[:, F:]
            hb = (gate / (1.0 + jnp.exp(-gate)) * up).astype(BF16)
            ys.append(_pack_pairs(jnp.dot(hb, wdb[...], preferred_element_type=F32)))
        for d in range(count):
            @pl.when(g + d >= ny)
            def _():
                y_copy(g + d - ny).wait()

            ybuf[(g + d) % ny] = ys[d]
            y_copy(g + d).start()
        for d in range(count):
            @pl.when(g + d + nx < total)
            def _():
                x_copy(g + d + nx).start()

    @pl.when(n > 0)
    def _():
        wgub[:, 0:F] = wg_ref[...].astype(BF16)
        wgub[:, F:] = wu_ref[...].astype(BF16)
        wdb[...] = wd_ref[...].astype(BF16)

        def pair(j, c):
            process(g0 + 2 * j, 2)
            return c

        lax.fori_loop(0, n // 2, pair, 0)

        @pl.when(n % 2 == 1)
        def _():
            process(g0 + n - 1, 1)

    @pl.when(e == pl.num_programs(0) - 1)
    def _():
        for back in range(ny, 0, -1):
            @pl.when(total >= back)
            def _():
                y_copy(total - back).wait()


def _experts(first_blk, nblk, wsel, total_blk, xs, w_gate, w_up, w_down, rb):
    P, W = xs.shape
    E, D, F = w_gate.shape
    wspec = lambda shape: pl.BlockSpec((None,) + shape, lambda e, fb, nbk, ws, tt: (ws[e], 0, 0))
    return pl.pallas_call(
        functools.partial(_experts_kernel, rb=rb),
        grid_spec=pltpu.PrefetchScalarGridSpec(
            num_scalar_prefetch=4,
            grid=(E,),
            in_specs=[pl.BlockSpec(memory_space=pl.ANY), wspec((D, F)), wspec((D, F)), wspec((F, D))],
            out_specs=pl.BlockSpec(memory_space=pl.ANY),
            scratch_shapes=[
                pltpu.VMEM((EXPERT_X_SLOTS, rb, W), PACKED), pltpu.VMEM((EXPERT_Y_SLOTS, rb, W), PACKED),
                pltpu.VMEM((D, 2 * F), BF16), pltpu.VMEM((F, D), BF16),
                pltpu.SemaphoreType.DMA((EXPERT_X_SLOTS,)), pltpu.SemaphoreType.DMA((EXPERT_Y_SLOTS,)),
            ],
        ),
        out_shape=jax.ShapeDtypeStruct((P, W), PACKED),
        compiler_params=_cparams(("arbitrary",), 48),
        name="experts",
    )(first_blk, nblk, wsel, total_blk, xs, w_gate, w_up, w_down)


def _combine_kernel(dest_ref, destn_ref, h1_ref, gate_ref, ys_ref, ys8_ref, sg_ref, su_ref, sd_ref,
                    g2_ref, b2_ref, o_ref, buf, sem):
    i = pl.program_id(0)
    n = pl.num_programs(0)
    tm = h1_ref.shape[0]
    slot = i % 2

    sub = V7X_SUBLANES

    def issue_rows(dref, s, r0, nrows):
        for r in range(r0, r0 + nrows):
            for k in range(TOP_K):
                src = ys_ref.at[pl.ds(dref[0, r * TOP_K + k], 1)]
                pltpu.make_async_copy(src, buf.at[s, r, pl.ds(k, 1)], sem.at[s]).start(priority=k % 2)

    def wait_tile(s):
        pltpu.make_async_copy(ys8_ref.at[pl.ds(0, tm)], buf.at[s], sem.at[s]).wait()

    @pl.when(i == 0)
    def _():
        def body(r8, c):
            for j in range(sub):
                for k in range(TOP_K):
                    src = ys_ref.at[pl.ds(dest_ref[0, r8 * (sub * TOP_K) + (j * TOP_K + k)], 1)]
                    pltpu.make_async_copy(src, buf.at[0, r8 * sub + j, pl.ds(k, 1)], sem.at[0]).start(priority=k % 2)
            return c
        lax.fori_loop(0, tm // sub, body, 0)

    wait_tile(slot)
    parts = 4
    rows_per = tm // parts
    nk = rows_per * TOP_K
    pick = lax.broadcasted_iota(jnp.int32, (TOP_K, nk), 1) % TOP_K == lax.broadcasted_iota(jnp.int32, (TOP_K, nk), 0)
    spread = pick.astype(BF16)
    own = lax.broadcasted_iota(jnp.int32, (rows_per, nk), 1) // TOP_K == lax.broadcasted_iota(jnp.int32, (rows_per, nk), 0)
    for part in range(parts):
        r0 = part * rows_per
        issue_rows(destn_ref, 1 - slot, r0, rows_per)
        rows = slice(r0, r0 + rows_per)
        h1 = h1_ref[rows, :]
        hb = h1.astype(BF16)
        g = jnp.dot(hb, sg_ref[...], preferred_element_type=F32)
        u = jnp.dot(hb, su_ref[...], preferred_element_type=F32)
        ffn = jnp.dot((g / (1.0 + jnp.exp(-g)) * u).astype(BF16), sd_ref[...], preferred_element_type=F32)
        lo, hi = _unpack_pairs(buf[slot, rows].reshape(nk, buf.shape[3]))
        y = jnp.concatenate([lo, hi], axis=1).astype(BF16)
        gates = gate_ref[rows, :]
        gh = gates.astype(BF16)
        gl = (gates - gh.astype(F32)).astype(BF16)
        routed = jnp.zeros((rows_per, y.shape[1]), F32)
        for piece in (gh, gl):
            gm = jnp.where(own, jnp.dot(piece, spread, preferred_element_type=F32), 0.0).astype(BF16)
            routed = routed + jnp.dot(gm, y, preferred_element_type=F32)
        o_ref[rows, :] = _layer_norm_rows(ALPHA * h1 + ffn + routed, g2_ref[...], b2_ref[...])

    @pl.when(i == n - 1)
    def _():
        wait_tile(1 - slot)


def _combine(dest, h1, gates_tk, ys, sg, su, sd, g2, b2, tm=128):
    T, D = h1.shape
    W = ys.shape[1]
    F = sg.shape[1]
    nt = T // tm
    row = lambda i: (i, 0)
    fixed = lambda i: (0, 0)
    smem = functools.partial(pl.BlockSpec, (None, 1, tm * TOP_K), memory_space=pltpu.SMEM)
    dest_t = _dest_tiles(dest, tm)
    ys8 = ys.reshape(ys.shape[0] // TOP_K, TOP_K, W)
    return pl.pallas_call(
        _combine_kernel,
        grid=(nt,),
        in_specs=[
            smem(index_map=lambda i: (i, 0, 0)),
            smem(index_map=lambda i: (jnp.minimum(i + 1, nt - 1), 0, 0)),
            pl.BlockSpec((tm, D), row),
            pl.BlockSpec((tm, TOP_K), row),
            pl.BlockSpec(memory_space=pl.ANY),
            pl.BlockSpec(memory_space=pl.ANY),
            pl.BlockSpec((D, F), fixed), pl.BlockSpec((D, F), fixed), pl.BlockSpec((F, D), fixed),
            pl.BlockSpec((1, D), fixed), pl.BlockSpec((1, D), fixed),
        ],
        out_specs=pl.BlockSpec((tm, D), row),
        out_shape=jax.ShapeDtypeStruct((T, D), F32),
        scratch_shapes=[pltpu.VMEM((2, tm, TOP_K, W), PACKED), pltpu.SemaphoreType.DMA((2,))],
        compiler_params=_cparams(("arbitrary",), 48),
        name="combine",
    )(dest_t, dest_t, h1, gates_tk, ys, ys8, sg, su, sd, g2, b2)


MOE_ROWS = 256


def _moe(h1, h1p, scores_t, router_bias, w_gate, w_up, w_down, ws_gate, ws_up, ws_down, g2, b2):
    T = h1.shape[0]
    E = N_EXPERTS
    rb = MOE_ROWS
    idx, gates, rank, counts = _route(scores_t, router_bias[:, None])
    counts = counts[:, 0]
    padded = (counts + rb - 1) // rb * rb
    pad_end = jnp.cumsum(padded)
    pad_start = pad_end - padded
    nb = (T * TOP_K) // rb + E
    nblk = (padded // rb).astype(jnp.int32)
    first_blk = (pad_start // rb).astype(jnp.int32)
    total_blk = (pad_end[-1:] // rb).astype(jnp.int32)
    wsel = lax.cummax(jnp.where(nblk > 0, jnp.arange(E, dtype=jnp.int32), 0))
    dest = _dest_rows(idx, rank, pad_start)
    xs = _scatter_rows(pad_start, pad_end, _dest_tiles(dest, 256), h1p, nb * rb, rb, tm=256)
    ys = _experts(first_blk, nblk, wsel, total_blk, xs, w_gate, w_up, w_down, rb)
    return _combine(dest, h1, gates.T, ys, ws_gate.astype(BF16), ws_up.astype(BF16), ws_down.astype(BF16), g2, b2)


def kernel(x, positions, emb_ln_g, emb_ln_b, w_in, hy_conv_w, hy_conv_b, hy_f_w1, hy_f_b1, hy_f_w2, hy_f_b2, hy_f_w3, hy_f_b3, hy_f_freq, hy_f_wout, hy_d, lambda_q1, lambda_k1, lambda_q2, lambda_k2, subln_g, w_o, ln1_g, ln1_b, w_router, router_bias, w_gate, w_up, w_down, ws_gate, ws_up, ws_down, ln2_g, ln2_b):
    B, L, D = x.shape
    T = B * L
    assert w_in.shape[0] == DEPTH == 1
    i = 0
    x2 = x.reshape(T, D)
    g0, b0 = emb_ln_g[None], emb_ln_b[None]
    ra, rm, rp = _rotary_tables(positions)
    q, k, v, u = _inproj(x2, g0, b0, w_in[i].astype(BF16), ra, rm, rp)
    lam = (jnp.exp(jnp.sum(lambda_q1[i] * lambda_k1[i])) - jnp.exp(jnp.sum(lambda_q2[i] * lambda_k2[i])) + LAM_INIT)
    attn = _attention(q, k, v, lam.reshape(1).astype(F32), subln_g[i][None], B, L)
    hy = _hyena(u, hy_conv_w[i], hy_conv_b[i], hy_f_w1[i], hy_f_b1[i], hy_f_w2[i], hy_f_b2[i], hy_f_w3[i], hy_f_b3[i],
                hy_f_freq[i], hy_f_wout[i], hy_d[i], B, L)
    h1, h1p, scores_t = _oproj(x2, g0, b0, attn, hy, w_o[i].astype(BF16), ln1_g[i][None], ln1_b[i][None], w_router[i].T)
    out = _moe(h1, h1p, scores_t, router_bias[i], w_gate[i], w_up[i], w_down[i], ws_gate[i], ws_up[i], ws_down[i],
               ln2_g[i][None], ln2_b[i][None])
    return out.reshape(B, L, D)
```

```python
import functools
import math

import numpy as np
import jax
import jax.numpy as jnp
from jax import lax
from jax.experimental import pallas as pl
from jax.experimental.pallas import tpu as pltpu

DA_HEADS = 4
DA_HEAD_DIM = 64
DA_V_DIM = 128
ATTN_WIDTH = 512
HYENA_WIDTH = 512
ROT_DIM = 16
ROPE_THETA = 500000.0
SHORT_CONV = 3
FILTER_EMB = 33
FILTER_BANDS = 16
DECAY_TARGET = 1e-2
FAST_DECAY = 0.3
SLOW_DECAY = 1.5
N_EXPERTS = 256
TOP_K = 8
N_GROUPS = 8
GROUP_SIZE = N_EXPERTS // N_GROUPS
TOPK_GROUPS = 4
EXPERT_DIM = 256
ROUTED_SCALE = 2.5
DEPTH = 1
ALPHA = (2 * DEPTH) ** 0.25
LN_EPS = 1e-5
LAM_INIT = 0.8 - 0.6 * math.exp(-0.3 * 0)

V7X_LANES = 128
V7X_SUBLANES = 8
V7X_VMEM_BYTES = 64 * 1024 * 1024

DFT_N2 = 128
DFT_N2H = DFT_N2 // V7X_SUBLANES

BF16 = jnp.bfloat16
F32 = jnp.float32


def _cparams(sem, vmem_mb):
    return pltpu.CompilerParams(dimension_semantics=sem, vmem_limit_bytes=vmem_mb * 1024 * 1024)


def _layer_norm_rows(x, g, b):
    mu = jnp.mean(x, axis=-1, keepdims=True)
    xc = x - mu
    var = jnp.mean(xc * xc, axis=-1, keepdims=True)
    return xc * lax.rsqrt(var + LN_EPS) * g + b


def _inproj_kernel(x_ref, g_ref, b_ref, w_ref, ra_ref, rm_ref, rp_ref, q_ref, k_ref, v_ref, u_ref):
    h = _layer_norm_rows(x_ref[...], g_ref[...], b_ref[...]).astype(BF16)
    ra, rm, rp = ra_ref[...], rm_ref[...], rp_ref[...]

    def rot(t):
        return t * ra + pltpu.roll(t, V7X_LANES - ROT_DIM // 2, axis=1) * rm + pltpu.roll(t, ROT_DIM // 2, axis=1) * rp

    aw = ATTN_WIDTH
    qp = jnp.dot(h, w_ref[:, 0:aw], preferred_element_type=F32)
    kp = jnp.dot(h, w_ref[:, aw:2 * aw], preferred_element_type=F32)
    scale = DA_HEAD_DIM ** -0.5 * math.log2(math.e)
    for c in range(aw // V7X_LANES):
        sl = slice(c * V7X_LANES, (c + 1) * V7X_LANES)
        q_ref[:, sl] = (rot(qp[:, sl]) * scale).astype(BF16)
        k_ref[:, sl] = rot(kp[:, sl]).astype(BF16)
    vp = jnp.dot(h, w_ref[:, 2 * aw:3 * aw], preferred_element_type=F32).astype(BF16)
    ones = jnp.ones((vp.shape[0], DA_V_DIM), BF16)
    for hd in range(DA_HEADS):
        v_ref[:, 2 * hd * DA_V_DIM:(2 * hd + 1) * DA_V_DIM] = vp[:, hd * DA_V_DIM:(hd + 1) * DA_V_DIM]
        v_ref[:, (2 * hd + 1) * DA_V_DIM:(2 * hd + 2) * DA_V_DIM] = ones
    u_ref[...] = jnp.dot(h, w_ref[:, 3 * aw:], preferred_element_type=F32)


def _inproj(x2, g, b, w_bf, ra, rm, rp, tm=256):
    T, D = x2.shape
    ncol = w_bf.shape[1]
    aw = ATTN_WIDTH
    uw = ncol - 3 * aw
    row = lambda i: (i, 0)
    fixed = lambda i: (0, 0)
    return pl.pallas_call(
        _inproj_kernel,
        grid=(T // tm,),
        in_specs=[
            pl.BlockSpec((tm, D), row),
            pl.BlockSpec((1, D), fixed),
            pl.BlockSpec((1, D), fixed),
            pl.BlockSpec((D, ncol), fixed),
            pl.BlockSpec((tm, V7X_LANES), row),
            pl.BlockSpec((tm, V7X_LANES), row),
            pl.BlockSpec((tm, V7X_LANES), row),
        ],
        out_specs=[
            pl.BlockSpec((tm, aw), row),
            pl.BlockSpec((tm, aw), row),
            pl.BlockSpec((tm, 2 * aw), row),
            pl.BlockSpec((tm, uw), row),
        ],
        out_shape=[
            jax.ShapeDtypeStruct((T, aw), BF16),
            jax.ShapeDtypeStruct((T, aw), BF16),
            jax.ShapeDtypeStruct((T, 2 * aw), BF16),
            jax.ShapeDtypeStruct((T, uw), F32),
        ],
        compiler_params=_cparams(("parallel",), 48),
        name="inproj",
    )(x2, g, b, w_bf, ra, rm, rp)


def _rotary_tables(positions):
    half = ROT_DIM // 2
    inv_freq = ROPE_THETA ** (-jnp.arange(0, ROT_DIM, 2, dtype=F32) / ROT_DIM)
    ang = positions.astype(F32).reshape(-1)[:, None] * inv_freq
    cos, sin = jnp.cos(ang), jnp.sin(ang)
    T = ang.shape[0]
    ones = jnp.ones((T, DA_HEAD_DIM - ROT_DIM), F32)
    zeros_h = jnp.zeros((T, half), F32)
    zeros_r = jnp.zeros((T, DA_HEAD_DIM - ROT_DIM), F32)
    a64 = jnp.concatenate([cos, cos, ones], axis=1)
    m64 = jnp.concatenate([-sin, zeros_h, zeros_r], axis=1)
    p64 = jnp.concatenate([zeros_h, sin, zeros_r], axis=1)
    rep = V7X_LANES // DA_HEAD_DIM
    return jnp.tile(a64, (1, rep)), jnp.tile(m64, (1, rep)), jnp.tile(p64, (1, rep))


def _attn_kernel(lam_ref, q_ref, k_ref, v_ref, g_ref, o_ref, s_ref, m_ref, acc_ref, *, kb, unroll):
    qb = q_ref.shape[0]
    L = k_ref.shape[0]
    nchunk = L // kb
    nl = kb // V7X_LANES
    q = q_ref[...]
    lane = lax.broadcasted_iota(jnp.int32, q.shape, 1)
    zero = jnp.zeros_like(q)
    qm = [jnp.where(lane < DA_HEAD_DIM, q, zero), jnp.where(lane >= DA_HEAD_DIM, q, zero)]
    m_ref[...] = jnp.full(m_ref.shape, -jnp.inf, F32)
    acc_ref[...] = jnp.zeros(acc_ref.shape, F32)

    def score_body(j, carry):
        kc = k_ref[pl.ds(pl.multiple_of(j * kb, kb), kb), :]
        for c in range(2):
            s = lax.dot_general(qm[c], kc, (((1,), (1,)), ((), ())), preferred_element_type=F32)
            s_ref[c, j] = s
            m = s[:, 0:V7X_LANES]
            for t in range(1, nl):
                m = jnp.maximum(m, s[:, t * V7X_LANES:(t + 1) * V7X_LANES])
            m_ref[c] = jnp.maximum(m_ref[c], m)
        return carry

    lax.fori_loop(0, nchunk, score_body, 0, unroll=unroll)
    m_row = [jnp.max(m_ref[c], axis=1, keepdims=True) for c in range(2)]

    def pv_body(j, carry):
        vc = v_ref[pl.ds(pl.multiple_of(j * kb, kb), kb), :]
        for c in range(2):
            p = jnp.exp2(s_ref[c, j] - m_row[c])
            acc_ref[c] += jnp.dot(p.astype(BF16), vc, preferred_element_type=F32)
        return carry

    lax.fori_loop(0, nchunk, pv_body, 0, unroll=unroll)
    outs = [acc_ref[c, :, 0:DA_V_DIM] / acc_ref[c, :, DA_V_DIM:2 * DA_V_DIM] for c in range(2)]
    o = outs[0] - lam_ref[0] * outs[1]
    ms = jnp.mean(o * o, axis=1, keepdims=True)
    o_ref[...] = (o * lax.rsqrt(ms + LN_EPS) * g_ref[...] * (1.0 - LAM_INIT)).astype(o_ref.dtype)


def _attention(q, k, v, lam, subln_g, B, L, qb=512, kb=1024, unroll=8):
    T = q.shape[0]
    nq = L // qb
    return pl.pallas_call(
        functools.partial(_attn_kernel, kb=kb, unroll=unroll),
        grid=(B, DA_HEADS, nq),
        in_specs=[
            pl.BlockSpec(memory_space=pltpu.SMEM),
            pl.BlockSpec((qb, DA_V_DIM), lambda b, h, i: (b * nq + i, h)),
            pl.BlockSpec((L, DA_V_DIM), lambda b, h, i: (b, h)),
            pl.BlockSpec((L, 2 * DA_V_DIM), lambda b, h, i: (b, h)),
            pl.BlockSpec((1, DA_V_DIM), lambda b, h, i: (0, 0)),
        ],
        out_specs=pl.BlockSpec((qb, DA_V_DIM), lambda b, h, i: (b * nq + i, h)),
        out_shape=jax.ShapeDtypeStruct((T, ATTN_WIDTH), BF16),
        scratch_shapes=[
            pltpu.VMEM((2, L // kb, qb, kb), F32),
            pltpu.VMEM((2, qb, V7X_LANES), F32),
            pltpu.VMEM((2, qb, 2 * DA_V_DIM), F32),
        ],
        compiler_params=_cparams(("parallel", "parallel", "parallel"), 56),
        name="attn",
    )(lam, q, k, v, subln_g)


def _hpre_kernel(u_ref, up_ref, un_ref, w_ref, b_ref, x0_ref, vx_ref):
    i = pl.program_id(1)
    n = pl.num_programs(1)
    tb = u_ref.shape[0]
    C = x0_ref.shape[-1]
    row = lax.broadcasted_iota(jnp.int32, (tb, V7X_LANES), 0)
    has_prev = (i > 0).astype(F32)
    has_next = (i < n - 1).astype(F32)

    def conv(c0):
        sl = slice(c0, c0 + V7X_LANES)
        u = u_ref[:, sl]
        prev_row = up_ref[V7X_SUBLANES - 1:V7X_SUBLANES, sl] * has_prev
        next_row = un_ref[0:1, sl] * has_next
        u_prev = jnp.where(row == 0, prev_row, pltpu.roll(u, 1, axis=0))
        u_next = jnp.where(row == tb - 1, next_row, pltpu.roll(u, tb - 1, axis=0))
        return u_prev * w_ref[0:1, sl] + u * w_ref[1:2, sl] + u_next * w_ref[2:3, sl] + b_ref[:, sl]

    tile = x0_ref.shape[:-1] + (V7X_LANES,)
    for c in range(C // V7X_LANES):
        c0 = c * V7X_LANES
        x0_ref[:, :, :, c0:c0 + V7X_LANES] = conv(c0).reshape(tile)
        vx_ref[:, :, :, c0:c0 + V7X_LANES] = (conv(2 * C + c0) * conv(C + c0)).reshape(tile)


def _hpre(u, conv_w, conv_b, B, L, tb=512):
    T, C3 = u.shape
    C = C3 // 3
    nt = L // tb
    sub = V7X_SUBLANES
    na = tb // DFT_N2
    split = pl.BlockSpec((None, na, DFT_N2H, sub, C), lambda b, i: (b, i, 0, 0, 0))
    split_shape = jax.ShapeDtypeStruct((B, L // DFT_N2, DFT_N2H, sub, C), F32)
    cur = lambda b, i: (b * nt + i, 0)
    prev = lambda b, i: (jnp.maximum((b * L + i * tb) // sub - 1, 0), 0)
    nxt = lambda b, i: (jnp.minimum((b * L + (i + 1) * tb) // sub, T // sub - 1), 0)
    fixed = lambda b, i: (0, 0)
    return pl.pallas_call(
        _hpre_kernel,
        grid=(B, nt),
        in_specs=[
            pl.BlockSpec((tb, C3), cur),
            pl.BlockSpec((sub, C3), prev),
            pl.BlockSpec((sub, C3), nxt),
            pl.BlockSpec((SHORT_CONV, C3), fixed),
            pl.BlockSpec((1, C3), fixed),
        ],
        out_specs=[split, split],
        out_shape=[split_shape, split_shape],
        compiler_params=_cparams(("parallel", "parallel"), 32),
        name="hpre",
    )(u, u, u, conv_w, conv_b)


def _filt_kernel(z_ref, w1_ref, b1_ref, w2_ref, b2_ref, w3_ref, b3_ref, fr_ref, wo_ref, dl_ref,
                 hfb_ref, asum_ref, *, L):
    i = pl.program_id(0)
    tl = z_ref.shape[0]
    C = dl_ref.shape[1]
    hp = lax.Precision.HIGHEST
    h = jnp.sin(fr_ref[0:1, :] * (jnp.dot(z_ref[...], w1_ref[...], precision=hp, preferred_element_type=F32) + b1_ref[...]))
    h = jnp.sin(fr_ref[1:2, :] * (jnp.dot(h, w2_ref[...], precision=hp, preferred_element_type=F32) + b2_ref[...]))
    h = jnp.sin(fr_ref[2:3, :] * (jnp.dot(h, w3_ref[...], precision=hp, preferred_element_type=F32) + b3_ref[...]))
    o = jnp.dot(h, wo_ref[...], precision=hp, preferred_element_type=F32)
    grow = lax.broadcasted_iota(jnp.int32, (tl, C), 0) + i * tl
    t = grow.astype(F32) * (1.0 / (L - 1))
    decay = jnp.exp(-t * dl_ref[...])
    hf = o[:, :C] * decay
    hb = jnp.where(grow == 0, 0.0, o[:, C:] * decay)
    tile = hfb_ref.shape[:-1] + (C,)
    hfb_ref[:, :, :, :C] = hf.reshape(tile)
    hfb_ref[:, :, :, C:] = hb.reshape(tile)

    @pl.when(i == 0)
    def _():
        asum_ref[...] = jnp.zeros_like(asum_ref)

    asum_ref[...] += jnp.sum(jnp.abs(hf) + jnp.abs(hb), axis=0, keepdims=True)


def _filter_taps(L, w1, b1, w2, b2, w3, b3, freq, wout, tl=512):
    C = wout.shape[1] // 2
    order = w1.shape[1]
    emb = w1.shape[0]
    t = jnp.linspace(0.0, 1.0, L, dtype=F32)[:, None]
    w = 2.0 * math.pi * jnp.arange(L, dtype=F32)[:, None] / L
    f = jnp.linspace(1e-4, FILTER_BANDS - 1, FILTER_BANDS, dtype=F32)[None, :]
    z = jnp.concatenate([t, jnp.cos(f * w), -jnp.sin(f * w)], axis=-1)
    zp = jnp.pad(z, ((0, 0), (0, V7X_LANES - emb)))
    w1p = jnp.pad(w1, ((0, V7X_LANES - emb), (0, 0)))
    deltas = jnp.abs(jnp.linspace(math.log(DECAY_TARGET) / SLOW_DECAY, math.log(DECAY_TARGET) / FAST_DECAY, C, dtype=F32))[None]
    fixed = lambda i: (0, 0)
    return pl.pallas_call(
        functools.partial(_filt_kernel, L=L),
        grid=(L // tl,),
        in_specs=[
            pl.BlockSpec((tl, V7X_LANES), lambda i: (i, 0)),
            pl.BlockSpec((V7X_LANES, order), fixed), pl.BlockSpec((1, order), fixed),
            pl.BlockSpec((order, order), fixed), pl.BlockSpec((1, order), fixed),
            pl.BlockSpec((order, order), fixed), pl.BlockSpec((1, order), fixed),
            pl.BlockSpec((3, order), fixed),
            pl.BlockSpec((order, 2 * C), fixed),
            pl.BlockSpec((1, C), fixed),
        ],
        out_specs=[pl.BlockSpec((None, tl // DFT_N2, DFT_N2H, V7X_SUBLANES, 2 * C), lambda i: (0, i, 0, 0, 0)),
                   pl.BlockSpec((1, C), fixed)],
        out_shape=[jax.ShapeDtypeStruct((1, L // DFT_N2, DFT_N2H, V7X_SUBLANES, 2 * C), F32),
                   jax.ShapeDtypeStruct((1, C), F32)],
        compiler_params=_cparams(("arbitrary",), 32),
        name="filt",
    )(zp, w1p, b1[None], w2, b2[None], w3, b3[None], freq, wout, deltas)


def _dft_constants(L):
    n1 = 2 * L // DFT_N2
    n1h = n1 // 2
    N = 2 * L
    sub = V7X_SUBLANES

    def cs(num, den):
        ang = (num % den).astype(np.float64) * (2.0 * np.pi / den)
        return np.cos(ang), np.sin(ang)

    k1 = np.arange(n1, dtype=np.int64)
    a = np.arange(n1h, dtype=np.int64)
    c, s = cs(k1[:, None] * a[None, :], n1)
    eye = np.eye(sub)

    def expand(m):
        r, kk = m.shape
        return (m[:, None, :, None] * eye[None, :, None, :]).reshape(r * sub, kk * sub)

    def const(m):
        return jnp.asarray(m.astype(np.float32).astype(BF16))

    m1c = np.stack([np.concatenate([c, s], axis=1), np.concatenate([-s, c], axis=1)], axis=1).reshape(2 * n1, 2 * n1h)
    m1r = np.stack([c, -s], axis=1).reshape(2 * n1, n1h)
    ct, st = c.T, s.T
    m3 = np.stack([np.stack([ct, -st], axis=2).reshape(n1h, 2 * n1),
                   np.stack([st, ct], axis=2).reshape(n1h, 2 * n1)], axis=0).reshape(2 * n1h, 2 * n1)
    n2 = np.arange(DFT_N2, dtype=np.int64)
    gc, gs = cs(n2[:, None] * n2[None, :], DFT_N2)
    g_fwd = np.concatenate([np.concatenate([gc, gs], axis=1), np.concatenate([-gs, gc], axis=1)], axis=0)
    g_inv = np.concatenate([np.concatenate([gc, -gs], axis=1), np.concatenate([gs, gc], axis=1)], axis=0)
    ph = (jnp.arange(n1, dtype=jnp.int32)[:, None] * jnp.arange(DFT_N2, dtype=jnp.int32)[None, :]) % N
    ang = ph.astype(F32) * (2.0 * math.pi / N)
    lanes = (n1, DFT_N2H, sub, V7X_LANES)
    twc = jnp.broadcast_to(jnp.cos(ang).reshape(n1, DFT_N2H, sub, 1), lanes)
    tws = jnp.broadcast_to(jnp.sin(ang).reshape(n1, DFT_N2H, sub, 1), lanes)
    return dict(n1=n1, n1h=n1h, m1c=const(expand(m1c)), m1r=const(expand(m1r)), m3=const(expand(m3)),
                g_fwd=const(g_fwd), g_inv=const(g_inv), twc=twc, tws=tws)


def _dft1_kernel(x_ref, m_ref, o_ref):
    rows = m_ref.shape[1]
    xs = x_ref[...].reshape(rows, x_ref.shape[-1]).astype(BF16)
    o_ref[...] = jnp.dot(m_ref[...], xs, preferred_element_type=F32).reshape(o_ref.shape)


def _dft1(x5, m, n1, cb):
    P, n1h, _, sub, Cx = x5.shape
    return pl.pallas_call(
        _dft1_kernel,
        grid=(DFT_N2H, Cx // cb),
        in_specs=[
            pl.BlockSpec((P, n1h, None, sub, cb), lambda h, c: (0, 0, h, 0, c)),
            pl.BlockSpec(m.shape, lambda h, c: (0, 0)),
        ],
        out_specs=pl.BlockSpec((None, n1, 2, sub, cb), lambda h, c: (h, 0, 0, 0, c)),
        out_shape=jax.ShapeDtypeStruct((DFT_N2H, n1, 2, sub, Cx), F32),
        compiler_params=_cparams(("parallel", "parallel"), 48),
        name="dft1",
    )(x5, m)


def _dft2_kernel(a_ref, f_ref, twc_ref, tws_ref, gf_ref, gi_ref, o_ref):
    C = a_ref.shape[-1]
    half = DFT_N2

    def lanes(fn):
        return jnp.concatenate([fn(slice(c0, c0 + V7X_LANES)) for c0 in range(0, C, V7X_LANES)], axis=-1)

    for kk in range(a_ref.shape[1]):
        twc, tws = twc_ref[kk], tws_ref[kk]

        def spectrum(ref, col0):
            def re_part(sl):
                return ref[:, kk, 0, :, col0 + sl.start:col0 + sl.stop] * twc + ref[:, kk, 1, :, col0 + sl.start:col0 + sl.stop] * tws

            def im_part(sl):
                return ref[:, kk, 1, :, col0 + sl.start:col0 + sl.stop] * twc - ref[:, kk, 0, :, col0 + sl.start:col0 + sl.stop] * tws

            t = jnp.concatenate([lanes(re_part).reshape(half, C), lanes(im_part).reshape(half, C)], axis=0).astype(BF16)
            s = jnp.dot(gf_ref[...], t, preferred_element_type=F32)
            return s[:half], s[half:]

        xr, xi = spectrum(a_ref, 0)
        fr, fi = spectrum(f_ref, 0)
        br, bi = spectrum(f_ref, C)
        hr, hi = fr + br, fi - bi
        y = jnp.concatenate([xr * hr - xi * hi, xr * hi + xi * hr], axis=0).astype(BF16)
        b = jnp.dot(gi_ref[...], y, preferred_element_type=F32)
        br2 = b[:half].reshape(DFT_N2H, V7X_SUBLANES, C)
        bi2 = b[half:].reshape(DFT_N2H, V7X_SUBLANES, C)
        for c0 in range(0, C, V7X_LANES):
            sl = slice(c0, c0 + V7X_LANES)
            o_ref[:, kk, 0, :, sl] = br2[:, :, sl] * twc - bi2[:, :, sl] * tws
            o_ref[:, kk, 1, :, sl] = bi2[:, :, sl] * twc + br2[:, :, sl] * tws


def _dft2(a5, f5, k, kb=4):
    n2h, n1, _, sub, C = a5.shape
    kb = min(kb, n1)
    return pl.pallas_call(
        _dft2_kernel,
        grid=(n1 // kb,),
        in_specs=[
            pl.BlockSpec((n2h, kb, 2, sub, C), lambda i: (0, i, 0, 0, 0)),
            pl.BlockSpec((n2h, kb, 2, sub, 2 * C), lambda i: (0, i, 0, 0, 0)),
            pl.BlockSpec((kb, n2h, sub, V7X_LANES), lambda i: (i, 0, 0, 0)),
            pl.BlockSpec((kb, n2h, sub, V7X_LANES), lambda i: (i, 0, 0, 0)),
            pl.BlockSpec((2 * DFT_N2, 2 * DFT_N2), lambda i: (0, 0)),
            pl.BlockSpec((2 * DFT_N2, 2 * DFT_N2), lambda i: (0, 0)),
        ],
        out_specs=pl.BlockSpec((n2h, kb, 2, sub, C), lambda i: (0, i, 0, 0, 0)),
        out_shape=jax.ShapeDtypeStruct(a5.shape, F32),
        compiler_params=_cparams(("parallel",), 48),
        name="dft2",
    )(a5, f5, k["twc"], k["tws"], k["g_fwd"], k["g_inv"])


def _dft3_kernel(b_ref, m_ref, x0_ref, vx_ref, sc_ref, d_ref, o_ref):
    C = b_ref.shape[-1]
    bs = b_ref[...].reshape(m_ref.shape[1], C).astype(BF16)
    y = jnp.dot(m_ref[...], bs, preferred_element_type=F32).reshape(o_ref.shape)
    o_ref[...] = x0_ref[...] * (y * sc_ref[...] + vx_ref[...] * d_ref[...])


def _dft3(b5, m3, x05, vx5, scale, d):
    n2h, n1, _, sub, C = b5.shape
    Bt, n1h = x05.shape[0], x05.shape[1]
    tok = pl.BlockSpec((Bt, n1h, None, sub, C), lambda h: (0, 0, h, 0, 0))
    vec = pl.BlockSpec((1, C), lambda h: (0, 0))
    return pl.pallas_call(
        _dft3_kernel,
        grid=(n2h,),
        in_specs=[
            pl.BlockSpec((None, n1, 2, sub, C), lambda h: (h, 0, 0, 0, 0)),
            pl.BlockSpec(m3.shape, lambda h: (0, 0)),
            tok, tok, vec, vec,
        ],
        out_specs=tok,
        out_shape=jax.ShapeDtypeStruct(x05.shape, F32),
        compiler_params=_cparams(("parallel",), 48),
        name="dft3",
    )(b5, m3, x05, vx5, scale, d)


def _hyena(u, conv_w, conv_b, w1, b1, w2, b2, w3, b3, freq, wout, d_skip, B, L):
    T = u.shape[0]
    C = HYENA_WIDTH
    k = _dft_constants(L)
    n1, n1h = k["n1"], k["n1h"]
    x0, vx = _hpre(u, conv_w, conv_b[None], B, L)
    hfb, asum = _filter_taps(L, w1, b1, w2, b2, w3, b3, freq, wout)
    a5 = _dft1(vx, k["m1c"], n1, cb=C)
    f5 = _dft1(hfb, k["m1r"], n1, cb=C)
    b5 = _dft2(a5, f5, k)
    scale = 1.0 / (asum * (2 * L))
    return _dft3(b5, k["m3"], x0, vx, scale, d_skip[None])


PACKED = jnp.uint32


def _pack_pairs(x):
    half = x.shape[1] // 2
    bits = pltpu.bitcast(x.astype(BF16).astype(F32), PACKED)
    return (bits[:, :half] >> 16) | bits[:, half:]


def _unpack_pairs(w):
    lo = pltpu.bitcast(w << 16, F32)
    hi = pltpu.bitcast(w & jnp.uint32(0xFFFF0000), F32)
    return lo, hi


def _oproj_kernel(x_ref, g0_ref, b0_ref, at_ref, hy_ref, wo_ref, g1_ref, b1_ref, wrh_ref, wrl_ref,
                  h1_ref, h1p_ref, sc_ref):
    aw = at_ref.shape[1]
    h0 = _layer_norm_rows(x_ref[...], g0_ref[...], b0_ref[...])
    mixed = jnp.dot(at_ref[...], wo_ref[0:aw, :], preferred_element_type=F32)
    hy = hy_ref[...].reshape(x_ref.shape[0], hy_ref.shape[-1])
    mixed = mixed + jnp.dot(hy.astype(BF16), wo_ref[aw:, :], preferred_element_type=F32)
    h1 = _layer_norm_rows(ALPHA * h0 + mixed, g1_ref[...], b1_ref[...])
    h1_ref[...] = h1
    h1p_ref[...] = _pack_pairs(h1)
    hh = h1.astype(BF16)
    hl = (h1 - hh.astype(F32)).astype(BF16)
    dn = (((1,), (1,)), ((), ()))
    logits = lax.dot_general(wrh_ref[...], hh, dn, preferred_element_type=F32)
    logits = logits + (lax.dot_general(wrh_ref[...], hl, dn, preferred_element_type=F32)
                       + lax.dot_general(wrl_ref[...], hh, dn, preferred_element_type=F32))
    sc_ref[...] = 1.0 / (1.0 + jnp.exp(-logits))


def _oproj(x2, g0, b0, attn, hy, wo_bf, g1, b1, wr_t, tm=256):
    T, D = x2.shape
    E = wr_t.shape[0]
    wrh = wr_t.astype(BF16)
    wrl = (wr_t - wrh.astype(F32)).astype(BF16)
    row = lambda i: (i, 0)
    fixed = lambda i: (0, 0)
    hy4 = hy.reshape((-1,) + hy.shape[2:])
    return pl.pallas_call(
        _oproj_kernel,
        grid=(T // tm,),
        in_specs=[
            pl.BlockSpec((tm, D), row), pl.BlockSpec((1, D), fixed), pl.BlockSpec((1, D), fixed),
            pl.BlockSpec((tm, attn.shape[1]), row),
            pl.BlockSpec((tm // DFT_N2,) + hy4.shape[1:], lambda i: (i, 0, 0, 0)),
            pl.BlockSpec((D, D), fixed), pl.BlockSpec((1, D), fixed), pl.BlockSpec((1, D), fixed),
            pl.BlockSpec((E, D), fixed), pl.BlockSpec((E, D), fixed),
        ],
        out_specs=[pl.BlockSpec((tm, D), row), pl.BlockSpec((tm, D // 2), row), pl.BlockSpec((E, tm), lambda i: (0, i))],
        out_shape=[
            jax.ShapeDtypeStruct((T, D), F32),
            jax.ShapeDtypeStruct((T, D // 2), PACKED),
            jax.ShapeDtypeStruct((E, T), F32),
        ],
        compiler_params=_cparams(("parallel",), 48),
        name="oproj",
    )(x2, g0, b0, attn, hy4, wo_bf, g1, b1, wrh, wrl)


def _route_kernel(sc_ref, bias_ref, tri_ref, idx_ref, gate_ref, rank_ref, cnt_ref, carry_ref):
    step = pl.program_id(0)

    @pl.when(step == 0)
    def _():
        carry_ref[...] = jnp.zeros_like(carry_ref)

    tm = V7X_LANES
    for part in range(sc_ref.shape[1] // tm):
        cols = slice(part * tm, (part + 1) * tm)
        _route_tokens(sc_ref[:, cols], bias_ref, tri_ref, idx_ref, gate_ref, rank_ref, carry_ref, cols)
    cnt_ref[...] = carry_ref[...].astype(jnp.int32)


def _route_tokens(scores, bias_ref, tri_ref, idx_ref, gate_ref, rank_ref, carry_ref, cols):
    E, tm = scores.shape
    neg = jnp.float32(-jnp.inf)
    biased = scores + bias_ref[...]
    erow = lax.broadcasted_iota(jnp.int32, (E, tm), 0)
    big = jnp.int32(E)

    def first_argmax(vals, rows):
        m = jnp.max(vals, axis=0, keepdims=True)
        pick = jnp.min(jnp.where(vals == m, rows, big), axis=0, keepdims=True)
        return m, pick

    gsc = []
    for g in range(N_GROUPS):
        blk = biased[g * GROUP_SIZE:(g + 1) * GROUP_SIZE, :]
        rows = erow[g * GROUP_SIZE:(g + 1) * GROUP_SIZE, :]
        m1, p1 = first_argmax(blk, rows)
        m2 = jnp.max(jnp.where(rows == p1, neg, blk), axis=0, keepdims=True)
        gsc.append(m1 + m2)
    gsc = jnp.concatenate(gsc, axis=0)
    grow = lax.broadcasted_iota(jnp.int32, (N_GROUPS, tm), 0)
    gsel = jnp.zeros((N_GROUPS, tm), jnp.bool_)
    work = gsc
    for _ in range(TOPK_GROUPS):
        _, p = first_argmax(work, grow)
        hit = grow == p
        gsel = gsel | hit
        work = jnp.where(hit, neg, work)
    emask = jnp.concatenate(
        [jnp.broadcast_to(gsel[g:g + 1, :], (GROUP_SIZE, tm)) for g in range(N_GROUPS)], axis=0)
    work = jnp.where(emask, biased, neg)

    sel = jnp.zeros((E, tm), jnp.bool_)
    picks, gvals = [], []
    for _ in range(TOP_K):
        _, p = first_argmax(work, erow)
        hit = erow == p
        sel = sel | hit
        picks.append(p)
        gvals.append(jnp.sum(jnp.where(hit, scores, 0.0), axis=0, keepdims=True))
        work = jnp.where(hit, neg, work)
    gv = jnp.concatenate(gvals, axis=0)
    idx_ref[:, cols] = jnp.concatenate(picks, axis=0)
    gate_ref[:, cols] = gv / jnp.sum(gv, axis=0, keepdims=True) * ROUTED_SCALE

    chosen = sel.astype(F32)
    before = jnp.dot(chosen.astype(BF16), tri_ref[...], preferred_element_type=F32) + carry_ref[...]
    rank_ref[:, cols] = jnp.concatenate(
        [jnp.sum(jnp.where(erow == p, before, 0.0), axis=0, keepdims=True) for p in picks], axis=0).astype(jnp.int32)
    carry_ref[...] += jnp.sum(chosen, axis=1, keepdims=True)


def _route(scores_t, bias, tm=256):
    E, T = scores_t.shape
    tm = min(tm, T)
    lanes = V7X_LANES
    tri = (jnp.arange(lanes)[:, None] < jnp.arange(lanes)[None, :]).astype(BF16)
    tok = lambda i: (0, i)
    fixed = lambda i: (0, 0)
    return pl.pallas_call(
        _route_kernel,
        grid=(T // tm,),
        in_specs=[pl.BlockSpec((E, tm), tok), pl.BlockSpec((E, 1), fixed), pl.BlockSpec((lanes, lanes), fixed)],
        out_specs=[pl.BlockSpec((TOP_K, tm), tok), pl.BlockSpec((TOP_K, tm), tok), pl.BlockSpec((TOP_K, tm), tok),
                   pl.BlockSpec((E, 1), fixed)],
        out_shape=[
            jax.ShapeDtypeStruct((TOP_K, T), jnp.int32),
            jax.ShapeDtypeStruct((TOP_K, T), F32),
            jax.ShapeDtypeStruct((TOP_K, T), jnp.int32),
            jax.ShapeDtypeStruct((E, 1), jnp.int32),
        ],
        scratch_shapes=[pltpu.VMEM((E, 1), F32)],
        compiler_params=_cparams(("arbitrary",), 32),
        name="route",
    )(scores_t, bias, tri)


def _dest_kernel(idx_ref, rank_ref, ps_ref, dest_ref):
    E = ps_ref.shape[0]
    tm = idx_ref.shape[1]
    erow = lax.broadcasted_iota(jnp.int32, (E, tm), 0)
    ps = ps_ref[...].astype(F32)
    rows = [jnp.sum(jnp.where(erow == idx_ref[k:k + 1, :], ps, 0.0), axis=0, keepdims=True) for k in range(TOP_K)]
    dest_ref[...] = jnp.concatenate(rows, axis=0).astype(jnp.int32) + rank_ref[...]


def _dest_rows(idx, rank, pad_start, tm=512):
    K, T = idx.shape
    E = pad_start.shape[0]
    tok = lambda i: (0, i)
    return pl.pallas_call(
        _dest_kernel,
        grid=(T // tm,),
        in_specs=[pl.BlockSpec((K, tm), tok), pl.BlockSpec((K, tm), tok), pl.BlockSpec((E, 1), lambda i: (0, 0))],
        out_specs=pl.BlockSpec((K, tm), tok),
        out_shape=jax.ShapeDtypeStruct((K, T), jnp.int32),
        compiler_params=_cparams(("arbitrary",), 32),
        name="dest",
    )(idx, rank, pad_start[:, None])


def _scatter_kernel(ps_ref, pe_ref, dest_ref, h_ref, xs_ref, zbuf, sem, *, rb):
    sub = V7X_SUBLANES
    tm = h_ref.shape[0]

    @pl.when(pl.program_id(0) == 0)
    def _():
        zbuf[...] = jnp.zeros_like(zbuf)

        def zcopy(e):
            return pltpu.make_async_copy(zbuf, xs_ref.at[pl.ds(pl.multiple_of(pe_ref[e] - rb, rb), rb)], sem)

        def zstart(e, c):
            @pl.when(pe_ref[e] > ps_ref[e])
            def _():
                zcopy(e).start()
            return c

        def zwait(e, c):
            @pl.when(pe_ref[e] > ps_ref[e])
            def _():
                zcopy(e).wait()
            return c

        lax.fori_loop(0, ps_ref.shape[0], zstart, 0)
        lax.fori_loop(0, ps_ref.shape[0], zwait, 0)

    def issue(r8, c):
        for j in range(sub):
            for k in range(TOP_K):
                dst = dest_ref[0, r8 * (sub * TOP_K) + (j * TOP_K + k)]
                pltpu.make_async_copy(h_ref.at[pl.ds(r8 * sub + j, 1)], xs_ref.at[pl.ds(dst, 1)], sem).start(priority=k % 2)
        return c

    lax.fori_loop(0, tm // sub, issue, 0)
    for k in range(TOP_K):
        pltpu.make_async_copy(h_ref, xs_ref.at[pl.ds(0, tm)], sem).wait()


def _dest_tiles(dest, tm):
    K, T = dest.shape
    return dest.T.reshape(T // tm, 1, tm * K)


def _scatter_rows(pad_start, pad_end, dest_t, h1p, n_rows, rb, tm=256):
    T, W = h1p.shape
    return pl.pallas_call(
        functools.partial(_scatter_kernel, rb=rb),
        grid_spec=pltpu.PrefetchScalarGridSpec(
            num_scalar_prefetch=2,
            grid=(T // tm,),
            in_specs=[
                pl.BlockSpec((None, 1, tm * TOP_K), lambda i, ps, pe: (i, 0, 0), memory_space=pltpu.SMEM),
                pl.BlockSpec((tm, W), lambda i, ps, pe: (i, 0)),
            ],
            out_specs=pl.BlockSpec(memory_space=pl.ANY),
            scratch_shapes=[pltpu.VMEM((rb, W), h1p.dtype), pltpu.SemaphoreType.DMA],
        ),
        out_shape=jax.ShapeDtypeStruct((n_rows, W), h1p.dtype),
        compiler_params=_cparams(("arbitrary",), 32),
        name="scatter",
    )(pad_start, pad_end, dest_t, h1p)


EXPERT_X_SLOTS = 4
EXPERT_Y_SLOTS = 3


def _experts_kernel(first_ref, nblk_ref, wsel_ref, tot_ref, xs_ref, wg_ref, wu_ref, wd_ref, ys_ref,
                    xbuf, ybuf, wgub, wdb, xsem, ysem, *, rb):
    del wsel_ref
    e = pl.program_id(0)
    F = wg_ref.shape[1]
    nx, ny = xbuf.shape[0], ybuf.shape[0]
    total = tot_ref[0]
    n = nblk_ref[e]
    g0 = first_ref[e]

    def x_copy(g):
        slot = g % nx
        return pltpu.make_async_copy(xs_ref.at[pl.ds(pl.multiple_of(g * rb, rb), rb)], xbuf.at[slot], xsem.at[slot])

    def y_copy(g):
        slot = g % ny
        return pltpu.make_async_copy(ybuf.at[slot], ys_ref.at[pl.ds(pl.multiple_of(g * rb, rb), rb)], ysem.at[slot])

    @pl.when(e == 0)
    def _():
        for g in range(nx):
            @pl.when(g < total)
            def _():
                x_copy(g).start()

    def process(g, count):
        for d in range(count):
            x_copy(g + d).wait()
        ys = []
        for d in range(count):
            lo, hi = _unpack_pairs(xbuf[(g + d) % nx])
            x = jnp.concatenate([lo, hi], axis=1).astype(BF16)
            gu = jnp.dot(x, wgub[...], preferred_element_type=F32)
            gate, up = gu[:, 0:F], gu[:, F:]
            hb = (gate / (1.0 + jnp.exp(-gate)) * up).astype(BF16)
            ys.append(_pack_pairs(jnp.dot(hb, wdb[...], preferred_element_type=F32)))
        for d in range(count):
            @pl.when(g + d >= ny)
            def _():
                y_copy(g + d - ny).wait()

            ybuf[(g + d) % ny] = ys[d]
            y_copy(g + d).start()
        for d in range(count):
            @pl.when(g + d + nx < total)
            def _():
                x_copy(g + d + nx).start()

    @pl.when(n > 0)
    def _():
        wgub[:, 0:F] = wg_ref[...].astype(BF16)
        wgub[:, F:] = wu_ref[...].astype(BF16)
        wdb[...] = wd_ref[...].astype(BF16)

        def pair(j, c):
            process(g0 + 2 * j, 2)
            return c

        lax.fori_loop(0, n // 2, pair, 0)

        @pl.when(n % 2 == 1)
        def _():
            process(g0 + n - 1, 1)

    @pl.when(e == pl.num_programs(0) - 1)
    def _():
        for back in range(ny, 0, -1):
            @pl.when(total >= back)
            def _():
                y_copy(total - back).wait()


def _experts(first_blk, nblk, wsel, total_blk, xs, w_gate, w_up, w_down, rb):
    P, W = xs.shape
    E, D, F = w_gate.shape
    wspec = lambda shape: pl.BlockSpec((None,) + shape, lambda e, fb, nbk, ws, tt: (ws[e], 0, 0))
    return pl.pallas_call(
        functools.partial(_experts_kernel, rb=rb),
        grid_spec=pltpu.PrefetchScalarGridSpec(
            num_scalar_prefetch=4,
            grid=(E,),
            in_specs=[pl.BlockSpec(memory_space=pl.ANY), wspec((D, F)), wspec((D, F)), wspec((F, D))],
            out_specs=pl.BlockSpec(memory_space=pl.ANY),
            scratch_shapes=[
                pltpu.VMEM((EXPERT_X_SLOTS, rb, W), PACKED), pltpu.VMEM((EXPERT_Y_SLOTS, rb, W), PACKED),
                pltpu.VMEM((D, 2 * F), BF16), pltpu.VMEM((F, D), BF16),
                pltpu.SemaphoreType.DMA((EXPERT_X_SLOTS,)), pltpu.SemaphoreType.DMA((EXPERT_Y_SLOTS,)),
            ],
        ),
        out_shape=jax.ShapeDtypeStruct((P, W), PACKED),
        compiler_params=_cparams(("arbitrary",), 48),
        name="experts",
    )(first_blk, nblk, wsel, total_blk, xs, w_gate, w_up, w_down)


def _combine_kernel(dest_ref, destn_ref, h1_ref, gate_ref, ys_ref, ys8_ref, sg_ref, su_ref, sd_ref,
                    g2_ref, b2_ref, o_ref, buf, sem):
    i = pl.program_id(0)
    n = pl.num_programs(0)
    tm = h1_ref.shape[0]
    slot = i % 2

    sub = V7X_SUBLANES

    def issue_rows(dref, s, r0, nrows):
        for r in range(r0, r0 + nrows):
            for k in range(TOP_K):
                src = ys_ref.at[pl.ds(dref[0, r * TOP_K + k], 1)]
                pltpu.make_async_copy(src, buf.at[s, r, pl.ds(k, 1)], sem.at[s]).start(priority=k % 2)

    def wait_tile(s):
        pltpu.make_async_copy(ys8_ref.at[pl.ds(0, tm)], buf.at[s], sem.at[s]).wait()

    @pl.when(i == 0)
    def _():
        def body(r8, c):
            for j in range(sub):
                for k in range(TOP_K):
                    src = ys_ref.at[pl.ds(dest_ref[0, r8 * (sub * TOP_K) + (j * TOP_K + k)], 1)]
                    pltpu.make_async_copy(src, buf.at[0, r8 * sub + j, pl.ds(k, 1)], sem.at[0]).start(priority=k % 2)
            return c
        lax.fori_loop(0, tm // sub, body, 0)

    wait_tile(slot)
    parts = 4
    rows_per = tm // parts
    nk = rows_per * TOP_K
    pick = lax.broadcasted_iota(jnp.int32, (TOP_K, nk), 1) % TOP_K == lax.broadcasted_iota(jnp.int32, (TOP_K, nk), 0)
    spread = pick.astype(BF16)
    own = lax.broadcasted_iota(jnp.int32, (rows_per, nk), 1) // TOP_K == lax.broadcasted_iota(jnp.int32, (rows_per, nk), 0)
    for part in range(parts):
        r0 = part * rows_per
        issue_rows(destn_ref, 1 - slot, r0, rows_per)
        rows = slice(r0, r0 + rows_per)
        h1 = h1_ref[rows, :]
        hb = h1.astype(BF16)
        g = jnp.dot(hb, sg_ref[...], preferred_element_type=F32)
        u = jnp.dot(hb, su_ref[...], preferred_element_type=F32)
        ffn = jnp.dot((g / (1.0 + jnp.exp(-g)) * u).astype(BF16), sd_ref[...], preferred_element_type=F32)
        lo, hi = _unpack_pairs(buf[slot, rows].reshape(nk, buf.shape[3]))
        y = jnp.concatenate([lo, hi], axis=1).astype(BF16)
        gates = gate_ref[rows, :]
        gh = gates.astype(BF16)
        gl = (gates - gh.astype(F32)).astype(BF16)
        routed = jnp.zeros((rows_per, y.shape[1]), F32)
        for piece in (gh, gl):
            gm = jnp.where(own, jnp.dot(piece, spread, preferred_element_type=F32), 0.0).astype(BF16)
            routed = routed + jnp.dot(gm, y, preferred_element_type=F32)
        o_ref[rows, :] = _layer_norm_rows(ALPHA * h1 + ffn + routed, g2_ref[...], b2_ref[...])

    @pl.when(i == n - 1)
    def _():
        wait_tile(1 - slot)


def _combine(dest, h1, gates_tk, ys, sg, su, sd, g2, b2, tm=128):
    T, D = h1.shape
    W = ys.shape[1]
    F = sg.shape[1]
    nt = T // tm
    row = lambda i: (i, 0)
    fixed = lambda i: (0, 0)
    smem = functools.partial(pl.BlockSpec, (None, 1, tm * TOP_K), memory_space=pltpu.SMEM)
    dest_t = _dest_tiles(dest, tm)
    ys8 = ys.reshape(ys.shape[0] // TOP_K, TOP_K, W)
    return pl.pallas_call(
        _combine_kernel,
        grid=(nt,),
        in_specs=[
            smem(index_map=lambda i: (i, 0, 0)),
            smem(index_map=lambda i: (jnp.minimum(i + 1, nt - 1), 0, 0)),
            pl.BlockSpec((tm, D), row),
            pl.BlockSpec((tm, TOP_K), row),
            pl.BlockSpec(memory_space=pl.ANY),
            pl.BlockSpec(memory_space=pl.ANY),
            pl.BlockSpec((D, F), fixed), pl.BlockSpec((D, F), fixed), pl.BlockSpec((F, D), fixed),
            pl.BlockSpec((1, D), fixed), pl.BlockSpec((1, D), fixed),
        ],
        out_specs=pl.BlockSpec((tm, D), row),
        out_shape=jax.ShapeDtypeStruct((T, D), F32),
        scratch_shapes=[pltpu.VMEM((2, tm, TOP_K, W), PACKED), pltpu.SemaphoreType.DMA((2,))],
        compiler_params=_cparams(("arbitrary",), 48),
        name="combine",
    )(dest_t, dest_t, h1, gates_tk, ys, ys8, sg, su, sd, g2, b2)


MOE_ROWS = 256


def _moe(h1, h1p, scores_t, router_bias, w_gate, w_up, w_down, ws_gate, ws_up, ws_down, g2, b2):
    T = h1.shape[0]
    E = N_EXPERTS
    rb = MOE_ROWS
    idx, gates, rank, counts = _route(scores_t, router_bias[:, None])
    counts = counts[:, 0]
    padded = (counts + rb - 1) // rb * rb
    pad_end = jnp.cumsum(padded)
    pad_start = pad_end - padded
    nb = (T * TOP_K) // rb + E
    nblk = (padded // rb).astype(jnp.int32)
    first_blk = (pad_start // rb).astype(jnp.int32)
    total_blk = (pad_end[-1:] // rb).astype(jnp.int32)
    wsel = lax.cummax(jnp.where(nblk > 0, jnp.arange(E, dtype=jnp.int32), 0))
    dest = _dest_rows(idx, rank, pad_start)
    xs = _scatter_rows(pad_start, pad_end, _dest_tiles(dest, 256), h1p, nb * rb, rb, tm=256)
    ys = _experts(first_blk, nblk, wsel, total_blk, xs, w_gate, w_up, w_down, rb)
    return _combine(dest, h1, gates.T, ys, ws_gate.astype(BF16), ws_up.astype(BF16), ws_down.astype(BF16), g2, b2)


def kernel(x, positions, emb_ln_g, emb_ln_b, w_in, hy_conv_w, hy_conv_b, hy_f_w1, hy_f_b1, hy_f_w2, hy_f_b2, hy_f_w3, hy_f_b3, hy_f_freq, hy_f_wout, hy_d, lambda_q1, lambda_k1, lambda_q2, lambda_k2, subln_g, w_o, ln1_g, ln1_b, w_router, router_bias, w_gate, w_up, w_down, ws_gate, ws_up, ws_down, ln2_g, ln2_b):
    B, L, D = x.shape
    T = B * L
    assert w_in.shape[0] == DEPTH == 1
    i = 0
    x2 = x.reshape(T, D)
    g0, b0 = emb_ln_g[None], emb_ln_b[None]
    ra, rm, rp = _rotary_tables(positions)
    q, k, v, u = _inproj(x2, g0, b0, w_in[i].astype(BF16), ra, rm, rp)
    lam = (jnp.exp(jnp.sum(lambda_q1[i] * lambda_k1[i])) - jnp.exp(jnp.sum(lambda_q2[i] * lambda_k2[i])) + LAM_INIT)
    attn = _attention(q, k, v, lam.reshape(1).astype(F32), subln_g[i][None], B, L)
    hy = _hyena(u, hy_conv_w[i], hy_conv_b[i], hy_f_w1[i], hy_f_b1[i], hy_f_w2[i], hy_f_b2[i], hy_f_w3[i], hy_f_b3[i],
                hy_f_freq[i], hy_f_wout[i], hy_d[i], B, L)
    h1, h1p, scores_t = _oproj(x2, g0, b0, attn, hy, w_o[i].astype(BF16), ln1_g[i][None], ln1_b[i][None], w_router[i].T)
    out = _moe(h1, h1p, scores_t, router_bias[i], w_gate[i], w_up[i], w_down[i], ws_gate[i], ws_up[i], ws_down[i],
               ln2_g[i][None], ln2_b[i][None])
    return out.reshape(B, L, D)
```

```python
import functools
import math

import numpy as np
import jax
import jax.numpy as jnp
from jax import lax
from jax.experimental import pallas as pl
from jax.experimental.pallas import tpu as pltpu

DA_HEADS = 4
DA_HEAD_DIM = 64
DA_V_DIM = 128
ATTN_WIDTH = 512
HYENA_WIDTH = 512
ROT_DIM = 16
ROPE_THETA = 500000.0
SHORT_CONV = 3
FILTER_EMB = 33
FILTER_BANDS = 16
DECAY_TARGET = 1e-2
FAST_DECAY = 0.3
SLOW_DECAY = 1.5
N_EXPERTS = 256
TOP_K = 8
N_GROUPS = 8
GROUP_SIZE = N_EXPERTS // N_GROUPS
TOPK_GROUPS = 4
EXPERT_DIM = 256
ROUTED_SCALE = 2.5
DEPTH = 1
ALPHA = (2 * DEPTH) ** 0.25
LN_EPS = 1e-5
LAM_INIT = 0.8 - 0.6 * math.exp(-0.3 * 0)

V7X_LANES = 128
V7X_SUBLANES = 8
V7X_VMEM_BYTES = 64 * 1024 * 1024

DFT_N2 = 128
DFT_N2H = DFT_N2 // V7X_SUBLANES

BF16 = jnp.bfloat16
F32 = jnp.float32


def _cparams(sem, vmem_mb):
    return pltpu.CompilerParams(dimension_semantics=sem, vmem_limit_bytes=vmem_mb * 1024 * 1024)


def _layer_norm_rows(x, g, b):
    mu = jnp.mean(x, axis=-1, keepdims=True)
    xc = x - mu
    var = jnp.mean(xc * xc, axis=-1, keepdims=True)
    return xc * lax.rsqrt(var + LN_EPS) * g + b


def _inproj_kernel(x_ref, g_ref, b_ref, w_ref, ra_ref, rm_ref, rp_ref, q_ref, k_ref, v_ref, u_ref):
    h = _layer_norm_rows(x_ref[...], g_ref[...], b_ref[...]).astype(BF16)
    ra, rm, rp = ra_ref[...], rm_ref[...], rp_ref[...]

    def rot(t):
        return t * ra + pltpu.roll(t, V7X_LANES - ROT_DIM // 2, axis=1) * rm + pltpu.roll(t, ROT_DIM // 2, axis=1) * rp

    aw = ATTN_WIDTH
    qp = jnp.dot(h, w_ref[:, 0:aw], preferred_element_type=F32)
    kp = jnp.dot(h, w_ref[:, aw:2 * aw], preferred_element_type=F32)
    scale = DA_HEAD_DIM ** -0.5 * math.log2(math.e)
    for c in range(aw // V7X_LANES):
        sl = slice(c * V7X_LANES, (c + 1) * V7X_LANES)
        q_ref[:, sl] = (rot(qp[:, sl]) * scale).astype(BF16)
        k_ref[:, sl] = rot(kp[:, sl]).astype(BF16)
    vp = jnp.dot(h, w_ref[:, 2 * aw:3 * aw], preferred_element_type=F32).astype(BF16)
    ones = jnp.ones((vp.shape[0], DA_V_DIM), BF16)
    for hd in range(DA_HEADS):
        v_ref[:, 2 * hd * DA_V_DIM:(2 * hd + 1) * DA_V_DIM] = vp[:, hd * DA_V_DIM:(hd + 1) * DA_V_DIM]
        v_ref[:, (2 * hd + 1) * DA_V_DIM:(2 * hd + 2) * DA_V_DIM] = ones
    u_ref[...] = jnp.dot(h, w_ref[:, 3 * aw:], preferred_element_type=F32)


def _inproj(x2, g, b, w_bf, ra, rm, rp, tm=256):
    T, D = x2.shape
    ncol = w_bf.shape[1]
    aw = ATTN_WIDTH
    uw = ncol - 3 * aw
    row = lambda i: (i, 0)
    fixed = lambda i: (0, 0)
    return pl.pallas_call(
        _inproj_kernel,
        grid=(T // tm,),
        in_specs=[
            pl.BlockSpec((tm, D), row),
            pl.BlockSpec((1, D), fixed),
            pl.BlockSpec((1, D), fixed),
            pl.BlockSpec((D, ncol), fixed),
            pl.BlockSpec((tm, V7X_LANES), row),
            pl.BlockSpec((tm, V7X_LANES), row),
            pl.BlockSpec((tm, V7X_LANES), row),
        ],
        out_specs=[
            pl.BlockSpec((tm, aw), row),
            pl.BlockSpec((tm, aw), row),
            pl.BlockSpec((tm, 2 * aw), row),
            pl.BlockSpec((tm, uw), row),
        ],
        out_shape=[
            jax.ShapeDtypeStruct((T, aw), BF16),
            jax.ShapeDtypeStruct((T, aw), BF16),
            jax.ShapeDtypeStruct((T, 2 * aw), BF16),
            jax.ShapeDtypeStruct((T, uw), F32),
        ],
        compiler_params=_cparams(("parallel",), 48),
        name="inproj",
    )(x2, g, b, w_bf, ra, rm, rp)


def _rotary_tables(positions):
    half = ROT_DIM // 2
    inv_freq = ROPE_THETA ** (-jnp.arange(0, ROT_DIM, 2, dtype=F32) / ROT_DIM)
    ang = positions.astype(F32).reshape(-1)[:, None] * inv_freq
    cos, sin = jnp.cos(ang), jnp.sin(ang)
    T = ang.shape[0]
    ones = jnp.ones((T, DA_HEAD_DIM - ROT_DIM), F32)
    zeros_h = jnp.zeros((T, half), F32)
    zeros_r = jnp.zeros((T, DA_HEAD_DIM - ROT_DIM), F32)
    a64 = jnp.concatenate([cos, cos, ones], axis=1)
    m64 = jnp.concatenate([-sin, zeros_h, zeros_r], axis=1)
    p64 = jnp.concatenate([zeros_h, sin, zeros_r], axis=1)
    rep = V7X_LANES // DA_HEAD_DIM
    return jnp.tile(a64, (1, rep)), jnp.tile(m64, (1, rep)), jnp.tile(p64, (1, rep))


def _attn_kernel(lam_ref, q_ref, k_ref, v_ref, g_ref, o_ref, s_ref, m_ref, acc_ref, *, kb, unroll):
    qb = q_ref.shape[0]
    L = k_ref.shape[0]
    nchunk = L // kb
    nl = kb // V7X_LANES
    q = q_ref[...]
    lane = lax.broadcasted_iota(jnp.int32, q.shape, 1)
    zero = jnp.zeros_like(q)
    qm = [jnp.where(lane < DA_HEAD_DIM, q, zero), jnp.where(lane >= DA_HEAD_DIM, q, zero)]
    m_ref[...] = jnp.full(m_ref.shape, -jnp.inf, F32)
    acc_ref[...] = jnp.zeros(acc_ref.shape, F32)

    def score_body(j, carry):
        kc = k_ref[pl.ds(pl.multiple_of(j * kb, kb), kb), :]
        for c in range(2):
            s = lax.dot_general(qm[c], kc, (((1,), (1,)), ((), ())), preferred_element_type=F32)
            s_ref[c, j] = s
            m = s[:, 0:V7X_LANES]
            for t in range(1, nl):
                m = jnp.maximum(m, s[:, t * V7X_LANES:(t + 1) * V7X_LANES])
            m_ref[c] = jnp.maximum(m_ref[c], m)
        return carry

    lax.fori_loop(0, nchunk, score_body, 0, unroll=unroll)
    m_row = [jnp.max(m_ref[c], axis=1, keepdims=True) for c in range(2)]

    def pv_body(j, carry):
        vc = v_ref[pl.ds(pl.multiple_of(j * kb, kb), kb), :]
        for c in range(2):
            p = jnp.exp2(s_ref[c, j] - m_row[c])
            acc_ref[c] += jnp.dot(p.astype(BF16), vc, preferred_element_type=F32)
        return carry

    lax.fori_loop(0, nchunk, pv_body, 0, unroll=unroll)
    outs = [acc_ref[c, :, 0:DA_V_DIM] / acc_ref[c, :, DA_V_DIM:2 * DA_V_DIM] for c in range(2)]
    o = outs[0] - lam_ref[0] * outs[1]
    ms = jnp.mean(o * o, axis=1, keepdims=True)
    o_ref[...] = (o * lax.rsqrt(ms + LN_EPS) * g_ref[...] * (1.0 - LAM_INIT)).astype(o_ref.dtype)


def _attention(q, k, v, lam, subln_g, B, L, qb=512, kb=1024, unroll=8):
    T = q.shape[0]
    nq = L // qb
    return pl.pallas_call(
        functools.partial(_attn_kernel, kb=kb, unroll=unroll),
        grid=(B, DA_HEADS, nq),
        in_specs=[
            pl.BlockSpec(memory_space=pltpu.SMEM),
            pl.BlockSpec((qb, DA_V_DIM), lambda b, h, i: (b * nq + i, h)),
            pl.BlockSpec((L, DA_V_DIM), lambda b, h, i: (b, h)),
            pl.BlockSpec((L, 2 * DA_V_DIM), lambda b, h, i: (b, h)),
            pl.BlockSpec((1, DA_V_DIM), lambda b, h, i: (0, 0)),
        ],
        out_specs=pl.BlockSpec((qb, DA_V_DIM), lambda b, h, i: (b * nq + i, h)),
        out_shape=jax.ShapeDtypeStruct((T, ATTN_WIDTH), BF16),
        scratch_shapes=[
            pltpu.VMEM((2, L // kb, qb, kb), F32),
            pltpu.VMEM((2, qb, V7X_LANES), F32),
            pltpu.VMEM((2, qb, 2 * DA_V_DIM), F32),
        ],
        compiler_params=_cparams(("parallel", "parallel", "parallel"), 56),
        name="attn",
    )(lam, q, k, v, subln_g)


def _hpre_kernel(u_ref, up_ref, un_ref, w_ref, b_ref, x0_ref, vx_ref):
    i = pl.program_id(1)
    n = pl.num_programs(1)
    tb = u_ref.shape[0]
    C = x0_ref.shape[-1]
    row = lax.broadcasted_iota(jnp.int32, (tb, V7X_LANES), 0)
    has_prev = (i > 0).astype(F32)
    has_next = (i < n - 1).astype(F32)

    def conv(c0):
        sl = slice(c0, c0 + V7X_LANES)
        u = u_ref[:, sl]
        prev_row = up_ref[V7X_SUBLANES - 1:V7X_SUBLANES, sl] * has_prev
        next_row = un_ref[0:1, sl] * has_next
        u_prev = jnp.where(row == 0, prev_row, pltpu.roll(u, 1, axis=0))
        u_next = jnp.where(row == tb - 1, next_row, pltpu.roll(u, tb - 1, axis=0))
        return u_prev * w_ref[0:1, sl] + u * w_ref[1:2, sl] + u_next * w_ref[2:3, sl] + b_ref[:, sl]

    tile = x0_ref.shape[:-1] + (V7X_LANES,)
    for c in range(C // V7X_LANES):
        c0 = c * V7X_LANES
        x0_ref[:, :, :, c0:c0 + V7X_LANES] = conv(c0).reshape(tile)
        vx_ref[:, :, :, c0:c0 + V7X_LANES] = (conv(2 * C + c0) * conv(C + c0)).reshape(tile)


def _hpre(u, conv_w, conv_b, B, L, tb=512):
    T, C3 = u.shape
    C = C3 // 3
    nt = L // tb
    sub = V7X_SUBLANES
    na = tb // DFT_N2
    split = pl.BlockSpec((None, na, DFT_N2H, sub, C), lambda b, i: (b, i, 0, 0, 0))
    split_shape = jax.ShapeDtypeStruct((B, L // DFT_N2, DFT_N2H, sub, C), F32)
    cur = lambda b, i: (b * nt + i, 0)
    prev = lambda b, i: (jnp.maximum((b * L + i * tb) // sub - 1, 0), 0)
    nxt = lambda b, i: (jnp.minimum((b * L + (i + 1) * tb) // sub, T // sub - 1), 0)
    fixed = lambda b, i: (0, 0)
    return pl.pallas_call(
        _hpre_kernel,
        grid=(B, nt),
        in_specs=[
            pl.BlockSpec((tb, C3), cur),
            pl.BlockSpec((sub, C3), prev),
            pl.BlockSpec((sub, C3), nxt),
            pl.BlockSpec((SHORT_CONV, C3), fixed),
            pl.BlockSpec((1, C3), fixed),
        ],
        out_specs=[split, split],
        out_shape=[split_shape, split_shape],
        compiler_params=_cparams(("parallel", "parallel"), 32),
        name="hpre",
    )(u, u, u, conv_w, conv_b)


def _filt_kernel(z_ref, w1_ref, b1_ref, w2_ref, b2_ref, w3_ref, b3_ref, fr_ref, wo_ref, dl_ref,
                 hfb_ref, asum_ref, *, L):
    i = pl.program_id(0)
    tl = z_ref.shape[0]
    C = dl_ref.shape[1]
    hp = lax.Precision.HIGHEST
    h = jnp.sin(fr_ref[0:1, :] * (jnp.dot(z_ref[...], w1_ref[...], precision=hp, preferred_element_type=F32) + b1_ref[...]))
    h = jnp.sin(fr_ref[1:2, :] * (jnp.dot(h, w2_ref[...], precision=hp, preferred_element_type=F32) + b2_ref[...]))
    h = jnp.sin(fr_ref[2:3, :] * (jnp.dot(h, w3_ref[...], precision=hp, preferred_element_type=F32) + b3_ref[...]))
    o = jnp.dot(h, wo_ref[...], precision=hp, preferred_element_type=F32)
    grow = lax.broadcasted_iota(jnp.int32, (tl, C), 0) + i * tl
    t = grow.astype(F32) * (1.0 / (L - 1))
    decay = jnp.exp(-t * dl_ref[...])
    hf = o[:, :C] * decay
    hb = jnp.where(grow == 0, 0.0, o[:, C:] * decay)
    tile = hfb_ref.shape[:-1] + (C,)
    hfb_ref[:, :, :, :C] = hf.reshape(tile)
    hfb_ref[:, :, :, C:] = hb.reshape(tile)

    @pl.when(i == 0)
    def _():
        asum_ref[...] = jnp.zeros_like(asum_ref)

    asum_ref[...] += jnp.sum(jnp.abs(hf) + jnp.abs(hb), axis=0, keepdims=True)


def _filter_taps(L, w1, b1, w2, b2, w3, b3, freq, wout, tl=512):
    C = wout.shape[1] // 2
    order = w1.shape[1]
    emb = w1.shape[0]
    t = jnp.linspace(0.0, 1.0, L, dtype=F32)[:, None]
    w = 2.0 * math.pi * jnp.arange(L, dtype=F32)[:, None] / L
    f = jnp.linspace(1e-4, FILTER_BANDS - 1, FILTER_BANDS, dtype=F32)[None, :]
    z = jnp.concatenate([t, jnp.cos(f * w), -jnp.sin(f * w)], axis=-1)
    zp = jnp.pad(z, ((0, 0), (0, V7X_LANES - emb)))
    w1p = jnp.pad(w1, ((0, V7X_LANES - emb), (0, 0)))
    deltas = jnp.abs(jnp.linspace(math.log(DECAY_TARGET) / SLOW_DECAY, math.log(DECAY_TARGET) / FAST_DECAY, C, dtype=F32))[None]
    fixed = lambda i: (0, 0)
    return pl.pallas_call(
        functools.partial(_filt_kernel, L=L),
        grid=(L // tl,),
        in_specs=[
            pl.BlockSpec((tl, V7X_LANES), lambda i: (i, 0)),
            pl.BlockSpec((V7X_LANES, order), fixed), pl.BlockSpec((1, order), fixed),
            pl.BlockSpec((order, order), fixed), pl.BlockSpec((1, order), fixed),
            pl.BlockSpec((order, order), fixed), pl.BlockSpec((1, order), fixed),
            pl.BlockSpec((3, order), fixed),
            pl.BlockSpec((order, 2 * C), fixed),
            pl.BlockSpec((1, C), fixed),
        ],
        out_specs=[pl.BlockSpec((None, tl // DFT_N2, DFT_N2H, V7X_SUBLANES, 2 * C), lambda i: (0, i, 0, 0, 0)),
                   pl.BlockSpec((1, C), fixed)],
        out_shape=[jax.ShapeDtypeStruct((1, L // DFT_N2, DFT_N2H, V7X_SUBLANES, 2 * C), F32),
                   jax.ShapeDtypeStruct((1, C), F32)],
        compiler_params=_cparams(("arbitrary",), 32),
        name="filt",
    )(zp, w1p, b1[None], w2, b2[None], w3, b3[None], freq, wout, deltas)


def _dft_constants(L):
    n1 = 2 * L // DFT_N2
    n1h = n1 // 2
    N = 2 * L
    sub = V7X_SUBLANES

    def cs(num, den):
        ang = (num % den).astype(np.float64) * (2.0 * np.pi / den)
        return np.cos(ang), np.sin(ang)

    k1 = np.arange(n1, dtype=np.int64)
    a = np.arange(n1h, dtype=np.int64)
    c, s = cs(k1[:, None] * a[None, :], n1)
    eye = np.eye(sub)

    def expand(m):
        r, kk = m.shape
        return (m[:, None, :, None] * eye[None, :, None, :]).reshape(r * sub, kk * sub)

    def const(m):
        return jnp.asarray(m.astype(np.float32).astype(BF16))

    m1c = np.stack([np.concatenate([c, s], axis=1), np.concatenate([-s, c], axis=1)], axis=1).reshape(2 * n1, 2 * n1h)
    m1r = np.stack([c, -s], axis=1).reshape(2 * n1, n1h)
    ct, st = c.T, s.T
    m3 = np.stack([np.stack([ct, -st], axis=2).reshape(n1h, 2 * n1),
                   np.stack([st, ct], axis=2).reshape(n1h, 2 * n1)], axis=0).reshape(2 * n1h, 2 * n1)
    n2 = np.arange(DFT_N2, dtype=np.int64)
    gc, gs = cs(n2[:, None] * n2[None, :], DFT_N2)
    g_fwd = np.concatenate([np.concatenate([gc, gs], axis=1), np.concatenate([-gs, gc], axis=1)], axis=0)
    g_inv = np.concatenate([np.concatenate([gc, -gs], axis=1), np.concatenate([gs, gc], axis=1)], axis=0)
    ph = (jnp.arange(n1, dtype=jnp.int32)[:, None] * jnp.arange(DFT_N2, dtype=jnp.int32)[None, :]) % N
    ang = ph.astype(F32) * (2.0 * math.pi / N)
    lanes = (n1, DFT_N2H, sub, V7X_LANES)
    twc = jnp.broadcast_to(jnp.cos(ang).reshape(n1, DFT_N2H, sub, 1), lanes)
    tws = jnp.broadcast_to(jnp.sin(ang).reshape(n1, DFT_N2H, sub, 1), lanes)
    return dict(n1=n1, n1h=n1h, m1c=const(expand(m1c)), m1r=const(expand(m1r)), m3=const(expand(m3)),
                g_fwd=const(g_fwd), g_inv=const(g_inv), twc=twc, tws=tws)


def _dft1_kernel(x_ref, m_ref, o_ref):
    rows = m_ref.shape[1]
    xs = x_ref[...].reshape(rows, x_ref.shape[-1]).astype(BF16)
    o_ref[...] = jnp.dot(m_ref[...], xs, preferred_element_type=F32).reshape(o_ref.shape)


def _dft1(x5, m, n1, cb):
    P, n1h, _, sub, Cx = x5.shape
    return pl.pallas_call(
        _dft1_kernel,
        grid=(DFT_N2H, Cx // cb),
        in_specs=[
            pl.BlockSpec((P, n1h, None, sub, cb), lambda h, c: (0, 0, h, 0, c)),
            pl.BlockSpec(m.shape, lambda h, c: (0, 0)),
        ],
        out_specs=pl.BlockSpec((None, n1, 2, sub, cb), lambda h, c: (h, 0, 0, 0, c)),
        out_shape=jax.ShapeDtypeStruct((DFT_N2H, n1, 2, sub, Cx), F32),
        compiler_params=_cparams(("parallel", "parallel"), 48),
        name="dft1",
    )(x5, m)


def _dft2_kernel(a_ref, f_ref, twc_ref, tws_ref, gf_ref, gi_ref, o_ref):
    C = a_ref.shape[-1]
    half = DFT_N2

    def lanes(fn):
        return jnp.concatenate([fn(slice(c0, c0 + V7X_LANES)) for c0 in range(0, C, V7X_LANES)], axis=-1)

    for kk in range(a_ref.shape[1]):
        twc, tws = twc_ref[kk], tws_ref[kk]

        def spectrum(ref, col0):
            def re_part(sl):
                return ref[:, kk, 0, :, col0 + sl.start:col0 + sl.stop] * twc + ref[:, kk, 1, :, col0 + sl.start:col0 + sl.stop] * tws

            def im_part(sl):
                return ref[:, kk, 1, :, col0 + sl.start:col0 + sl.stop] * twc - ref[:, kk, 0, :, col0 + sl.start:col0 + sl.stop] * tws

            t = jnp.concatenate([lanes(re_part).reshape(half, C), lanes(im_part).reshape(half, C)], axis=0).astype(BF16)
            s = jnp.dot(gf_ref[...], t, preferred_element_type=F32)
            return s[:half], s[half:]

        xr, xi = spectrum(a_ref, 0)
        fr, fi = spectrum(f_ref, 0)
        br, bi = spectrum(f_ref, C)
        hr, hi = fr + br, fi - bi
        y = jnp.concatenate([xr * hr - xi * hi, xr * hi + xi * hr], axis=0).astype(BF16)
        b = jnp.dot(gi_ref[...], y, preferred_element_type=F32)
        br2 = b[:half].reshape(DFT_N2H, V7X_SUBLANES, C)
        bi2 = b[half:].reshape(DFT_N2H, V7X_SUBLANES, C)
        for c0 in range(0, C, V7X_LANES):
            sl = slice(c0, c0 + V7X_LANES)
            o_ref[:, kk, 0, :, sl] = br2[:, :, sl] * twc - bi2[:, :, sl] * tws
            o_ref[:, kk, 1, :, sl] = bi2[:, :, sl] * twc + br2[:, :, sl] * tws


def _dft2(a5, f5, k, kb=4):
    n2h, n1, _, sub, C = a5.shape
    kb = min(kb, n1)
    return pl.pallas_call(
        _dft2_kernel,
        grid=(n1 // kb,),
        in_specs=[
            pl.BlockSpec((n2h, kb, 2, sub, C), lambda i: (0, i, 0, 0, 0)),
            pl.BlockSpec((n2h, kb, 2, sub, 2 * C), lambda i: (0, i, 0, 0, 0)),
            pl.BlockSpec((kb, n2h, sub, V7X_LANES), lambda i: (i, 0, 0, 0)),
            pl.BlockSpec((kb, n2h, sub, V7X_LANES), lambda i: (i, 0, 0, 0)),
            pl.BlockSpec((2 * DFT_N2, 2 * DFT_N2), lambda i: (0, 0)),
            pl.BlockSpec((2 * DFT_N2, 2 * DFT_N2), lambda i: (0, 0)),
        ],
        out_specs=pl.BlockSpec((n2h, kb, 2, sub, C), lambda i: (0, i, 0, 0, 0)),
        out_shape=jax.ShapeDtypeStruct(a5.shape, F32),
        compiler_params=_cparams(("parallel",), 48),
        name="dft2",
    )(a5, f5, k["twc"], k["tws"], k["g_fwd"], k["g_inv"])


def _dft3_kernel(b_ref, m_ref, x0_ref, vx_ref, sc_ref, d_ref, o_ref):
    C = b_ref.shape[-1]
    bs = b_ref[...].reshape(m_ref.shape[1], C).astype(BF16)
    y = jnp.dot(m_ref[...], bs, preferred_element_type=F32).reshape(o_ref.shape)
    o_ref[...] = x0_ref[...] * (y * sc_ref[...] + vx_ref[...] * d_ref[...])


def _dft3(b5, m3, x05, vx5, scale, d):
    n2h, n1, _, sub, C = b5.shape
    Bt, n1h = x05.shape[0], x05.shape[1]
    tok = pl.BlockSpec((Bt, n1h, None, sub, C), lambda h: (0, 0, h, 0, 0))
    vec = pl.BlockSpec((1, C), lambda h: (0, 0))
    return pl.pallas_call(
        _dft3_kernel,
        grid=(n2h,),
        in_specs=[
            pl.BlockSpec((None, n1, 2, sub, C), lambda h: (h, 0, 0, 0, 0)),
            pl.BlockSpec(m3.shape, lambda h: (0, 0)),
            tok, tok, vec, vec,
        ],
        out_specs=tok,
        out_shape=jax.ShapeDtypeStruct(x05.shape, F32),
        compiler_params=_cparams(("parallel",), 48),
        name="dft3",
    )(b5, m3, x05, vx5, scale, d)


def _hyena(u, conv_w, conv_b, w1, b1, w2, b2, w3, b3, freq, wout, d_skip, B, L):
    T = u.shape[0]
    C = HYENA_WIDTH
    k = _dft_constants(L)
    n1, n1h = k["n1"], k["n1h"]
    x0, vx = _hpre(u, conv_w, conv_b[None], B, L)
    hfb, asum = _filter_taps(L, w1, b1, w2, b2, w3, b3, freq, wout)
    a5 = _dft1(vx, k["m1c"], n1, cb=C)
    f5 = _dft1(hfb, k["m1r"], n1, cb=C)
    b5 = _dft2(a5, f5, k)
    scale = 1.0 / (asum * (2 * L))
    return _dft3(b5, k["m3"], x0, vx, scale, d_skip[None])


PACKED = jnp.uint32


def _pack_pairs(x):
    half = x.shape[1] // 2
    bits = pltpu.bitcast(x.astype(BF16).astype(F32), PACKED)
    return (bits[:, :half] >> 16) | bits[:, half:]


def _unpack_pairs(w):
    lo = pltpu.bitcast(w << 16, F32)
    hi = pltpu.bitcast(w & jnp.uint32(0xFFFF0000), F32)
    return lo, hi


def _oproj_kernel(x_ref, g0_ref, b0_ref, at_ref, hy_ref, wo_ref, g1_ref, b1_ref, wrh_ref, wrl_ref,
                  h1_ref, h1p_ref, sc_ref):
    aw = at_ref.shape[1]
    h0 = _layer_norm_rows(x_ref[...], g0_ref[...], b0_ref[...])
    mixed = jnp.dot(at_ref[...], wo_ref[0:aw, :], preferred_element_type=F32)
    hy = hy_ref[...].reshape(x_ref.shape[0], hy_ref.shape[-1])
    mixed = mixed + jnp.dot(hy.astype(BF16), wo_ref[aw:, :], preferred_element_type=F32)
    h1 = _layer_norm_rows(ALPHA * h0 + mixed, g1_ref[...], b1_ref[...])
    h1_ref[...] = h1
    h1p_ref[...] = _pack_pairs(h1)
    hh = h1.astype(BF16)
    hl = (h1 - hh.astype(F32)).astype(BF16)
    dn = (((1,), (1,)), ((), ()))
    logits = lax.dot_general(wrh_ref[...], hh, dn, preferred_element_type=F32)
    logits = logits + (lax.dot_general(wrh_ref[...], hl, dn, preferred_element_type=F32)
                       + lax.dot_general(wrl_ref[...], hh, dn, preferred_element_type=F32))
    sc_ref[...] = 1.0 / (1.0 + jnp.exp(-logits))


def _oproj(x2, g0, b0, attn, hy, wo_bf, g1, b1, wr_t, tm=256):
    T, D = x2.shape
    E = wr_t.shape[0]
    wrh = wr_t.astype(BF16)
    wrl = (wr_t - wrh.astype(F32)).astype(BF16)
    row = lambda i: (i, 0)
    fixed = lambda i: (0, 0)
    hy4 = hy.reshape((-1,) + hy.shape[2:])
    return pl.pallas_call(
        _oproj_kernel,
        grid=(T // tm,),
        in_specs=[
            pl.BlockSpec((tm, D), row), pl.BlockSpec((1, D), fixed), pl.BlockSpec((1, D), fixed),
            pl.BlockSpec((tm, attn.shape[1]), row),
            pl.BlockSpec((tm // DFT_N2,) + hy4.shape[1:], lambda i: (i, 0, 0, 0)),
            pl.BlockSpec((D, D), fixed), pl.BlockSpec((1, D), fixed), pl.BlockSpec((1, D), fixed),
            pl.BlockSpec((E, D), fixed), pl.BlockSpec((E, D), fixed),
        ],
        out_specs=[pl.BlockSpec((tm, D), row), pl.BlockSpec((tm, D // 2), row), pl.BlockSpec((E, tm), lambda i: (0, i))],
        out_shape=[
            jax.ShapeDtypeStruct((T, D), F32),
            jax.ShapeDtypeStruct((T, D // 2), PACKED),
            jax.ShapeDtypeStruct((E, T), F32),
        ],
        compiler_params=_cparams(("parallel",), 48),
        name="oproj",
    )(x2, g0, b0, attn, hy4, wo_bf, g1, b1, wrh, wrl)


def _route_kernel(sc_ref, bias_ref, tri_ref, idx_ref, gate_ref, rank_ref, cnt_ref, carry_ref):
    step = pl.program_id(0)

    @pl.when(step == 0)
    def _():
        carry_ref[...] = jnp.zeros_like(carry_ref)

    tm = V7X_LANES
    for part in range(sc_ref.shape[1] // tm):
        cols = slice(part * tm, (part + 1) * tm)
        _route_tokens(sc_ref[:, cols], bias_ref, tri_ref, idx_ref, gate_ref, rank_ref, carry_ref, cols)
    cnt_ref[...] = carry_ref[...].astype(jnp.int32)


def _route_tokens(scores, bias_ref, tri_ref, idx_ref, gate_ref, rank_ref, carry_ref, cols):
    E, tm = scores.shape
    neg = jnp.float32(-jnp.inf)
    biased = scores + bias_ref[...]
    erow = lax.broadcasted_iota(jnp.int32, (E, tm), 0)
    big = jnp.int32(E)

    def first_argmax(vals, rows):
        m = jnp.max(vals, axis=0, keepdims=True)
        pick = jnp.min(jnp.where(vals == m, rows, big), axis=0, keepdims=True)
        return m, pick

    gsc = []
    for g in range(N_GROUPS):
        blk = biased[g * GROUP_SIZE:(g + 1) * GROUP_SIZE, :]
        rows = erow[g * GROUP_SIZE:(g + 1) * GROUP_SIZE, :]
        m1, p1 = first_argmax(blk, rows)
        m2 = jnp.max(jnp.where(rows == p1, neg, blk), axis=0, keepdims=True)
        gsc.append(m1 + m2)
    gsc = jnp.concatenate(gsc, axis=0)
    grow = lax.broadcasted_iota(jnp.int32, (N_GROUPS, tm), 0)
    gsel = jnp.zeros((N_GROUPS, tm), jnp.bool_)
    work = gsc
    for _ in range(TOPK_GROUPS):
        _, p = first_argmax(work, grow)
        hit = grow == p
        gsel = gsel | hit
        work = jnp.where(hit, neg, work)
    emask = jnp.concatenate(
        [jnp.broadcast_to(gsel[g:g + 1, :], (GROUP_SIZE, tm)) for g in range(N_GROUPS)], axis=0)
    work = jnp.where(emask, biased, neg)

    sel = jnp.zeros((E, tm), jnp.bool_)
    picks, gvals = [], []
    for _ in range(TOP_K):
        _, p = first_argmax(work, erow)
        hit = erow == p
        sel = sel | hit
        picks.append(p)
        gvals.append(jnp.sum(jnp.where(hit, scores, 0.0), axis=0, keepdims=True))
        work = jnp.where(hit, neg, work)
    gv = jnp.concatenate(gvals, axis=0)
    idx_ref[:, cols] = jnp.concatenate(picks, axis=0)
    gate_ref[:, cols] = gv / jnp.sum(gv, axis=0, keepdims=True) * ROUTED_SCALE

    chosen = sel.astype(F32)
    before = jnp.dot(chosen.astype(BF16), tri_ref[...], preferred_element_type=F32) + carry_ref[...]
    rank_ref[:, cols] = jnp.concatenate(
        [jnp.sum(jnp.where(erow == p, before, 0.0), axis=0, keepdims=True) for p in picks], axis=0).astype(jnp.int32)
    carry_ref[...] += jnp.sum(chosen, axis=1, keepdims=True)


def _route(scores_t, bias, tm=256):
    E, T = scores_t.shape
    tm = min(tm, T)
    lanes = V7X_LANES
    tri = (jnp.arange(lanes)[:, None] < jnp.arange(lanes)[None, :]).astype(BF16)
    tok = lambda i: (0, i)
    fixed = lambda i: (0, 0)
    return pl.pallas_call(
        _route_kernel,
        grid=(T // tm,),
        in_specs=[pl.BlockSpec((E, tm), tok), pl.BlockSpec((E, 1), fixed), pl.BlockSpec((lanes, lanes), fixed)],
        out_specs=[pl.BlockSpec((TOP_K, tm), tok), pl.BlockSpec((TOP_K, tm), tok), pl.BlockSpec((TOP_K, tm), tok),
                   pl.BlockSpec((E, 1), fixed)],
        out_shape=[
            jax.ShapeDtypeStruct((TOP_K, T), jnp.int32),
            jax.ShapeDtypeStruct((TOP_K, T), F32),
            jax.ShapeDtypeStruct((TOP_K, T), jnp.int32),
            jax.ShapeDtypeStruct((E, 1), jnp.int32),
        ],
        scratch_shapes=[pltpu.VMEM((E, 1), F32)],
        compiler_params=_cparams(("arbitrary",), 32),
        name="route",
    )(scores_t, bias, tri)


def _dest_kernel(idx_ref, rank_ref, ps_ref, dest_ref):
    E = ps_ref.shape[0]
    tm = idx_ref.shape[1]
    erow = lax.broadcasted_iota(jnp.int32, (E, tm), 0)
    ps = ps_ref[...].astype(F32)
    rows = [jnp.sum(jnp.where(erow == idx_ref[k:k + 1, :], ps, 0.0), axis=0, keepdims=True) for k in range(TOP_K)]
    dest_ref[...] = jnp.concatenate(rows, axis=0).astype(jnp.int32) + rank_ref[...]


def _dest_rows(idx, rank, pad_start, tm=512):
    K, T = idx.shape
    E = pad_start.shape[0]
    tok = lambda i: (0, i)
    return pl.pallas_call(
        _dest_kernel,
        grid=(T // tm,),
        in_specs=[pl.BlockSpec((K, tm), tok), pl.BlockSpec((K, tm), tok), pl.BlockSpec((E, 1), lambda i: (0, 0))],
        out_specs=pl.BlockSpec((K, tm), tok),
        out_shape=jax.ShapeDtypeStruct((K, T), jnp.int32),
        compiler_params=_cparams(("arbitrary",), 32),
        name="dest",
    )(idx, rank, pad_start[:, None])


def _scatter_kernel(ps_ref, pe_ref, dest_ref, h_ref, xs_ref, zbuf, sem, *, rb):
    sub = V7X_SUBLANES
    tm = h_ref.shape[0]

    @pl.when(pl.program_id(0) == 0)
    def _():
        zbuf[...] = jnp.zeros_like(zbuf)

        def zcopy(e):
            return pltpu.make_async_copy(zbuf, xs_ref.at[pl.ds(pl.multiple_of(pe_ref[e] - rb, rb), rb)], sem)

        def zstart(e, c):
            @pl.when(pe_ref[e] > ps_ref[e])
            def _():
                zcopy(e).start()
            return c

        def zwait(e, c):
            @pl.when(pe_ref[e] > ps_ref[e])
            def _():
                zcopy(e).wait()
            return c

        lax.fori_loop(0, ps_ref.shape[0], zstart, 0)
        lax.fori_loop(0, ps_ref.shape[0], zwait, 0)

    def issue(r8, c):
        for j in range(sub):
            for k in range(TOP_K):
                dst = dest_ref[0, r8 * (sub * TOP_K) + (j * TOP_K + k)]
                pltpu.make_async_copy(h_ref.at[pl.ds(r8 * sub + j, 1)], xs_ref.at[pl.ds(dst, 1)], sem).start(priority=k % 2)
        return c

    lax.fori_loop(0, tm // sub, issue, 0)
    for k in range(TOP_K):
        pltpu.make_async_copy(h_ref, xs_ref.at[pl.ds(0, tm)], sem).wait()


def _dest_tiles(dest, tm):
    K, T = dest.shape
    return dest.T.reshape(T // tm, 1, tm * K)


def _scatter_rows(pad_start, pad_end, dest_t, h1p, n_rows, rb, tm=256):
    T, W = h1p.shape
    return pl.pallas_call(
        functools.partial(_scatter_kernel, rb=rb),
        grid_spec=pltpu.PrefetchScalarGridSpec(
            num_scalar_prefetch=2,
            grid=(T // tm,),
            in_specs=[
                pl.BlockSpec((None, 1, tm * TOP_K), lambda i, ps, pe: (i, 0, 0), memory_space=pltpu.SMEM),
                pl.BlockSpec((tm, W), lambda i, ps, pe: (i, 0)),
            ],
            out_specs=pl.BlockSpec(memory_space=pl.ANY),
            scratch_shapes=[pltpu.VMEM((rb, W), h1p.dtype), pltpu.SemaphoreType.DMA],
        ),
        out_shape=jax.ShapeDtypeStruct((n_rows, W), h1p.dtype),
        compiler_params=_cparams(("arbitrary",), 32),
        name="scatter",
    )(pad_start, pad_end, dest_t, h1p)


EXPERT_X_SLOTS = 4
EXPERT_Y_SLOTS = 3


def _experts_kernel(first_ref, nblk_ref, wsel_ref, tot_ref, xs_ref, wg_ref, wu_ref, wd_ref, ys_ref,
                    xbuf, ybuf, wgub, wdb, xsem, ysem, *, rb):
    del wsel_ref
    e = pl.program_id(0)
    F = wg_ref.shape[1]
    nx, ny = xbuf.shape[0], ybuf.shape[0]
    total = tot_ref[0]
    n = nblk_ref[e]
    g0 = first_ref[e]

    def x_copy(g):
        slot = g % nx
        return pltpu.make_async_copy(xs_ref.at[pl.ds(pl.multiple_of(g * rb, rb), rb)], xbuf.at[slot], xsem.at[slot])

    def y_copy(g):
        slot = g % ny
        return pltpu.make_async_copy(ybuf.at[slot], ys_ref.at[pl.ds(pl.multiple_of(g * rb, rb), rb)], ysem.at[slot])

    @pl.when(e == 0)
    def _():
        for g in range(nx):
            @pl.when(g < total)
            def _():
                x_copy(g).start()

    def process(g, count):
        for d in range(count):
            x_copy(g + d).wait()
        ys = []
        for d in range(count):
            lo, hi = _unpack_pairs(xbuf[(g + d) % nx])
            x = jnp.concatenate([lo, hi], axis=1).astype(BF16)
            gu = jnp.dot(x, wgub[...], preferred_element_type=F32)
            gate, up = gu[:, 0:F], gu[:, F:]
            hb = (gate / (1.0 + jnp.exp(-gate)) * up).astype(BF16)
            ys.append(_pack_pairs(jnp.dot(hb, wdb[...], preferred_element_type=F32)))
        for d in range(count):
            @pl.when(g + d >= ny)
            def _():
                y_copy(g + d - ny).wait()

            ybuf[(g + d) % ny] = ys[d]
            y_copy(g + d).start()
        for d in range(count):
            @pl.when(g + d + nx < total)
            def _():
                x_copy(g + d + nx).start()

    @pl.when(n > 0)
    def _():
        wgub[:, 0:F] = wg_ref[...].astype(BF16)
        wgub[:, F:] = wu_ref[...].astype(BF16)
        wdb[...] = wd_ref[...].astype(BF16)

        def pair(j, c):
            process(g0 + 2 * j, 2)
            return c

        lax.fori_loop(0, n // 2, pair, 0)

        @pl.when(n % 2 == 1)
        def _():
            process(g0 + n - 1, 1)

    @pl.when(e == pl.num_programs(0) - 1)
    def _():
        for back in range(ny, 0, -1):
            @pl.when(total >= back)
            def _():
                y_copy(total - back).wait()


def _experts(first_blk, nblk, wsel, total_blk, xs, w_gate, w_up, w_down, rb):
    P, W = xs.shape
    E, D, F = w_gate.shape
    wspec = lambda shape: pl.BlockSpec((None,) + shape, lambda e, fb, nbk, ws, tt: (ws[e], 0, 0))
    return pl.pallas_call(
        functools.partial(_experts_kernel, rb=rb),
        grid_spec=pltpu.PrefetchScalarGridSpec(
            num_scalar_prefetch=4,
            grid=(E,),
            in_specs=[pl.BlockSpec(memory_space=pl.ANY), wspec((D, F)), wspec((D, F)), wspec((F, D))],
            out_specs=pl.BlockSpec(memory_space=pl.ANY),
            scratch_shapes=[
                pltpu.VMEM((EXPERT_X_SLOTS, rb, W), PACKED), pltpu.VMEM((EXPERT_Y_SLOTS, rb, W), PACKED),
                pltpu.VMEM((D, 2 * F), BF16), pltpu.VMEM((F, D), BF16),
                pltpu.SemaphoreType.DMA((EXPERT_X_SLOTS,)), pltpu.SemaphoreType.DMA((EXPERT_Y_SLOTS,)),
            ],
        ),
        out_shape=jax.ShapeDtypeStruct((P, W), PACKED),
        compiler_params=_cparams(("arbitrary",), 48),
        name="experts",
    )(first_blk, nblk, wsel, total_blk, xs, w_gate, w_up, w_down)


def _combine_kernel(dest_ref, destn_ref, h1_ref, gate_ref, ys_ref, ys8_ref, sg_ref, su_ref, sd_ref,
                    g2_ref, b2_ref, o_ref, buf, sem):
    i = pl.program_id(0)
    n = pl.num_programs(0)
    tm = h1_ref.shape[0]
    slot = i % 2

    sub = V7X_SUBLANES

    def issue_rows(dref, s, r0, nrows):
        for r in range(r0, r0 + nrows):
            for k in range(TOP_K):
                src = ys_ref.at[pl.ds(dref[0, r * TOP_K + k], 1)]
                pltpu.make_async_copy(src, buf.at[s, r, pl.ds(k, 1)], sem.at[s]).start(priority=k % 2)

    def wait_tile(s):
        pltpu.make_async_copy(ys8_ref.at[pl.ds(0, tm)], buf.at[s], sem.at[s]).wait()

    @pl.when(i == 0)
    def _():
        def body(r8, c):
            for j in range(sub):
                for k in range(TOP_K):
                    src = ys_ref.at[pl.ds(dest_ref[0, r8 * (sub * TOP_K) + (j * TOP_K + k)], 1)]
                    pltpu.make_async_copy(src, buf.at[0, r8 * sub + j, pl.ds(k, 1)], sem.at[0]).start(priority=k % 2)
            return c
        lax.fori_loop(0, tm // sub, body, 0)

    wait_tile(slot)
    parts = 4
    rows_per = tm // parts
    nk = rows_per * TOP_K
    pick = lax.broadcasted_iota(jnp.int32, (TOP_K, nk), 1) % TOP_K == lax.broadcasted_iota(jnp.int32, (TOP_K, nk), 0)
    spread = pick.astype(BF16)
    own = lax.broadcasted_iota(jnp.int32, (rows_per, nk), 1) // TOP_K == lax.broadcasted_iota(jnp.int32, (rows_per, nk), 0)
    for part in range(parts):
        r0 = part * rows_per
        issue_rows(destn_ref, 1 - slot, r0, rows_per)
        rows = slice(r0, r0 + rows_per)
        h1 = h1_ref[rows, :]
        hb = h1.astype(BF16)
        g = jnp.dot(hb, sg_ref[...], preferred_element_type=F32)
        u = jnp.dot(hb, su_ref[...], preferred_element_type=F32)
        ffn = jnp.dot((g / (1.0 + jnp.exp(-g)) * u).astype(BF16), sd_ref[...], preferred_element_type=F32)
        lo, hi = _unpack_pairs(buf[slot, rows].reshape(nk, buf.shape[3]))
        y = jnp.concatenate([lo, hi], axis=1).astype(BF16)
        gates = gate_ref[rows, :]
        gh = gates.astype(BF16)
        gl = (gates - gh.astype(F32)).astype(BF16)
        routed = jnp.zeros((rows_per, y.shape[1]), F32)
        for piece in (gh, gl):
            gm = jnp.where(own, jnp.dot(piece, spread, preferred_element_type=F32), 0.0).astype(BF16)
            routed = routed + jnp.dot(gm, y, preferred_element_type=F32)
        o_ref[rows, :] = _layer_norm_rows(ALPHA * h1 + ffn + routed, g2_ref[...], b2_ref[...])

    @pl.when(i == n - 1)
    def _():
        wait_tile(1 - slot)


def _combine(dest, h1, gates_tk, ys, sg, su, sd, g2, b2, tm=256):
    T, D = h1.shape
    W = ys.shape[1]
    F = sg.shape[1]
    nt = T // tm
    row = lambda i: (i, 0)
    fixed = lambda i: (0, 0)
    smem = functools.partial(pl.BlockSpec, (None, 1, tm * TOP_K), memory_space=pltpu.SMEM)
    dest_t = _dest_tiles(dest, tm)
    ys8 = ys.reshape(ys.shape[0] // TOP_K, TOP_K, W)
    return pl.pallas_call(
        _combine_kernel,
        grid=(nt,),
        in_specs=[
            smem(index_map=lambda i: (i, 0, 0)),
            smem(index_map=lambda i: (jnp.minimum(i + 1, nt - 1), 0, 0)),
            pl.BlockSpec((tm, D), row),
            pl.BlockSpec((tm, TOP_K), row),
            pl.BlockSpec(memory_space=pl.ANY),
            pl.BlockSpec(memory_space=pl.ANY),
            pl.BlockSpec((D, F), fixed), pl.BlockSpec((D, F), fixed), pl.BlockSpec((F, D), fixed),
            pl.BlockSpec((1, D), fixed), pl.BlockSpec((1, D), fixed),
        ],
        out_specs=pl.BlockSpec((tm, D), row),
        out_shape=jax.ShapeDtypeStruct((T, D), F32),
        scratch_shapes=[pltpu.VMEM((2, tm, TOP_K, W), PACKED), pltpu.SemaphoreType.DMA((2,))],
        compiler_params=_cparams(("arbitrary",), 48),
        name="combine",
    )(dest_t, dest_t, h1, gates_tk, ys, ys8, sg, su, sd, g2, b2)


MOE_ROWS = 256


def _moe(h1, h1p, scores_t, router_bias, w_gate, w_up, w_down, ws_gate, ws_up, ws_down, g2, b2):
    T = h1.shape[0]
    E = N_EXPERTS
    rb = MOE_ROWS
    idx, gates, rank, counts = _route(scores_t, router_bias[:, None])
    counts = counts[:, 0]
    padded = (counts + rb - 1) // rb * rb
    pad_end = jnp.cumsum(padded)
    pad_start = pad_end - padded
    nb = (T * TOP_K) // rb + E
    nblk = (padded // rb).astype(jnp.int32)
    first_blk = (pad_start // rb).astype(jnp.int32)
    total_blk = (pad_end[-1:] // rb).astype(jnp.int32)
    wsel = lax.cummax(jnp.where(nblk > 0, jnp.arange(E, dtype=jnp.int32), 0))
    dest = _dest_rows(idx, rank, pad_start)
    xs = _scatter_rows(pad_start, pad_end, _dest_tiles(dest, 256), h1p, nb * rb, rb, tm=256)
    ys = _experts(first_blk, nblk, wsel, total_blk, xs, w_gate, w_up, w_down, rb)
    return _combine(dest, h1, gates.T, ys, ws_gate.astype(BF16), ws_up.astype(BF16), ws_down.astype(BF16), g2, b2)


def kernel(x, positions, emb_ln_g, emb_ln_b, w_in, hy_conv_w, hy_conv_b, hy_f_w1, hy_f_b1, hy_f_w2, hy_f_b2, hy_f_w3, hy_f_b3, hy_f_freq, hy_f_wout, hy_d, lambda_q1, lambda_k1, lambda_q2, lambda_k2, subln_g, w_o, ln1_g, ln1_b, w_router, router_bias, w_gate, w_up, w_down, ws_gate, ws_up, ws_down, ln2_g, ln2_b):
    B, L, D = x.shape
    T = B * L
    assert w_in.shape[0] == DEPTH == 1
    i = 0
    x2 = x.reshape(T, D)
    g0, b0 = emb_ln_g[None], emb_ln_b[None]
    ra, rm, rp = _rotary_tables(positions)
    q, k, v, u = _inproj(x2, g0, b0, w_in[i].astype(BF16), ra, rm, rp)
    lam = (jnp.exp(jnp.sum(lambda_q1[i] * lambda_k1[i])) - jnp.exp(jnp.sum(lambda_q2[i] * lambda_k2[i])) + LAM_INIT)
    attn = _attention(q, k, v, lam.reshape(1).astype(F32), subln_g[i][None], B, L)
    hy = _hyena(u, hy_conv_w[i], hy_conv_b[i], hy_f_w1[i], hy_f_b1[i], hy_f_w2[i], hy_f_b2[i], hy_f_w3[i], hy_f_b3[i],
                hy_f_freq[i], hy_f_wout[i], hy_d[i], B, L)
    h1, h1p, scores_t = _oproj(x2, g0, b0, attn, hy, w_o[i].astype(BF16), ln1_g[i][None], ln1_b[i][None], w_router[i].T)
    out = _moe(h1, h1p, scores_t, router_bias[i], w_gate[i], w_up[i], w_down[i], ws_gate[i], ws_up[i], ws_down[i],
               ln2_g[i][None], ln2_b[i][None])
    return out.reshape(B, L, D)
```

```python
import functools
import math

import numpy as np
import jax
import jax.numpy as jnp
from jax import lax
from jax.experimental import pallas as pl
from jax.experimental.pallas import tpu as pltpu

DA_HEADS = 4
DA_HEAD_DIM = 64
DA_V_DIM = 128
ATTN_WIDTH = 512
HYENA_WIDTH = 512
ROT_DIM = 16
ROPE_THETA = 500000.0
SHORT_CONV = 3
FILTER_EMB = 33
FILTER_BANDS = 16
DECAY_TARGET = 1e-2
FAST_DECAY = 0.3
SLOW_DECAY = 1.5
N_EXPERTS = 256
TOP_K = 8
N_GROUPS = 8
GROUP_SIZE = N_EXPERTS // N_GROUPS
TOPK_GROUPS = 4
EXPERT_DIM = 256
ROUTED_SCALE = 2.5
DEPTH = 1
ALPHA = (2 * DEPTH) ** 0.25
LN_EPS = 1e-5
LAM_INIT = 0.8 - 0.6 * math.exp(-0.3 * 0)

V7X_LANES = 128
V7X_SUBLANES = 8
V7X_VMEM_BYTES = 64 * 1024 * 1024

DFT_N2 = 128
DFT_N2H = DFT_N2 // V7X_SUBLANES

BF16 = jnp.bfloat16
F32 = jnp.float32


def _cparams(sem, vmem_mb):
    return pltpu.CompilerParams(dimension_semantics=sem, vmem_limit_bytes=vmem_mb * 1024 * 1024)


def _layer_norm_rows(x, g, b):
    mu = jnp.mean(x, axis=-1, keepdims=True)
    xc = x - mu
    var = jnp.mean(xc * xc, axis=-1, keepdims=True)
    return xc * lax.rsqrt(var + LN_EPS) * g + b


def _inproj_kernel(x_ref, g_ref, b_ref, w_ref, ra_ref, rm_ref, rp_ref, q_ref, k_ref, v_ref, u_ref):
    h = _layer_norm_rows(x_ref[...], g_ref[...], b_ref[...]).astype(BF16)
    ra, rm, rp = ra_ref[...], rm_ref[...], rp_ref[...]

    def rot(t):
        return t * ra + pltpu.roll(t, V7X_LANES - ROT_DIM // 2, axis=1) * rm + pltpu.roll(t, ROT_DIM // 2, axis=1) * rp

    aw = ATTN_WIDTH
    qp = jnp.dot(h, w_ref[:, 0:aw], preferred_element_type=F32)
    kp = jnp.dot(h, w_ref[:, aw:2 * aw], preferred_element_type=F32)
    scale = DA_HEAD_DIM ** -0.5 * math.log2(math.e)
    for c in range(aw // V7X_LANES):
        sl = slice(c * V7X_LANES, (c + 1) * V7X_LANES)
        q_ref[:, sl] = (rot(qp[:, sl]) * scale).astype(BF16)
        k_ref[:, sl] = rot(kp[:, sl]).astype(BF16)
    vp = jnp.dot(h, w_ref[:, 2 * aw:3 * aw], preferred_element_type=F32).astype(BF16)
    ones = jnp.ones((vp.shape[0], DA_V_DIM), BF16)
    for hd in range(DA_HEADS):
        v_ref[:, 2 * hd * DA_V_DIM:(2 * hd + 1) * DA_V_DIM] = vp[:, hd * DA_V_DIM:(hd + 1) * DA_V_DIM]
        v_ref[:, (2 * hd + 1) * DA_V_DIM:(2 * hd + 2) * DA_V_DIM] = ones
    u_ref[...] = jnp.dot(h, w_ref[:, 3 * aw:], preferred_element_type=F32)


def _inproj(x2, g, b, w_bf, ra, rm, rp, tm=256):
    T, D = x2.shape
    ncol = w_bf.shape[1]
    aw = ATTN_WIDTH
    uw = ncol - 3 * aw
    row = lambda i: (i, 0)
    fixed = lambda i: (0, 0)
    return pl.pallas_call(
        _inproj_kernel,
        grid=(T // tm,),
        in_specs=[
            pl.BlockSpec((tm, D), row),
            pl.BlockSpec((1, D), fixed),
            pl.BlockSpec((1, D), fixed),
            pl.BlockSpec((D, ncol), fixed),
            pl.BlockSpec((tm, V7X_LANES), row),
            pl.BlockSpec((tm, V7X_LANES), row),
            pl.BlockSpec((tm, V7X_LANES), row),
        ],
        out_specs=[
            pl.BlockSpec((tm, aw), row),
            pl.BlockSpec((tm, aw), row),
            pl.BlockSpec((tm, 2 * aw), row),
            pl.BlockSpec((tm, uw), row),
        ],
        out_shape=[
            jax.ShapeDtypeStruct((T, aw), BF16),
            jax.ShapeDtypeStruct((T, aw), BF16),
            jax.ShapeDtypeStruct((T, 2 * aw), BF16),
            jax.ShapeDtypeStruct((T, uw), F32),
        ],
        compiler_params=_cparams(("parallel",), 48),
        name="inproj",
    )(x2, g, b, w_bf, ra, rm, rp)


def _rotary_tables(positions):
    half = ROT_DIM // 2
    inv_freq = ROPE_THETA ** (-jnp.arange(0, ROT_DIM, 2, dtype=F32) / ROT_DIM)
    ang = positions.astype(F32).reshape(-1)[:, None] * inv_freq
    cos, sin = jnp.cos(ang), jnp.sin(ang)
    T = ang.shape[0]
    ones = jnp.ones((T, DA_HEAD_DIM - ROT_DIM), F32)
    zeros_h = jnp.zeros((T, half), F32)
    zeros_r = jnp.zeros((T, DA_HEAD_DIM - ROT_DIM), F32)
    a64 = jnp.concatenate([cos, cos, ones], axis=1)
    m64 = jnp.concatenate([-sin, zeros_h, zeros_r], axis=1)
    p64 = jnp.concatenate([zeros_h, sin, zeros_r], axis=1)
    rep = V7X_LANES // DA_HEAD_DIM
    return jnp.tile(a64, (1, rep)), jnp.tile(m64, (1, rep)), jnp.tile(p64, (1, rep))


def _attn_kernel(lam_ref, q_ref, k_ref, v_ref, g_ref, o_ref, s_ref, m_ref, acc_ref, *, kb, unroll):
    qb = q_ref.shape[0]
    L = k_ref.shape[0]
    nchunk = L // kb
    nl = kb // V7X_LANES
    q = q_ref[...]
    lane = lax.broadcasted_iota(jnp.int32, q.shape, 1)
    zero = jnp.zeros_like(q)
    qm = [jnp.where(lane < DA_HEAD_DIM, q, zero), jnp.where(lane >= DA_HEAD_DIM, q, zero)]
    m_ref[...] = jnp.full(m_ref.shape, -jnp.inf, F32)
    acc_ref[...] = jnp.zeros(acc_ref.shape, F32)

    def score_body(j, carry):
        kc = k_ref[pl.ds(pl.multiple_of(j * kb, kb), kb), :]
        for c in range(2):
            s = lax.dot_general(qm[c], kc, (((1,), (1,)), ((), ())), preferred_element_type=F32)
            s_ref[c, j] = s
            m = s[:, 0:V7X_LANES]
            for t in range(1, nl):
                m = jnp.maximum(m, s[:, t * V7X_LANES:(t + 1) * V7X_LANES])
            m_ref[c] = jnp.maximum(m_ref[c], m)
        return carry

    lax.fori_loop(0, nchunk, score_body, 0, unroll=unroll)
    m_row = [jnp.max(m_ref[c], axis=1, keepdims=True) for c in range(2)]

    def pv_body(j, carry):
        vc = v_ref[pl.ds(pl.multiple_of(j * kb, kb), kb), :]
        for c in range(2):
            p = jnp.exp2(s_ref[c, j] - m_row[c])
            acc_ref[c] += jnp.dot(p.astype(BF16), vc, preferred_element_type=F32)
        return carry

    lax.fori_loop(0, nchunk, pv_body, 0, unroll=unroll)
    outs = [acc_ref[c, :, 0:DA_V_DIM] / acc_ref[c, :, DA_V_DIM:2 * DA_V_DIM] for c in range(2)]
    o = outs[0] - lam_ref[0] * outs[1]
    ms = jnp.mean(o * o, axis=1, keepdims=True)
    o_ref[...] = (o * lax.rsqrt(ms + LN_EPS) * g_ref[...] * (1.0 - LAM_INIT)).astype(o_ref.dtype)


def _attention(q, k, v, lam, subln_g, B, L, qb=512, kb=1024, unroll=8):
    T = q.shape[0]
    nq = L // qb
    return pl.pallas_call(
        functools.partial(_attn_kernel, kb=kb, unroll=unroll),
        grid=(B, DA_HEADS, nq),
        in_specs=[
            pl.BlockSpec(memory_space=pltpu.SMEM),
            pl.BlockSpec((qb, DA_V_DIM), lambda b, h, i: (b * nq + i, h)),
            pl.BlockSpec((L, DA_V_DIM), lambda b, h, i: (b, h)),
            pl.BlockSpec((L, 2 * DA_V_DIM), lambda b, h, i: (b, h)),
            pl.BlockSpec((1, DA_V_DIM), lambda b, h, i: (0, 0)),
        ],
        out_specs=pl.BlockSpec((qb, DA_V_DIM), lambda b, h, i: (b * nq + i, h)),
        out_shape=jax.ShapeDtypeStruct((T, ATTN_WIDTH), BF16),
        scratch_shapes=[
            pltpu.VMEM((2, L // kb, qb, kb), F32),
            pltpu.VMEM((2, qb, V7X_LANES), F32),
            pltpu.VMEM((2, qb, 2 * DA_V_DIM), F32),
        ],
        compiler_params=_cparams(("parallel", "parallel", "parallel"), 56),
        name="attn",
    )(lam, q, k, v, subln_g)


def _hpre_kernel(u_ref, up_ref, un_ref, w_ref, b_ref, x0_ref, vx_ref):
    i = pl.program_id(1)
    n = pl.num_programs(1)
    tb = u_ref.shape[0]
    C = x0_ref.shape[-1]
    row = lax.broadcasted_iota(jnp.int32, (tb, V7X_LANES), 0)
    has_prev = (i > 0).astype(F32)
    has_next = (i < n - 1).astype(F32)

    def conv(c0):
        sl = slice(c0, c0 + V7X_LANES)
        u = u_ref[:, sl]
        prev_row = up_ref[V7X_SUBLANES - 1:V7X_SUBLANES, sl] * has_prev
        next_row = un_ref[0:1, sl] * has_next
        u_prev = jnp.where(row == 0, prev_row, pltpu.roll(u, 1, axis=0))
        u_next = jnp.where(row == tb - 1, next_row, pltpu.roll(u, tb - 1, axis=0))
        return u_prev * w_ref[0:1, sl] + u * w_ref[1:2, sl] + u_next * w_ref[2:3, sl] + b_ref[:, sl]

    tile = x0_ref.shape[:-1] + (V7X_LANES,)
    for c in range(C // V7X_LANES):
        c0 = c * V7X_LANES
        x0_ref[:, :, :, c0:c0 + V7X_LANES] = conv(c0).reshape(tile)
        vx_ref[:, :, :, c0:c0 + V7X_LANES] = (conv(2 * C + c0) * conv(C + c0)).reshape(tile)


def _hpre(u, conv_w, conv_b, B, L, tb=512):
    T, C3 = u.shape
    C = C3 // 3
    nt = L // tb
    sub = V7X_SUBLANES
    na = tb // DFT_N2
    split = pl.BlockSpec((None, na, DFT_N2H, sub, C), lambda b, i: (b, i, 0, 0, 0))
    split_shape = jax.ShapeDtypeStruct((B, L // DFT_N2, DFT_N2H, sub, C), F32)
    cur = lambda b, i: (b * nt + i, 0)
    prev = lambda b, i: (jnp.maximum((b * L + i * tb) // sub - 1, 0), 0)
    nxt = lambda b, i: (jnp.minimum((b * L + (i + 1) * tb) // sub, T // sub - 1), 0)
    fixed = lambda b, i: (0, 0)
    return pl.pallas_call(
        _hpre_kernel,
        grid=(B, nt),
        in_specs=[
            pl.BlockSpec((tb, C3), cur),
            pl.BlockSpec((sub, C3), prev),
            pl.BlockSpec((sub, C3), nxt),
            pl.BlockSpec((SHORT_CONV, C3), fixed),
            pl.BlockSpec((1, C3), fixed),
        ],
        out_specs=[split, split],
        out_shape=[split_shape, split_shape],
        compiler_params=_cparams(("parallel", "parallel"), 32),
        name="hpre",
    )(u, u, u, conv_w, conv_b)


def _filt_kernel(z_ref, w1_ref, b1_ref, w2_ref, b2_ref, w3_ref, b3_ref, fr_ref, wo_ref, dl_ref,
                 hfb_ref, asum_ref, *, L):
    i = pl.program_id(0)
    tl = z_ref.shape[0]
    C = dl_ref.shape[1]
    hp = lax.Precision.HIGHEST
    h = jnp.sin(fr_ref[0:1, :] * (jnp.dot(z_ref[...], w1_ref[...], precision=hp, preferred_element_type=F32) + b1_ref[...]))
    h = jnp.sin(fr_ref[1:2, :] * (jnp.dot(h, w2_ref[...], precision=hp, preferred_element_type=F32) + b2_ref[...]))
    h = jnp.sin(fr_ref[2:3, :] * (jnp.dot(h, w3_ref[...], precision=hp, preferred_element_type=F32) + b3_ref[...]))
    o = jnp.dot(h, wo_ref[...], precision=hp, preferred_element_type=F32)
    grow = lax.broadcasted_iota(jnp.int32, (tl, C), 0) + i * tl
    t = grow.astype(F32) * (1.0 / (L - 1))
    decay = jnp.exp(-t * dl_ref[...])
    hf = o[:, :C] * decay
    hb = jnp.where(grow == 0, 0.0, o[:, C:] * decay)
    tile = hfb_ref.shape[:-1] + (C,)
    hfb_ref[:, :, :, :C] = hf.reshape(tile)
    hfb_ref[:, :, :, C:] = hb.reshape(tile)

    @pl.when(i == 0)
    def _():
        asum_ref[...] = jnp.zeros_like(asum_ref)

    asum_ref[...] += jnp.sum(jnp.abs(hf) + jnp.abs(hb), axis=0, keepdims=True)


def _filter_taps(L, w1, b1, w2, b2, w3, b3, freq, wout, tl=512):
    C = wout.shape[1] // 2
    order = w1.shape[1]
    emb = w1.shape[0]
    t = jnp.linspace(0.0, 1.0, L, dtype=F32)[:, None]
    w = 2.0 * math.pi * jnp.arange(L, dtype=F32)[:, None] / L
    f = jnp.linspace(1e-4, FILTER_BANDS - 1, FILTER_BANDS, dtype=F32)[None, :]
    z = jnp.concatenate([t, jnp.cos(f * w), -jnp.sin(f * w)], axis=-1)
    zp = jnp.pad(z, ((0, 0), (0, V7X_LANES - emb)))
    w1p = jnp.pad(w1, ((0, V7X_LANES - emb), (0, 0)))
    deltas = jnp.abs(jnp.linspace(math.log(DECAY_TARGET) / SLOW_DECAY, math.log(DECAY_TARGET) / FAST_DECAY, C, dtype=F32))[None]
    fixed = lambda i: (0, 0)
    return pl.pallas_call(
        functools.partial(_filt_kernel, L=L),
        grid=(L // tl,),
        in_specs=[
            pl.BlockSpec((tl, V7X_LANES), lambda i: (i, 0)),
            pl.BlockSpec((V7X_LANES, order), fixed), pl.BlockSpec((1, order), fixed),
            pl.BlockSpec((order, order), fixed), pl.BlockSpec((1, order), fixed),
            pl.BlockSpec((order, order), fixed), pl.BlockSpec((1, order), fixed),
            pl.BlockSpec((3, order), fixed),
            pl.BlockSpec((order, 2 * C), fixed),
            pl.BlockSpec((1, C), fixed),
        ],
        out_specs=[pl.BlockSpec((None, tl // DFT_N2, DFT_N2H, V7X_SUBLANES, 2 * C), lambda i: (0, i, 0, 0, 0)),
                   pl.BlockSpec((1, C), fixed)],
        out_shape=[jax.ShapeDtypeStruct((1, L // DFT_N2, DFT_N2H, V7X_SUBLANES, 2 * C), F32),
                   jax.ShapeDtypeStruct((1, C), F32)],
        compiler_params=_cparams(("arbitrary",), 32),
        name="filt",
    )(zp, w1p, b1[None], w2, b2[None], w3, b3[None], freq, wout, deltas)


def _dft_constants(L):
    n1 = 2 * L // DFT_N2
    n1h = n1 // 2
    N = 2 * L
    sub = V7X_SUBLANES

    def cs(num, den):
        ang = (num % den).astype(np.float64) * (2.0 * np.pi / den)
        return np.cos(ang), np.sin(ang)

    k1 = np.arange(n1, dtype=np.int64)
    a = np.arange(n1h, dtype=np.int64)
    c, s = cs(k1[:, None] * a[None, :], n1)
    eye = np.eye(sub)

    def expand(m):
        r, kk = m.shape
        return (m[:, None, :, None] * eye[None, :, None, :]).reshape(r * sub, kk * sub)

    def const(m):
        return jnp.asarray(m.astype(np.float32).astype(BF16))

    m1c = np.stack([np.concatenate([c, s], axis=1), np.concatenate([-s, c], axis=1)], axis=1).reshape(2 * n1, 2 * n1h)
    m1r = np.stack([c, -s], axis=1).reshape(2 * n1, n1h)
    ct, st = c.T, s.T
    m3 = np.stack([np.stack([ct, -st], axis=2).reshape(n1h, 2 * n1),
                   np.stack([st, ct], axis=2).reshape(n1h, 2 * n1)], axis=0).reshape(2 * n1h, 2 * n1)
    n2 = np.arange(DFT_N2, dtype=np.int64)
    gc, gs = cs(n2[:, None] * n2[None, :], DFT_N2)
    g_fwd = np.concatenate([np.concatenate([gc, gs], axis=1), np.concatenate([-gs, gc], axis=1)], axis=0)
    g_inv = np.concatenate([np.concatenate([gc, -gs], axis=1), np.concatenate([gs, gc], axis=1)], axis=0)
    ph = (jnp.arange(n1, dtype=jnp.int32)[:, None] * jnp.arange(DFT_N2, dtype=jnp.int32)[None, :]) % N
    ang = ph.astype(F32) * (2.0 * math.pi / N)
    lanes = (n1, DFT_N2H, sub, V7X_LANES)
    twc = jnp.broadcast_to(jnp.cos(ang).reshape(n1, DFT_N2H, sub, 1), lanes)
    tws = jnp.broadcast_to(jnp.sin(ang).reshape(n1, DFT_N2H, sub, 1), lanes)
    return dict(n1=n1, n1h=n1h, m1c=const(expand(m1c)), m1r=const(expand(m1r)), m3=const(expand(m3)),
                g_fwd=const(g_fwd), g_inv=const(g_inv), twc=twc, tws=tws)


def _dft1_kernel(x_ref, m_ref, o_ref):
    rows = m_ref.shape[1]
    xs = x_ref[...].reshape(rows, x_ref.shape[-1]).astype(BF16)
    o_ref[...] = jnp.dot(m_ref[...], xs, preferred_element_type=F32).reshape(o_ref.shape)


def _dft1(x5, m, n1, cb):
    P, n1h, _, sub, Cx = x5.shape
    return pl.pallas_call(
        _dft1_kernel,
        grid=(DFT_N2H, Cx // cb),
        in_specs=[
            pl.BlockSpec((P, n1h, None, sub, cb), lambda h, c: (0, 0, h, 0, c)),
            pl.BlockSpec(m.shape, lambda h, c: (0, 0)),
        ],
        out_specs=pl.BlockSpec((None, n1, 2, sub, cb), lambda h, c: (h, 0, 0, 0, c)),
        out_shape=jax.ShapeDtypeStruct((DFT_N2H, n1, 2, sub, Cx), F32),
        compiler_params=_cparams(("parallel", "parallel"), 48),
        name="dft1",
    )(x5, m)


def _dft2_kernel(a_ref, f_ref, twc_ref, tws_ref, gf_ref, gi_ref, o_ref):
    C = a_ref.shape[-1]
    half = DFT_N2

    def lanes(fn):
        return jnp.concatenate([fn(slice(c0, c0 + V7X_LANES)) for c0 in range(0, C, V7X_LANES)], axis=-1)

    for kk in range(a_ref.shape[1]):
        twc, tws = twc_ref[kk], tws_ref[kk]

        def spectrum(ref, col0):
            def re_part(sl):
                return ref[:, kk, 0, :, col0 + sl.start:col0 + sl.stop] * twc + ref[:, kk, 1, :, col0 + sl.start:col0 + sl.stop] * tws

            def im_part(sl):
                return ref[:, kk, 1, :, col0 + sl.start:col0 + sl.stop] * twc - ref[:, kk, 0, :, col0 + sl.start:col0 + sl.stop] * tws

            t = jnp.concatenate([lanes(re_part).reshape(half, C), lanes(im_part).reshape(half, C)], axis=0).astype(BF16)
            s = jnp.dot(gf_ref[...], t, preferred_element_type=F32)
            return s[:half], s[half:]

        xr, xi = spectrum(a_ref, 0)
        fr, fi = spectrum(f_ref, 0)
        br, bi = spectrum(f_ref, C)
        hr, hi = fr + br, fi - bi
        y = jnp.concatenate([xr * hr - xi * hi, xr * hi + xi * hr], axis=0).astype(BF16)
        b = jnp.dot(gi_ref[...], y, preferred_element_type=F32)
        br2 = b[:half].reshape(DFT_N2H, V7X_SUBLANES, C)
        bi2 = b[half:].reshape(DFT_N2H, V7X_SUBLANES, C)
        for c0 in range(0, C, V7X_LANES):
            sl = slice(c0, c0 + V7X_LANES)
            o_ref[:, kk, 0, :, sl] = br2[:, :, sl] * twc - bi2[:, :, sl] * tws
            o_ref[:, kk, 1, :, sl] = bi2[:, :, sl] * twc + br2[:, :, sl] * tws


def _dft2(a5, f5, k, kb=4):
    n2h, n1, _, sub, C = a5.shape
    kb = min(kb, n1)
    return pl.pallas_call(
        _dft2_kernel,
        grid=(n1 // kb,),
        in_specs=[
            pl.BlockSpec((n2h, kb, 2, sub, C), lambda i: (0, i, 0, 0, 0)),
            pl.BlockSpec((n2h, kb, 2, sub, 2 * C), lambda i: (0, i, 0, 0, 0)),
            pl.BlockSpec((kb, n2h, sub, V7X_LANES), lambda i: (i, 0, 0, 0)),
            pl.BlockSpec((kb, n2h, sub, V7X_LANES), lambda i: (i, 0, 0, 0)),
            pl.BlockSpec((2 * DFT_N2, 2 * DFT_N2), lambda i: (0, 0)),
            pl.BlockSpec((2 * DFT_N2, 2 * DFT_N2), lambda i: (0, 0)),
        ],
        out_specs=pl.BlockSpec((n2h, kb, 2, sub, C), lambda i: (0, i, 0, 0, 0)),
        out_shape=jax.ShapeDtypeStruct(a5.shape, F32),
        compiler_params=_cparams(("parallel",), 48),
        name="dft2",
    )(a5, f5, k["twc"], k["tws"], k["g_fwd"], k["g_inv"])


def _dft3_kernel(b_ref, m_ref, x0_ref, vx_ref, sc_ref, d_ref, o_ref):
    C = b_ref.shape[-1]
    bs = b_ref[...].reshape(m_ref.shape[1], C).astype(BF16)
    y = jnp.dot(m_ref[...], bs, preferred_element_type=F32).reshape(o_ref.shape)
    o_ref[...] = x0_ref[...] * (y * sc_ref[...] + vx_ref[...] * d_ref[...])


def _dft3(b5, m3, x05, vx5, scale, d):
    n2h, n1, _, sub, C = b5.shape
    Bt, n1h = x05.shape[0], x05.shape[1]
    tok = pl.BlockSpec((Bt, n1h, None, sub, C), lambda h: (0, 0, h, 0, 0))
    vec = pl.BlockSpec((1, C), lambda h: (0, 0))
    return pl.pallas_call(
        _dft3_kernel,
        grid=(n2h,),
        in_specs=[
            pl.BlockSpec((None, n1, 2, sub, C), lambda h: (h, 0, 0, 0, 0)),
            pl.BlockSpec(m3.shape, lambda h: (0, 0)),
            tok, tok, vec, vec,
        ],
        out_specs=tok,
        out_shape=jax.ShapeDtypeStruct(x05.shape, F32),
        compiler_params=_cparams(("parallel",), 48),
        name="dft3",
    )(b5, m3, x05, vx5, scale, d)


def _hyena(u, conv_w, conv_b, w1, b1, w2, b2, w3, b3, freq, wout, d_skip, B, L):
    T = u.shape[0]
    C = HYENA_WIDTH
    k = _dft_constants(L)
    n1, n1h = k["n1"], k["n1h"]
    x0, vx = _hpre(u, conv_w, conv_b[None], B, L)
    hfb, asum = _filter_taps(L, w1, b1, w2, b2, w3, b3, freq, wout)
    a5 = _dft1(vx, k["m1c"], n1, cb=C)
    f5 = _dft1(hfb, k["m1r"], n1, cb=C)
    b5 = _dft2(a5, f5, k)
    scale = 1.0 / (asum * (2 * L))
    return _dft3(b5, k["m3"], x0, vx, scale, d_skip[None])


PACKED = jnp.uint32


def _pack_pairs(x):
    half = x.shape[1] // 2
    bits = pltpu.bitcast(x.astype(BF16).astype(F32), PACKED)
    return (bits[:, :half] >> 16) | bits[:, half:]


def _unpack_pairs(w):
    lo = pltpu.bitcast(w << 16, F32)
    hi = pltpu.bitcast(w & jnp.uint32(0xFFFF0000), F32)
    return lo, hi


def _oproj_kernel(x_ref, g0_ref, b0_ref, at_ref, hy_ref, wo_ref, g1_ref, b1_ref, wrh_ref, wrl_ref,
                  h1_ref, h1p_ref, sc_ref):
    aw = at_ref.shape[1]
    h0 = _layer_norm_rows(x_ref[...], g0_ref[...], b0_ref[...])
    mixed = jnp.dot(at_ref[...], wo_ref[0:aw, :], preferred_element_type=F32)
    hy = hy_ref[...].reshape(x_ref.shape[0], hy_ref.shape[-1])
    mixed = mixed + jnp.dot(hy.astype(BF16), wo_ref[aw:, :], preferred_element_type=F32)
    h1 = _layer_norm_rows(ALPHA * h0 + mixed, g1_ref[...], b1_ref[...])
    h1_ref[...] = h1
    h1p_ref[...] = _pack_pairs(h1)
    hh = h1.astype(BF16)
    hl = (h1 - hh.astype(F32)).astype(BF16)
    dn = (((1,), (1,)), ((), ()))
    logits = lax.dot_general(wrh_ref[...], hh, dn, preferred_element_type=F32)
    logits = logits + (lax.dot_general(wrh_ref[...], hl, dn, preferred_element_type=F32)
                       + lax.dot_general(wrl_ref[...], hh, dn, preferred_element_type=F32))
    sc_ref[...] = 1.0 / (1.0 + jnp.exp(-logits))


def _oproj(x2, g0, b0, attn, hy, wo_bf, g1, b1, wr_t, tm=256):
    T, D = x2.shape
    E = wr_t.shape[0]
    wrh = wr_t.astype(BF16)
    wrl = (wr_t - wrh.astype(F32)).astype(BF16)
    row = lambda i: (i, 0)
    fixed = lambda i: (0, 0)
    hy4 = hy.reshape((-1,) + hy.shape[2:])
    return pl.pallas_call(
        _oproj_kernel,
        grid=(T // tm,),
        in_specs=[
            pl.BlockSpec((tm, D), row), pl.BlockSpec((1, D), fixed), pl.BlockSpec((1, D), fixed),
            pl.BlockSpec((tm, attn.shape[1]), row),
            pl.BlockSpec((tm // DFT_N2,) + hy4.shape[1:], lambda i: (i, 0, 0, 0)),
            pl.BlockSpec((D, D), fixed), pl.BlockSpec((1, D), fixed), pl.BlockSpec((1, D), fixed),
            pl.BlockSpec((E, D), fixed), pl.BlockSpec((E, D), fixed),
        ],
        out_specs=[pl.BlockSpec((tm, D), row), pl.BlockSpec((tm, D // 2), row), pl.BlockSpec((E, tm), lambda i: (0, i))],
        out_shape=[
            jax.ShapeDtypeStruct((T, D), F32),
            jax.ShapeDtypeStruct((T, D // 2), PACKED),
            jax.ShapeDtypeStruct((E, T), F32),
        ],
        compiler_params=_cparams(("parallel",), 48),
        name="oproj",
    )(x2, g0, b0, attn, hy4, wo_bf, g1, b1, wrh, wrl)


def _route_kernel(sc_ref, bias_ref, tri_ref, idx_ref, gate_ref, rank_ref, cnt_ref, carry_ref):
    step = pl.program_id(0)

    @pl.when(step == 0)
    def _():
        carry_ref[...] = jnp.zeros_like(carry_ref)

    tm = V7X_LANES
    for part in range(sc_ref.shape[1] // tm):
        cols = slice(part * tm, (part + 1) * tm)
        _route_tokens(sc_ref[:, cols], bias_ref, tri_ref, idx_ref, gate_ref, rank_ref, carry_ref, cols)
    cnt_ref[...] = carry_ref[...].astype(jnp.int32)


def _route_tokens(scores, bias_ref, tri_ref, idx_ref, gate_ref, rank_ref, carry_ref, cols):
    E, tm = scores.shape
    neg = jnp.float32(-jnp.inf)
    biased = scores + bias_ref[...]
    erow = lax.broadcasted_iota(jnp.int32, (E, tm), 0)
    big = jnp.int32(E)

    def first_argmax(vals, rows):
        m = jnp.max(vals, axis=0, keepdims=True)
        pick = jnp.min(jnp.where(vals == m, rows, big), axis=0, keepdims=True)
        return m, pick

    gsc = []
    for g in range(N_GROUPS):
        blk = biased[g * GROUP_SIZE:(g + 1) * GROUP_SIZE, :]
        rows = erow[g * GROUP_SIZE:(g + 1) * GROUP_SIZE, :]
        m1, p1 = first_argmax(blk, rows)
        m2 = jnp.max(jnp.where(rows == p1, neg, blk), axis=0, keepdims=True)
        gsc.append(m1 + m2)
    gsc = jnp.concatenate(gsc, axis=0)
    grow = lax.broadcasted_iota(jnp.int32, (N_GROUPS, tm), 0)
    gsel = jnp.zeros((N_GROUPS, tm), jnp.bool_)
    work = gsc
    for _ in range(TOPK_GROUPS):
        _, p = first_argmax(work, grow)
        hit = grow == p
        gsel = gsel | hit
        work = jnp.where(hit, neg, work)
    emask = jnp.concatenate(
        [jnp.broadcast_to(gsel[g:g + 1, :], (GROUP_SIZE, tm)) for g in range(N_GROUPS)], axis=0)
    work = jnp.where(emask, biased, neg)

    sel = jnp.zeros((E, tm), jnp.bool_)
    picks, gvals = [], []
    for _ in range(TOP_K):
        _, p = first_argmax(work, erow)
        hit = erow == p
        sel = sel | hit
        picks.append(p)
        gvals.append(jnp.sum(jnp.where(hit, scores, 0.0), axis=0, keepdims=True))
        work = jnp.where(hit, neg, work)
    gv = jnp.concatenate(gvals, axis=0)
    idx_ref[:, cols] = jnp.concatenate(picks, axis=0)
    gate_ref[:, cols] = gv / jnp.sum(gv, axis=0, keepdims=True) * ROUTED_SCALE

    chosen = sel.astype(F32)
    before = jnp.dot(chosen.astype(BF16), tri_ref[...], preferred_element_type=F32) + carry_ref[...]
    rank_ref[:, cols] = jnp.concatenate(
        [jnp.sum(jnp.where(erow == p, before, 0.0), axis=0, keepdims=True) for p in picks], axis=0).astype(jnp.int32)
    carry_ref[...] += jnp.sum(chosen, axis=1, keepdims=True)


def _route(scores_t, bias, tm=256):
    E, T = scores_t.shape
    tm = min(tm, T)
    lanes = V7X_LANES
    tri = (jnp.arange(lanes)[:, None] < jnp.arange(lanes)[None, :]).astype(BF16)
    tok = lambda i: (0, i)
    fixed = lambda i: (0, 0)
    return pl.pallas_call(
        _route_kernel,
        grid=(T // tm,),
        in_specs=[pl.BlockSpec((E, tm), tok), pl.BlockSpec((E, 1), fixed), pl.BlockSpec((lanes, lanes), fixed)],
        out_specs=[pl.BlockSpec((TOP_K, tm), tok), pl.BlockSpec((TOP_K, tm), tok), pl.BlockSpec((TOP_K, tm), tok),
                   pl.BlockSpec((E, 1), fixed)],
        out_shape=[
            jax.ShapeDtypeStruct((TOP_K, T), jnp.int32),
            jax.ShapeDtypeStruct((TOP_K, T), F32),
            jax.ShapeDtypeStruct((TOP_K, T), jnp.int32),
            jax.ShapeDtypeStruct((E, 1), jnp.int32),
        ],
        scratch_shapes=[pltpu.VMEM((E, 1), F32)],
        compiler_params=_cparams(("arbitrary",), 32),
        name="route",
    )(scores_t, bias, tri)


def _dest_kernel(idx_ref, rank_ref, ps_ref, dest_ref):
    E = ps_ref.shape[0]
    tm = idx_ref.shape[1]
    erow = lax.broadcasted_iota(jnp.int32, (E, tm), 0)
    ps = ps_ref[...].astype(F32)
    rows = [jnp.sum(jnp.where(erow == idx_ref[k:k + 1, :], ps, 0.0), axis=0, keepdims=True) for k in range(TOP_K)]
    dest_ref[...] = jnp.concatenate(rows, axis=0).astype(jnp.int32) + rank_ref[...]


def _dest_rows(idx, rank, pad_start, tm=512):
    K, T = idx.shape
    E = pad_start.shape[0]
    tok = lambda i: (0, i)
    return pl.pallas_call(
        _dest_kernel,
        grid=(T // tm,),
        in_specs=[pl.BlockSpec((K, tm), tok), pl.BlockSpec((K, tm), tok), pl.BlockSpec((E, 1), lambda i: (0, 0))],
        out_specs=pl.BlockSpec((K, tm), tok),
        out_shape=jax.ShapeDtypeStruct((K, T), jnp.int32),
        compiler_params=_cparams(("arbitrary",), 32),
        name="dest",
    )(idx, rank, pad_start[:, None])


def _scatter_kernel(ps_ref, pe_ref, dest_ref, h_ref, xs_ref, zbuf, sem, *, rb):
    sub = V7X_SUBLANES
    tm = h_ref.shape[0]

    @pl.when(pl.program_id(0) == 0)
    def _():
        zbuf[...] = jnp.zeros_like(zbuf)

        def zcopy(e):
            return pltpu.make_async_copy(zbuf, xs_ref.at[pl.ds(pl.multiple_of(pe_ref[e] - rb, rb), rb)], sem)

        def zstart(e, c):
            @pl.when(pe_ref[e] > ps_ref[e])
            def _():
                zcopy(e).start()
            return c

        def zwait(e, c):
            @pl.when(pe_ref[e] > ps_ref[e])
            def _():
                zcopy(e).wait()
            return c

        lax.fori_loop(0, ps_ref.shape[0], zstart, 0)
        lax.fori_loop(0, ps_ref.shape[0], zwait, 0)

    def issue(r8, c):
        for j in range(sub):
            for k in range(TOP_K):
                dst = dest_ref[0, r8 * (sub * TOP_K) + (j * TOP_K + k)]
                pltpu.make_async_copy(h_ref.at[pl.ds(r8 * sub + j, 1)], xs_ref.at[pl.ds(dst, 1)], sem).start(priority=k % 2)
        return c

    lax.fori_loop(0, tm // sub, issue, 0)
    for k in range(TOP_K):
        pltpu.make_async_copy(h_ref, xs_ref.at[pl.ds(0, tm)], sem).wait()


def _dest_tiles(dest, tm):
    K, T = dest.shape
    return dest.T.reshape(T // tm, 1, tm * K)


def _scatter_rows(pad_start, pad_end, dest_t, h1p, n_rows, rb, tm=256):
    T, W = h1p.shape
    return pl.pallas_call(
        functools.partial(_scatter_kernel, rb=rb),
        grid_spec=pltpu.PrefetchScalarGridSpec(
            num_scalar_prefetch=2,
            grid=(T // tm,),
            in_specs=[
                pl.BlockSpec((None, 1, tm * TOP_K), lambda i, ps, pe: (i, 0, 0), memory_space=pltpu.SMEM),
                pl.BlockSpec((tm, W), lambda i, ps, pe: (i, 0)),
            ],
            out_specs=pl.BlockSpec(memory_space=pl.ANY),
            scratch_shapes=[pltpu.VMEM((rb, W), h1p.dtype), pltpu.SemaphoreType.DMA],
        ),
        out_shape=jax.ShapeDtypeStruct((n_rows, W), h1p.dtype),
        compiler_params=_cparams(("arbitrary",), 32),
        name="scatter",
    )(pad_start, pad_end, dest_t, h1p)


EXPERT_X_SLOTS = 4
EXPERT_Y_SLOTS = 3


def _experts_kernel(first_ref, nblk_ref, wsel_ref, tot_ref, xs_ref, wg_ref, wu_ref, wd_ref, ys_ref,
                    xbuf, ybuf, wgub, wdb, xsem, ysem, *, rb):
    del wsel_ref
    e = pl.program_id(0)
    F = wg_ref.shape[1]
    nx, ny = xbuf.shape[0], ybuf.shape[0]
    total = tot_ref[0]
    n = nblk_ref[e]
    g0 = first_ref[e]

    def x_copy(g):
        slot = g % nx
        return pltpu.make_async_copy(xs_ref.at[pl.ds(pl.multiple_of(g * rb, rb), rb)], xbuf.at[slot], xsem.at[slot])

    def y_copy(g):
        slot = g % ny
        return pltpu.make_async_copy(ybuf.at[slot], ys_ref.at[pl.ds(pl.multiple_of(g * rb, rb), rb)], ysem.at[slot])

    @pl.when(e == 0)
    def _():
        for g in range(nx):
            @pl.when(g < total)
            def _():
                x_copy(g).start()

    def process(g, count):
        for d in range(count):
            x_copy(g + d).wait()
        ys = []
        for d in range(count):
            lo, hi = _unpack_pairs(xbuf[(g + d) % nx])
            x = jnp.concatenate([lo, hi], axis=1).astype(BF16)
            gu = jnp.dot(x, wgub[...], preferred_element_type=F32)
            gate, up = gu[:, 0:F], gu[:, F:]
            hb = (gate / (1.0 + jnp.exp(-gate)) * up).astype(BF16)
            ys.append(_pack_pairs(jnp.dot(hb, wdb[...], preferred_element_type=F32)))
        for d in range(count):
            @pl.when(g + d >= ny)
            def _():
                y_copy(g + d - ny).wait()

            ybuf[(g + d) % ny] = ys[d]
            y_copy(g + d).start()
        for d in range(count):
            @pl.when(g + d + nx < total)
            def _():
                x_copy(g + d + nx).start()

    @pl.when(n > 0)
    def _():
        wgub[:, 0:F] = wg_ref[...].astype(BF16)
        wgub[:, F:] = wu_ref[...].astype(BF16)
        wdb[...] = wd_ref[...].astype(BF16)

        def pair(j, c):
            process(g0 + 2 * j, 2)
            return c

        lax.fori_loop(0, n // 2, pair, 0)

        @pl.when(n % 2 == 1)
        def _():
            process(g0 + n - 1, 1)

    @pl.when(e == pl.num_programs(0) - 1)
    def _():
        for back in range(ny, 0, -1):
            @pl.when(total >= back)
            def _():
                y_copy(total - back).wait()


def _experts(first_blk, nblk, wsel, total_blk, xs, w_gate, w_up, w_down, rb):
    P, W = xs.shape
    E, D, F = w_gate.shape
    wspec = lambda shape: pl.BlockSpec((None,) + shape, lambda e, fb, nbk, ws, tt: (ws[e], 0, 0))
    return pl.pallas_call(
        functools.partial(_experts_kernel, rb=rb),
        grid_spec=pltpu.PrefetchScalarGridSpec(
            num_scalar_prefetch=4,
            grid=(E,),
            in_specs=[pl.BlockSpec(memory_space=pl.ANY), wspec((D, F)), wspec((D, F)), wspec((F, D))],
            out_specs=pl.BlockSpec(memory_space=pl.ANY),
            scratch_shapes=[
                pltpu.VMEM((EXPERT_X_SLOTS, rb, W), PACKED), pltpu.VMEM((EXPERT_Y_SLOTS, rb, W), PACKED),
                pltpu.VMEM((D, 2 * F), BF16), pltpu.VMEM((F, D), BF16),
                pltpu.SemaphoreType.DMA((EXPERT_X_SLOTS,)), pltpu.SemaphoreType.DMA((EXPERT_Y_SLOTS,)),
            ],
        ),
        out_shape=jax.ShapeDtypeStruct((P, W), PACKED),
        compiler_params=_cparams(("arbitrary",), 48),
        name="experts",
    )(first_blk, nblk, wsel, total_blk, xs, w_gate, w_up, w_down)


def _combine_kernel(dest_ref, destn_ref, h1_ref, gate_ref, ys_ref, ys8_ref, sg_ref, su_ref, sd_ref,
                    g2_ref, b2_ref, o_ref, buf, sem):
    i = pl.program_id(0)
    n = pl.num_programs(0)
    tm = h1_ref.shape[0]
    slot = i % 2

    sub = V7X_SUBLANES

    def issue_rows(dref, s, r0, nrows):
        for r in range(r0, r0 + nrows):
            for k in range(TOP_K):
                src = ys_ref.at[pl.ds(dref[0, r * TOP_K + k], 1)]
                pltpu.make_async_copy(src, buf.at[s, r, pl.ds(k, 1)], sem.at[s]).start(priority=k % 2)

    def wait_tile(s):
        pltpu.make_async_copy(ys8_ref.at[pl.ds(0, tm)], buf.at[s], sem.at[s]).wait()

    @pl.when(i == 0)
    def _():
        def body(r8, c):
            for j in range(sub):
                for k in range(TOP_K):
                    src = ys_ref.at[pl.ds(dest_ref[0, r8 * (sub * TOP_K) + (j * TOP_K + k)], 1)]
                    pltpu.make_async_copy(src, buf.at[0, r8 * sub + j, pl.ds(k, 1)], sem.at[0]).start(priority=k % 2)
            return c
        lax.fori_loop(0, tm // sub, body, 0)

    wait_tile(slot)
    parts = 4
    rows_per = tm // parts
    nk = rows_per * TOP_K
    pick = lax.broadcasted_iota(jnp.int32, (TOP_K, nk), 1) % TOP_K == lax.broadcasted_iota(jnp.int32, (TOP_K, nk), 0)
    spread = pick.astype(BF16)
    own = lax.broadcasted_iota(jnp.int32, (rows_per, nk), 1) // TOP_K == lax.broadcasted_iota(jnp.int32, (rows_per, nk), 0)
    for part in range(parts):
        r0 = part * rows_per
        issue_rows(destn_ref, 1 - slot, r0, rows_per)
        rows = slice(r0, r0 + rows_per)
        h1 = h1_ref[rows, :]
        hb = h1.astype(BF16)
        g = jnp.dot(hb, sg_ref[...], preferred_element_type=F32)
        u = jnp.dot(hb, su_ref[...], preferred_element_type=F32)
        ffn = jnp.dot((g / (1.0 + jnp.exp(-g)) * u).astype(BF16), sd_ref[...], preferred_element_type=F32)
        lo, hi = _unpack_pairs(buf[slot, rows].reshape(nk, buf.shape[3]))
        y = jnp.concatenate([lo, hi], axis=1).astype(BF16)
        gates = gate_ref[rows, :]
        gh = gates.astype(BF16)
        gl = (gates - gh.astype(F32)).astype(BF16)
        routed = jnp.zeros((rows_per, y.shape[1]), F32)
        for piece in (gh, gl):
            gm = jnp.where(own, jnp.dot(piece, spread, preferred_element_type=F32), 0.0).astype(BF16)
            routed = routed + jnp.dot(gm, y, preferred_element_type=F32)
        o_ref[rows, :] = _layer_norm_rows(ALPHA * h1 + ffn + routed, g2_ref[...], b2_ref[...])

    @pl.when(i == n - 1)
    def _():
        wait_tile(1 - slot)


def _combine(dest, h1, gates_tk, ys, sg, su, sd, g2, b2, tm=512):
    T, D = h1.shape
    W = ys.shape[1]
    F = sg.shape[1]
    nt = T // tm
    row = lambda i: (i, 0)
    fixed = lambda i: (0, 0)
    smem = functools.partial(pl.BlockSpec, (None, 1, tm * TOP_K), memory_space=pltpu.SMEM)
    dest_t = _dest_tiles(dest, tm)
    ys8 = ys.reshape(ys.shape[0] // TOP_K, TOP_K, W)
    return pl.pallas_call(
        _combine_kernel,
        grid=(nt,),
        in_specs=[
            smem(index_map=lambda i: (i, 0, 0)),
            smem(index_map=lambda i: (jnp.minimum(i + 1, nt - 1), 0, 0)),
            pl.BlockSpec((tm, D), row),
            pl.BlockSpec((tm, TOP_K), row),
            pl.BlockSpec(memory_space=pl.ANY),
            pl.BlockSpec(memory_space=pl.ANY),
            pl.BlockSpec((D, F), fixed), pl.BlockSpec((D, F), fixed), pl.BlockSpec((F, D), fixed),
            pl.BlockSpec((1, D), fixed), pl.BlockSpec((1, D), fixed),
        ],
        out_specs=pl.BlockSpec((tm, D), row),
        out_shape=jax.ShapeDtypeStruct((T, D), F32),
        scratch_shapes=[pltpu.VMEM((2, tm, TOP_K, W), PACKED), pltpu.SemaphoreType.DMA((2,))],
        compiler_params=_cparams(("arbitrary",), 48),
        name="combine",
    )(dest_t, dest_t, h1, gates_tk, ys, ys8, sg, su, sd, g2, b2)


MOE_ROWS = 256


def _moe(h1, h1p, scores_t, router_bias, w_gate, w_up, w_down, ws_gate, ws_up, ws_down, g2, b2):
    T = h1.shape[0]
    E = N_EXPERTS
    rb = MOE_ROWS
    idx, gates, rank, counts = _route(scores_t, router_bias[:, None])
    counts = counts[:, 0]
    padded = (counts + rb - 1) // rb * rb
    pad_end = jnp.cumsum(padded)
    pad_start = pad_end - padded
    nb = (T * TOP_K) // rb + E
    nblk = (padded // rb).astype(jnp.int32)
    first_blk = (pad_start // rb).astype(jnp.int32)
    total_blk = (pad_end[-1:] // rb).astype(jnp.int32)
    wsel = lax.cummax(jnp.where(nblk > 0, jnp.arange(E, dtype=jnp.int32), 0))
    dest = _dest_rows(idx, rank, pad_start)
    scatter_tm = min(512, T)
    xs = _scatter_rows(pad_start, pad_end, _dest_tiles(dest, scatter_tm), h1p, nb * rb, rb, tm=scatter_tm)
    ys = _experts(first_blk, nblk, wsel, total_blk, xs, w_gate, w_up, w_down, rb)
    return _combine(dest, h1, gates.T, ys, ws_gate.astype(BF16), ws_up.astype(BF16), ws_down.astype(BF16), g2, b2)


def kernel(x, positions, emb_ln_g, emb_ln_b, w_in, hy_conv_w, hy_conv_b, hy_f_w1, hy_f_b1, hy_f_w2, hy_f_b2, hy_f_w3, hy_f_b3, hy_f_freq, hy_f_wout, hy_d, lambda_q1, lambda_k1, lambda_q2, lambda_k2, subln_g, w_o, ln1_g, ln1_b, w_router, router_bias, w_gate, w_up, w_down, ws_gate, ws_up, ws_down, ln2_g, ln2_b):
    B, L, D = x.shape
    T = B * L
    assert w_in.shape[0] == DEPTH == 1
    i = 0
    x2 = x.reshape(T, D)
    g0, b0 = emb_ln_g[None], emb_ln_b[None]
    ra, rm, rp = _rotary_tables(positions)
    q, k, v, u = _inproj(x2, g0, b0, w_in[i].astype(BF16), ra, rm, rp)
    lam = (jnp.exp(jnp.sum(lambda_q1[i] * lambda_k1[i])) - jnp.exp(jnp.sum(lambda_q2[i] * lambda_k2[i])) + LAM_INIT)
    attn = _attention(q, k, v, lam.reshape(1).astype(F32), subln_g[i][None], B, L)
    hy = _hyena(u, hy_conv_w[i], hy_conv_b[i], hy_f_w1[i], hy_f_b1[i], hy_f_w2[i], hy_f_b2[i], hy_f_w3[i], hy_f_b3[i],
                hy_f_freq[i], hy_f_wout[i], hy_d[i], B, L)
    h1, h1p, scores_t = _oproj(x2, g0, b0, attn, hy, w_o[i].astype(BF16), ln1_g[i][None], ln1_b[i][None], w_router[i].T)
    out = _moe(h1, h1p, scores_t, router_bias[i], w_gate[i], w_up[i], w_down[i], ws_gate[i], ws_up[i], ws_down[i],
               ln2_g[i][None], ln2_b[i][None])
    return out.reshape(B, L, D)
```

```python
import functools
import math

import numpy as np
import jax
import jax.numpy as jnp
from jax import lax
from jax.experimental import pallas as pl
from jax.experimental.pallas import tpu as pltpu

DA_HEADS = 4
DA_HEAD_DIM = 64
DA_V_DIM = 128
ATTN_WIDTH = 512
HYENA_WIDTH = 512
ROT_DIM = 16
ROPE_THETA = 500000.0
SHORT_CONV = 3
FILTER_EMB = 33
FILTER_BANDS = 16
DECAY_TARGET = 1e-2
FAST_DECAY = 0.3
SLOW_DECAY = 1.5
N_EXPERTS = 256
TOP_K = 8
N_GROUPS = 8
GROUP_SIZE = N_EXPERTS // N_GROUPS
TOPK_GROUPS = 4
EXPERT_DIM = 256
ROUTED_SCALE = 2.5
DEPTH = 1
ALPHA = (2 * DEPTH) ** 0.25
LN_EPS = 1e-5
LAM_INIT = 0.8 - 0.6 * math.exp(-0.3 * 0)

V7X_LANES = 128
V7X_SUBLANES = 8
V7X_VMEM_BYTES = 64 * 1024 * 1024

DFT_N2 = 128
DFT_N2H = DFT_N2 // V7X_SUBLANES

BF16 = jnp.bfloat16
F32 = jnp.float32


def _cparams(sem, vmem_mb):
    return pltpu.CompilerParams(dimension_semantics=sem, vmem_limit_bytes=vmem_mb * 1024 * 1024)


def _layer_norm_rows(x, g, b):
    mu = jnp.mean(x, axis=-1, keepdims=True)
    xc = x - mu
    var = jnp.mean(xc * xc, axis=-1, keepdims=True)
    return xc * lax.rsqrt(var + LN_EPS) * g + b


def _inproj_kernel(x_ref, g_ref, b_ref, w_ref, ra_ref, rm_ref, rp_ref, q_ref, k_ref, v_ref, u_ref):
    h = _layer_norm_rows(x_ref[...], g_ref[...], b_ref[...]).astype(BF16)
    ra, rm, rp = ra_ref[...], rm_ref[...], rp_ref[...]

    def rot(t):
        return t * ra + pltpu.roll(t, V7X_LANES - ROT_DIM // 2, axis=1) * rm + pltpu.roll(t, ROT_DIM // 2, axis=1) * rp

    aw = ATTN_WIDTH
    qp = jnp.dot(h, w_ref[:, 0:aw], preferred_element_type=F32)
    kp = jnp.dot(h, w_ref[:, aw:2 * aw], preferred_element_type=F32)
    scale = DA_HEAD_DIM ** -0.5 * math.log2(math.e)
    for c in range(aw // V7X_LANES):
        sl = slice(c * V7X_LANES, (c + 1) * V7X_LANES)
        q_ref[:, sl] = (rot(qp[:, sl]) * scale).astype(BF16)
        k_ref[:, sl] = rot(kp[:, sl]).astype(BF16)
    vp = jnp.dot(h, w_ref[:, 2 * aw:3 * aw], preferred_element_type=F32).astype(BF16)
    ones = jnp.ones((vp.shape[0], DA_V_DIM), BF16)
    for hd in range(DA_HEADS):
        v_ref[:, 2 * hd * DA_V_DIM:(2 * hd + 1) * DA_V_DIM] = vp[:, hd * DA_V_DIM:(hd + 1) * DA_V_DIM]
        v_ref[:, (2 * hd + 1) * DA_V_DIM:(2 * hd + 2) * DA_V_DIM] = ones
    u_ref[...] = jnp.dot(h, w_ref[:, 3 * aw:], preferred_element_type=F32)


def _inproj(x2, g, b, w_bf, ra, rm, rp, tm=512):
    T, D = x2.shape
    ncol = w_bf.shape[1]
    aw = ATTN_WIDTH
    uw = ncol - 3 * aw
    row = lambda i: (i, 0)
    fixed = lambda i: (0, 0)
    return pl.pallas_call(
        _inproj_kernel,
        grid=(T // tm,),
        in_specs=[
            pl.BlockSpec((tm, D), row),
            pl.BlockSpec((1, D), fixed),
            pl.BlockSpec((1, D), fixed),
            pl.BlockSpec((D, ncol), fixed),
            pl.BlockSpec((tm, V7X_LANES), row),
            pl.BlockSpec((tm, V7X_LANES), row),
            pl.BlockSpec((tm, V7X_LANES), row),
        ],
        out_specs=[
            pl.BlockSpec((tm, aw), row),
            pl.BlockSpec((tm, aw), row),
            pl.BlockSpec((tm, 2 * aw), row),
            pl.BlockSpec((tm, uw), row),
        ],
        out_shape=[
            jax.ShapeDtypeStruct((T, aw), BF16),
            jax.ShapeDtypeStruct((T, aw), BF16),
            jax.ShapeDtypeStruct((T, 2 * aw), BF16),
            jax.ShapeDtypeStruct((T, uw), F32),
        ],
        compiler_params=_cparams(("parallel",), 48),
        name="inproj",
    )(x2, g, b, w_bf, ra, rm, rp)


def _rotary_tables(positions):
    half = ROT_DIM // 2
    inv_freq = ROPE_THETA ** (-jnp.arange(0, ROT_DIM, 2, dtype=F32) / ROT_DIM)
    ang = positions.astype(F32).reshape(-1)[:, None] * inv_freq
    cos, sin = jnp.cos(ang), jnp.sin(ang)
    T = ang.shape[0]
    ones = jnp.ones((T, DA_HEAD_DIM - ROT_DIM), F32)
    zeros_h = jnp.zeros((T, half), F32)
    zeros_r = jnp.zeros((T, DA_HEAD_DIM - ROT_DIM), F32)
    a64 = jnp.concatenate([cos, cos, ones], axis=1)
    m64 = jnp.concatenate([-sin, zeros_h, zeros_r], axis=1)
    p64 = jnp.concatenate([zeros_h, sin, zeros_r], axis=1)
    rep = V7X_LANES // DA_HEAD_DIM
    return jnp.tile(a64, (1, rep)), jnp.tile(m64, (1, rep)), jnp.tile(p64, (1, rep))


def _attn_kernel(lam_ref, q_ref, k_ref, v_ref, g_ref, o_ref, s_ref, m_ref, acc_ref, *, kb, unroll):
    qb = q_ref.shape[0]
    L = k_ref.shape[0]
    nchunk = L // kb
    nl = kb // V7X_LANES
    q = q_ref[...]
    lane = lax.broadcasted_iota(jnp.int32, q.shape, 1)
    zero = jnp.zeros_like(q)
    qm = [jnp.where(lane < DA_HEAD_DIM, q, zero), jnp.where(lane >= DA_HEAD_DIM, q, zero)]
    m_ref[...] = jnp.full(m_ref.shape, -jnp.inf, F32)
    acc_ref[...] = jnp.zeros(acc_ref.shape, F32)

    def score_body(j, carry):
        kc = k_ref[pl.ds(pl.multiple_of(j * kb, kb), kb), :]
        for c in range(2):
            s = lax.dot_general(qm[c], kc, (((1,), (1,)), ((), ())), preferred_element_type=F32)
            s_ref[c, j] = s
            m = s[:, 0:V7X_LANES]
            for t in range(1, nl):
                m = jnp.maximum(m, s[:, t * V7X_LANES:(t + 1) * V7X_LANES])
            m_ref[c] = jnp.maximum(m_ref[c], m)
        return carry

    lax.fori_loop(0, nchunk, score_body, 0, unroll=unroll)
    m_row = [jnp.max(m_ref[c], axis=1, keepdims=True) for c in range(2)]

    def pv_body(j, carry):
        vc = v_ref[pl.ds(pl.multiple_of(j * kb, kb), kb), :]
        for c in range(2):
            p = jnp.exp2(s_ref[c, j] - m_row[c])
            acc_ref[c] += jnp.dot(p.astype(BF16), vc, preferred_element_type=F32)
        return carry

    lax.fori_loop(0, nchunk, pv_body, 0, unroll=unroll)
    outs = [acc_ref[c, :, 0:DA_V_DIM] / acc_ref[c, :, DA_V_DIM:2 * DA_V_DIM] for c in range(2)]
    o = outs[0] - lam_ref[0] * outs[1]
    ms = jnp.mean(o * o, axis=1, keepdims=True)
    o_ref[...] = (o * lax.rsqrt(ms + LN_EPS) * g_ref[...] * (1.0 - LAM_INIT)).astype(o_ref.dtype)


def _attention(q, k, v, lam, subln_g, B, L, qb=512, kb=1024, unroll=8):
    T = q.shape[0]
    nq = L // qb
    return pl.pallas_call(
        functools.partial(_attn_kernel, kb=kb, unroll=unroll),
        grid=(B, DA_HEADS, nq),
        in_specs=[
            pl.BlockSpec(memory_space=pltpu.SMEM),
            pl.BlockSpec((qb, DA_V_DIM), lambda b, h, i: (b * nq + i, h)),
            pl.BlockSpec((L, DA_V_DIM), lambda b, h, i: (b, h)),
            pl.BlockSpec((L, 2 * DA_V_DIM), lambda b, h, i: (b, h)),
            pl.BlockSpec((1, DA_V_DIM), lambda b, h, i: (0, 0)),
        ],
        out_specs=pl.BlockSpec((qb, DA_V_DIM), lambda b, h, i: (b * nq + i, h)),
        out_shape=jax.ShapeDtypeStruct((T, ATTN_WIDTH), BF16),
        scratch_shapes=[
            pltpu.VMEM((2, L // kb, qb, kb), F32),
            pltpu.VMEM((2, qb, V7X_LANES), F32),
            pltpu.VMEM((2, qb, 2 * DA_V_DIM), F32),
        ],
        compiler_params=_cparams(("parallel", "parallel", "parallel"), 56),
        name="attn",
    )(lam, q, k, v, subln_g)


def _hpre_kernel(u_ref, up_ref, un_ref, w_ref, b_ref, x0_ref, vx_ref):
    i = pl.program_id(1)
    n = pl.num_programs(1)
    tb = u_ref.shape[0]
    C = x0_ref.shape[-1]
    row = lax.broadcasted_iota(jnp.int32, (tb, V7X_LANES), 0)
    has_prev = (i > 0).astype(F32)
    has_next = (i < n - 1).astype(F32)

    def conv(c0):
        sl = slice(c0, c0 + V7X_LANES)
        u = u_ref[:, sl]
        prev_row = up_ref[V7X_SUBLANES - 1:V7X_SUBLANES, sl] * has_prev
        next_row = un_ref[0:1, sl] * has_next
        u_prev = jnp.where(row == 0, prev_row, pltpu.roll(u, 1, axis=0))
        u_next = jnp.where(row == tb - 1, next_row, pltpu.roll(u, tb - 1, axis=0))
        return u_prev * w_ref[0:1, sl] + u * w_ref[1:2, sl] + u_next * w_ref[2:3, sl] + b_ref[:, sl]

    tile = x0_ref.shape[:-1] + (V7X_LANES,)
    for c in range(C // V7X_LANES):
        c0 = c * V7X_LANES
        x0_ref[:, :, :, c0:c0 + V7X_LANES] = conv(c0).reshape(tile)
        vx_ref[:, :, :, c0:c0 + V7X_LANES] = (conv(2 * C + c0) * conv(C + c0)).reshape(tile)


def _hpre(u, conv_w, conv_b, B, L, tb=1024):
    T, C3 = u.shape
    C = C3 // 3
    nt = L // tb
    sub = V7X_SUBLANES
    na = tb // DFT_N2
    split = pl.BlockSpec((None, na, DFT_N2H, sub, C), lambda b, i: (b, i, 0, 0, 0))
    split_shape = jax.ShapeDtypeStruct((B, L // DFT_N2, DFT_N2H, sub, C), F32)
    cur = lambda b, i: (b * nt + i, 0)
    prev = lambda b, i: (jnp.maximum((b * L + i * tb) // sub - 1, 0), 0)
    nxt = lambda b, i: (jnp.minimum((b * L + (i + 1) * tb) // sub, T // sub - 1), 0)
    fixed = lambda b, i: (0, 0)
    return pl.pallas_call(
        _hpre_kernel,
        grid=(B, nt),
        in_specs=[
            pl.BlockSpec((tb, C3), cur),
            pl.BlockSpec((sub, C3), prev),
            pl.BlockSpec((sub, C3), nxt),
            pl.BlockSpec((SHORT_CONV, C3), fixed),
            pl.BlockSpec((1, C3), fixed),
        ],
        out_specs=[split, split],
        out_shape=[split_shape, split_shape],
        compiler_params=_cparams(("parallel", "parallel"), 32),
        name="hpre",
    )(u, u, u, conv_w, conv_b)


def _filt_kernel(z_ref, w1_ref, b1_ref, w2_ref, b2_ref, w3_ref, b3_ref, fr_ref, wo_ref, dl_ref,
                 hfb_ref, asum_ref, *, L):
    i = pl.program_id(0)
    tl = z_ref.shape[0]
    C = dl_ref.shape[1]
    hp = lax.Precision.HIGHEST
    h = jnp.sin(fr_ref[0:1, :] * (jnp.dot(z_ref[...], w1_ref[...], precision=hp, preferred_element_type=F32) + b1_ref[...]))
    h = jnp.sin(fr_ref[1:2, :] * (jnp.dot(h, w2_ref[...], precision=hp, preferred_element_type=F32) + b2_ref[...]))
    h = jnp.sin(fr_ref[2:3, :] * (jnp.dot(h, w3_ref[...], precision=hp, preferred_element_type=F32) + b3_ref[...]))
    o = jnp.dot(h, wo_ref[...], precision=hp, preferred_element_type=F32)
    grow = lax.broadcasted_iota(jnp.int32, (tl, C), 0) + i * tl
    t = grow.astype(F32) * (1.0 / (L - 1))
    decay = jnp.exp(-t * dl_ref[...])
    hf = o[:, :C] * decay
    hb = jnp.where(grow == 0, 0.0, o[:, C:] * decay)
    tile = hfb_ref.shape[:-1] + (C,)
    hfb_ref[:, :, :, :C] = hf.reshape(tile)
    hfb_ref[:, :, :, C:] = hb.reshape(tile)

    @pl.when(i == 0)
    def _():
        asum_ref[...] = jnp.zeros_like(asum_ref)

    asum_ref[...] += jnp.sum(jnp.abs(hf) + jnp.abs(hb), axis=0, keepdims=True)


def _filter_taps(L, w1, b1, w2, b2, w3, b3, freq, wout, tl=512):
    C = wout.shape[1] // 2
    order = w1.shape[1]
    emb = w1.shape[0]
    t = jnp.linspace(0.0, 1.0, L, dtype=F32)[:, None]
    w = 2.0 * math.pi * jnp.arange(L, dtype=F32)[:, None] / L
    f = jnp.linspace(1e-4, FILTER_BANDS - 1, FILTER_BANDS, dtype=F32)[None, :]
    z = jnp.concatenate([t, jnp.cos(f * w), -jnp.sin(f * w)], axis=-1)
    zp = jnp.pad(z, ((0, 0), (0, V7X_LANES - emb)))
    w1p = jnp.pad(w1, ((0, V7X_LANES - emb), (0, 0)))
    deltas = jnp.abs(jnp.linspace(math.log(DECAY_TARGET) / SLOW_DECAY, math.log(DECAY_TARGET) / FAST_DECAY, C, dtype=F32))[None]
    fixed = lambda i: (0, 0)
    return pl.pallas_call(
        functools.partial(_filt_kernel, L=L),
        grid=(L // tl,),
        in_specs=[
            pl.BlockSpec((tl, V7X_LANES), lambda i: (i, 0)),
            pl.BlockSpec((V7X_LANES, order), fixed), pl.BlockSpec((1, order), fixed),
            pl.BlockSpec((order, order), fixed), pl.BlockSpec((1, order), fixed),
            pl.BlockSpec((order, order), fixed), pl.BlockSpec((1, order), fixed),
            pl.BlockSpec((3, order), fixed),
            pl.BlockSpec((order, 2 * C), fixed),
            pl.BlockSpec((1, C), fixed),
        ],
        out_specs=[pl.BlockSpec((None, tl // DFT_N2, DFT_N2H, V7X_SUBLANES, 2 * C), lambda i: (0, i, 0, 0, 0)),
                   pl.BlockSpec((1, C), fixed)],
        out_shape=[jax.ShapeDtypeStruct((1, L // DFT_N2, DFT_N2H, V7X_SUBLANES, 2 * C), F32),
                   jax.ShapeDtypeStruct((1, C), F32)],
        compiler_params=_cparams(("arbitrary",), 32),
        name="filt",
    )(zp, w1p, b1[None], w2, b2[None], w3, b3[None], freq, wout, deltas)


def _dft_constants(L):
    n1 = 2 * L // DFT_N2
    n1h = n1 // 2
    N = 2 * L
    sub = V7X_SUBLANES

    def cs(num, den):
        ang = (num % den).astype(np.float64) * (2.0 * np.pi / den)
        return np.cos(ang), np.sin(ang)

    k1 = np.arange(n1, dtype=np.int64)
    a = np.arange(n1h, dtype=np.int64)
    c, s = cs(k1[:, None] * a[None, :], n1)
    eye = np.eye(sub)

    def expand(m):
        r, kk = m.shape
        return (m[:, None, :, None] * eye[None, :, None, :]).reshape(r * sub, kk * sub)

    def const(m):
        return jnp.asarray(m.astype(np.float32).astype(BF16))

    m1c = np.stack([np.concatenate([c, s], axis=1), np.concatenate([-s, c], axis=1)], axis=1).reshape(2 * n1, 2 * n1h)
    m1r = np.stack([c, -s], axis=1).reshape(2 * n1, n1h)
    ct, st = c.T, s.T
    m3 = np.stack([np.stack([ct, -st], axis=2).reshape(n1h, 2 * n1),
                   np.stack([st, ct], axis=2).reshape(n1h, 2 * n1)], axis=0).reshape(2 * n1h, 2 * n1)
    n2 = np.arange(DFT_N2, dtype=np.int64)
    gc, gs = cs(n2[:, None] * n2[None, :], DFT_N2)
    g_fwd = np.concatenate([np.concatenate([gc, gs], axis=1), np.concatenate([-gs, gc], axis=1)], axis=0)
    g_inv = np.concatenate([np.concatenate([gc, -gs], axis=1), np.concatenate([gs, gc], axis=1)], axis=0)
    ph = (jnp.arange(n1, dtype=jnp.int32)[:, None] * jnp.arange(DFT_N2, dtype=jnp.int32)[None, :]) % N
    ang = ph.astype(F32) * (2.0 * math.pi / N)
    lanes = (n1, DFT_N2H, sub, V7X_LANES)
    twc = jnp.broadcast_to(jnp.cos(ang).reshape(n1, DFT_N2H, sub, 1), lanes)
    tws = jnp.broadcast_to(jnp.sin(ang).reshape(n1, DFT_N2H, sub, 1), lanes)
    return dict(n1=n1, n1h=n1h, m1c=const(expand(m1c)), m1r=const(expand(m1r)), m3=const(expand(m3)),
                g_fwd=const(g_fwd), g_inv=const(g_inv), twc=twc, tws=tws)


def _dft1_kernel(x_ref, m_ref, o_ref):
    rows = m_ref.shape[1]
    xs = x_ref[...].reshape(rows, x_ref.shape[-1]).astype(BF16)
    o_ref[...] = jnp.dot(m_ref[...], xs, preferred_element_type=F32).reshape(o_ref.shape)


def _dft1(x5, m, n1, cb):
    P, n1h, _, sub, Cx = x5.shape
    return pl.pallas_call(
        _dft1_kernel,
        grid=(DFT_N2H, Cx // cb),
        in_specs=[
            pl.BlockSpec((P, n1h, None, sub, cb), lambda h, c: (0, 0, h, 0, c)),
            pl.BlockSpec(m.shape, lambda h, c: (0, 0)),
        ],
        out_specs=pl.BlockSpec((None, n1, 2, sub, cb), lambda h, c: (h, 0, 0, 0, c)),
        out_shape=jax.ShapeDtypeStruct((DFT_N2H, n1, 2, sub, Cx), F32),
        compiler_params=_cparams(("parallel", "parallel"), 48),
        name="dft1",
    )(x5, m)


def _dft2_kernel(a_ref, f_ref, twc_ref, tws_ref, gf_ref, gi_ref, o_ref):
    C = a_ref.shape[-1]
    half = DFT_N2

    def lanes(fn):
        return jnp.concatenate([fn(slice(c0, c0 + V7X_LANES)) for c0 in range(0, C, V7X_LANES)], axis=-1)

    for kk in range(a_ref.shape[1]):
        twc, tws = twc_ref[kk], tws_ref[kk]

        def spectrum(ref, col0):
            def re_part(sl):
                return ref[:, kk, 0, :, col0 + sl.start:col0 + sl.stop] * twc + ref[:, kk, 1, :, col0 + sl.start:col0 + sl.stop] * tws

            def im_part(sl):
                return ref[:, kk, 1, :, col0 + sl.start:col0 + sl.stop] * twc - ref[:, kk, 0, :, col0 + sl.start:col0 + sl.stop] * tws

            t = jnp.concatenate([lanes(re_part).reshape(half, C), lanes(im_part).reshape(half, C)], axis=0).astype(BF16)
            s = jnp.dot(gf_ref[...], t, preferred_element_type=F32)
            return s[:half], s[half:]

        xr, xi = spectrum(a_ref, 0)
        fr, fi = spectrum(f_ref, 0)
        br, bi = spectrum(f_ref, C)
        hr, hi = fr + br, fi - bi
        y = jnp.concatenate([xr * hr - xi * hi, xr * hi + xi * hr], axis=0).astype(BF16)
        b = jnp.dot(gi_ref[...], y, preferred_element_type=F32)
        br2 = b[:half].reshape(DFT_N2H, V7X_SUBLANES, C)
        bi2 = b[half:].reshape(DFT_N2H, V7X_SUBLANES, C)
        for c0 in range(0, C, V7X_LANES):
            sl = slice(c0, c0 + V7X_LANES)
            o_ref[:, kk, 0, :, sl] = br2[:, :, sl] * twc - bi2[:, :, sl] * tws
            o_ref[:, kk, 1, :, sl] = bi2[:, :, sl] * twc + br2[:, :, sl] * tws


def _dft2(a5, f5, k, kb=4):
    n2h, n1, _, sub, C = a5.shape
    kb = min(kb, n1)
    return pl.pallas_call(
        _dft2_kernel,
        grid=(n1 // kb,),
        in_specs=[
            pl.BlockSpec((n2h, kb, 2, sub, C), lambda i: (0, i, 0, 0, 0)),
            pl.BlockSpec((n2h, kb, 2, sub, 2 * C), lambda i: (0, i, 0, 0, 0)),
            pl.BlockSpec((kb, n2h, sub, V7X_LANES), lambda i: (i, 0, 0, 0)),
            pl.BlockSpec((kb, n2h, sub, V7X_LANES), lambda i: (i, 0, 0, 0)),
            pl.BlockSpec((2 * DFT_N2, 2 * DFT_N2), lambda i: (0, 0)),
            pl.BlockSpec((2 * DFT_N2, 2 * DFT_N2), lambda i: (0, 0)),
        ],
        out_specs=pl.BlockSpec((n2h, kb, 2, sub, C), lambda i: (0, i, 0, 0, 0)),
        out_shape=jax.ShapeDtypeStruct(a5.shape, F32),
        compiler_params=_cparams(("parallel",), 48),
        name="dft2",
    )(a5, f5, k["twc"], k["tws"], k["g_fwd"], k["g_inv"])


def _dft3_kernel(b_ref, m_ref, x0_ref, vx_ref, sc_ref, d_ref, o_ref):
    C = b_ref.shape[-1]
    bs = b_ref[...].reshape(m_ref.shape[1], C).astype(BF16)
    y = jnp.dot(m_ref[...], bs, preferred_element_type=F32).reshape(o_ref.shape)
    o_ref[...] = x0_ref[...] * (y * sc_ref[...] + vx_ref[...] * d_ref[...])


def _dft3(b5, m3, x05, vx5, scale, d):
    n2h, n1, _, sub, C = b5.shape
    Bt, n1h = x05.shape[0], x05.shape[1]
    tok = pl.BlockSpec((Bt, n1h, None, sub, C), lambda h: (0, 0, h, 0, 0))
    vec = pl.BlockSpec((1, C), lambda h: (0, 0))
    return pl.pallas_call(
        _dft3_kernel,
        grid=(n2h,),
        in_specs=[
            pl.BlockSpec((None, n1, 2, sub, C), lambda h: (h, 0, 0, 0, 0)),
            pl.BlockSpec(m3.shape, lambda h: (0, 0)),
            tok, tok, vec, vec,
        ],
        out_specs=tok,
        out_shape=jax.ShapeDtypeStruct(x05.shape, F32),
        compiler_params=_cparams(("parallel",), 48),
        name="dft3",
    )(b5, m3, x05, vx5, scale, d)


def _hyena(u, conv_w, conv_b, w1, b1, w2, b2, w3, b3, freq, wout, d_skip, B, L):
    T = u.shape[0]
    C = HYENA_WIDTH
    k = _dft_constants(L)
    n1, n1h = k["n1"], k["n1h"]
    x0, vx = _hpre(u, conv_w, conv_b[None], B, L)
    hfb, asum = _filter_taps(L, w1, b1, w2, b2, w3, b3, freq, wout)
    a5 = _dft1(vx, k["m1c"], n1, cb=C)
    f5 = _dft1(hfb, k["m1r"], n1, cb=C)
    b5 = _dft2(a5, f5, k)
    scale = 1.0 / (asum * (2 * L))
    return _dft3(b5, k["m3"], x0, vx, scale, d_skip[None])


PACKED = jnp.uint32


def _pack_pairs(x):
    half = x.shape[1] // 2
    bits = pltpu.bitcast(x.astype(BF16).astype(F32), PACKED)
    return (bits[:, :half] >> 16) | bits[:, half:]


def _unpack_pairs(w):
    lo = pltpu.bitcast(w << 16, F32)
    hi = pltpu.bitcast(w & jnp.uint32(0xFFFF0000), F32)
    return lo, hi


def _oproj_kernel(x_ref, g0_ref, b0_ref, at_ref, hy_ref, wo_ref, g1_ref, b1_ref, wrh_ref, wrl_ref,
                  h1_ref, h1p_ref, sc_ref):
    aw = at_ref.shape[1]
    h0 = _layer_norm_rows(x_ref[...], g0_ref[...], b0_ref[...])
    mixed = jnp.dot(at_ref[...], wo_ref[0:aw, :], preferred_element_type=F32)
    hy = hy_ref[...].reshape(x_ref.shape[0], hy_ref.shape[-1])
    mixed = mixed + jnp.dot(hy.astype(BF16), wo_ref[aw:, :], preferred_element_type=F32)
    h1 = _layer_norm_rows(ALPHA * h0 + mixed, g1_ref[...], b1_ref[...])
    h1_ref[...] = h1
    h1p_ref[...] = _pack_pairs(h1)
    hh = h1.astype(BF16)
    hl = (h1 - hh.astype(F32)).astype(BF16)
    dn = (((1,), (1,)), ((), ()))
    logits = lax.dot_general(wrh_ref[...], hh, dn, preferred_element_type=F32)
    logits = logits + (lax.dot_general(wrh_ref[...], hl, dn, preferred_element_type=F32)
                       + lax.dot_general(wrl_ref[...], hh, dn, preferred_element_type=F32))
    sc_ref[...] = 1.0 / (1.0 + jnp.exp(-logits))


def _oproj(x2, g0, b0, attn, hy, wo_bf, g1, b1, wr_t, tm=256):
    T, D = x2.shape
    E = wr_t.shape[0]
    wrh = wr_t.astype(BF16)
    wrl = (wr_t - wrh.astype(F32)).astype(BF16)
    row = lambda i: (i, 0)
    fixed = lambda i: (0, 0)
    hy4 = hy.reshape((-1,) + hy.shape[2:])
    return pl.pallas_call(
        _oproj_kernel,
        grid=(T // tm,),
        in_specs=[
            pl.BlockSpec((tm, D), row), pl.BlockSpec((1, D), fixed), pl.BlockSpec((1, D), fixed),
            pl.BlockSpec((tm, attn.shape[1]), row),
            pl.BlockSpec((tm // DFT_N2,) + hy4.shape[1:], lambda i: (i, 0, 0, 0)),
            pl.BlockSpec((D, D), fixed), pl.BlockSpec((1, D), fixed), pl.BlockSpec((1, D), fixed),
            pl.BlockSpec((E, D), fixed), pl.BlockSpec((E, D), fixed),
        ],
        out_specs=[pl.BlockSpec((tm, D), row), pl.BlockSpec((tm, D // 2), row), pl.BlockSpec((E, tm), lambda i: (0, i))],
        out_shape=[
            jax.ShapeDtypeStruct((T, D), F32),
            jax.ShapeDtypeStruct((T, D // 2), PACKED),
            jax.ShapeDtypeStruct((E, T), F32),
        ],
        compiler_params=_cparams(("parallel",), 48),
        name="oproj",
    )(x2, g0, b0, attn, hy4, wo_bf, g1, b1, wrh, wrl)


def _route_kernel(sc_ref, bias_ref, tri_ref, idx_ref, gate_ref, rank_ref, cnt_ref, carry_ref):
    step = pl.program_id(0)

    @pl.when(step == 0)
    def _():
        carry_ref[...] = jnp.zeros_like(carry_ref)

    tm = V7X_LANES
    for part in range(sc_ref.shape[1] // tm):
        cols = slice(part * tm, (part + 1) * tm)
        _route_tokens(sc_ref[:, cols], bias_ref, tri_ref, idx_ref, gate_ref, rank_ref, carry_ref, cols)
    cnt_ref[...] = carry_ref[...].astype(jnp.int32)


def _route_tokens(scores, bias_ref, tri_ref, idx_ref, gate_ref, rank_ref, carry_ref, cols):
    E, tm = scores.shape
    neg = jnp.float32(-jnp.inf)
    biased = scores + bias_ref[...]
    erow = lax.broadcasted_iota(jnp.int32, (E, tm), 0)
    big = jnp.int32(E)

    def first_argmax(vals, rows):
        m = jnp.max(vals, axis=0, keepdims=True)
        pick = jnp.min(jnp.where(vals == m, rows, big), axis=0, keepdims=True)
        return m, pick

    gsc = []
    for g in range(N_GROUPS):
        blk = biased[g * GROUP_SIZE:(g + 1) * GROUP_SIZE, :]
        rows = erow[g * GROUP_SIZE:(g + 1) * GROUP_SIZE, :]
        m1, p1 = first_argmax(blk, rows)
        m2 = jnp.max(jnp.where(rows == p1, neg, blk), axis=0, keepdims=True)
        gsc.append(m1 + m2)
    gsc = jnp.concatenate(gsc, axis=0)
    grow = lax.broadcasted_iota(jnp.int32, (N_GROUPS, tm), 0)
    gsel = jnp.zeros((N_GROUPS, tm), jnp.bool_)
    work = gsc
    for _ in range(TOPK_GROUPS):
        _, p = first_argmax(work, grow)
        hit = grow == p
        gsel = gsel | hit
        work = jnp.where(hit, neg, work)
    emask = jnp.concatenate(
        [jnp.broadcast_to(gsel[g:g + 1, :], (GROUP_SIZE, tm)) for g in range(N_GROUPS)], axis=0)
    work = jnp.where(emask, biased, neg)

    sel = jnp.zeros((E, tm), jnp.bool_)
    picks, gvals = [], []
    for _ in range(TOP_K):
        _, p = first_argmax(work, erow)
        hit = erow == p
        sel = sel | hit
        picks.append(p)
        gvals.append(jnp.sum(jnp.where(hit, scores, 0.0), axis=0, keepdims=True))
        work = jnp.where(hit, neg, work)
    gv = jnp.concatenate(gvals, axis=0)
    idx_ref[:, cols] = jnp.concatenate(picks, axis=0)
    gate_ref[:, cols] = gv / jnp.sum(gv, axis=0, keepdims=True) * ROUTED_SCALE

    chosen = sel.astype(F32)
    before = jnp.dot(chosen.astype(BF16), tri_ref[...], preferred_element_type=F32) + carry_ref[...]
    rank_ref[:, cols] = jnp.concatenate(
        [jnp.sum(jnp.where(erow == p, before, 0.0), axis=0, keepdims=True) for p in picks], axis=0).astype(jnp.int32)
    carry_ref[...] += jnp.sum(chosen, axis=1, keepdims=True)


def _route(scores_t, bias, tm=256):
    E, T = scores_t.shape
    tm = min(tm, T)
    lanes = V7X_LANES
    tri = (jnp.arange(lanes)[:, None] < jnp.arange(lanes)[None, :]).astype(BF16)
    tok = lambda i: (0, i)
    fixed = lambda i: (0, 0)
    return pl.pallas_call(
        _route_kernel,
        grid=(T // tm,),
        in_specs=[pl.BlockSpec((E, tm), tok), pl.BlockSpec((E, 1), fixed), pl.BlockSpec((lanes, lanes), fixed)],
        out_specs=[pl.BlockSpec((TOP_K, tm), tok), pl.BlockSpec((TOP_K, tm), tok), pl.BlockSpec((TOP_K, tm), tok),
                   pl.BlockSpec((E, 1), fixed)],
        out_shape=[
            jax.ShapeDtypeStruct((TOP_K, T), jnp.int32),
            jax.ShapeDtypeStruct((TOP_K, T), F32),
            jax.ShapeDtypeStruct((TOP_K, T), jnp.int32),
            jax.ShapeDtypeStruct((E, 1), jnp.int32),
        ],
        scratch_shapes=[pltpu.VMEM((E, 1), F32)],
        compiler_params=_cparams(("arbitrary",), 32),
        name="route",
    )(scores_t, bias, tri)


def _dest_kernel(idx_ref, rank_ref, ps_ref, dest_ref):
    E = ps_ref.shape[0]
    tm = idx_ref.shape[1]
    erow = lax.broadcasted_iota(jnp.int32, (E, tm), 0)
    ps = ps_ref[...].astype(F32)
    rows = [jnp.sum(jnp.where(erow == idx_ref[k:k + 1, :], ps, 0.0), axis=0, keepdims=True) for k in range(TOP_K)]
    dest_ref[...] = jnp.concatenate(rows, axis=0).astype(jnp.int32) + rank_ref[...]


def _dest_rows(idx, rank, pad_start, tm=1024):
    K, T = idx.shape
    E = pad_start.shape[0]
    tm = min(tm, T)
    tok = lambda i: (0, i)
    return pl.pallas_call(
        _dest_kernel,
        grid=(T // tm,),
        in_specs=[pl.BlockSpec((K, tm), tok), pl.BlockSpec((K, tm), tok), pl.BlockSpec((E, 1), lambda i: (0, 0))],
        out_specs=pl.BlockSpec((K, tm), tok),
        out_shape=jax.ShapeDtypeStruct((K, T), jnp.int32),
        compiler_params=_cparams(("arbitrary",), 32),
        name="dest",
    )(idx, rank, pad_start[:, None])


def _scatter_kernel(ps_ref, pe_ref, dest_ref, h_ref, xs_ref, zbuf, sem, *, rb):
    sub = V7X_SUBLANES
    tm = h_ref.shape[0]

    @pl.when(pl.program_id(0) == 0)
    def _():
        zbuf[...] = jnp.zeros_like(zbuf)

        def zcopy(e):
            return pltpu.make_async_copy(zbuf, xs_ref.at[pl.ds(pl.multiple_of(pe_ref[e] - rb, rb), rb)], sem)

        def zstart(e, c):
            @pl.when(pe_ref[e] > ps_ref[e])
            def _():
                zcopy(e).start()
            return c

        def zwait(e, c):
            @pl.when(pe_ref[e] > ps_ref[e])
            def _():
                zcopy(e).wait()
            return c

        lax.fori_loop(0, ps_ref.shape[0], zstart, 0)
        lax.fori_loop(0, ps_ref.shape[0], zwait, 0)

    def issue(r8, c):
        for j in range(sub):
            for k in range(TOP_K):
                dst = dest_ref[0, r8 * (sub * TOP_K) + (j * TOP_K + k)]
                pltpu.make_async_copy(h_ref.at[pl.ds(r8 * sub + j, 1)], xs_ref.at[pl.ds(dst, 1)], sem).start(priority=k % 2)
        return c

    lax.fori_loop(0, tm // sub, issue, 0)
    for k in range(TOP_K):
        pltpu.make_async_copy(h_ref, xs_ref.at[pl.ds(0, tm)], sem).wait()


def _dest_tiles(dest, tm):
    K, T = dest.shape
    return dest.T.reshape(T // tm, 1, tm * K)


def _scatter_rows(pad_start, pad_end, dest_t, h1p, n_rows, rb, tm=256):
    T, W = h1p.shape
    return pl.pallas_call(
        functools.partial(_scatter_kernel, rb=rb),
        grid_spec=pltpu.PrefetchScalarGridSpec(
            num_scalar_prefetch=2,
            grid=(T // tm,),
            in_specs=[
                pl.BlockSpec((None, 1, tm * TOP_K), lambda i, ps, pe: (i, 0, 0), memory_space=pltpu.SMEM),
                pl.BlockSpec((tm, W), lambda i, ps, pe: (i, 0)),
            ],
            out_specs=pl.BlockSpec(memory_space=pl.ANY),
            scratch_shapes=[pltpu.VMEM((rb, W), h1p.dtype), pltpu.SemaphoreType.DMA],
        ),
        out_shape=jax.ShapeDtypeStruct((n_rows, W), h1p.dtype),
        compiler_params=_cparams(("arbitrary",), 32),
        name="scatter",
    )(pad_start, pad_end, dest_t, h1p)


EXPERT_X_SLOTS = 4
EXPERT_Y_SLOTS = 3


def _experts_kernel(first_ref, nblk_ref, wsel_ref, tot_ref, xs_ref, wg_ref, wu_ref, wd_ref, ys_ref,
                    xbuf, ybuf, wgub, wdb, xsem, ysem, *, rb):
    del wsel_ref
    e = pl.program_id(0)
    F = wg_ref.shape[1]
    nx, ny = xbuf.shape[0], ybuf.shape[0]
    total = tot_ref[0]
    n = nblk_ref[e]
    g0 = first_ref[e]

    def x_copy(g):
        slot = g % nx
        return pltpu.make_async_copy(xs_ref.at[pl.ds(pl.multiple_of(g * rb, rb), rb)], xbuf.at[slot], xsem.at[slot])

    def y_copy(g):
        slot = g % ny
        return pltpu.make_async_copy(ybuf.at[slot], ys_ref.at[pl.ds(pl.multiple_of(g * rb, rb), rb)], ysem.at[slot])

    @pl.when(e == 0)
    def _():
        for g in range(nx):
            @pl.when(g < total)
            def _():
                x_copy(g).start()

    def process(g, count):
        for d in range(count):
            x_copy(g + d).wait()
        ys = []
        for d in range(count):
            lo, hi = _unpack_pairs(xbuf[(g + d) % nx])
            x = jnp.concatenate([lo, hi], axis=1).astype(BF16)
            gu = jnp.dot(x, wgub[...], preferred_element_type=F32)
            gate, up = gu[:, 0:F], gu[:, F:]
            hb = (gate / (1.0 + jnp.exp(-gate)) * up).astype(BF16)
            ys.append(_pack_pairs(jnp.dot(hb, wdb[...], preferred_element_type=F32)))
        for d in range(count):
            @pl.when(g + d >= ny)
            def _():
                y_copy(g + d - ny).wait()

            ybuf[(g + d) % ny] = ys[d]
            y_copy(g + d).start()
        for d in range(count):
            @pl.when(g + d + nx < total)
            def _():
                x_copy(g + d + nx).start()

    @pl.when(n > 0)
    def _():
        wgub[:, 0:F] = wg_ref[...].astype(BF16)
        wgub[:, F:] = wu_ref[...].astype(BF16)
        wdb[...] = wd_ref[...].astype(BF16)

        def pair(j, c):
            process(g0 + 2 * j, 2)
            return c

        lax.fori_loop(0, n // 2, pair, 0)

        @pl.when(n % 2 == 1)
        def _():
            process(g0 + n - 1, 1)

    @pl.when(e == pl.num_programs(0) - 1)
    def _():
        for back in range(ny, 0, -1):
            @pl.when(total >= back)
            def _():
                y_copy(total - back).wait()


def _experts(first_blk, nblk, wsel, total_blk, xs, w_gate, w_up, w_down, rb):
    P, W = xs.shape
    E, D, F = w_gate.shape
    wspec = lambda shape: pl.BlockSpec((None,) + shape, lambda e, fb, nbk, ws, tt: (ws[e], 0, 0))
    return pl.pallas_call(
        functools.partial(_experts_kernel, rb=rb),
        grid_spec=pltpu.PrefetchScalarGridSpec(
            num_scalar_prefetch=4,
            grid=(E,),
            in_specs=[pl.BlockSpec(memory_space=pl.ANY), wspec((D, F)), wspec((D, F)), wspec((F, D))],
            out_specs=pl.BlockSpec(memory_space=pl.ANY),
            scratch_shapes=[
                pltpu.VMEM((EXPERT_X_SLOTS, rb, W), PACKED), pltpu.VMEM((EXPERT_Y_SLOTS, rb, W), PACKED),
                pltpu.VMEM((D, 2 * F), BF16), pltpu.VMEM((F, D), BF16),
                pltpu.SemaphoreType.DMA((EXPERT_X_SLOTS,)), pltpu.SemaphoreType.DMA((EXPERT_Y_SLOTS,)),
            ],
        ),
        out_shape=jax.ShapeDtypeStruct((P, W), PACKED),
        compiler_params=_cparams(("arbitrary",), 48),
        name="experts",
    )(first_blk, nblk, wsel, total_blk, xs, w_gate, w_up, w_down)


def _combine_kernel(dest_ref, destn_ref, h1_ref, gate_ref, ys_ref, ys8_ref, sg_ref, su_ref, sd_ref,
                    g2_ref, b2_ref, o_ref, buf, sem):
    i = pl.program_id(0)
    n = pl.num_programs(0)
    tm = h1_ref.shape[0]
    slot = i % 2

    sub = V7X_SUBLANES

    def issue_rows(dref, s, r0, nrows):
        for r in range(r0, r0 + nrows):
            for k in range(TOP_K):
                src = ys_ref.at[pl.ds(dref[0, r * TOP_K + k], 1)]
                pltpu.make_async_copy(src, buf.at[s, r, pl.ds(k, 1)], sem.at[s]).start(priority=k % 2)

    def wait_tile(s):
        pltpu.make_async_copy(ys8_ref.at[pl.ds(0, tm)], buf.at[s], sem.at[s]).wait()

    @pl.when(i == 0)
    def _():
        def body(r8, c):
            for j in range(sub):
                for k in range(TOP_K):
                    src = ys_ref.at[pl.ds(dest_ref[0, r8 * (sub * TOP_K) + (j * TOP_K + k)], 1)]
                    pltpu.make_async_copy(src, buf.at[0, r8 * sub + j, pl.ds(k, 1)], sem.at[0]).start(priority=k % 2)
            return c
        lax.fori_loop(0, tm // sub, body, 0)

    wait_tile(slot)
    parts = 4
    rows_per = tm // parts
    nk = rows_per * TOP_K
    pick = lax.broadcasted_iota(jnp.int32, (TOP_K, nk), 1) % TOP_K == lax.broadcasted_iota(jnp.int32, (TOP_K, nk), 0)
    spread = pick.astype(BF16)
    own = lax.broadcasted_iota(jnp.int32, (rows_per, nk), 1) // TOP_K == lax.broadcasted_iota(jnp.int32, (rows_per, nk), 0)
    for part in range(parts):
        r0 = part * rows_per
        issue_rows(destn_ref, 1 - slot, r0, rows_per)
        rows = slice(r0, r0 + rows_per)
        h1 = h1_ref[rows, :]
        hb = h1.astype(BF16)
        g = jnp.dot(hb, sg_ref[...], preferred_element_type=F32)
        u = jnp.dot(hb, su_ref[...], preferred_element_type=F32)
        ffn = jnp.dot((g / (1.0 + jnp.exp(-g)) * u).astype(BF16), sd_ref[...], preferred_element_type=F32)
        lo, hi = _unpack_pairs(buf[slot, rows].reshape(nk, buf.shape[3]))
        y = jnp.concatenate([lo, hi], axis=1).astype(BF16)
        gates = gate_ref[rows, :]
        gh = gates.astype(BF16)
        gl = (gates - gh.astype(F32)).astype(BF16)
        routed = jnp.zeros((rows_per, y.shape[1]), F32)
        for piece in (gh, gl):
            gm = jnp.where(own, jnp.dot(piece, spread, preferred_element_type=F32), 0.0).astype(BF16)
            routed = routed + jnp.dot(gm, y, preferred_element_type=F32)
        o_ref[rows, :] = _layer_norm_rows(ALPHA * h1 + ffn + routed, g2_ref[...], b2_ref[...])

    @pl.when(i == n - 1)
    def _():
        wait_tile(1 - slot)


def _combine(dest, h1, gates_tk, ys, sg, su, sd, g2, b2, tm=512):
    T, D = h1.shape
    W = ys.shape[1]
    F = sg.shape[1]
    nt = T // tm
    row = lambda i: (i, 0)
    fixed = lambda i: (0, 0)
    smem = functools.partial(pl.BlockSpec, (None, 1, tm * TOP_K), memory_space=pltpu.SMEM)
    dest_t = _dest_tiles(dest, tm)
    ys8 = ys.reshape(ys.shape[0] // TOP_K, TOP_K, W)
    return pl.pallas_call(
        _combine_kernel,
        grid=(nt,),
        in_specs=[
            smem(index_map=lambda i: (i, 0, 0)),
            smem(index_map=lambda i: (jnp.minimum(i + 1, nt - 1), 0, 0)),
            pl.BlockSpec((tm, D), row),
            pl.BlockSpec((tm, TOP_K), row),
            pl.BlockSpec(memory_space=pl.ANY),
            pl.BlockSpec(memory_space=pl.ANY),
            pl.BlockSpec((D, F), fixed), pl.BlockSpec((D, F), fixed), pl.BlockSpec((F, D), fixed),
            pl.BlockSpec((1, D), fixed), pl.BlockSpec((1, D), fixed),
        ],
        out_specs=pl.BlockSpec((tm, D), row),
        out_shape=jax.ShapeDtypeStruct((T, D), F32),
        scratch_shapes=[pltpu.VMEM((2, tm, TOP_K, W), PACKED), pltpu.SemaphoreType.DMA((2,))],
        compiler_params=_cparams(("arbitrary",), 48),
        name="combine",
    )(dest_t, dest_t, h1, gates_tk, ys, ys8, sg, su, sd, g2, b2)


MOE_ROWS = 256


def _moe(h1, h1p, scores_t, router_bias, w_gate, w_up, w_down, ws_gate, ws_up, ws_down, g2, b2):
    T = h1.shape[0]
    E = N_EXPERTS
    rb = MOE_ROWS
    idx, gates, rank, counts = _route(scores_t, router_bias[:, None])
    counts = counts[:, 0]
    padded = (counts + rb - 1) // rb * rb
    pad_end = jnp.cumsum(padded)
    pad_start = pad_end - padded
    nb = (T * TOP_K) // rb + E
    nblk = (padded // rb).astype(jnp.int32)
    first_blk = (pad_start // rb).astype(jnp.int32)
    total_blk = (pad_end[-1:] // rb).astype(jnp.int32)
    wsel = lax.cummax(jnp.where(nblk > 0, jnp.arange(E, dtype=jnp.int32), 0))
    dest = _dest_rows(idx, rank, pad_start)
    scatter_tm = min(512, T)
    xs = _scatter_rows(pad_start, pad_end, _dest_tiles(dest, scatter_tm), h1p, nb * rb, rb, tm=scatter_tm)
    ys = _experts(first_blk, nblk, wsel, total_blk, xs, w_gate, w_up, w_down, rb)
    return _combine(dest, h1, gates.T, ys, ws_gate.astype(BF16), ws_up.astype(BF16), ws_down.astype(BF16), g2, b2)


def kernel(x, positions, emb_ln_g, emb_ln_b, w_in, hy_conv_w, hy_conv_b, hy_f_w1, hy_f_b1, hy_f_w2, hy_f_b2, hy_f_w3, hy_f_b3, hy_f_freq, hy_f_wout, hy_d, lambda_q1, lambda_k1, lambda_q2, lambda_k2, subln_g, w_o, ln1_g, ln1_b, w_router, router_bias, w_gate, w_up, w_down, ws_gate, ws_up, ws_down, ln2_g, ln2_b):
    B, L, D = x.shape
    T = B * L
    assert w_in.shape[0] == DEPTH == 1
    i = 0
    x2 = x.reshape(T, D)
    g0, b0 = emb_ln_g[None], emb_ln_b[None]
    ra, rm, rp = _rotary_tables(positions)
    q, k, v, u = _inproj(x2, g0, b0, w_in[i].astype(BF16), ra, rm, rp)
    lam = (jnp.exp(jnp.sum(lambda_q1[i] * lambda_k1[i])) - jnp.exp(jnp.sum(lambda_q2[i] * lambda_k2[i])) + LAM_INIT)
    attn = _attention(q, k, v, lam.reshape(1).astype(F32), subln_g[i][None], B, L)
    hy = _hyena(u, hy_conv_w[i], hy_conv_b[i], hy_f_w1[i], hy_f_b1[i], hy_f_w2[i], hy_f_b2[i], hy_f_w3[i], hy_f_b3[i],
                hy_f_freq[i], hy_f_wout[i], hy_d[i], B, L)
    h1, h1p, scores_t = _oproj(x2, g0, b0, attn, hy, w_o[i].astype(BF16), ln1_g[i][None], ln1_b[i][None], w_router[i].T)
    out = _moe(h1, h1p, scores_t, router_bias[i], w_gate[i], w_up[i], w_down[i], ws_gate[i], ws_up[i], ws_down[i],
               ln2_g[i][None], ln2_b[i][None])
    return out.reshape(B, L, D)
```

```python
import functools
import math

import numpy as np
import jax
import jax.numpy as jnp
from jax import lax
from jax.experimental import pallas as pl
from jax.experimental.pallas import tpu as pltpu

DA_HEADS = 4
DA_HEAD_DIM = 64
DA_V_DIM = 128
ATTN_WIDTH = 512
HYENA_WIDTH = 512
ROT_DIM = 16
ROPE_THETA = 500000.0
SHORT_CONV = 3
FILTER_EMB = 33
FILTER_BANDS = 16
DECAY_TARGET = 1e-2
FAST_DECAY = 0.3
SLOW_DECAY = 1.5
N_EXPERTS = 256
TOP_K = 8
N_GROUPS = 8
GROUP_SIZE = N_EXPERTS // N_GROUPS
TOPK_GROUPS = 4
EXPERT_DIM = 256
ROUTED_SCALE = 2.5
DEPTH = 1
ALPHA = (2 * DEPTH) ** 0.25
LN_EPS = 1e-5
LAM_INIT = 0.8 - 0.6 * math.exp(-0.3 * 0)

V7X_LANES = 128
V7X_SUBLANES = 8
V7X_VMEM_BYTES = 64 * 1024 * 1024

DFT_N2 = 128
DFT_N2H = DFT_N2 // V7X_SUBLANES

BF16 = jnp.bfloat16
F32 = jnp.float32


def _cparams(sem, vmem_mb):
    return pltpu.CompilerParams(dimension_semantics=sem, vmem_limit_bytes=vmem_mb * 1024 * 1024)


def _layer_norm_rows(x, g, b):
    mu = jnp.mean(x, axis=-1, keepdims=True)
    xc = x - mu
    var = jnp.mean(xc * xc, axis=-1, keepdims=True)
    return xc * lax.rsqrt(var + LN_EPS) * g + b


def _inproj_kernel(x_ref, g_ref, b_ref, w_ref, ra_ref, rm_ref, rp_ref, q_ref, k_ref, v_ref, u_ref):
    h = _layer_norm_rows(x_ref[...], g_ref[...], b_ref[...]).astype(BF16)
    ra, rm, rp = ra_ref[...], rm_ref[...], rp_ref[...]

    def rot(t):
        return t * ra + pltpu.roll(t, V7X_LANES - ROT_DIM // 2, axis=1) * rm + pltpu.roll(t, ROT_DIM // 2, axis=1) * rp

    aw = ATTN_WIDTH
    qp = jnp.dot(h, w_ref[:, 0:aw], preferred_element_type=F32)
    kp = jnp.dot(h, w_ref[:, aw:2 * aw], preferred_element_type=F32)
    scale = DA_HEAD_DIM ** -0.5 * math.log2(math.e)
    for c in range(aw // V7X_LANES):
        sl = slice(c * V7X_LANES, (c + 1) * V7X_LANES)
        q_ref[:, sl] = (rot(qp[:, sl]) * scale).astype(BF16)
        k_ref[:, sl] = rot(kp[:, sl]).astype(BF16)
    vp = jnp.dot(h, w_ref[:, 2 * aw:3 * aw], preferred_element_type=F32).astype(BF16)
    ones = jnp.ones((vp.shape[0], DA_V_DIM), BF16)
    for hd in range(DA_HEADS):
        v_ref[:, 2 * hd * DA_V_DIM:(2 * hd + 1) * DA_V_DIM] = vp[:, hd * DA_V_DIM:(hd + 1) * DA_V_DIM]
        v_ref[:, (2 * hd + 1) * DA_V_DIM:(2 * hd + 2) * DA_V_DIM] = ones
    u_ref[...] = jnp.dot(h, w_ref[:, 3 * aw:], preferred_element_type=F32)


def _inproj(x2, g, b, w_bf, ra, rm, rp, tm=512):
    T, D = x2.shape
    ncol = w_bf.shape[1]
    aw = ATTN_WIDTH
    uw = ncol - 3 * aw
    row = lambda i: (i, 0)
    fixed = lambda i: (0, 0)
    return pl.pallas_call(
        _inproj_kernel,
        grid=(T // tm,),
        in_specs=[
            pl.BlockSpec((tm, D), row),
            pl.BlockSpec((1, D), fixed),
            pl.BlockSpec((1, D), fixed),
            pl.BlockSpec((D, ncol), fixed),
            pl.BlockSpec((tm, V7X_LANES), row),
            pl.BlockSpec((tm, V7X_LANES), row),
            pl.BlockSpec((tm, V7X_LANES), row),
        ],
        out_specs=[
            pl.BlockSpec((tm, aw), row),
            pl.BlockSpec((tm, aw), row),
            pl.BlockSpec((tm, 2 * aw), row),
            pl.BlockSpec((tm, uw), row),
        ],
        out_shape=[
            jax.ShapeDtypeStruct((T, aw), BF16),
            jax.ShapeDtypeStruct((T, aw), BF16),
            jax.ShapeDtypeStruct((T, 2 * aw), BF16),
            jax.ShapeDtypeStruct((T, uw), F32),
        ],
        compiler_params=_cparams(("parallel",), 48),
        name="inproj",
    )(x2, g, b, w_bf, ra, rm, rp)


def _rotary_tables(positions):
    half = ROT_DIM // 2
    inv_freq = ROPE_THETA ** (-jnp.arange(0, ROT_DIM, 2, dtype=F32) / ROT_DIM)
    ang = positions.astype(F32).reshape(-1)[:, None] * inv_freq
    cos, sin = jnp.cos(ang), jnp.sin(ang)
    T = ang.shape[0]
    ones = jnp.ones((T, DA_HEAD_DIM - ROT_DIM), F32)
    zeros_h = jnp.zeros((T, half), F32)
    zeros_r = jnp.zeros((T, DA_HEAD_DIM - ROT_DIM), F32)
    a64 = jnp.concatenate([cos, cos, ones], axis=1)
    m64 = jnp.concatenate([-sin, zeros_h, zeros_r], axis=1)
    p64 = jnp.concatenate([zeros_h, sin, zeros_r], axis=1)
    rep = V7X_LANES // DA_HEAD_DIM
    return jnp.tile(a64, (1, rep)), jnp.tile(m64, (1, rep)), jnp.tile(p64, (1, rep))


def _attn_kernel(lam_ref, q_ref, k_ref, v_ref, g_ref, o_ref, s_ref, m_ref, acc_ref, *, kb, unroll):
    qb = q_ref.shape[0]
    L = k_ref.shape[0]
    nchunk = L // kb
    nl = kb // V7X_LANES
    q = q_ref[...]
    lane = lax.broadcasted_iota(jnp.int32, q.shape, 1)
    zero = jnp.zeros_like(q)
    qm = [jnp.where(lane < DA_HEAD_DIM, q, zero), jnp.where(lane >= DA_HEAD_DIM, q, zero)]
    m_ref[...] = jnp.full(m_ref.shape, -jnp.inf, F32)
    acc_ref[...] = jnp.zeros(acc_ref.shape, F32)

    def score_body(j, carry):
        kc = k_ref[pl.ds(pl.multiple_of(j * kb, kb), kb), :]
        for c in range(2):
            s = lax.dot_general(qm[c], kc, (((1,), (1,)), ((), ())), preferred_element_type=F32)
            s_ref[c, j] = s
            m = s[:, 0:V7X_LANES]
            for t in range(1, nl):
                m = jnp.maximum(m, s[:, t * V7X_LANES:(t + 1) * V7X_LANES])
            m_ref[c] = jnp.maximum(m_ref[c], m)
        return carry

    lax.fori_loop(0, nchunk, score_body, 0, unroll=unroll)
    m_row = [jnp.max(m_ref[c], axis=1, keepdims=True) for c in range(2)]

    def pv_body(j, carry):
        vc = v_ref[pl.ds(pl.multiple_of(j * kb, kb), kb), :]
        for c in range(2):
            p = jnp.exp2(s_ref[c, j] - m_row[c])
            acc_ref[c] += jnp.dot(p.astype(BF16), vc, preferred_element_type=F32)
        return carry

    lax.fori_loop(0, nchunk, pv_body, 0, unroll=unroll)
    outs = [acc_ref[c, :, 0:DA_V_DIM] / acc_ref[c, :, DA_V_DIM:2 * DA_V_DIM] for c in range(2)]
    o = outs[0] - lam_ref[0] * outs[1]
    ms = jnp.mean(o * o, axis=1, keepdims=True)
    o_ref[...] = (o * lax.rsqrt(ms + LN_EPS) * g_ref[...] * (1.0 - LAM_INIT)).astype(o_ref.dtype)


def _attention(q, k, v, lam, subln_g, B, L, qb=512, kb=1024, unroll=8):
    T = q.shape[0]
    nq = L // qb
    return pl.pallas_call(
        functools.partial(_attn_kernel, kb=kb, unroll=unroll),
        grid=(B, DA_HEADS, nq),
        in_specs=[
            pl.BlockSpec(memory_space=pltpu.SMEM),
            pl.BlockSpec((qb, DA_V_DIM), lambda b, h, i: (b * nq + i, h)),
            pl.BlockSpec((L, DA_V_DIM), lambda b, h, i: (b, h)),
            pl.BlockSpec((L, 2 * DA_V_DIM), lambda b, h, i: (b, h)),
            pl.BlockSpec((1, DA_V_DIM), lambda b, h, i: (0, 0)),
        ],
        out_specs=pl.BlockSpec((qb, DA_V_DIM), lambda b, h, i: (b * nq + i, h)),
        out_shape=jax.ShapeDtypeStruct((T, ATTN_WIDTH), BF16),
        scratch_shapes=[
            pltpu.VMEM((2, L // kb, qb, kb), F32),
            pltpu.VMEM((2, qb, V7X_LANES), F32),
            pltpu.VMEM((2, qb, 2 * DA_V_DIM), F32),
        ],
        compiler_params=_cparams(("parallel", "parallel", "parallel"), 56),
        name="attn",
    )(lam, q, k, v, subln_g)


def _hpre_kernel(u_ref, up_ref, un_ref, w_ref, b_ref, x0_ref, vx_ref):
    i = pl.program_id(1)
    n = pl.num_programs(1)
    tb = u_ref.shape[0]
    C = x0_ref.shape[-1]
    row = lax.broadcasted_iota(jnp.int32, (tb, V7X_LANES), 0)
    has_prev = (i > 0).astype(F32)
    has_next = (i < n - 1).astype(F32)

    def conv(c0):
        sl = slice(c0, c0 + V7X_LANES)
        u = u_ref[:, sl]
        prev_row = up_ref[V7X_SUBLANES - 1:V7X_SUBLANES, sl] * has_prev
        next_row = un_ref[0:1, sl] * has_next
        u_prev = jnp.where(row == 0, prev_row, pltpu.roll(u, 1, axis=0))
        u_next = jnp.where(row == tb - 1, next_row, pltpu.roll(u, tb - 1, axis=0))
        return u_prev * w_ref[0:1, sl] + u * w_ref[1:2, sl] + u_next * w_ref[2:3, sl] + b_ref[:, sl]

    tile = x0_ref.shape[:-1] + (V7X_LANES,)
    for c in range(C // V7X_LANES):
        c0 = c * V7X_LANES
        x0_ref[:, :, :, c0:c0 + V7X_LANES] = conv(c0).reshape(tile)
        vx_ref[:, :, :, c0:c0 + V7X_LANES] = (conv(2 * C + c0) * conv(C + c0)).reshape(tile)


def _hpre(u, conv_w, conv_b, B, L, tb=1024):
    T, C3 = u.shape
    C = C3 // 3
    nt = L // tb
    sub = V7X_SUBLANES
    na = tb // DFT_N2
    split = pl.BlockSpec((None, na, DFT_N2H, sub, C), lambda b, i: (b, i, 0, 0, 0))
    split_shape = jax.ShapeDtypeStruct((B, L // DFT_N2, DFT_N2H, sub, C), F32)
    cur = lambda b, i: (b * nt + i, 0)
    prev = lambda b, i: (jnp.maximum((b * L + i * tb) // sub - 1, 0), 0)
    nxt = lambda b, i: (jnp.minimum((b * L + (i + 1) * tb) // sub, T // sub - 1), 0)
    fixed = lambda b, i: (0, 0)
    return pl.pallas_call(
        _hpre_kernel,
        grid=(B, nt),
        in_specs=[
            pl.BlockSpec((tb, C3), cur),
            pl.BlockSpec((sub, C3), prev),
            pl.BlockSpec((sub, C3), nxt),
            pl.BlockSpec((SHORT_CONV, C3), fixed),
            pl.BlockSpec((1, C3), fixed),
        ],
        out_specs=[split, split],
        out_shape=[split_shape, split_shape],
        compiler_params=_cparams(("parallel", "parallel"), 32),
        name="hpre",
    )(u, u, u, conv_w, conv_b)


def _filt_kernel(z_ref, w1_ref, b1_ref, w2_ref, b2_ref, w3_ref, b3_ref, fr_ref, wo_ref, dl_ref,
                 hfb_ref, asum_ref, *, L):
    i = pl.program_id(0)
    tl = z_ref.shape[0]
    C = dl_ref.shape[1]
    hp = lax.Precision.HIGHEST
    h = jnp.sin(fr_ref[0:1, :] * (jnp.dot(z_ref[...], w1_ref[...], precision=hp, preferred_element_type=F32) + b1_ref[...]))
    h = jnp.sin(fr_ref[1:2, :] * (jnp.dot(h, w2_ref[...], precision=hp, preferred_element_type=F32) + b2_ref[...]))
    h = jnp.sin(fr_ref[2:3, :] * (jnp.dot(h, w3_ref[...], precision=hp, preferred_element_type=F32) + b3_ref[...]))
    o = jnp.dot(h, wo_ref[...], precision=hp, preferred_element_type=F32)
    grow = lax.broadcasted_iota(jnp.int32, (tl, C), 0) + i * tl
    t = grow.astype(F32) * (1.0 / (L - 1))
    decay = jnp.exp(-t * dl_ref[...])
    hf = o[:, :C] * decay
    hb = jnp.where(grow == 0, 0.0, o[:, C:] * decay)
    tile = hfb_ref.shape[:-1] + (C,)
    hfb_ref[:, :, :, :C] = hf.reshape(tile)
    hfb_ref[:, :, :, C:] = hb.reshape(tile)

    @pl.when(i == 0)
    def _():
        asum_ref[...] = jnp.zeros_like(asum_ref)

    asum_ref[...] += jnp.sum(jnp.abs(hf) + jnp.abs(hb), axis=0, keepdims=True)


def _filter_taps(L, w1, b1, w2, b2, w3, b3, freq, wout, tl=512):
    C = wout.shape[1] // 2
    order = w1.shape[1]
    emb = w1.shape[0]
    t = jnp.linspace(0.0, 1.0, L, dtype=F32)[:, None]
    w = 2.0 * math.pi * jnp.arange(L, dtype=F32)[:, None] / L
    f = jnp.linspace(1e-4, FILTER_BANDS - 1, FILTER_BANDS, dtype=F32)[None, :]
    z = jnp.concatenate([t, jnp.cos(f * w), -jnp.sin(f * w)], axis=-1)
    zp = jnp.pad(z, ((0, 0), (0, V7X_LANES - emb)))
    w1p = jnp.pad(w1, ((0, V7X_LANES - emb), (0, 0)))
    deltas = jnp.abs(jnp.linspace(math.log(DECAY_TARGET) / SLOW_DECAY, math.log(DECAY_TARGET) / FAST_DECAY, C, dtype=F32))[None]
    fixed = lambda i: (0, 0)
    return pl.pallas_call(
        functools.partial(_filt_kernel, L=L),
        grid=(L // tl,),
        in_specs=[
            pl.BlockSpec((tl, V7X_LANES), lambda i: (i, 0)),
            pl.BlockSpec((V7X_LANES, order), fixed), pl.BlockSpec((1, order), fixed),
            pl.BlockSpec((order, order), fixed), pl.BlockSpec((1, order), fixed),
            pl.BlockSpec((order, order), fixed), pl.BlockSpec((1, order), fixed),
            pl.BlockSpec((3, order), fixed),
            pl.BlockSpec((order, 2 * C), fixed),
            pl.BlockSpec((1, C), fixed),
        ],
        out_specs=[pl.BlockSpec((None, tl // DFT_N2, DFT_N2H, V7X_SUBLANES, 2 * C), lambda i: (0, i, 0, 0, 0)),
                   pl.BlockSpec((1, C), fixed)],
        out_shape=[jax.ShapeDtypeStruct((1, L // DFT_N2, DFT_N2H, V7X_SUBLANES, 2 * C), F32),
                   jax.ShapeDtypeStruct((1, C), F32)],
        compiler_params=_cparams(("arbitrary",), 32),
        name="filt",
    )(zp, w1p, b1[None], w2, b2[None], w3, b3[None], freq, wout, deltas)


def _dft_constants(L):
    n1 = 2 * L // DFT_N2
    n1h = n1 // 2
    N = 2 * L
    sub = V7X_SUBLANES

    def cs(num, den):
        ang = (num % den).astype(np.float64) * (2.0 * np.pi / den)
        return np.cos(ang), np.sin(ang)

    k1 = np.arange(n1, dtype=np.int64)
    a = np.arange(n1h, dtype=np.int64)
    c, s = cs(k1[:, None] * a[None, :], n1)
    eye = np.eye(sub)

    def expand(m):
        r, kk = m.shape
        return (m[:, None, :, None] * eye[None, :, None, :]).reshape(r * sub, kk * sub)

    def const(m):
        return jnp.asarray(m.astype(np.float32).astype(BF16))

    m1c = np.stack([np.concatenate([c, s], axis=1), np.concatenate([-s, c], axis=1)], axis=1).reshape(2 * n1, 2 * n1h)
    m1r = np.stack([c, -s], axis=1).reshape(2 * n1, n1h)
    ct, st = c.T, s.T
    m3 = np.stack([np.stack([ct, -st], axis=2).reshape(n1h, 2 * n1),
                   np.stack([st, ct], axis=2).reshape(n1h, 2 * n1)], axis=0).reshape(2 * n1h, 2 * n1)
    n2 = np.arange(DFT_N2, dtype=np.int64)
    gc, gs = cs(n2[:, None] * n2[None, :], DFT_N2)
    g_fwd = np.concatenate([np.concatenate([gc, gs], axis=1), np.concatenate([-gs, gc], axis=1)], axis=0)
    g_inv = np.concatenate([np.concatenate([gc, -gs], axis=1), np.concatenate([gs, gc], axis=1)], axis=0)
    ph = (jnp.arange(n1, dtype=jnp.int32)[:, None] * jnp.arange(DFT_N2, dtype=jnp.int32)[None, :]) % N
    ang = ph.astype(F32) * (2.0 * math.pi / N)
    lanes = (n1, DFT_N2H, sub, V7X_LANES)
    twc = jnp.broadcast_to(jnp.cos(ang).reshape(n1, DFT_N2H, sub, 1), lanes)
    tws = jnp.broadcast_to(jnp.sin(ang).reshape(n1, DFT_N2H, sub, 1), lanes)
    return dict(n1=n1, n1h=n1h, m1c=const(expand(m1c)), m1r=const(expand(m1r)), m3=const(expand(m3)),
                g_fwd=const(g_fwd), g_inv=const(g_inv), twc=twc, tws=tws)


def _dft1_kernel(x_ref, m_ref, o_ref):
    rows = m_ref.shape[1]
    xs = x_ref[...].reshape(rows, x_ref.shape[-1]).astype(BF16)
    o_ref[...] = jnp.dot(m_ref[...], xs, preferred_element_type=F32).reshape(o_ref.shape)


def _dft1(x5, m, n1, cb):
    P, n1h, _, sub, Cx = x5.shape
    return pl.pallas_call(
        _dft1_kernel,
        grid=(DFT_N2H, Cx // cb),
        in_specs=[
            pl.BlockSpec((P, n1h, None, sub, cb), lambda h, c: (0, 0, h, 0, c)),
            pl.BlockSpec(m.shape, lambda h, c: (0, 0)),
        ],
        out_specs=pl.BlockSpec((None, n1, 2, sub, cb), lambda h, c: (h, 0, 0, 0, c)),
        out_shape=jax.ShapeDtypeStruct((DFT_N2H, n1, 2, sub, Cx), F32),
        compiler_params=_cparams(("parallel", "parallel"), 48),
        name="dft1",
    )(x5, m)


def _dft2_kernel(a_ref, f_ref, twc_ref, tws_ref, gf_ref, gi_ref, o_ref):
    C = a_ref.shape[-1]
    half = DFT_N2

    def lanes(fn):
        return jnp.concatenate([fn(slice(c0, c0 + V7X_LANES)) for c0 in range(0, C, V7X_LANES)], axis=-1)

    for kk in range(a_ref.shape[1]):
        twc, tws = twc_ref[kk], tws_ref[kk]

        def spectrum(ref, col0):
            def re_part(sl):
                return ref[:, kk, 0, :, col0 + sl.start:col0 + sl.stop] * twc + ref[:, kk, 1, :, col0 + sl.start:col0 + sl.stop] * tws

            def im_part(sl):
                return ref[:, kk, 1, :, col0 + sl.start:col0 + sl.stop] * twc - ref[:, kk, 0, :, col0 + sl.start:col0 + sl.stop] * tws

            t = jnp.concatenate([lanes(re_part).reshape(half, C), lanes(im_part).reshape(half, C)], axis=0).astype(BF16)
            s = jnp.dot(gf_ref[...], t, preferred_element_type=F32)
            return s[:half], s[half:]

        xr, xi = spectrum(a_ref, 0)
        fr, fi = spectrum(f_ref, 0)
        br, bi = spectrum(f_ref, C)
        hr, hi = fr + br, fi - bi
        y = jnp.concatenate([xr * hr - xi * hi, xr * hi + xi * hr], axis=0).astype(BF16)
        b = jnp.dot(gi_ref[...], y, preferred_element_type=F32)
        br2 = b[:half].reshape(DFT_N2H, V7X_SUBLANES, C)
        bi2 = b[half:].reshape(DFT_N2H, V7X_SUBLANES, C)
        for c0 in range(0, C, V7X_LANES):
            sl = slice(c0, c0 + V7X_LANES)
            o_ref[:, kk, 0, :, sl] = br2[:, :, sl] * twc - bi2[:, :, sl] * tws
            o_ref[:, kk, 1, :, sl] = bi2[:, :, sl] * twc + br2[:, :, sl] * tws


def _dft2(a5, f5, k, kb=8):
    n2h, n1, _, sub, C = a5.shape
    kb = min(kb, n1)
    return pl.pallas_call(
        _dft2_kernel,
        grid=(n1 // kb,),
        in_specs=[
            pl.BlockSpec((n2h, kb, 2, sub, C), lambda i: (0, i, 0, 0, 0)),
            pl.BlockSpec((n2h, kb, 2, sub, 2 * C), lambda i: (0, i, 0, 0, 0)),
            pl.BlockSpec((kb, n2h, sub, V7X_LANES), lambda i: (i, 0, 0, 0)),
            pl.BlockSpec((kb, n2h, sub, V7X_LANES), lambda i: (i, 0, 0, 0)),
            pl.BlockSpec((2 * DFT_N2, 2 * DFT_N2), lambda i: (0, 0)),
            pl.BlockSpec((2 * DFT_N2, 2 * DFT_N2), lambda i: (0, 0)),
        ],
        out_specs=pl.BlockSpec((n2h, kb, 2, sub, C), lambda i: (0, i, 0, 0, 0)),
        out_shape=jax.ShapeDtypeStruct(a5.shape, F32),
        compiler_params=_cparams(("parallel",), 48),
        name="dft2",
    )(a5, f5, k["twc"], k["tws"], k["g_fwd"], k["g_inv"])


def _dft3_kernel(b_ref, m_ref, x0_ref, vx_ref, sc_ref, d_ref, o_ref):
    C = b_ref.shape[-1]
    bs = b_ref[...].reshape(m_ref.shape[1], C).astype(BF16)
    y = jnp.dot(m_ref[...], bs, preferred_element_type=F32).reshape(o_ref.shape)
    o_ref[...] = x0_ref[...] * (y * sc_ref[...] + vx_ref[...] * d_ref[...])


def _dft3(b5, m3, x05, vx5, scale, d):
    n2h, n1, _, sub, C = b5.shape
    Bt, n1h = x05.shape[0], x05.shape[1]
    tok = pl.BlockSpec((Bt, n1h, None, sub, C), lambda h: (0, 0, h, 0, 0))
    vec = pl.BlockSpec((1, C), lambda h: (0, 0))
    return pl.pallas_call(
        _dft3_kernel,
        grid=(n2h,),
        in_specs=[
            pl.BlockSpec((None, n1, 2, sub, C), lambda h: (h, 0, 0, 0, 0)),
            pl.BlockSpec(m3.shape, lambda h: (0, 0)),
            tok, tok, vec, vec,
        ],
        out_specs=tok,
        out_shape=jax.ShapeDtypeStruct(x05.shape, F32),
        compiler_params=_cparams(("parallel",), 48),
        name="dft3",
    )(b5, m3, x05, vx5, scale, d)


def _hyena(u, conv_w, conv_b, w1, b1, w2, b2, w3, b3, freq, wout, d_skip, B, L):
    T = u.shape[0]
    C = HYENA_WIDTH
    k = _dft_constants(L)
    n1, n1h = k["n1"], k["n1h"]
    x0, vx = _hpre(u, conv_w, conv_b[None], B, L)
    hfb, asum = _filter_taps(L, w1, b1, w2, b2, w3, b3, freq, wout)
    a5 = _dft1(vx, k["m1c"], n1, cb=C)
    f5 = _dft1(hfb, k["m1r"], n1, cb=C)
    b5 = _dft2(a5, f5, k)
    scale = 1.0 / (asum * (2 * L))
    return _dft3(b5, k["m3"], x0, vx, scale, d_skip[None])


PACKED = jnp.uint32


def _pack_pairs(x):
    half = x.shape[1] // 2
    bits = pltpu.bitcast(x.astype(BF16).astype(F32), PACKED)
    return (bits[:, :half] >> 16) | bits[:, half:]


def _unpack_pairs(w):
    lo = pltpu.bitcast(w << 16, F32)
    hi = pltpu.bitcast(w & jnp.uint32(0xFFFF0000), F32)
    return lo, hi


def _oproj_kernel(x_ref, g0_ref, b0_ref, at_ref, hy_ref, wo_ref, g1_ref, b1_ref, wrh_ref, wrl_ref,
                  h1_ref, h1p_ref, sc_ref):
    aw = at_ref.shape[1]
    h0 = _layer_norm_rows(x_ref[...], g0_ref[...], b0_ref[...])
    mixed = jnp.dot(at_ref[...], wo_ref[0:aw, :], preferred_element_type=F32)
    hy = hy_ref[...].reshape(x_ref.shape[0], hy_ref.shape[-1])
    mixed = mixed + jnp.dot(hy.astype(BF16), wo_ref[aw:, :], preferred_element_type=F32)
    h1 = _layer_norm_rows(ALPHA * h0 + mixed, g1_ref[...], b1_ref[...])
    h1_ref[...] = h1
    h1p_ref[...] = _pack_pairs(h1)
    hh = h1.astype(BF16)
    hl = (h1 - hh.astype(F32)).astype(BF16)
    dn = (((1,), (1,)), ((), ()))
    logits = lax.dot_general(wrh_ref[...], hh, dn, preferred_element_type=F32)
    logits = logits + (lax.dot_general(wrh_ref[...], hl, dn, preferred_element_type=F32)
                       + lax.dot_general(wrl_ref[...], hh, dn, preferred_element_type=F32))
    sc_ref[...] = 1.0 / (1.0 + jnp.exp(-logits))


def _oproj(x2, g0, b0, attn, hy, wo_bf, g1, b1, wr_t, tm=512):
    T, D = x2.shape
    E = wr_t.shape[0]
    wrh = wr_t.astype(BF16)
    wrl = (wr_t - wrh.astype(F32)).astype(BF16)
    row = lambda i: (i, 0)
    fixed = lambda i: (0, 0)
    hy4 = hy.reshape((-1,) + hy.shape[2:])
    return pl.pallas_call(
        _oproj_kernel,
        grid=(T // tm,),
        in_specs=[
            pl.BlockSpec((tm, D), row), pl.BlockSpec((1, D), fixed), pl.BlockSpec((1, D), fixed),
            pl.BlockSpec((tm, attn.shape[1]), row),
            pl.BlockSpec((tm // DFT_N2,) + hy4.shape[1:], lambda i: (i, 0, 0, 0)),
            pl.BlockSpec((D, D), fixed), pl.BlockSpec((1, D), fixed), pl.BlockSpec((1, D), fixed),
            pl.BlockSpec((E, D), fixed), pl.BlockSpec((E, D), fixed),
        ],
        out_specs=[pl.BlockSpec((tm, D), row), pl.BlockSpec((tm, D // 2), row), pl.BlockSpec((E, tm), lambda i: (0, i))],
        out_shape=[
            jax.ShapeDtypeStruct((T, D), F32),
            jax.ShapeDtypeStruct((T, D // 2), PACKED),
            jax.ShapeDtypeStruct((E, T), F32),
        ],
        compiler_params=_cparams(("parallel",), 48),
        name="oproj",
    )(x2, g0, b0, attn, hy4, wo_bf, g1, b1, wrh, wrl)


def _route_kernel(sc_ref, bias_ref, tri_ref, idx_ref, gate_ref, rank_ref, cnt_ref, carry_ref):
    step = pl.program_id(0)

    @pl.when(step == 0)
    def _():
        carry_ref[...] = jnp.zeros_like(carry_ref)

    tm = V7X_LANES
    for part in range(sc_ref.shape[1] // tm):
        cols = slice(part * tm, (part + 1) * tm)
        _route_tokens(sc_ref[:, cols], bias_ref, tri_ref, idx_ref, gate_ref, rank_ref, carry_ref, cols)
    cnt_ref[...] = carry_ref[...].astype(jnp.int32)


def _route_tokens(scores, bias_ref, tri_ref, idx_ref, gate_ref, rank_ref, carry_ref, cols):
    E, tm = scores.shape
    neg = jnp.float32(-jnp.inf)
    biased = scores + bias_ref[...]
    erow = lax.broadcasted_iota(jnp.int32, (E, tm), 0)
    big = jnp.int32(E)

    def first_argmax(vals, rows):
        m = jnp.max(vals, axis=0, keepdims=True)
        pick = jnp.min(jnp.where(vals == m, rows, big), axis=0, keepdims=True)
        return m, pick

    gsc = []
    for g in range(N_GROUPS):
        blk = biased[g * GROUP_SIZE:(g + 1) * GROUP_SIZE, :]
        rows = erow[g * GROUP_SIZE:(g + 1) * GROUP_SIZE, :]
        m1, p1 = first_argmax(blk, rows)
        m2 = jnp.max(jnp.where(rows == p1, neg, blk), axis=0, keepdims=True)
        gsc.append(m1 + m2)
    gsc = jnp.concatenate(gsc, axis=0)
    grow = lax.broadcasted_iota(jnp.int32, (N_GROUPS, tm), 0)
    gsel = jnp.zeros((N_GROUPS, tm), jnp.bool_)
    work = gsc
    for _ in range(TOPK_GROUPS):
        _, p = first_argmax(work, grow)
        hit = grow == p
        gsel = gsel | hit
        work = jnp.where(hit, neg, work)
    emask = jnp.concatenate(
        [jnp.broadcast_to(gsel[g:g + 1, :], (GROUP_SIZE, tm)) for g in range(N_GROUPS)], axis=0)
    work = jnp.where(emask, biased, neg)

    sel = jnp.zeros((E, tm), jnp.bool_)
    picks, gvals = [], []
    for _ in range(TOP_K):
        _, p = first_argmax(work, erow)
        hit = erow == p
        sel = sel | hit
        picks.append(p)
        gvals.append(jnp.sum(jnp.where(hit, scores, 0.0), axis=0, keepdims=True))
        work = jnp.where(hit, neg, work)
    gv = jnp.concatenate(gvals, axis=0)
    idx_ref[:, cols] = jnp.concatenate(picks, axis=0)
    gate_ref[:, cols] = gv / jnp.sum(gv, axis=0, keepdims=True) * ROUTED_SCALE

    chosen = sel.astype(F32)
    before = jnp.dot(chosen.astype(BF16), tri_ref[...], preferred_element_type=F32) + carry_ref[...]
    rank_ref[:, cols] = jnp.concatenate(
        [jnp.sum(jnp.where(erow == p, before, 0.0), axis=0, keepdims=True) for p in picks], axis=0).astype(jnp.int32)
    carry_ref[...] += jnp.sum(chosen, axis=1, keepdims=True)


def _route(scores_t, bias, tm=256):
    E, T = scores_t.shape
    tm = min(tm, T)
    lanes = V7X_LANES
    tri = (jnp.arange(lanes)[:, None] < jnp.arange(lanes)[None, :]).astype(BF16)
    tok = lambda i: (0, i)
    fixed = lambda i: (0, 0)
    return pl.pallas_call(
        _route_kernel,
        grid=(T // tm,),
        in_specs=[pl.BlockSpec((E, tm), tok), pl.BlockSpec((E, 1), fixed), pl.BlockSpec((lanes, lanes), fixed)],
        out_specs=[pl.BlockSpec((TOP_K, tm), tok), pl.BlockSpec((TOP_K, tm), tok), pl.BlockSpec((TOP_K, tm), tok),
                   pl.BlockSpec((E, 1), fixed)],
        out_shape=[
            jax.ShapeDtypeStruct((TOP_K, T), jnp.int32),
            jax.ShapeDtypeStruct((TOP_K, T), F32),
            jax.ShapeDtypeStruct((TOP_K, T), jnp.int32),
            jax.ShapeDtypeStruct((E, 1), jnp.int32),
        ],
        scratch_shapes=[pltpu.VMEM((E, 1), F32)],
        compiler_params=_cparams(("arbitrary",), 32),
        name="route",
    )(scores_t, bias, tri)


def _dest_kernel(idx_ref, rank_ref, ps_ref, dest_ref):
    E = ps_ref.shape[0]
    tm = idx_ref.shape[1]
    erow = lax.broadcasted_iota(jnp.int32, (E, tm), 0)
    ps = ps_ref[...].astype(F32)
    rows = [jnp.sum(jnp.where(erow == idx_ref[k:k + 1, :], ps, 0.0), axis=0, keepdims=True) for k in range(TOP_K)]
    dest_ref[...] = jnp.concatenate(rows, axis=0).astype(jnp.int32) + rank_ref[...]


def _dest_rows(idx, rank, pad_start, tm=1024):
    K, T = idx.shape
    E = pad_start.shape[0]
    tm = min(tm, T)
    tok = lambda i: (0, i)
    return pl.pallas_call(
        _dest_kernel,
        grid=(T // tm,),
        in_specs=[pl.BlockSpec((K, tm), tok), pl.BlockSpec((K, tm), tok), pl.BlockSpec((E, 1), lambda i: (0, 0))],
        out_specs=pl.BlockSpec((K, tm), tok),
        out_shape=jax.ShapeDtypeStruct((K, T), jnp.int32),
        compiler_params=_cparams(("arbitrary",), 32),
        name="dest",
    )(idx, rank, pad_start[:, None])


def _scatter_kernel(ps_ref, pe_ref, dest_ref, h_ref, xs_ref, zbuf, sem, *, rb):
    sub = V7X_SUBLANES
    tm = h_ref.shape[0]

    @pl.when(pl.program_id(0) == 0)
    def _():
        zbuf[...] = jnp.zeros_like(zbuf)

        def zcopy(e):
            return pltpu.make_async_copy(zbuf, xs_ref.at[pl.ds(pl.multiple_of(pe_ref[e] - rb, rb), rb)], sem)

        def zstart(e, c):
            @pl.when(pe_ref[e] > ps_ref[e])
            def _():
                zcopy(e).start()
            return c

        def zwait(e, c):
            @pl.when(pe_ref[e] > ps_ref[e])
            def _():
                zcopy(e).wait()
            return c

        lax.fori_loop(0, ps_ref.shape[0], zstart, 0)
        lax.fori_loop(0, ps_ref.shape[0], zwait, 0)

    def issue(r8, c):
        for j in range(sub):
            for k in range(TOP_K):
                dst = dest_ref[0, r8 * (sub * TOP_K) + (j * TOP_K + k)]
                pltpu.make_async_copy(h_ref.at[pl.ds(r8 * sub + j, 1)], xs_ref.at[pl.ds(dst, 1)], sem).start(priority=k % 2)
        return c

    lax.fori_loop(0, tm // sub, issue, 0)
    for k in range(TOP_K):
        pltpu.make_async_copy(h_ref, xs_ref.at[pl.ds(0, tm)], sem).wait()


def _dest_tiles(dest, tm):
    K, T = dest.shape
    return dest.T.reshape(T // tm, 1, tm * K)


def _scatter_rows(pad_start, pad_end, dest_t, h1p, n_rows, rb, tm=256):
    T, W = h1p.shape
    return pl.pallas_call(
        functools.partial(_scatter_kernel, rb=rb),
        grid_spec=pltpu.PrefetchScalarGridSpec(
            num_scalar_prefetch=2,
            grid=(T // tm,),
            in_specs=[
                pl.BlockSpec((None, 1, tm * TOP_K), lambda i, ps, pe: (i, 0, 0), memory_space=pltpu.SMEM),
                pl.BlockSpec((tm, W), lambda i, ps, pe: (i, 0)),
            ],
            out_specs=pl.BlockSpec(memory_space=pl.ANY),
            scratch_shapes=[pltpu.VMEM((rb, W), h1p.dtype), pltpu.SemaphoreType.DMA],
        ),
        out_shape=jax.ShapeDtypeStruct((n_rows, W), h1p.dtype),
        compiler_params=_cparams(("arbitrary",), 32),
        name="scatter",
    )(pad_start, pad_end, dest_t, h1p)


EXPERT_X_SLOTS = 4
EXPERT_Y_SLOTS = 3


def _experts_kernel(first_ref, nblk_ref, wsel_ref, tot_ref, xs_ref, wg_ref, wu_ref, wd_ref, ys_ref,
                    xbuf, ybuf, wgub, wdb, xsem, ysem, *, rb):
    del wsel_ref
    e = pl.program_id(0)
    F = wg_ref.shape[1]
    nx, ny = xbuf.shape[0], ybuf.shape[0]
    total = tot_ref[0]
    n = nblk_ref[e]
    g0 = first_ref[e]

    def x_copy(g):
        slot = g % nx
        return pltpu.make_async_copy(xs_ref.at[pl.ds(pl.multiple_of(g * rb, rb), rb)], xbuf.at[slot], xsem.at[slot])

    def y_copy(g):
        slot = g % ny
        return pltpu.make_async_copy(ybuf.at[slot], ys_ref.at[pl.ds(pl.multiple_of(g * rb, rb), rb)], ysem.at[slot])

    @pl.when(e == 0)
    def _():
        for g in range(nx):
            @pl.when(g < total)
            def _():
                x_copy(g).start()

    def process(g, count):
        for d in range(count):
            x_copy(g + d).wait()
        ys = []
        for d in range(count):
            lo, hi = _unpack_pairs(xbuf[(g + d) % nx])
            x = jnp.concatenate([lo, hi], axis=1).astype(BF16)
            gu = jnp.dot(x, wgub[...], preferred_element_type=F32)
            gate, up = gu[:, 0:F], gu[:, F:]
            hb = (gate / (1.0 + jnp.exp(-gate)) * up).astype(BF16)
            ys.append(_pack_pairs(jnp.dot(hb, wdb[...], preferred_element_type=F32)))
        for d in range(count):
            @pl.when(g + d >= ny)
            def _():
                y_copy(g + d - ny).wait()

            ybuf[(g + d) % ny] = ys[d]
            y_copy(g + d).start()
        for d in range(count):
            @pl.when(g + d + nx < total)
            def _():
                x_copy(g + d + nx).start()

    @pl.when(n > 0)
    def _():
        wgub[:, 0:F] = wg_ref[...].astype(BF16)
        wgub[:, F:] = wu_ref[...].astype(BF16)
        wdb[...] = wd_ref[...].astype(BF16)

        def pair(j, c):
            process(g0 + 2 * j, 2)
            return c

        lax.fori_loop(0, n // 2, pair, 0)

        @pl.when(n % 2 == 1)
        def _():
            process(g0 + n - 1, 1)

    @pl.when(e == pl.num_programs(0) - 1)
    def _():
        for back in range(ny, 0, -1):
            @pl.when(total >= back)
            def _():
                y_copy(total - back).wait()


def _experts(first_blk, nblk, wsel, total_blk, xs, w_gate, w_up, w_down, rb):
    P, W = xs.shape
    E, D, F = w_gate.shape
    wspec = lambda shape: pl.BlockSpec((None,) + shape, lambda e, fb, nbk, ws, tt: (ws[e], 0, 0))
    return pl.pallas_call(
        functools.partial(_experts_kernel, rb=rb),
        grid_spec=pltpu.PrefetchScalarGridSpec(
            num_scalar_prefetch=4,
            grid=(E,),
            in_specs=[pl.BlockSpec(memory_space=pl.ANY), wspec((D, F)), wspec((D, F)), wspec((F, D))],
            out_specs=pl.BlockSpec(memory_space=pl.ANY),
            scratch_shapes=[
                pltpu.VMEM((EXPERT_X_SLOTS, rb, W), PACKED), pltpu.VMEM((EXPERT_Y_SLOTS, rb, W), PACKED),
                pltpu.VMEM((D, 2 * F), BF16), pltpu.VMEM((F, D), BF16),
                pltpu.SemaphoreType.DMA((EXPERT_X_SLOTS,)), pltpu.SemaphoreType.DMA((EXPERT_Y_SLOTS,)),
            ],
        ),
        out_shape=jax.ShapeDtypeStruct((P, W), PACKED),
        compiler_params=_cparams(("arbitrary",), 48),
        name="experts",
    )(first_blk, nblk, wsel, total_blk, xs, w_gate, w_up, w_down)


def _combine_kernel(dest_ref, destn_ref, h1_ref, gate_ref, ys_ref, ys8_ref, sg_ref, su_ref, sd_ref,
                    g2_ref, b2_ref, o_ref, buf, sem):
    i = pl.program_id(0)
    n = pl.num_programs(0)
    tm = h1_ref.shape[0]
    slot = i % 2

    sub = V7X_SUBLANES

    def issue_rows(dref, s, r0, nrows):
        for r in range(r0, r0 + nrows):
            for k in range(TOP_K):
                src = ys_ref.at[pl.ds(dref[0, r * TOP_K + k], 1)]
                pltpu.make_async_copy(src, buf.at[s, r, pl.ds(k, 1)], sem.at[s]).start(priority=k % 2)

    def wait_tile(s):
        pltpu.make_async_copy(ys8_ref.at[pl.ds(0, tm)], buf.at[s], sem.at[s]).wait()

    @pl.when(i == 0)
    def _():
        def body(r8, c):
            for j in range(sub):
                for k in range(TOP_K):
                    src = ys_ref.at[pl.ds(dest_ref[0, r8 * (sub * TOP_K) + (j * TOP_K + k)], 1)]
                    pltpu.make_async_copy(src, buf.at[0, r8 * sub + j, pl.ds(k, 1)], sem.at[0]).start(priority=k % 2)
            return c
        lax.fori_loop(0, tm // sub, body, 0)

    wait_tile(slot)
    parts = 4
    rows_per = tm // parts
    nk = rows_per * TOP_K
    pick = lax.broadcasted_iota(jnp.int32, (TOP_K, nk), 1) % TOP_K == lax.broadcasted_iota(jnp.int32, (TOP_K, nk), 0)
    spread = pick.astype(BF16)
    own = lax.broadcasted_iota(jnp.int32, (rows_per, nk), 1) // TOP_K == lax.broadcasted_iota(jnp.int32, (rows_per, nk), 0)
    for part in range(parts):
        r0 = part * rows_per
        issue_rows(destn_ref, 1 - slot, r0, rows_per)
        rows = slice(r0, r0 + rows_per)
        h1 = h1_ref[rows, :]
        hb = h1.astype(BF16)
        g = jnp.dot(hb, sg_ref[...], preferred_element_type=F32)
        u = jnp.dot(hb, su_ref[...], preferred_element_type=F32)
        ffn = jnp.dot((g / (1.0 + jnp.exp(-g)) * u).astype(BF16), sd_ref[...], preferred_element_type=F32)
        lo, hi = _unpack_pairs(buf[slot, rows].reshape(nk, buf.shape[3]))
        y = jnp.concatenate([lo, hi], axis=1).astype(BF16)
        gates = gate_ref[rows, :]
        gh = gates.astype(BF16)
        gl = (gates - gh.astype(F32)).astype(BF16)
        routed = jnp.zeros((rows_per, y.shape[1]), F32)
        for piece in (gh, gl):
            gm = jnp.where(own, jnp.dot(piece, spread, preferred_element_type=F32), 0.0).astype(BF16)
            routed = routed + jnp.dot(gm, y, preferred_element_type=F32)
        o_ref[rows, :] = _layer_norm_rows(ALPHA * h1 + ffn + routed, g2_ref[...], b2_ref[...])

    @pl.when(i == n - 1)
    def _():
        wait_tile(1 - slot)


def _combine(dest, h1, gates_tk, ys, sg, su, sd, g2, b2, tm=512):
    T, D = h1.shape
    W = ys.shape[1]
    F = sg.shape[1]
    nt = T // tm
    row = lambda i: (i, 0)
    fixed = lambda i: (0, 0)
    smem = functools.partial(pl.BlockSpec, (None, 1, tm * TOP_K), memory_space=pltpu.SMEM)
    dest_t = _dest_tiles(dest, tm)
    ys8 = ys.reshape(ys.shape[0] // TOP_K, TOP_K, W)
    return pl.pallas_call(
        _combine_kernel,
        grid=(nt,),
        in_specs=[
            smem(index_map=lambda i: (i, 0, 0)),
            smem(index_map=lambda i: (jnp.minimum(i + 1, nt - 1), 0, 0)),
            pl.BlockSpec((tm, D), row),
            pl.BlockSpec((tm, TOP_K), row),
            pl.BlockSpec(memory_space=pl.ANY),
            pl.BlockSpec(memory_space=pl.ANY),
            pl.BlockSpec((D, F), fixed), pl.BlockSpec((D, F), fixed), pl.BlockSpec((F, D), fixed),
            pl.BlockSpec((1, D), fixed), pl.BlockSpec((1, D), fixed),
        ],
        out_specs=pl.BlockSpec((tm, D), row),
        out_shape=jax.ShapeDtypeStruct((T, D), F32),
        scratch_shapes=[pltpu.VMEM((2, tm, TOP_K, W), PACKED), pltpu.SemaphoreType.DMA((2,))],
        compiler_params=_cparams(("arbitrary",), 48),
        name="combine",
    )(dest_t, dest_t, h1, gates_tk, ys, ys8, sg, su, sd, g2, b2)


MOE_ROWS = 256


def _moe(h1, h1p, scores_t, router_bias, w_gate, w_up, w_down, ws_gate, ws_up, ws_down, g2, b2):
    T = h1.shape[0]
    E = N_EXPERTS
    rb = MOE_ROWS
    idx, gates, rank, counts = _route(scores_t, router_bias[:, None])
    counts = counts[:, 0]
    padded = (counts + rb - 1) // rb * rb
    pad_end = jnp.cumsum(padded)
    pad_start = pad_end - padded
    nb = (T * TOP_K) // rb + E
    nblk = (padded // rb).astype(jnp.int32)
    first_blk = (pad_start // rb).astype(jnp.int32)
    total_blk = (pad_end[-1:] // rb).astype(jnp.int32)
    wsel = lax.cummax(jnp.where(nblk > 0, jnp.arange(E, dtype=jnp.int32), 0))
    dest = _dest_rows(idx, rank, pad_start)
    scatter_tm = min(512, T)
    xs = _scatter_rows(pad_start, pad_end, _dest_tiles(dest, scatter_tm), h1p, nb * rb, rb, tm=scatter_tm)
    ys = _experts(first_blk, nblk, wsel, total_blk, xs, w_gate, w_up, w_down, rb)
    return _combine(dest, h1, gates.T, ys, ws_gate.astype(BF16), ws_up.astype(BF16), ws_down.astype(BF16), g2, b2)


def kernel(x, positions, emb_ln_g, emb_ln_b, w_in, hy_conv_w, hy_conv_b, hy_f_w1, hy_f_b1, hy_f_w2, hy_f_b2, hy_f_w3, hy_f_b3, hy_f_freq, hy_f_wout, hy_d, lambda_q1, lambda_k1, lambda_q2, lambda_k2, subln_g, w_o, ln1_g, ln1_b, w_router, router_bias, w_gate, w_up, w_down, ws_gate, ws_up, ws_down, ln2_g, ln2_b):
    B, L, D = x.shape
    T = B * L
    assert w_in.shape[0] == DEPTH == 1
    i = 0
    x2 = x.reshape(T, D)
    g0, b0 = emb_ln_g[None], emb_ln_b[None]
    ra, rm, rp = _rotary_tables(positions)
    q, k, v, u = _inproj(x2, g0, b0, w_in[i].astype(BF16), ra, rm, rp)
    lam = (jnp.exp(jnp.sum(lambda_q1[i] * lambda_k1[i])) - jnp.exp(jnp.sum(lambda_q2[i] * lambda_k2[i])) + LAM_INIT)
    attn = _attention(q, k, v, lam.reshape(1).astype(F32), subln_g[i][None], B, L)
    hy = _hyena(u, hy_conv_w[i], hy_conv_b[i], hy_f_w1[i], hy_f_b1[i], hy_f_w2[i], hy_f_b2[i], hy_f_w3[i], hy_f_b3[i],
                hy_f_freq[i], hy_f_wout[i], hy_d[i], B, L)
    h1, h1p, scores_t = _oproj(x2, g0, b0, attn, hy, w_o[i].astype(BF16), ln1_g[i][None], ln1_b[i][None], w_router[i].T)
    out = _moe(h1, h1p, scores_t, router_bias[i], w_gate[i], w_up[i], w_down[i], ws_gate[i], ws_up[i], ws_down[i],
               ln2_g[i][None], ln2_b[i][None])
    return out.reshape(B, L, D)
```

```python
import functools
import math

import numpy as np
import jax
import jax.numpy as jnp
from jax import lax
from jax.experimental import pallas as pl
from jax.experimental.pallas import tpu as pltpu

DA_HEADS = 4
DA_HEAD_DIM = 64
DA_V_DIM = 128
ATTN_WIDTH = 512
HYENA_WIDTH = 512
ROT_DIM = 16
ROPE_THETA = 500000.0
SHORT_CONV = 3
FILTER_EMB = 33
FILTER_BANDS = 16
DECAY_TARGET = 1e-2
FAST_DECAY = 0.3
SLOW_DECAY = 1.5
N_EXPERTS = 256
TOP_K = 8
N_GROUPS = 8
GROUP_SIZE = N_EXPERTS // N_GROUPS
TOPK_GROUPS = 4
EXPERT_DIM = 256
ROUTED_SCALE = 2.5
DEPTH = 1
ALPHA = (2 * DEPTH) ** 0.25
LN_EPS = 1e-5
LAM_INIT = 0.8 - 0.6 * math.exp(-0.3 * 0)

V7X_LANES = 128
V7X_SUBLANES = 8
V7X_VMEM_BYTES = 64 * 1024 * 1024

DFT_N2 = 128
DFT_N2H = DFT_N2 // V7X_SUBLANES

BF16 = jnp.bfloat16
F32 = jnp.float32


def _cparams(sem, vmem_mb):
    return pltpu.CompilerParams(dimension_semantics=sem, vmem_limit_bytes=vmem_mb * 1024 * 1024)


def _layer_norm_rows(x, g, b):
    mu = jnp.mean(x, axis=-1, keepdims=True)
    xc = x - mu
    var = jnp.mean(xc * xc, axis=-1, keepdims=True)
    return xc * lax.rsqrt(var + LN_EPS) * g + b


def _inproj_kernel(x_ref, g_ref, b_ref, w_ref, ra_ref, rm_ref, rp_ref, q_ref, k_ref, v_ref, u_ref):
    h = _layer_norm_rows(x_ref[...], g_ref[...], b_ref[...]).astype(BF16)
    ra, rm, rp = ra_ref[...], rm_ref[...], rp_ref[...]

    def rot(t):
        return t * ra + pltpu.roll(t, V7X_LANES - ROT_DIM // 2, axis=1) * rm + pltpu.roll(t, ROT_DIM // 2, axis=1) * rp

    aw = ATTN_WIDTH
    qp = jnp.dot(h, w_ref[:, 0:aw], preferred_element_type=F32)
    kp = jnp.dot(h, w_ref[:, aw:2 * aw], preferred_element_type=F32)
    scale = DA_HEAD_DIM ** -0.5 * math.log2(math.e)
    for c in range(aw // V7X_LANES):
        sl = slice(c * V7X_LANES, (c + 1) * V7X_LANES)
        q_ref[:, sl] = (rot(qp[:, sl]) * scale).astype(BF16)
        k_ref[:, sl] = rot(kp[:, sl]).astype(BF16)
    vp = jnp.dot(h, w_ref[:, 2 * aw:3 * aw], preferred_element_type=F32).astype(BF16)
    ones = jnp.ones((vp.shape[0], DA_V_DIM), BF16)
    for hd in range(DA_HEADS):
        v_ref[:, 2 * hd * DA_V_DIM:(2 * hd + 1) * DA_V_DIM] = vp[:, hd * DA_V_DIM:(hd + 1) * DA_V_DIM]
        v_ref[:, (2 * hd + 1) * DA_V_DIM:(2 * hd + 2) * DA_V_DIM] = ones
    u_ref[...] = jnp.dot(h, w_ref[:, 3 * aw:], preferred_element_type=F32)


def _inproj(x2, g, b, w_bf, ra, rm, rp, tm=512):
    T, D = x2.shape
    ncol = w_bf.shape[1]
    aw = ATTN_WIDTH
    uw = ncol - 3 * aw
    row = lambda i: (i, 0)
    fixed = lambda i: (0, 0)
    return pl.pallas_call(
        _inproj_kernel,
        grid=(T // tm,),
        in_specs=[
            pl.BlockSpec((tm, D), row),
            pl.BlockSpec((1, D), fixed),
            pl.BlockSpec((1, D), fixed),
            pl.BlockSpec((D, ncol), fixed),
            pl.BlockSpec((tm, V7X_LANES), row),
            pl.BlockSpec((tm, V7X_LANES), row),
            pl.BlockSpec((tm, V7X_LANES), row),
        ],
        out_specs=[
            pl.BlockSpec((tm, aw), row),
            pl.BlockSpec((tm, aw), row),
            pl.BlockSpec((tm, 2 * aw), row),
            pl.BlockSpec((tm, uw), row),
        ],
        out_shape=[
            jax.ShapeDtypeStruct((T, aw), BF16),
            jax.ShapeDtypeStruct((T, aw), BF16),
            jax.ShapeDtypeStruct((T, 2 * aw), BF16),
            jax.ShapeDtypeStruct((T, uw), F32),
        ],
        compiler_params=_cparams(("parallel",), 48),
        name="inproj",
    )(x2, g, b, w_bf, ra, rm, rp)


def _rotary_tables(positions):
    half = ROT_DIM // 2
    inv_freq = ROPE_THETA ** (-jnp.arange(0, ROT_DIM, 2, dtype=F32) / ROT_DIM)
    ang = positions.astype(F32).reshape(-1)[:, None] * inv_freq
    cos, sin = jnp.cos(ang), jnp.sin(ang)
    T = ang.shape[0]
    ones = jnp.ones((T, DA_HEAD_DIM - ROT_DIM), F32)
    zeros_h = jnp.zeros((T, half), F32)
    zeros_r = jnp.zeros((T, DA_HEAD_DIM - ROT_DIM), F32)
    a64 = jnp.concatenate([cos, cos, ones], axis=1)
    m64 = jnp.concatenate([-sin, zeros_h, zeros_r], axis=1)
    p64 = jnp.concatenate([zeros_h, sin, zeros_r], axis=1)
    rep = V7X_LANES // DA_HEAD_DIM
    return jnp.tile(a64, (1, rep)), jnp.tile(m64, (1, rep)), jnp.tile(p64, (1, rep))


def _attn_kernel(lam_ref, q_ref, k_ref, v_ref, g_ref, o_ref, s_ref, m_ref, acc_ref, *, kb, unroll):
    qb = q_ref.shape[0]
    L = k_ref.shape[0]
    nchunk = L // kb
    nl = kb // V7X_LANES
    q = q_ref[...]
    lane = lax.broadcasted_iota(jnp.int32, q.shape, 1)
    zero = jnp.zeros_like(q)
    qm = [jnp.where(lane < DA_HEAD_DIM, q, zero), jnp.where(lane >= DA_HEAD_DIM, q, zero)]
    m_ref[...] = jnp.full(m_ref.shape, -jnp.inf, F32)
    acc_ref[...] = jnp.zeros(acc_ref.shape, F32)

    def score_body(j, carry):
        kc = k_ref[pl.ds(pl.multiple_of(j * kb, kb), kb), :]
        for c in range(2):
            s = lax.dot_general(qm[c], kc, (((1,), (1,)), ((), ())), preferred_element_type=F32)
            s_ref[c, j] = s
            m = s[:, 0:V7X_LANES]
            for t in range(1, nl):
                m = jnp.maximum(m, s[:, t * V7X_LANES:(t + 1) * V7X_LANES])
            m_ref[c] = jnp.maximum(m_ref[c], m)
        return carry

    lax.fori_loop(0, nchunk, score_body, 0, unroll=unroll)
    m_row = [jnp.max(m_ref[c], axis=1, keepdims=True) for c in range(2)]

    def pv_body(j, carry):
        vc = v_ref[pl.ds(pl.multiple_of(j * kb, kb), kb), :]
        for c in range(2):
            p = jnp.exp2(s_ref[c, j] - m_row[c])
            acc_ref[c] += jnp.dot(p.astype(BF16), vc, preferred_element_type=F32)
        return carry

    lax.fori_loop(0, nchunk, pv_body, 0, unroll=unroll)
    outs = [acc_ref[c, :, 0:DA_V_DIM] / acc_ref[c, :, DA_V_DIM:2 * DA_V_DIM] for c in range(2)]
    o = outs[0] - lam_ref[0] * outs[1]
    ms = jnp.mean(o * o, axis=1, keepdims=True)
    o_ref[...] = (o * lax.rsqrt(ms + LN_EPS) * g_ref[...] * (1.0 - LAM_INIT)).astype(o_ref.dtype)


def _attention(q, k, v, lam, subln_g, B, L, qb=512, kb=1024, unroll=8):
    T = q.shape[0]
    nq = L // qb
    return pl.pallas_call(
        functools.partial(_attn_kernel, kb=kb, unroll=unroll),
        grid=(B, DA_HEADS, nq),
        in_specs=[
            pl.BlockSpec(memory_space=pltpu.SMEM),
            pl.BlockSpec((qb, DA_V_DIM), lambda b, h, i: (b * nq + i, h)),
            pl.BlockSpec((L, DA_V_DIM), lambda b, h, i: (b, h)),
            pl.BlockSpec((L, 2 * DA_V_DIM), lambda b, h, i: (b, h)),
            pl.BlockSpec((1, DA_V_DIM), lambda b, h, i: (0, 0)),
        ],
        out_specs=pl.BlockSpec((qb, DA_V_DIM), lambda b, h, i: (b * nq + i, h)),
        out_shape=jax.ShapeDtypeStruct((T, ATTN_WIDTH), BF16),
        scratch_shapes=[
            pltpu.VMEM((2, L // kb, qb, kb), F32),
            pltpu.VMEM((2, qb, V7X_LANES), F32),
            pltpu.VMEM((2, qb, 2 * DA_V_DIM), F32),
        ],
        compiler_params=_cparams(("parallel", "parallel", "parallel"), 56),
        name="attn",
    )(lam, q, k, v, subln_g)


def _hpre_kernel(u_ref, up_ref, un_ref, w_ref, b_ref, x0_ref, vx_ref):
    i = pl.program_id(1)
    n = pl.num_programs(1)
    tb = u_ref.shape[0]
    C = x0_ref.shape[-1]
    row = lax.broadcasted_iota(jnp.int32, (tb, V7X_LANES), 0)
    has_prev = (i > 0).astype(F32)
    has_next = (i < n - 1).astype(F32)

    def conv(c0):
        sl = slice(c0, c0 + V7X_LANES)
        u = u_ref[:, sl]
        prev_row = up_ref[V7X_SUBLANES - 1:V7X_SUBLANES, sl] * has_prev
        next_row = un_ref[0:1, sl] * has_next
        u_prev = jnp.where(row == 0, prev_row, pltpu.roll(u, 1, axis=0))
        u_next = jnp.where(row == tb - 1, next_row, pltpu.roll(u, tb - 1, axis=0))
        return u_prev * w_ref[0:1, sl] + u * w_ref[1:2, sl] + u_next * w_ref[2:3, sl] + b_ref[:, sl]

    tile = x0_ref.shape[:-1] + (V7X_LANES,)
    for c in range(C // V7X_LANES):
        c0 = c * V7X_LANES
        x0_ref[:, :, :, c0:c0 + V7X_LANES] = conv(c0).reshape(tile)
        vx_ref[:, :, :, c0:c0 + V7X_LANES] = (conv(2 * C + c0) * conv(C + c0)).reshape(tile)


def _hpre(u, conv_w, conv_b, B, L, tb=1024):
    T, C3 = u.shape
    C = C3 // 3
    nt = L // tb
    sub = V7X_SUBLANES
    na = tb // DFT_N2
    split = pl.BlockSpec((None, na, DFT_N2H, sub, C), lambda b, i: (b, i, 0, 0, 0))
    split_shape = jax.ShapeDtypeStruct((B, L // DFT_N2, DFT_N2H, sub, C), F32)
    cur = lambda b, i: (b * nt + i, 0)
    prev = lambda b, i: (jnp.maximum((b * L + i * tb) // sub - 1, 0), 0)
    nxt = lambda b, i: (jnp.minimum((b * L + (i + 1) * tb) // sub, T // sub - 1), 0)
    fixed = lambda b, i: (0, 0)
    return pl.pallas_call(
        _hpre_kernel,
        grid=(B, nt),
        in_specs=[
            pl.BlockSpec((tb, C3), cur),
            pl.BlockSpec((sub, C3), prev),
            pl.BlockSpec((sub, C3), nxt),
            pl.BlockSpec((SHORT_CONV, C3), fixed),
            pl.BlockSpec((1, C3), fixed),
        ],
        out_specs=[split, split],
        out_shape=[split_shape, split_shape],
        compiler_params=_cparams(("parallel", "parallel"), 32),
        name="hpre",
    )(u, u, u, conv_w, conv_b)


def _filt_kernel(z_ref, w1_ref, b1_ref, w2_ref, b2_ref, w3_ref, b3_ref, fr_ref, wo_ref, dl_ref,
                 hfb_ref, asum_ref, *, L):
    i = pl.program_id(0)
    tl = z_ref.shape[0]
    C = dl_ref.shape[1]
    hp = lax.Precision.HIGHEST
    h = jnp.sin(fr_ref[0:1, :] * (jnp.dot(z_ref[...], w1_ref[...], precision=hp, preferred_element_type=F32) + b1_ref[...]))
    h = jnp.sin(fr_ref[1:2, :] * (jnp.dot(h, w2_ref[...], precision=hp, preferred_element_type=F32) + b2_ref[...]))
    h = jnp.sin(fr_ref[2:3, :] * (jnp.dot(h, w3_ref[...], precision=hp, preferred_element_type=F32) + b3_ref[...]))
    o = jnp.dot(h, wo_ref[...], precision=hp, preferred_element_type=F32)
    grow = lax.broadcasted_iota(jnp.int32, (tl, C), 0) + i * tl
    t = grow.astype(F32) * (1.0 / (L - 1))
    decay = jnp.exp(-t * dl_ref[...])
    hf = o[:, :C] * decay
    hb = jnp.where(grow == 0, 0.0, o[:, C:] * decay)
    tile = hfb_ref.shape[:-1] + (C,)
    hfb_ref[:, :, :, :C] = hf.reshape(tile)
    hfb_ref[:, :, :, C:] = hb.reshape(tile)

    @pl.when(i == 0)
    def _():
        asum_ref[...] = jnp.zeros_like(asum_ref)

    asum_ref[...] += jnp.sum(jnp.abs(hf) + jnp.abs(hb), axis=0, keepdims=True)


def _filter_taps(L, w1, b1, w2, b2, w3, b3, freq, wout, tl=512):
    C = wout.shape[1] // 2
    order = w1.shape[1]
    emb = w1.shape[0]
    t = jnp.linspace(0.0, 1.0, L, dtype=F32)[:, None]
    w = 2.0 * math.pi * jnp.arange(L, dtype=F32)[:, None] / L
    f = jnp.linspace(1e-4, FILTER_BANDS - 1, FILTER_BANDS, dtype=F32)[None, :]
    z = jnp.concatenate([t, jnp.cos(f * w), -jnp.sin(f * w)], axis=-1)
    zp = jnp.pad(z, ((0, 0), (0, V7X_LANES - emb)))
    w1p = jnp.pad(w1, ((0, V7X_LANES - emb), (0, 0)))
    deltas = jnp.abs(jnp.linspace(math.log(DECAY_TARGET) / SLOW_DECAY, math.log(DECAY_TARGET) / FAST_DECAY, C, dtype=F32))[None]
    fixed = lambda i: (0, 0)
    return pl.pallas_call(
        functools.partial(_filt_kernel, L=L),
        grid=(L // tl,),
        in_specs=[
            pl.BlockSpec((tl, V7X_LANES), lambda i: (i, 0)),
            pl.BlockSpec((V7X_LANES, order), fixed), pl.BlockSpec((1, order), fixed),
            pl.BlockSpec((order, order), fixed), pl.BlockSpec((1, order), fixed),
            pl.BlockSpec((order, order), fixed), pl.BlockSpec((1, order), fixed),
            pl.BlockSpec((3, order), fixed),
            pl.BlockSpec((order, 2 * C), fixed),
            pl.BlockSpec((1, C), fixed),
        ],
        out_specs=[pl.BlockSpec((None, tl // DFT_N2, DFT_N2H, V7X_SUBLANES, 2 * C), lambda i: (0, i, 0, 0, 0)),
                   pl.BlockSpec((1, C), fixed)],
        out_shape=[jax.ShapeDtypeStruct((1, L // DFT_N2, DFT_N2H, V7X_SUBLANES, 2 * C), F32),
                   jax.ShapeDtypeStruct((1, C), F32)],
        compiler_params=_cparams(("arbitrary",), 32),
        name="filt",
    )(zp, w1p, b1[None], w2, b2[None], w3, b3[None], freq, wout, deltas)


def _dft_constants(L):
    n1 = 2 * L // DFT_N2
    n1h = n1 // 2
    N = 2 * L
    sub = V7X_SUBLANES

    def cs(num, den):
        ang = (num % den).astype(np.float64) * (2.0 * np.pi / den)
        return np.cos(ang), np.sin(ang)

    k1 = np.arange(n1, dtype=np.int64)
    a = np.arange(n1h, dtype=np.int64)
    c, s = cs(k1[:, None] * a[None, :], n1)
    eye = np.eye(sub)

    def expand(m):
        r, kk = m.shape
        return (m[:, None, :, None] * eye[None, :, None, :]).reshape(r * sub, kk * sub)

    def const(m):
        return jnp.asarray(m.astype(np.float32).astype(BF16))

    m1c = np.stack([np.concatenate([c, s], axis=1), np.concatenate([-s, c], axis=1)], axis=1).reshape(2 * n1, 2 * n1h)
    m1r = np.stack([c, -s], axis=1).reshape(2 * n1, n1h)
    ct, st = c.T, s.T
    m3 = np.stack([np.stack([ct, -st], axis=2).reshape(n1h, 2 * n1),
                   np.stack([st, ct], axis=2).reshape(n1h, 2 * n1)], axis=0).reshape(2 * n1h, 2 * n1)
    n2 = np.arange(DFT_N2, dtype=np.int64)
    gc, gs = cs(n2[:, None] * n2[None, :], DFT_N2)
    g_fwd = np.concatenate([np.concatenate([gc, gs], axis=1), np.concatenate([-gs, gc], axis=1)], axis=0)
    g_inv = np.concatenate([np.concatenate([gc, -gs], axis=1), np.concatenate([gs, gc], axis=1)], axis=0)
    ph = (jnp.arange(n1, dtype=jnp.int32)[:, None] * jnp.arange(DFT_N2, dtype=jnp.int32)[None, :]) % N
    ang = ph.astype(F32) * (2.0 * math.pi / N)
    lanes = (n1, DFT_N2H, sub, V7X_LANES)
    twc = jnp.broadcast_to(jnp.cos(ang).reshape(n1, DFT_N2H, sub, 1), lanes)
    tws = jnp.broadcast_to(jnp.sin(ang).reshape(n1, DFT_N2H, sub, 1), lanes)
    return dict(n1=n1, n1h=n1h, m1c=const(expand(m1c)), m1r=const(expand(m1r)), m3=const(expand(m3)),
                g_fwd=const(g_fwd), g_inv=const(g_inv), twc=twc, tws=tws)


def _dft1_kernel(x_ref, m_ref, o_ref):
    rows = m_ref.shape[1]
    xs = x_ref[...].reshape(rows, x_ref.shape[-1]).astype(BF16)
    o_ref[...] = jnp.dot(m_ref[...], xs, preferred_element_type=F32).reshape(o_ref.shape)


def _dft1(x5, m, n1, cb):
    P, n1h, _, sub, Cx = x5.shape
    return pl.pallas_call(
        _dft1_kernel,
        grid=(DFT_N2H, Cx // cb),
        in_specs=[
            pl.BlockSpec((P, n1h, None, sub, cb), lambda h, c: (0, 0, h, 0, c)),
            pl.BlockSpec(m.shape, lambda h, c: (0, 0)),
        ],
        out_specs=pl.BlockSpec((None, n1, 2, sub, cb), lambda h, c: (h, 0, 0, 0, c)),
        out_shape=jax.ShapeDtypeStruct((DFT_N2H, n1, 2, sub, Cx), F32),
        compiler_params=_cparams(("parallel", "parallel"), 48),
        name="dft1",
    )(x5, m)


def _dft2_kernel(a_ref, f_ref, twc_ref, tws_ref, gf_ref, gi_ref, o_ref):
    C = a_ref.shape[-1]
    half = DFT_N2

    def lanes(fn):
        return jnp.concatenate([fn(slice(c0, c0 + V7X_LANES)) for c0 in range(0, C, V7X_LANES)], axis=-1)

    for kk in range(a_ref.shape[1]):
        twc, tws = twc_ref[kk], tws_ref[kk]

        def spectrum(ref, col0):
            def re_part(sl):
                return ref[:, kk, 0, :, col0 + sl.start:col0 + sl.stop] * twc + ref[:, kk, 1, :, col0 + sl.start:col0 + sl.stop] * tws

            def im_part(sl):
                return ref[:, kk, 1, :, col0 + sl.start:col0 + sl.stop] * twc - ref[:, kk, 0, :, col0 + sl.start:col0 + sl.stop] * tws

            t = jnp.concatenate([lanes(re_part).reshape(half, C), lanes(im_part).reshape(half, C)], axis=0).astype(BF16)
            s = jnp.dot(gf_ref[...], t, preferred_element_type=F32)
            return s[:half], s[half:]

        xr, xi = spectrum(a_ref, 0)
        fr, fi = spectrum(f_ref, 0)
        br, bi = spectrum(f_ref, C)
        hr, hi = fr + br, fi - bi
        y = jnp.concatenate([xr * hr - xi * hi, xr * hi + xi * hr], axis=0).astype(BF16)
        b = jnp.dot(gi_ref[...], y, preferred_element_type=F32)
        br2 = b[:half].reshape(DFT_N2H, V7X_SUBLANES, C)
        bi2 = b[half:].reshape(DFT_N2H, V7X_SUBLANES, C)
        for c0 in range(0, C, V7X_LANES):
            sl = slice(c0, c0 + V7X_LANES)
            o_ref[:, kk, 0, :, sl] = br2[:, :, sl] * twc - bi2[:, :, sl] * tws
            o_ref[:, kk, 1, :, sl] = bi2[:, :, sl] * twc + br2[:, :, sl] * tws


def _dft2(a5, f5, k, kb=8):
    n2h, n1, _, sub, C = a5.shape
    kb = min(kb, n1)
    return pl.pallas_call(
        _dft2_kernel,
        grid=(n1 // kb,),
        in_specs=[
            pl.BlockSpec((n2h, kb, 2, sub, C), lambda i: (0, i, 0, 0, 0)),
            pl.BlockSpec((n2h, kb, 2, sub, 2 * C), lambda i: (0, i, 0, 0, 0)),
            pl.BlockSpec((kb, n2h, sub, V7X_LANES), lambda i: (i, 0, 0, 0)),
            pl.BlockSpec((kb, n2h, sub, V7X_LANES), lambda i: (i, 0, 0, 0)),
            pl.BlockSpec((2 * DFT_N2, 2 * DFT_N2), lambda i: (0, 0)),
            pl.BlockSpec((2 * DFT_N2, 2 * DFT_N2), lambda i: (0, 0)),
        ],
        out_specs=pl.BlockSpec((n2h, kb, 2, sub, C), lambda i: (0, i, 0, 0, 0)),
        out_shape=jax.ShapeDtypeStruct(a5.shape, F32),
        compiler_params=_cparams(("parallel",), 48),
        name="dft2",
    )(a5, f5, k["twc"], k["tws"], k["g_fwd"], k["g_inv"])


def _dft3_kernel(b_ref, m_ref, x0_ref, vx_ref, sc_ref, d_ref, o_ref):
    C = b_ref.shape[-1]
    bs = b_ref[...].reshape(m_ref.shape[1], C).astype(BF16)
    y = jnp.dot(m_ref[...], bs, preferred_element_type=F32).reshape(o_ref.shape)
    o_ref[...] = x0_ref[...] * (y * sc_ref[...] + vx_ref[...] * d_ref[...])


def _dft3(b5, m3, x05, vx5, scale, d):
    n2h, n1, _, sub, C = b5.shape
    Bt, n1h = x05.shape[0], x05.shape[1]
    tok = pl.BlockSpec((Bt, n1h, None, sub, C), lambda h: (0, 0, h, 0, 0))
    vec = pl.BlockSpec((1, C), lambda h: (0, 0))
    return pl.pallas_call(
        _dft3_kernel,
        grid=(n2h,),
        in_specs=[
            pl.BlockSpec((None, n1, 2, sub, C), lambda h: (h, 0, 0, 0, 0)),
            pl.BlockSpec(m3.shape, lambda h: (0, 0)),
            tok, tok, vec, vec,
        ],
        out_specs=tok,
        out_shape=jax.ShapeDtypeStruct(x05.shape, F32),
        compiler_params=_cparams(("parallel",), 48),
        name="dft3",
    )(b5, m3, x05, vx5, scale, d)


def _hyena(u, conv_w, conv_b, w1, b1, w2, b2, w3, b3, freq, wout, d_skip, B, L):
    T = u.shape[0]
    C = HYENA_WIDTH
    k = _dft_constants(L)
    n1, n1h = k["n1"], k["n1h"]
    x0, vx = _hpre(u, conv_w, conv_b[None], B, L)
    hfb, asum = _filter_taps(L, w1, b1, w2, b2, w3, b3, freq, wout)
    a5 = _dft1(vx, k["m1c"], n1, cb=C)
    f5 = _dft1(hfb, k["m1r"], n1, cb=2 * C)
    b5 = _dft2(a5, f5, k)
    scale = 1.0 / (asum * (2 * L))
    return _dft3(b5, k["m3"], x0, vx, scale, d_skip[None])


PACKED = jnp.uint32


def _pack_pairs(x):
    half = x.shape[1] // 2
    bits = pltpu.bitcast(x.astype(BF16).astype(F32), PACKED)
    return (bits[:, :half] >> 16) | bits[:, half:]


def _unpack_pairs(w):
    lo = pltpu.bitcast(w << 16, F32)
    hi = pltpu.bitcast(w & jnp.uint32(0xFFFF0000), F32)
    return lo, hi


def _oproj_kernel(x_ref, g0_ref, b0_ref, at_ref, hy_ref, wo_ref, g1_ref, b1_ref, wrh_ref, wrl_ref,
                  h1_ref, h1p_ref, sc_ref):
    aw = at_ref.shape[1]
    h0 = _layer_norm_rows(x_ref[...], g0_ref[...], b0_ref[...])
    mixed = jnp.dot(at_ref[...], wo_ref[0:aw, :], preferred_element_type=F32)
    hy = hy_ref[...].reshape(x_ref.shape[0], hy_ref.shape[-1])
    mixed = mixed + jnp.dot(hy.astype(BF16), wo_ref[aw:, :], preferred_element_type=F32)
    h1 = _layer_norm_rows(ALPHA * h0 + mixed, g1_ref[...], b1_ref[...])
    h1_ref[...] = h1
    h1p_ref[...] = _pack_pairs(h1)
    hh = h1.astype(BF16)
    hl = (h1 - hh.astype(F32)).astype(BF16)
    dn = (((1,), (1,)), ((), ()))
    logits = lax.dot_general(wrh_ref[...], hh, dn, preferred_element_type=F32)
    logits = logits + (lax.dot_general(wrh_ref[...], hl, dn, preferred_element_type=F32)
                       + lax.dot_general(wrl_ref[...], hh, dn, preferred_element_type=F32))
    sc_ref[...] = 1.0 / (1.0 + jnp.exp(-logits))


def _oproj(x2, g0, b0, attn, hy, wo_bf, g1, b1, wr_t, tm=512):
    T, D = x2.shape
    E = wr_t.shape[0]
    wrh = wr_t.astype(BF16)
    wrl = (wr_t - wrh.astype(F32)).astype(BF16)
    row = lambda i: (i, 0)
    fixed = lambda i: (0, 0)
    hy4 = hy.reshape((-1,) + hy.shape[2:])
    return pl.pallas_call(
        _oproj_kernel,
        grid=(T // tm,),
        in_specs=[
            pl.BlockSpec((tm, D), row), pl.BlockSpec((1, D), fixed), pl.BlockSpec((1, D), fixed),
            pl.BlockSpec((tm, attn.shape[1]), row),
            pl.BlockSpec((tm // DFT_N2,) + hy4.shape[1:], lambda i: (i, 0, 0, 0)),
            pl.BlockSpec((D, D), fixed), pl.BlockSpec((1, D), fixed), pl.BlockSpec((1, D), fixed),
            pl.BlockSpec((E, D), fixed), pl.BlockSpec((E, D), fixed),
        ],
        out_specs=[pl.BlockSpec((tm, D), row), pl.BlockSpec((tm, D // 2), row), pl.BlockSpec((E, tm), lambda i: (0, i))],
        out_shape=[
            jax.ShapeDtypeStruct((T, D), F32),
            jax.ShapeDtypeStruct((T, D // 2), PACKED),
            jax.ShapeDtypeStruct((E, T), F32),
        ],
        compiler_params=_cparams(("parallel",), 48),
        name="oproj",
    )(x2, g0, b0, attn, hy4, wo_bf, g1, b1, wrh, wrl)


def _route_kernel(sc_ref, bias_ref, tri_ref, idx_ref, gate_ref, rank_ref, cnt_ref, carry_ref):
    step = pl.program_id(0)

    @pl.when(step == 0)
    def _():
        carry_ref[...] = jnp.zeros_like(carry_ref)

    tm = V7X_LANES
    for part in range(sc_ref.shape[1] // tm):
        cols = slice(part * tm, (part + 1) * tm)
        _route_tokens(sc_ref[:, cols], bias_ref, tri_ref, idx_ref, gate_ref, rank_ref, carry_ref, cols)
    cnt_ref[...] = carry_ref[...].astype(jnp.int32)


def _route_tokens(scores, bias_ref, tri_ref, idx_ref, gate_ref, rank_ref, carry_ref, cols):
    E, tm = scores.shape
    neg = jnp.float32(-jnp.inf)
    biased = scores + bias_ref[...]
    erow = lax.broadcasted_iota(jnp.int32, (E, tm), 0)
    big = jnp.int32(E)

    def first_argmax(vals, rows):
        m = jnp.max(vals, axis=0, keepdims=True)
        pick = jnp.min(jnp.where(vals == m, rows, big), axis=0, keepdims=True)
        return m, pick

    gsc = []
    for g in range(N_GROUPS):
        blk = biased[g * GROUP_SIZE:(g + 1) * GROUP_SIZE, :]
        rows = erow[g * GROUP_SIZE:(g + 1) * GROUP_SIZE, :]
        m1, p1 = first_argmax(blk, rows)
        m2 = jnp.max(jnp.where(rows == p1, neg, blk), axis=0, keepdims=True)
        gsc.append(m1 + m2)
    gsc = jnp.concatenate(gsc, axis=0)
    grow = lax.broadcasted_iota(jnp.int32, (N_GROUPS, tm), 0)
    gsel = jnp.zeros((N_GROUPS, tm), jnp.bool_)
    work = gsc
    for _ in range(TOPK_GROUPS):
        _, p = first_argmax(work, grow)
        hit = grow == p
        gsel = gsel | hit
        work = jnp.where(hit, neg, work)
    emask = jnp.concatenate(
        [jnp.broadcast_to(gsel[g:g + 1, :], (GROUP_SIZE, tm)) for g in range(N_GROUPS)], axis=0)
    work = jnp.where(emask, biased, neg)

    sel = jnp.zeros((E, tm), jnp.bool_)
    picks, gvals = [], []
    for _ in range(TOP_K):
        _, p = first_argmax(work, erow)
        hit = erow == p
        sel = sel | hit
        picks.append(p)
        gvals.append(jnp.sum(jnp.where(hit, scores, 0.0), axis=0, keepdims=True))
        work = jnp.where(hit, neg, work)
    gv = jnp.concatenate(gvals, axis=0)
    idx_ref[:, cols] = jnp.concatenate(picks, axis=0)
    gate_ref[:, cols] = gv / jnp.sum(gv, axis=0, keepdims=True) * ROUTED_SCALE

    chosen = sel.astype(F32)
    before = jnp.dot(chosen.astype(BF16), tri_ref[...], preferred_element_type=F32) + carry_ref[...]
    rank_ref[:, cols] = jnp.concatenate(
        [jnp.sum(jnp.where(erow == p, before, 0.0), axis=0, keepdims=True) for p in picks], axis=0).astype(jnp.int32)
    carry_ref[...] += jnp.sum(chosen, axis=1, keepdims=True)


def _route(scores_t, bias, tm=256):
    E, T = scores_t.shape
    tm = min(tm, T)
    lanes = V7X_LANES
    tri = (jnp.arange(lanes)[:, None] < jnp.arange(lanes)[None, :]).astype(BF16)
    tok = lambda i: (0, i)
    fixed = lambda i: (0, 0)
    return pl.pallas_call(
        _route_kernel,
        grid=(T // tm,),
        in_specs=[pl.BlockSpec((E, tm), tok), pl.BlockSpec((E, 1), fixed), pl.BlockSpec((lanes, lanes), fixed)],
        out_specs=[pl.BlockSpec((TOP_K, tm), tok), pl.BlockSpec((TOP_K, tm), tok), pl.BlockSpec((TOP_K, tm), tok),
                   pl.BlockSpec((E, 1), fixed)],
        out_shape=[
            jax.ShapeDtypeStruct((TOP_K, T), jnp.int32),
            jax.ShapeDtypeStruct((TOP_K, T), F32),
            jax.ShapeDtypeStruct((TOP_K, T), jnp.int32),
            jax.ShapeDtypeStruct((E, 1), jnp.int32),
        ],
        scratch_shapes=[pltpu.VMEM((E, 1), F32)],
        compiler_params=_cparams(("arbitrary",), 32),
        name="route",
    )(scores_t, bias, tri)


def _dest_kernel(idx_ref, rank_ref, ps_ref, dest_ref):
    E = ps_ref.shape[0]
    tm = idx_ref.shape[1]
    erow = lax.broadcasted_iota(jnp.int32, (E, tm), 0)
    ps = ps_ref[...].astype(F32)
    rows = [jnp.sum(jnp.where(erow == idx_ref[k:k + 1, :], ps, 0.0), axis=0, keepdims=True) for k in range(TOP_K)]
    dest_ref[...] = jnp.concatenate(rows, axis=0).astype(jnp.int32) + rank_ref[...]


def _dest_rows(idx, rank, pad_start, tm=1024):
    K, T = idx.shape
    E = pad_start.shape[0]
    tm = min(tm, T)
    tok = lambda i: (0, i)
    return pl.pallas_call(
        _dest_kernel,
        grid=(T // tm,),
        in_specs=[pl.BlockSpec((K, tm), tok), pl.BlockSpec((K, tm), tok), pl.BlockSpec((E, 1), lambda i: (0, 0))],
        out_specs=pl.BlockSpec((K, tm), tok),
        out_shape=jax.ShapeDtypeStruct((K, T), jnp.int32),
        compiler_params=_cparams(("arbitrary",), 32),
        name="dest",
    )(idx, rank, pad_start[:, None])


def _scatter_kernel(ps_ref, pe_ref, dest_ref, h_ref, xs_ref, zbuf, sem, *, rb):
    sub = V7X_SUBLANES
    tm = h_ref.shape[0]

    @pl.when(pl.program_id(0) == 0)
    def _():
        zbuf[...] = jnp.zeros_like(zbuf)

        def zcopy(e):
            return pltpu.make_async_copy(zbuf, xs_ref.at[pl.ds(pl.multiple_of(pe_ref[e] - rb, rb), rb)], sem)

        def zstart(e, c):
            @pl.when(pe_ref[e] > ps_ref[e])
            def _():
                zcopy(e).start()
            return c

        def zwait(e, c):
            @pl.when(pe_ref[e] > ps_ref[e])
            def _():
                zcopy(e).wait()
            return c

        lax.fori_loop(0, ps_ref.shape[0], zstart, 0)
        lax.fori_loop(0, ps_ref.shape[0], zwait, 0)

    def issue(r8, c):
        for j in range(sub):
            for k in range(TOP_K):
                dst = dest_ref[0, r8 * (sub * TOP_K) + (j * TOP_K + k)]
                pltpu.make_async_copy(h_ref.at[pl.ds(r8 * sub + j, 1)], xs_ref.at[pl.ds(dst, 1)], sem).start(priority=k % 2)
        return c

    lax.fori_loop(0, tm // sub, issue, 0)
    for k in range(TOP_K):
        pltpu.make_async_copy(h_ref, xs_ref.at[pl.ds(0, tm)], sem).wait()


def _dest_tiles(dest, tm):
    K, T = dest.shape
    return dest.T.reshape(T // tm, 1, tm * K)


def _scatter_rows(pad_start, pad_end, dest_t, h1p, n_rows, rb, tm=256):
    T, W = h1p.shape
    return pl.pallas_call(
        functools.partial(_scatter_kernel, rb=rb),
        grid_spec=pltpu.PrefetchScalarGridSpec(
            num_scalar_prefetch=2,
            grid=(T // tm,),
            in_specs=[
                pl.BlockSpec((None, 1, tm * TOP_K), lambda i, ps, pe: (i, 0, 0), memory_space=pltpu.SMEM),
                pl.BlockSpec((tm, W), lambda i, ps, pe: (i, 0)),
            ],
            out_specs=pl.BlockSpec(memory_space=pl.ANY),
            scratch_shapes=[pltpu.VMEM((rb, W), h1p.dtype), pltpu.SemaphoreType.DMA],
        ),
        out_shape=jax.ShapeDtypeStruct((n_rows, W), h1p.dtype),
        compiler_params=_cparams(("arbitrary",), 32),
        name="scatter",
    )(pad_start, pad_end, dest_t, h1p)


EXPERT_X_SLOTS = 4
EXPERT_Y_SLOTS = 3


def _experts_kernel(first_ref, nblk_ref, wsel_ref, tot_ref, xs_ref, wg_ref, wu_ref, wd_ref, ys_ref,
                    xbuf, ybuf, wgub, wdb, xsem, ysem, *, rb):
    del wsel_ref
    e = pl.program_id(0)
    F = wg_ref.shape[1]
    nx, ny = xbuf.shape[0], ybuf.shape[0]
    total = tot_ref[0]
    n = nblk_ref[e]
    g0 = first_ref[e]

    def x_copy(g):
        slot = g % nx
        return pltpu.make_async_copy(xs_ref.at[pl.ds(pl.multiple_of(g * rb, rb), rb)], xbuf.at[slot], xsem.at[slot])

    def y_copy(g):
        slot = g % ny
        return pltpu.make_async_copy(ybuf.at[slot], ys_ref.at[pl.ds(pl.multiple_of(g * rb, rb), rb)], ysem.at[slot])

    @pl.when(e == 0)
    def _():
        for g in range(nx):
            @pl.when(g < total)
            def _():
                x_copy(g).start()

    def process(g, count):
        for d in range(count):
            x_copy(g + d).wait()
        ys = []
        for d in range(count):
            lo, hi = _unpack_pairs(xbuf[(g + d) % nx])
            x = jnp.concatenate([lo, hi], axis=1).astype(BF16)
            gu = jnp.dot(x, wgub[...], preferred_element_type=F32)
            gate, up = gu[:, 0:F], gu[:, F:]
            hb = (gate / (1.0 + jnp.exp(-gate)) * up).astype(BF16)
            ys.append(_pack_pairs(jnp.dot(hb, wdb[...], preferred_element_type=F32)))
        for d in range(count):
            @pl.when(g + d >= ny)
            def _():
                y_copy(g + d - ny).wait()

            ybuf[(g + d) % ny] = ys[d]
            y_copy(g + d).start()
        for d in range(count):
            @pl.when(g + d + nx < total)
            def _():
                x_copy(g + d + nx).start()

    @pl.when(n > 0)
    def _():
        wgub[:, 0:F] = wg_ref[...].astype(BF16)
        wgub[:, F:] = wu_ref[...].astype(BF16)
        wdb[...] = wd_ref[...].astype(BF16)

        def pair(j, c):
            process(g0 + 2 * j, 2)
            return c

        lax.fori_loop(0, n // 2, pair, 0)

        @pl.when(n % 2 == 1)
        def _():
            process(g0 + n - 1, 1)

    @pl.when(e == pl.num_programs(0) - 1)
    def _():
        for back in range(ny, 0, -1):
            @pl.when(total >= back)
            def _():
                y_copy(total - back).wait()


def _experts(first_blk, nblk, wsel, total_blk, xs, w_gate, w_up, w_down, rb):
    P, W = xs.shape
    E, D, F = w_gate.shape
    wspec = lambda shape: pl.BlockSpec((None,) + shape, lambda e, fb, nbk, ws, tt: (ws[e], 0, 0))
    return pl.pallas_call(
        functools.partial(_experts_kernel, rb=rb),
        grid_spec=pltpu.PrefetchScalarGridSpec(
            num_scalar_prefetch=4,
            grid=(E,),
            in_specs=[pl.BlockSpec(memory_space=pl.ANY), wspec((D, F)), wspec((D, F)), wspec((F, D))],
            out_specs=pl.BlockSpec(memory_space=pl.ANY),
            scratch_shapes=[
                pltpu.VMEM((EXPERT_X_SLOTS, rb, W), PACKED), pltpu.VMEM((EXPERT_Y_SLOTS, rb, W), PACKED),
                pltpu.VMEM((D, 2 * F), BF16), pltpu.VMEM((F, D), BF16),
                pltpu.SemaphoreType.DMA((EXPERT_X_SLOTS,)), pltpu.SemaphoreType.DMA((EXPERT_Y_SLOTS,)),
            ],
        ),
        out_shape=jax.ShapeDtypeStruct((P, W), PACKED),
        compiler_params=_cparams(("arbitrary",), 48),
        name="experts",
    )(first_blk, nblk, wsel, total_blk, xs, w_gate, w_up, w_down)


def _combine_kernel(dest_ref, destn_ref, h1_ref, gate_ref, ys_ref, ys8_ref, sg_ref, su_ref, sd_ref,
                    g2_ref, b2_ref, o_ref, buf, sem):
    i = pl.program_id(0)
    n = pl.num_programs(0)
    tm = h1_ref.shape[0]
    slot = i % 2

    sub = V7X_SUBLANES

    def issue_rows(dref, s, r0, nrows):
        for r in range(r0, r0 + nrows):
            for k in range(TOP_K):
                src = ys_ref.at[pl.ds(dref[0, r * TOP_K + k], 1)]
                pltpu.make_async_copy(src, buf.at[s, r, pl.ds(k, 1)], sem.at[s]).start(priority=k % 2)

    def wait_tile(s):
        pltpu.make_async_copy(ys8_ref.at[pl.ds(0, tm)], buf.at[s], sem.at[s]).wait()

    @pl.when(i == 0)
    def _():
        def body(r8, c):
            for j in range(sub):
                for k in range(TOP_K):
                    src = ys_ref.at[pl.ds(dest_ref[0, r8 * (sub * TOP_K) + (j * TOP_K + k)], 1)]
                    pltpu.make_async_copy(src, buf.at[0, r8 * sub + j, pl.ds(k, 1)], sem.at[0]).start(priority=k % 2)
            return c
        lax.fori_loop(0, tm // sub, body, 0)

    wait_tile(slot)
    parts = 4
    rows_per = tm // parts
    nk = rows_per * TOP_K
    pick = lax.broadcasted_iota(jnp.int32, (TOP_K, nk), 1) % TOP_K == lax.broadcasted_iota(jnp.int32, (TOP_K, nk), 0)
    spread = pick.astype(BF16)
    own = lax.broadcasted_iota(jnp.int32, (rows_per, nk), 1) // TOP_K == lax.broadcasted_iota(jnp.int32, (rows_per, nk), 0)
    for part in range(parts):
        r0 = part * rows_per
        issue_rows(destn_ref, 1 - slot, r0, rows_per)
        rows = slice(r0, r0 + rows_per)
        h1 = h1_ref[rows, :]
        hb = h1.astype(BF16)
        g = jnp.dot(hb, sg_ref[...], preferred_element_type=F32)
        u = jnp.dot(hb, su_ref[...], preferred_element_type=F32)
        ffn = jnp.dot((g / (1.0 + jnp.exp(-g)) * u).astype(BF16), sd_ref[...], preferred_element_type=F32)
        lo, hi = _unpack_pairs(buf[slot, rows].reshape(nk, buf.shape[3]))
        y = jnp.concatenate([lo, hi], axis=1).astype(BF16)
        gates = gate_ref[rows, :]
        gh = gates.astype(BF16)
        gl = (gates - gh.astype(F32)).astype(BF16)
        routed = jnp.zeros((rows_per, y.shape[1]), F32)
        for piece in (gh, gl):
            gm = jnp.where(own, jnp.dot(piece, spread, preferred_element_type=F32), 0.0).astype(BF16)
            routed = routed + jnp.dot(gm, y, preferred_element_type=F32)
        o_ref[rows, :] = _layer_norm_rows(ALPHA * h1 + ffn + routed, g2_ref[...], b2_ref[...])

    @pl.when(i == n - 1)
    def _():
        wait_tile(1 - slot)


def _combine(dest, h1, gates_tk, ys, sg, su, sd, g2, b2, tm=512):
    T, D = h1.shape
    W = ys.shape[1]
    F = sg.shape[1]
    nt = T // tm
    row = lambda i: (i, 0)
    fixed = lambda i: (0, 0)
    smem = functools.partial(pl.BlockSpec, (None, 1, tm * TOP_K), memory_space=pltpu.SMEM)
    dest_t = _dest_tiles(dest, tm)
    ys8 = ys.reshape(ys.shape[0] // TOP_K, TOP_K, W)
    return pl.pallas_call(
        _combine_kernel,
        grid=(nt,),
        in_specs=[
            smem(index_map=lambda i: (i, 0, 0)),
            smem(index_map=lambda i: (jnp.minimum(i + 1, nt - 1), 0, 0)),
            pl.BlockSpec((tm, D), row),
            pl.BlockSpec((tm, TOP_K), row),
            pl.BlockSpec(memory_space=pl.ANY),
            pl.BlockSpec(memory_space=pl.ANY),
            pl.BlockSpec((D, F), fixed), pl.BlockSpec((D, F), fixed), pl.BlockSpec((F, D), fixed),
            pl.BlockSpec((1, D), fixed), pl.BlockSpec((1, D), fixed),
        ],
        out_specs=pl.BlockSpec((tm, D), row),
        out_shape=jax.ShapeDtypeStruct((T, D), F32),
        scratch_shapes=[pltpu.VMEM((2, tm, TOP_K, W), PACKED), pltpu.SemaphoreType.DMA((2,))],
        compiler_params=_cparams(("arbitrary",), 48),
        name="combine",
    )(dest_t, dest_t, h1, gates_tk, ys, ys8, sg, su, sd, g2, b2)


MOE_ROWS = 256


def _moe(h1, h1p, scores_t, router_bias, w_gate, w_up, w_down, ws_gate, ws_up, ws_down, g2, b2):
    T = h1.shape[0]
    E = N_EXPERTS
    rb = MOE_ROWS
    idx, gates, rank, counts = _route(scores_t, router_bias[:, None])
    counts = counts[:, 0]
    padded = (counts + rb - 1) // rb * rb
    pad_end = jnp.cumsum(padded)
    pad_start = pad_end - padded
    nb = (T * TOP_K) // rb + E
    nblk = (padded // rb).astype(jnp.int32)
    first_blk = (pad_start // rb).astype(jnp.int32)
    total_blk = (pad_end[-1:] // rb).astype(jnp.int32)
    wsel = lax.cummax(jnp.where(nblk > 0, jnp.arange(E, dtype=jnp.int32), 0))
    dest = _dest_rows(idx, rank, pad_start)
    scatter_tm = min(1024, T)
    xs = _scatter_rows(pad_start, pad_end, _dest_tiles(dest, scatter_tm), h1p, nb * rb, rb, tm=scatter_tm)
    ys = _experts(first_blk, nblk, wsel, total_blk, xs, w_gate, w_up, w_down, rb)
    return _combine(dest, h1, gates.T, ys, ws_gate.astype(BF16), ws_up.astype(BF16), ws_down.astype(BF16), g2, b2)


def kernel(x, positions, emb_ln_g, emb_ln_b, w_in, hy_conv_w, hy_conv_b, hy_f_w1, hy_f_b1, hy_f_w2, hy_f_b2, hy_f_w3, hy_f_b3, hy_f_freq, hy_f_wout, hy_d, lambda_q1, lambda_k1, lambda_q2, lambda_k2, subln_g, w_o, ln1_g, ln1_b, w_router, router_bias, w_gate, w_up, w_down, ws_gate, ws_up, ws_down, ln2_g, ln2_b):
    B, L, D = x.shape
    T = B * L
    assert w_in.shape[0] == DEPTH == 1
    i = 0
    x2 = x.reshape(T, D)
    g0, b0 = emb_ln_g[None], emb_ln_b[None]
    ra, rm, rp = _rotary_tables(positions)
    q, k, v, u = _inproj(x2, g0, b0, w_in[i].astype(BF16), ra, rm, rp)
    lam = (jnp.exp(jnp.sum(lambda_q1[i] * lambda_k1[i])) - jnp.exp(jnp.sum(lambda_q2[i] * lambda_k2[i])) + LAM_INIT)
    attn = _attention(q, k, v, lam.reshape(1).astype(F32), subln_g[i][None], B, L)
    hy = _hyena(u, hy_conv_w[i], hy_conv_b[i], hy_f_w1[i], hy_f_b1[i], hy_f_w2[i], hy_f_b2[i], hy_f_w3[i], hy_f_b3[i],
                hy_f_freq[i], hy_f_wout[i], hy_d[i], B, L)
    h1, h1p, scores_t = _oproj(x2, g0, b0, attn, hy, w_o[i].astype(BF16), ln1_g[i][None], ln1_b[i][None], w_router[i].T)
    out = _moe(h1, h1p, scores_t, router_bias[i], w_gate[i], w_up[i], w_down[i], ws_gate[i], ws_up[i], ws_down[i],
               ln2_g[i][None], ln2_b[i][None])
    return out.reshape(B, L, D)
```
